```python
import math
import jax, jax.numpy as jnp
from jax import lax
import numpy as np

D_MODEL = 1024
BATCH = 16
SEQ = 2048
DEPTH = 1

EPS = 1e-6
D_MIX = D_MODEL
HGRN_HEADS = 4
HGRN_EXPAND = 128
HGRN_DV = 128
HGRN_KEY_WIDTH = HGRN_HEADS * HGRN_EXPAND
HGRN_WIDTH = HGRN_HEADS * HGRN_DV
HGRN_CHUNK = 64
DIFF_HEADS = 4
DIFF_HALF = 64
DIFF_VDIM = 2 * DIFF_HALF
DIFF_WIDTH = DIFF_HEADS * DIFF_VDIM
Q_BLOCK = 128
REL_BUCKETS = 32
REL_MAX_DIST = 128
IN_SPLIT_SIZES = (HGRN_KEY_WIDTH, HGRN_KEY_WIDTH, HGRN_KEY_WIDTH, HGRN_WIDTH, HGRN_WIDTH,
                  DIFF_HEADS * 2 * DIFF_HALF, DIFF_HEADS * 2 * DIFF_HALF, DIFF_WIDTH)
D_IN_PROJ = 3 * HGRN_KEY_WIDTH + 2 * HGRN_WIDTH + 4 * DIFF_HEADS * DIFF_HALF + DIFF_WIDTH
PEER_HEADS = 8
PEER_NKEYS = 128
PEER_N_EXPERTS = PEER_NKEYS * PEER_NKEYS
PEER_DKEY = 128
PEER_TOPK = 16
PEER_TOKEN_BLOCK = 128

kernel_name = 'hymba_hgrn2_diffattn_peer_encoder'


def rms_norm(x, g):
    xf = x.astype(jnp.float32)
    y = xf * lax.rsqrt(jnp.mean(xf * xf, axis=-1, keepdims=True) + EPS)
    return (y * g.astype(jnp.float32)).astype(x.dtype)


def t5_bucket(rel):
    nb = REL_BUCKETS // 2
    ret = jnp.where(rel > 0, nb, 0)
    n = jnp.abs(rel)
    max_exact = nb // 2
    nf = jnp.maximum(n, 1).astype(jnp.float32)
    large = max_exact + (jnp.log(nf / max_exact) / math.log(REL_MAX_DIST / max_exact)
                         * (nb - max_exact)).astype(jnp.int32)
    large = jnp.minimum(large, nb - 1)
    return ret + jnp.where(n < max_exact, n, large)


def hgrn2_chunk_scan(q, k, v, log_f):
    B, H, L, DK = q.shape
    DV = v.shape[-1]
    C = HGRN_CHUNK
    n = L // C

    def to_chunks(t):
        return jnp.moveaxis(t.reshape(B, H, n, C, t.shape[-1]), 2, 0)

    qc, kc, vc, fc = to_chunks(q), to_chunks(k), to_chunks(v), to_chunks(log_f)
    mask = jnp.tril(jnp.ones((C, C), dtype=bool))[:, :, None]

    def step(S, inp):
        qi, ki, vi, fi = inp
        b = jnp.cumsum(fi, axis=2)
        o_inter = jnp.einsum('bhtk,bhkv->bhtv', qi * jnp.exp(b), S)
        rel = b[:, :, :, None, :] - b[:, :, None, :, :]
        decay = jnp.exp(jnp.where(mask, rel, -jnp.inf))
        A = jnp.einsum('bhtk,bhsk,bhtsk->bhts', qi, ki, decay)
        o_intra = jnp.einsum('bhts,bhsv->bhtv', A, vi)
        b_last = b[:, :, -1:, :]
        S_new = jnp.exp(b_last[:, :, 0, :])[..., None] * S + \
            jnp.einsum('bhsk,bhsv->bhkv', ki * jnp.exp(b_last - b), vi)
        return S_new, o_inter + o_intra

    S0 = jnp.zeros((B, H, DK, DV), jnp.float32)
    _, o = lax.scan(step, S0, (qc, kc, vc, fc))
    return jnp.moveaxis(o, 0, 2).reshape(B, H, L, DV)


def hgrn2_group(hq, hf_f, hf_b, hi, hg, lb_f_tab, lb_b_tab, out_g, layer):
    B, L, _ = hq.shape

    def heads(t, dh):
        return t.reshape(B, L, HGRN_HEADS, dh).transpose(0, 2, 1, 3).astype(jnp.float32)

    q = jax.nn.silu(heads(hq, HGRN_EXPAND))
    v = heads(hi, HGRN_DV)

    def gates(z, lb_tab):
        lb = jnp.cumsum(jax.nn.softmax(lb_tab.astype(jnp.float32), axis=0), axis=0)[layer]
        lb = lb.reshape(HGRN_HEADS, 1, HGRN_EXPAND)
        z = heads(z, HGRN_EXPAND)
        log_f = jnp.logaddexp(jnp.log(lb), jnp.log1p(-lb) + jax.nn.log_sigmoid(z))
        k = (1.0 - lb) * jax.nn.sigmoid(-z)
        return log_f, k

    lf_f, k_f = gates(hf_f, lb_f_tab)
    lf_b, k_b = gates(hf_b, lb_b_tab)
    flip = lambda t: jnp.flip(t, axis=2)
    o_f = hgrn2_chunk_scan(q, k_f, v, lf_f)
    o_b = flip(hgrn2_chunk_scan(flip(q), flip(k_b), flip(v), flip(lf_b)))
    o = rms_norm(o_f + o_b, out_g) * jax.nn.silu(heads(hg, HGRN_DV))
    return o.transpose(0, 2, 1, 3).reshape(B, L, HGRN_WIDTH).astype(hq.dtype)


def diff_attention(q, k, v, lam, rel_bias):
    B, H, L = q.shape[0], q.shape[1], q.shape[2]
    nb = L // Q_BLOCK
    scale = DIFF_HALF ** -0.5
    qb = jnp.moveaxis(q.reshape(B, H, nb, Q_BLOCK, 2, DIFF_HALF), 2, 0)
    key_pos = jnp.arange(L, dtype=jnp.int32)

    def block(args):
        q_blk, blk_idx = args
        q_pos = blk_idx * Q_BLOCK + jnp.arange(Q_BLOCK, dtype=jnp.int32)
        bucket = t5_bucket(key_pos[None, :] - q_pos[:, None])
        bias = jnp.moveaxis(rel_bias[bucket], -1, 0).astype(jnp.float32)
        s = jnp.einsum('bhqmd,bhkmd->bhmqk', q_blk, k).astype(jnp.float32) * scale
        p = jax.nn.softmax(s + bias[None, :, None], axis=-1)
        w = p[:, :, 0] - lam * p[:, :, 1]
        return jnp.einsum('bhqk,bhkv->bhqv', w.astype(v.dtype), v)

    o = lax.map(block, (qb, jnp.arange(nb, dtype=jnp.int32)))
    return jnp.moveaxis(o, 0, 2).reshape(B, H, L, DIFF_VDIM)


def diff_attn_group(dq, dk, dv, lq1, lk1, lq2, lk2, out_g, rel_bias, layer):
    B, L, _ = dq.shape
    q = dq.reshape(B, L, DIFF_HEADS, 2, DIFF_HALF).transpose(0, 2, 1, 3, 4)
    k = dk.reshape(B, L, DIFF_HEADS, 2, DIFF_HALF).transpose(0, 2, 1, 3, 4)
    v = dv.reshape(B, L, DIFF_HEADS, DIFF_VDIM).transpose(0, 2, 1, 3)
    lam_init = 0.8 - 0.6 * math.exp(-0.3 * layer)
    f32 = jnp.float32
    lam = jnp.exp(jnp.sum(lq1.astype(f32) * lk1.astype(f32))) - \
        jnp.exp(jnp.sum(lq2.astype(f32) * lk2.astype(f32))) + lam_init
    o = diff_attention(q, k, v, lam, rel_bias)
    o = rms_norm(o, out_g) * (1.0 - lam_init)
    return o.transpose(0, 2, 1, 3).reshape(B, L, DIFF_WIDTH).astype(dq.dtype)


def peer_ffn(x, w_q, sub_keys, expert_u, expert_v):
    B, L, D = x.shape
    T = B * L
    xt = x.reshape(T, D)
    q = jnp.einsum('td,dhe->the', xt, w_q).reshape(T, PEER_HEADS, 2, PEER_DKEY)
    s = jnp.einsum('thpe,hpne->thpn', q, sub_keys).astype(jnp.float32)
    s_top, i_top = lax.top_k(s, PEER_TOPK)
    cand_s = (s_top[:, :, 0, :, None] + s_top[:, :, 1, None, :]).reshape(T, PEER_HEADS, PEER_TOPK * PEER_TOPK)
    cand_i = (i_top[:, :, 0, :, None] * PEER_NKEYS + i_top[:, :, 1, None, :]).reshape(T, PEER_HEADS, PEER_TOPK * PEER_TOPK)
    best_s, pos = lax.top_k(cand_s, PEER_TOPK)
    idx = jnp.take_along_axis(cand_i, pos, axis=-1)
    gate = jax.nn.softmax(best_s, axis=-1).astype(x.dtype)
    nblk = T // PEER_TOKEN_BLOCK

    def block(args):
        xb, ib, gb = args
        u = expert_u[ib]
        hdn = jax.nn.gelu(jnp.einsum('cd,chkd->chk', xb, u), approximate=False)
        return jnp.einsum('chk,chkd->cd', gb * hdn, expert_v[ib])

    out = lax.map(block, (xt.reshape(nblk, PEER_TOKEN_BLOCK, D),
                          idx.reshape(nblk, PEER_TOKEN_BLOCK, PEER_HEADS, PEER_TOPK),
                          gate.reshape(nblk, PEER_TOKEN_BLOCK, PEER_HEADS, PEER_TOPK)))
    return out.reshape(B, L, D)


def setup_inputs(seed: int = 0) -> dict:
    key = jax.random.key(seed)
    ks = jax.random.split(key, 20)
    nrm = lambda k, shape, s: jax.random.normal(k, shape, jnp.float32) * s
    gain = lambda k, shape: 1.0 + nrm(k, shape, 0.02)
    return {
        'x': nrm(ks[0], (BATCH, SEQ, D_MODEL), 1.0),
        'norm1_g': gain(ks[1], (DEPTH, D_MODEL)),
        'w_in': nrm(ks[2], (DEPTH, D_MODEL, D_IN_PROJ), D_MODEL ** -0.5),
        'hgrn_lb_fwd': nrm(ks[3], (DEPTH + 1, HGRN_KEY_WIDTH), 0.5),
        'hgrn_lb_bwd': nrm(ks[4], (DEPTH + 1, HGRN_KEY_WIDTH), 0.5),
        'hgrn_out_g': gain(ks[5], (DEPTH, HGRN_DV)),
        'diff_lam_q1': nrm(ks[6], (DEPTH, DIFF_HALF), 0.1),
        'diff_lam_k1': nrm(ks[7], (DEPTH, DIFF_HALF), 0.1),
        'diff_lam_q2': nrm(ks[8], (DEPTH, DIFF_HALF), 0.1),
        'diff_lam_k2': nrm(ks[9], (DEPTH, DIFF_HALF), 0.1),
        'diff_out_g': gain(ks[10], (DEPTH, DIFF_VDIM)),
        'rel_bias': nrm(ks[11], (REL_BUCKETS, DIFF_HEADS), 0.3),
        'w_out': nrm(ks[12], (DEPTH, D_MIX, D_MODEL), D_MIX ** -0.5),
        'norm2_g': gain(ks[13], (DEPTH, D_MODEL)),
        'peer_w_q': nrm(ks[14], (DEPTH, D_MODEL, PEER_HEADS, 2 * PEER_DKEY), D_MODEL ** -0.5),
        'peer_sub_keys': nrm(ks[15], (DEPTH, PEER_HEADS, 2, PEER_NKEYS, PEER_DKEY), PEER_DKEY ** -0.5),
        'peer_u': nrm(ks[16], (DEPTH, PEER_N_EXPERTS, D_MODEL), D_MODEL ** -0.5),
        'peer_v': nrm(ks[17], (DEPTH, PEER_N_EXPERTS, D_MODEL), 0.5 * PEER_HEADS ** -0.5),
        'final_g': gain(ks[18], (D_MODEL,)),
    }


def reference(x, norm1_g, w_in, hgrn_lb_fwd, hgrn_lb_bwd, hgrn_out_g, diff_lam_q1, diff_lam_k1,
              diff_lam_q2, diff_lam_k2, diff_out_g, rel_bias, w_out, norm2_g, peer_w_q,
              peer_sub_keys, peer_u, peer_v, final_g):
    h = x
    split_at = [int(v) for v in np.cumsum(IN_SPLIT_SIZES)[:-1]]
    for layer in range(DEPTH):
        xn = rms_norm(h, norm1_g[layer])
        proj = jnp.einsum('bld,de->ble', xn, w_in[layer])
        hq, hf_f, hf_b, hi, hg, dq, dk, dv = jnp.split(proj, split_at, axis=-1)
        o_hgrn = hgrn2_group(hq, hf_f, hf_b, hi, hg, hgrn_lb_fwd, hgrn_lb_bwd,
                             hgrn_out_g[layer], layer)
        o_diff = diff_attn_group(dq, dk, dv, diff_lam_q1[layer], diff_lam_k1[layer],
                                 diff_lam_q2[layer], diff_lam_k2[layer], diff_out_g[layer],
                                 rel_bias, layer)
        mixed = jnp.concatenate([o_hgrn, o_diff], axis=-1)
        h = h + jnp.einsum('ble,ed->bld', mixed, w_out[layer])
        hn = rms_norm(h, norm2_g[layer])
        h = h + peer_ffn(hn, peer_w_q[layer], peer_sub_keys[layer], peer_u[layer], peer_v[layer])
    return rms_norm(h, final_g)
```

```python
import functools
import math

import numpy as np
import jax
import jax.numpy as jnp
from jax import lax
from jax.experimental import pallas as pl
from jax.experimental.pallas import tpu as pltpu

F32 = jnp.float32
BF16 = jnp.bfloat16
EPS = 1e-6

HGRN_HEADS = 4
HGRN_DK = 128
DIFF_HEADS = 4
DIFF_HALF = 64
REL_BUCKETS = 32
REL_MAX_DIST = 128
PEER_HEADS = 8
PEER_NKEYS = 128
PEER_DKEY = 128
PEER_TOPK = 16
LAYER = 0

LANES = 128
SUBLANES = 8
VMEM_LIMIT = 48 * 1024 * 1024

INPROJ_TM = 512
HG_C = 64
HG_LEVELS = (64, 32, 16, 8, 4, 2)
ATTN_TQ = 256
ROUTE_TM = 256
PEER_TP = 16


def _dot(a, b):
    return jnp.dot(a, b, preferred_element_type=F32)


def _dot_nt(a, b):
    return lax.dot_general(a, b, (((1,), (1,)), ((), ())), preferred_element_type=F32)


def _dot_tn(a, b):
    return lax.dot_general(a, b, (((0,), (0,)), ((), ())), preferred_element_type=F32)


def _silu(x):
    return x * (1.0 / (1.0 + jnp.exp(-x)))


def _inproj_kernel(x_ref, g_ref, w_ref, zf_ref, pb_ref):
    x = x_ref[...]
    ms = jnp.mean(x * x, axis=-1, keepdims=True)
    xn = (x * lax.rsqrt(ms + EPS) * g_ref[...]).astype(BF16)
    nz = zf_ref.shape[1]
    zf_ref[...] = _dot(xn, w_ref[:, 0:nz])
    nb = pb_ref.shape[1]
    step = 1024
    for j in range(nb // step):
        pb_ref[:, j * step:(j + 1) * step] = _dot(
            xn, w_ref[:, nz + j * step: nz + (j + 1) * step]).astype(BF16)


def _inproj(x2, g, w):
    T, D = x2.shape
    N = w.shape[1]
    nz = 2 * HGRN_HEADS * HGRN_DK
    tm = INPROJ_TM
    return pl.pallas_call(
        _inproj_kernel,
        grid=(T // tm,),
        in_specs=[
            pl.BlockSpec((tm, D), lambda i: (i, 0)),
            pl.BlockSpec((1, D), lambda i: (0, 0)),
            pl.BlockSpec((D, N), lambda i: (0, 0)),
        ],
        out_specs=[
            pl.BlockSpec((tm, nz), lambda i: (i, 0)),
            pl.BlockSpec((tm, N - nz), lambda i: (i, 0)),
        ],
        out_shape=[
            jax.ShapeDtypeStruct((T, nz), F32),
            jax.ShapeDtypeStruct((T, N - nz), BF16),
        ],
        compiler_params=pltpu.CompilerParams(
            dimension_semantics=("parallel",), vmem_limit_bytes=VMEM_LIMIT),
        name="inproj",
    )(x2, g, w)


def _hgrn_consts():
    C = HG_C
    r = np.arange(C)
    t = r[:, None]
    u = r[None, :]
    blocks = [u <= t, u > t]
    masks = [np.eye(C, dtype=bool)]
    for B in HG_LEVELS:
        half = B // 2
        a = (r // B) * B
        m = (a + half - 1)[:, None]
        upper = (r - a) >= half
        blocks.append(np.where(upper[:, None], (u > m) & (u <= t), (u > t) & (u <= m)))
        same = a[:, None] == a[None, :]
        masks.append(same & upper[:, None] & (~upper)[None, :])
    m_f = np.concatenate(blocks, 0).astype(np.float32)
    m_b = np.concatenate([b[::-1, ::-1] for b in blocks], 0).astype(np.float32)
    k_f = np.stack(masks).astype(np.float32)
    k_b = np.ascontiguousarray(k_f.transpose(0, 2, 1))
    return m_f, m_b, k_f, k_b


def _hgrn_gates(z, tab):
    tabf = tab.astype(F32)
    e = jnp.exp(tabf - jnp.max(tabf, axis=0, keepdims=True))
    lb = jnp.sum(e[0:LAYER + 1], axis=0, keepdims=True) / jnp.sum(e, axis=0, keepdims=True)
    log_lb = jnp.log(lb)
    log_1m = jnp.log1p(-lb)
    ez = jnp.exp(-jnp.abs(z))
    l1p = jnp.log1p(ez)
    log_sig = jnp.minimum(z, 0.0) - l1p
    c = log_1m + log_sig
    hi = jnp.maximum(log_lb, c)
    lo = jnp.minimum(log_lb, c)
    log_f = hi + jnp.log1p(jnp.exp(lo - hi))
    sig_neg = jnp.where(z >= 0.0, ez, 1.0) / (1.0 + ez)
    k = (1.0 - lb) * sig_neg
    return log_f, k


def _hgrn_kernel(lbf_ref, lbb_ref, og_ref, mf_ref, mb_ref, kf_ref, kb_ref,
                 q_ref, zf_ref, zb_ref, v_ref, g_ref, o_ref, of_s):
    C = HG_C
    L = q_ref.shape[0]
    n = L // C
    dv = v_ref.shape[1]

    def chunk(c, st, m_ref, k_ref, z_ref, tab_ref, forward):
        sl = pl.ds(pl.multiple_of(c * C, C), C)
        qh = q_ref[sl, :].astype(F32)
        q = _silu(qh)
        v = v_ref[sl, :]
        log_f, k = _hgrn_gates(z_ref[sl, :], tab_ref[...])
        lf_hi = log_f.astype(BF16)
        lf_lo = (log_f - lf_hi.astype(F32)).astype(BF16)
        m = m_ref[...]
        e = jnp.exp(_dot(m, lf_hi) + _dot(m, lf_lo))
        e_b = e[0:C]
        e_s = e[C:2 * C]
        dec = e[C - 1:C] if forward else e[0:1]
        o = _dot_nt((q * e_b).astype(BF16), st.astype(BF16))
        a = _dot_nt(q.astype(BF16), k.astype(BF16)) * k_ref[0]
        for l in range(len(HG_LEVELS)):
            e_l = e[(2 + l) * C:(3 + l) * C]
            a = a + _dot_nt((q * e_l).astype(BF16), (k * e_l).astype(BF16)) * k_ref[l + 1]
        o = o + _dot(a.astype(BF16), v)
        st = st * dec + _dot_tn(v, (k * e_s).astype(BF16))
        return sl, o, st

    st0 = jnp.zeros((dv, q_ref.shape[1]), F32)

    def fwd(c, st):
        sl, o, st = chunk(c, st, mf_ref, kf_ref, zf_ref, lbf_ref, True)
        of_s[sl, :] = o
        return st

    lax.fori_loop(0, n, fwd, st0)

    def bwd(i, st):
        c = n - 1 - i
        sl, o, st = chunk(c, st, mb_ref, kb_ref, zb_ref, lbb_ref, False)
        tot = of_s[sl, :] + o
        ms = jnp.mean(tot * tot, axis=-1, keepdims=True)
        y = tot * lax.rsqrt(ms + EPS) * og_ref[...]
        o_ref[sl, :] = (y * _silu(g_ref[sl, :].astype(F32))).astype(o_ref.dtype)
        return st

    lax.fori_loop(0, n, bwd, st0)


def _hgrn(zf3, pb3, lb_f, lb_b, out_g):
    B, L, _ = zf3.shape
    H, dk = HGRN_HEADS, HGRN_DK
    m_f, m_b, k_f, k_b = _hgrn_consts()
    nlev = k_f.shape[0]
    full2 = lambda b, h: (0, 0)
    full3 = lambda b, h: (0, 0, 0)
    seq = lambda off: pl.BlockSpec((None, L, dk), lambda b, h: (b, 0, off + h))
    return pl.pallas_call(
        _hgrn_kernel,
        grid=(B, H),
        in_specs=[
            pl.BlockSpec((lb_f.shape[0], dk), lambda b, h: (0, h)),
            pl.BlockSpec((lb_b.shape[0], dk), lambda b, h: (0, h)),
            pl.BlockSpec((1, dk), full2),
            pl.BlockSpec(m_f.shape, full2),
            pl.BlockSpec(m_b.shape, full2),
            pl.BlockSpec((nlev, HG_C, HG_C), full3),
            pl.BlockSpec((nlev, HG_C, HG_C), full3),
            seq(0),
            seq(0),
            seq(H),
            seq(H),
            seq(2 * H),
        ],
        out_specs=pl.BlockSpec((None, L, dk), lambda b, h: (b, 0, h)),
        out_shape=jax.ShapeDtypeStruct((B, L, H * dk), BF16),
        scratch_shapes=[pltpu.VMEM((L, dk), F32)],
        compiler_params=pltpu.CompilerParams(
            dimension_semantics=("parallel", "parallel"), vmem_limit_bytes=VMEM_LIMIT),
        name="hgrn",
    )(lb_f, lb_b, out_g, jnp.asarray(m_f, BF16), jnp.asarray(m_b, BF16),
      jnp.asarray(k_f), jnp.asarray(k_b), pb3, zf3, zf3, pb3, pb3)


def _t5_bucket(rel):
    nb = REL_BUCKETS // 2
    ret = jnp.where(rel > 0, nb, 0)
    n = jnp.abs(rel)
    max_exact = nb // 2
    nf = jnp.maximum(n, 1).astype(jnp.float32)
    large = max_exact + (jnp.log(nf / max_exact) / math.log(REL_MAX_DIST / max_exact)
                         * (nb - max_exact)).astype(jnp.int32)
    large = jnp.minimum(large, nb - 1)
    return ret + jnp.where(n < max_exact, n, large)


def _rel_bias_ext(rel_bias, L):
    j = jnp.arange(2 * L - LANES, dtype=jnp.int32)
    ql = jnp.arange(LANES, dtype=jnp.int32)
    rel = j[None, :] - (L - LANES) - ql[:, None]
    return jnp.moveaxis(rel_bias[_t5_bucket(rel)], -1, 0).astype(F32)


def _attn_kernel(lam_ref, q_ref, k_ref, v_ref, bias_ref, og_ref, o_ref, *, lam_init):
    tq = q_ref.shape[0]
    L = k_ref.shape[0]
    qi = pl.program_id(2)
    lam = lam_ref[0, 0]
    q = q_ref[...]
    k = k_ref[...]
    lane = lax.broadcasted_iota(jnp.int32, q.shape, 1)
    zero = jnp.zeros_like(q)
    q0 = jnp.where(lane < DIFF_HALF, q, zero)
    q1 = jnp.where(lane >= DIFF_HALF, q, zero)
    parts = []
    for j in range(tq // LANES):
        off = pl.multiple_of(L - LANES - (qi * tq + j * LANES), LANES)
        parts.append(bias_ref[:, pl.ds(off, L)])
    bias = jnp.concatenate(parts, axis=0) if len(parts) > 1 else parts[0]

    def soft(qm):
        s = _dot_nt(qm, k) + bias
        e = jnp.exp(s - jnp.max(s, axis=-1, keepdims=True))
        return e, 1.0 / jnp.sum(e, axis=-1, keepdims=True)

    e0, r0 = soft(q0)
    e1, r1 = soft(q1)
    w = e0 * r0 - e1 * (lam * r1)
    o = _dot(w.astype(BF16), v_ref[...])
    ms = jnp.mean(o * o, axis=-1, keepdims=True)
    y = o * lax.rsqrt(ms + EPS) * og_ref[...] * (1.0 - lam_init)
    o_ref[...] = y.astype(o_ref.dtype)


def _attn(pb3, bias_ext, lam, out_g):
    B, L, _ = pb3.shape
    H, dh = DIFF_HEADS, 2 * DIFF_HALF
    tq = ATTN_TQ
    base = (3 * HGRN_HEADS * HGRN_DK) // dh
    lam_init = 0.8 - 0.6 * math.exp(-0.3 * LAYER)
    return pl.pallas_call(
        functools.partial(_attn_kernel, lam_init=lam_init),
        grid=(B, H, L // tq),
        in_specs=[
            pl.BlockSpec(memory_space=pltpu.SMEM),
            pl.BlockSpec((None, tq, dh), lambda b, h, i: (b, i, base + h)),
            pl.BlockSpec((None, L, dh), lambda b, h, i: (b, 0, base + H + h)),
            pl.BlockSpec((None, L, dh), lambda b, h, i: (b, 0, base + 2 * H + h)),
            pl.BlockSpec((None, LANES, 2 * L - LANES), lambda b, h, i: (h, 0, 0)),
            pl.BlockSpec((1, dh), lambda b, h, i: (0, 0)),
        ],
        out_specs=pl.BlockSpec((None, tq, dh), lambda b, h, i: (b, i, h)),
        out_shape=jax.ShapeDtypeStruct((B, L, H * dh), BF16),
        compiler_params=pltpu.CompilerParams(
            dimension_semantics=("parallel", "parallel", "parallel"),
            vmem_limit_bytes=VMEM_LIMIT),
        name="attn",
    )(lam, pb3, pb3, pb3, bias_ext, out_g)


def _cand_layout():
    K = PEER_TOPK
    groups = [("a", 0, 0), ("a", 0, 8), ("a", 1, 0), ("a", 2, 0), ("a", 3, 0),
              ("b", 0, 8), ("b", 0, 0), ("b", 1, 0), ("b", 2, 0)]
    seen = set()
    pos, valid = [], []
    for kind, fixed, start in groups:
        for r in range(SUBLANES):
            a, b = (fixed, start + r) if kind == "a" else (start + r, fixed)
            ok = (a + 1) * (b + 1) <= K and (a, b) not in seen
            if ok:
                seen.add((a, b))
            pos.append(a * K + b if ok else K * K + len(pos))
            valid.append(ok)
    assert len(seen) == sum(K // (a + 1) for a in range(K))
    return groups, np.array(pos, np.int32), np.array(valid, bool)


def _extract_topk(s, key, payload, k, big):
    rows = lax.broadcasted_iota(jnp.int32, (k, s.shape[1]), 0)

    def body(j, carry):
        s, vals, pay = carry
        m = jnp.max(s, axis=0, keepdims=True)
        kk = jnp.min(jnp.where(s == m, key, big), axis=0, keepdims=True)
        sel = key == kk
        p = jnp.sum(jnp.where(sel, payload, 0), axis=0, keepdims=True)
        vals = jnp.where(rows == j, m, vals)
        pay = jnp.where(rows == j, p, pay)
        s = jnp.where(sel, -jnp.inf, s)
        return s, vals, pay

    init = (s, jnp.zeros((k, s.shape[1]), F32), jnp.zeros((k, s.shape[1]), jnp.int32))
    _, vals, pay = lax.fori_loop(0, k, body, init)
    return vals, pay


def _route_kernel(x_ref, oh_ref, od_ref, woh_ref, wod_ref, g2_ref, wq_ref, sk_ref, cpos_ref, cmask_ref,
                  h_ref, hn_ref, idx_ref, gate_ref, q_s, tv_s, ti_s, gs_s, is_s, *, groups):
    K = PEER_TOPK
    tm = x_ref.shape[0]
    h = x_ref[...] + _dot(oh_ref[...], woh_ref[...]) + _dot(od_ref[...], wod_ref[...])
    h_ref[...] = h
    hn = h * lax.rsqrt(jnp.mean(h * h, axis=-1, keepdims=True) + EPS) * g2_ref[...]
    hn_ref[...] = hn
    q_s[...] = _dot(hn.astype(BF16), wq_ref[...]).astype(BF16)

    key_iota = lax.broadcasted_iota(jnp.int32, (PEER_NKEYS, tm), 0)

    def half_topk(hp, _):
        col = pl.multiple_of(hp * PEER_DKEY, PEER_DKEY)
        s = _dot_nt(sk_ref[hp], q_s[:, pl.ds(col, PEER_DKEY)])
        vals, idxs = _extract_topk(s, key_iota, key_iota, K, PEER_NKEYS)
        tv_s[hp] = vals
        ti_s[hp] = idxs
        return 0

    lax.fori_loop(0, 2 * PEER_HEADS, half_topk, 0)

    cpos = cpos_ref[...]
    cmask = cmask_ref[...]

    def head(hd, _):
        s0, s1 = tv_s[2 * hd], tv_s[2 * hd + 1]
        i0, i1 = ti_s[2 * hd] * PEER_NKEYS, ti_s[2 * hd + 1]
        cs, ci = [], []
        for kind, fixed, start in groups:
            if kind == "a":
                cs.append(s0[fixed:fixed + 1] + s1[start:start + SUBLANES])
                ci.append(i0[fixed:fixed + 1] + i1[start:start + SUBLANES])
            else:
                cs.append(s0[start:start + SUBLANES] + s1[fixed:fixed + 1])
                ci.append(i0[start:start + SUBLANES] + i1[fixed:fixed + 1])
        cand = jnp.concatenate(cs, axis=0) + cmask
        cidx = jnp.concatenate(ci, axis=0)
        best, eidx = _extract_topk(cand, cpos, cidx, K, 2 * K * K)
        ex = jnp.exp(best - best[0:1])
        gate = ex / jnp.sum(ex, axis=0, keepdims=True)
        row = pl.ds(pl.multiple_of(hd * K, K), K)
        gs_s[row, :] = gate
        is_s[row, :] = eidx.astype(F32)
        return 0

    lax.fori_loop(0, PEER_HEADS, head, 0)
    gate_ref[...] = gs_s[...].T
    idx_ref[...] = is_s[...].T.astype(jnp.int32)


def _route(x2, oh2, od2, w_out, g2, w_q, sub_keys):
    T, D = x2.shape
    tm = ROUTE_TM
    K = PEER_TOPK
    nh = oh2.shape[1]
    nq = w_q.shape[1]
    npk = PEER_HEADS * K
    groups, pos, valid = _cand_layout()
    ncand = pos.shape[0]
    cpos = jnp.asarray(np.broadcast_to(pos[:, None], (ncand, tm)))
    cmask = jnp.asarray(np.broadcast_to(np.where(valid, 0.0, -np.inf).astype(np.float32)[:, None], (ncand, tm)))
    row = lambda i: (i, 0)
    full2 = lambda i: (0, 0)
    return pl.pallas_call(
        functools.partial(_route_kernel, groups=groups),
        grid=(T // tm,),
        in_specs=[
            pl.BlockSpec((tm, D), row),
            pl.BlockSpec((tm, nh), row),
            pl.BlockSpec((tm, nh), row),
            pl.BlockSpec((nh, D), full2),
            pl.BlockSpec((nh, D), lambda i: (1, 0)),
            pl.BlockSpec((1, D), full2),
            pl.BlockSpec((D, nq), full2),
            pl.BlockSpec(sub_keys.shape, lambda i: (0, 0, 0)),
            pl.BlockSpec((ncand, tm), full2),
            pl.BlockSpec((ncand, tm), full2),
        ],
        out_specs=[
            pl.BlockSpec((tm, D), row),
            pl.BlockSpec((tm, D), row),
            pl.BlockSpec((tm, npk), row),
            pl.BlockSpec((tm, npk), row),
        ],
        out_shape=[
            jax.ShapeDtypeStruct((T, D), F32),
            jax.ShapeDtypeStruct((T, D), F32),
            jax.ShapeDtypeStruct((T, npk), jnp.int32),
            jax.ShapeDtypeStruct((T, npk), F32),
        ],
        scratch_shapes=[
            pltpu.VMEM((tm, nq), BF16),
            pltpu.VMEM((2 * PEER_HEADS, K, tm), F32),
            pltpu.VMEM((2 * PEER_HEADS, K, tm), jnp.int32),
            pltpu.VMEM((npk, tm), F32),
            pltpu.VMEM((npk, tm), F32),
        ],
        compiler_params=pltpu.CompilerParams(
            dimension_semantics=("parallel",), vmem_limit_bytes=VMEM_LIMIT),
        name="route",
    )(x2, oh2, od2, w_out, w_out, g2, w_q, sub_keys, cpos, cmask)


def _gelu(x):
    return 0.5 * x * (1.0 + lax.erf(x * (1.0 / math.sqrt(2.0))))


def _peer_tc_kernel(idx_hbm, gate_ref, hn_ref, h_ref, fg_ref, u_hbm, v_hbm, y_ref,
                    idx_s, ug, vg, sem):
    tp, npk = gate_ref.shape
    D = hn_ref.shape[1]
    nrow = tp * npk
    i = pl.program_id(0)
    cp = pltpu.make_async_copy(idx_hbm.at[pl.ds(i * tp, tp)], idx_s, sem.at[0])
    cp.start()
    cp.wait()

    def issue(r, _):
        e = idx_s[r // npk, r % npk]
        pltpu.make_async_copy(u_hbm.at[pl.ds(e, 1)], ug.at[pl.ds(r, 1)], sem.at[1]).start()
        pltpu.make_async_copy(v_hbm.at[pl.ds(e, 1)], vg.at[pl.ds(r, 1)], sem.at[2]).start()
        return 0

    lax.fori_loop(0, nrow, issue, 0)
    pltpu.make_async_copy(u_hbm.at[pl.ds(0, nrow)], ug, sem.at[1]).wait()
    pltpu.make_async_copy(v_hbm.at[pl.ds(0, nrow)], vg, sem.at[2]).wait()

    nchunk = D // LANES

    def token(c, _):
        rows = pl.ds(pl.multiple_of(c * npk, npk), npk)
        xc = hn_ref[pl.ds(c, 1), :]
        part = jnp.zeros((npk, LANES), F32)
        for j in range(nchunk):
            cs = slice(j * LANES, (j + 1) * LANES)
            part = part + ug[rows, cs] * xc[:, cs]
        hrow = jnp.sum(part.T, axis=0, keepdims=True)
        act = _gelu(hrow) * gate_ref[pl.ds(c, 1), :]
        wb = jnp.broadcast_to(act, (npk, npk)).T
        outs = []
        for j in range(nchunk):
            cs = slice(j * LANES, (j + 1) * LANES)
            outs.append(jnp.sum(vg[rows, cs] * wb, axis=0, keepdims=True))
        yc = h_ref[pl.ds(c, 1), :] + jnp.concatenate(outs, axis=1)
        ms = jnp.mean(yc * yc, axis=-1, keepdims=True)
        y_ref[pl.ds(c, 1), :] = yc * lax.rsqrt(ms + EPS) * fg_ref[...]
        return 0

    lax.fori_loop(0, tp, token, 0)


def _peer_tc(idx, gate, hn, h, final_g, u, v):
    T, D = hn.shape
    npk = idx.shape[1]
    tp = PEER_TP
    row = lambda i: (i, 0)
    return pl.pallas_call(
        _peer_tc_kernel,
        grid=(T // tp,),
        in_specs=[
            pl.BlockSpec(memory_space=pl.ANY),
            pl.BlockSpec((tp, npk), row),
            pl.BlockSpec((tp, D), row),
            pl.BlockSpec((tp, D), row),
            pl.BlockSpec((1, D), lambda i: (0, 0)),
            pl.BlockSpec(memory_space=pl.ANY),
            pl.BlockSpec(memory_space=pl.ANY),
        ],
        out_specs=pl.BlockSpec((tp, D), row),
        out_shape=jax.ShapeDtypeStruct((T, D), F32),
        scratch_shapes=[
            pltpu.SMEM((tp, npk), jnp.int32),
            pltpu.VMEM((tp * npk, D), F32),
            pltpu.VMEM((tp * npk, D), F32),
            pltpu.SemaphoreType.DMA((3,)),
        ],
        compiler_params=pltpu.CompilerParams(
            dimension_semantics=("arbitrary",), vmem_limit_bytes=VMEM_LIMIT),
        name="peer_tc",
    )(idx, gate, hn, h, final_g, u, v)


def kernel(x, norm1_g, w_in, hgrn_lb_fwd, hgrn_lb_bwd, hgrn_out_g, diff_lam_q1, diff_lam_k1,
           diff_lam_q2, diff_lam_k2, diff_out_g, rel_bias, w_out, norm2_g, peer_w_q,
           peer_sub_keys, peer_u, peer_v, final_g):
    B, L, D = x.shape
    T = B * L
    hw = HGRN_HEADS * HGRN_DK
    x2 = x.reshape(T, D)

    w = w_in[LAYER]
    scale = DIFF_HALF ** -0.5
    cols = lambda j: w[:, j * hw:(j + 1) * hw]
    w_r = jnp.concatenate([cols(1), cols(2), cols(0), cols(3), cols(4), cols(5) * scale, cols(6), cols(7)],
                          axis=1).astype(BF16)

    zf, pb = _inproj(x2, norm1_g[LAYER][None, :], w_r)
    zf3 = zf.reshape(B, L, -1)
    pb3 = pb.reshape(B, L, -1)

    o_h = _hgrn(zf3, pb3, hgrn_lb_fwd, hgrn_lb_bwd, hgrn_out_g[LAYER][None, :])

    f32 = jnp.float32
    lam_init = 0.8 - 0.6 * math.exp(-0.3 * LAYER)
    lam = (jnp.exp(jnp.sum(diff_lam_q1[LAYER].astype(f32) * diff_lam_k1[LAYER].astype(f32)))
           - jnp.exp(jnp.sum(diff_lam_q2[LAYER].astype(f32) * diff_lam_k2[LAYER].astype(f32))) + lam_init)
    o_d = _attn(pb3, _rel_bias_ext(rel_bias, L), lam.reshape(1, 1), diff_out_g[LAYER][None, :])

    sk = peer_sub_keys[LAYER].reshape(2 * PEER_HEADS, PEER_NKEYS, PEER_DKEY).astype(BF16)
    wq = peer_w_q[LAYER].reshape(D, -1).astype(BF16)
    h, hn, idx, gate = _route(x2, o_h.reshape(T, -1), o_d.reshape(T, -1), w_out[LAYER].astype(BF16),
                              norm2_g[LAYER][None, :], wq, sk)

    y = _peer_tc(idx, gate, hn, h, final_g[None, :], peer_u[LAYER], peer_v[LAYER])
    return y.reshape(B, L, D)
```

```python
import functools
import math

import numpy as np
import jax
import jax.numpy as jnp
from jax import lax
from jax.experimental import pallas as pl
from jax.experimental.pallas import tpu as pltpu
from jax.experimental.pallas import tpu_sc as plsc

F32 = jnp.float32
BF16 = jnp.bfloat16
EPS = 1e-6

HGRN_HEADS = 4
HGRN_DK = 128
DIFF_HEADS = 4
DIFF_HALF = 64
REL_BUCKETS = 32
REL_MAX_DIST = 128
PEER_HEADS = 8
PEER_NKEYS = 128
PEER_DKEY = 128
PEER_TOPK = 16
LAYER = 0

LANES = 128
SUBLANES = 8
VMEM_LIMIT = 48 * 1024 * 1024

INPROJ_TM = 512
HG_C = 64
HG_LEVELS = (64, 32, 16, 8, 4, 2)
ATTN_TQ = 256
ROUTE_TM = 256
PEER_ACT_TM = 2048
FINAL_TM = 512

SC_CORES = 2
SC_SUBCORES = 16
SC_LANES = 16
SC_WORKERS = SC_CORES * SC_SUBCORES
SC_GATHER_ROWS = 32
SC_TOKEN_GROUP = 8


def _dot(a, b):
    return jnp.dot(a, b, preferred_element_type=F32)


def _dot_nt(a, b):
    return lax.dot_general(a, b, (((1,), (1,)), ((), ())), preferred_element_type=F32)


def _dot_tn(a, b):
    return lax.dot_general(a, b, (((0,), (0,)), ((), ())), preferred_element_type=F32)


def _silu(x):
    return x * (1.0 / (1.0 + jnp.exp(-x)))


def _inproj_kernel(x_ref, g_ref, w_ref, zf_ref, pb_ref):
    x = x_ref[...]
    ms = jnp.mean(x * x, axis=-1, keepdims=True)
    xn = (x * lax.rsqrt(ms + EPS) * g_ref[...]).astype(BF16)
    nz = zf_ref.shape[1]
    zf_ref[...] = _dot(xn, w_ref[:, 0:nz])
    nb = pb_ref.shape[1]
    step = 1024
    for j in range(nb // step):
        pb_ref[:, j * step:(j + 1) * step] = _dot(
            xn, w_ref[:, nz + j * step: nz + (j + 1) * step]).astype(BF16)


def _inproj(x2, g, w):
    T, D = x2.shape
    N = w.shape[1]
    nz = 2 * HGRN_HEADS * HGRN_DK
    tm = INPROJ_TM
    return pl.pallas_call(
        _inproj_kernel,
        grid=(T // tm,),
        in_specs=[
            pl.BlockSpec((tm, D), lambda i: (i, 0)),
            pl.BlockSpec((1, D), lambda i: (0, 0)),
            pl.BlockSpec((D, N), lambda i: (0, 0)),
        ],
        out_specs=[
            pl.BlockSpec((tm, nz), lambda i: (i, 0)),
            pl.BlockSpec((tm, N - nz), lambda i: (i, 0)),
        ],
        out_shape=[
            jax.ShapeDtypeStruct((T, nz), F32),
            jax.ShapeDtypeStruct((T, N - nz), BF16),
        ],
        compiler_params=pltpu.CompilerParams(
            dimension_semantics=("parallel",), vmem_limit_bytes=VMEM_LIMIT),
        name="inproj",
    )(x2, g, w)


def _hgrn_consts():
    C = HG_C
    r = np.arange(C)
    t = r[:, None]
    u = r[None, :]
    blocks = [u <= t, u > t]
    masks = [np.eye(C, dtype=bool)]
    for B in HG_LEVELS:
        half = B // 2
        a = (r // B) * B
        m = (a + half - 1)[:, None]
        upper = (r - a) >= half
        blocks.append(np.where(upper[:, None], (u > m) & (u <= t), (u > t) & (u <= m)))
        same = a[:, None] == a[None, :]
        masks.append(same & upper[:, None] & (~upper)[None, :])
    m_f = np.concatenate(blocks, 0).astype(np.float32)
    m_b = np.concatenate([b[::-1, ::-1] for b in blocks], 0).astype(np.float32)
    k_f = np.stack(masks).astype(np.float32)
    k_b = np.ascontiguousarray(k_f.transpose(0, 2, 1))
    return m_f, m_b, k_f, k_b


def _hgrn_gates(z, tab):
    tabf = tab.astype(F32)
    e = jnp.exp(tabf - jnp.max(tabf, axis=0, keepdims=True))
    lb = jnp.sum(e[0:LAYER + 1], axis=0, keepdims=True) / jnp.sum(e, axis=0, keepdims=True)
    log_lb = jnp.log(lb)
    log_1m = jnp.log1p(-lb)
    ez = jnp.exp(-jnp.abs(z))
    l1p = jnp.log1p(ez)
    log_sig = jnp.minimum(z, 0.0) - l1p
    c = log_1m + log_sig
    hi = jnp.maximum(log_lb, c)
    lo = jnp.minimum(log_lb, c)
    log_f = hi + jnp.log1p(jnp.exp(lo - hi))
    sig_neg = jnp.where(z >= 0.0, ez, 1.0) / (1.0 + ez)
    k = (1.0 - lb) * sig_neg
    return log_f, k


def _hgrn_kernel(lbf_ref, lbb_ref, og_ref, mf_ref, mb_ref, kf_ref, kb_ref,
                 q_ref, zf_ref, zb_ref, v_ref, g_ref, o_ref, of_s):
    C = HG_C
    L = q_ref.shape[0]
    n = L // C
    dv = v_ref.shape[1]

    def chunk(c, st, m_ref, k_ref, z_ref, tab_ref, forward):
        sl = pl.ds(pl.multiple_of(c * C, C), C)
        qh = q_ref[sl, :].astype(F32)
        q = _silu(qh)
        v = v_ref[sl, :]
        log_f, k = _hgrn_gates(z_ref[sl, :], tab_ref[...])
        lf_hi = log_f.astype(BF16)
        lf_lo = (log_f - lf_hi.astype(F32)).astype(BF16)
        m = m_ref[...]
        e = jnp.exp(_dot(m, lf_hi) + _dot(m, lf_lo))
        e_b = e[0:C]
        e_s = e[C:2 * C]
        dec = e[C - 1:C] if forward else e[0:1]
        o = _dot_nt((q * e_b).astype(BF16), st.astype(BF16))
        a = _dot_nt(q.astype(BF16), k.astype(BF16)) * k_ref[0]
        for l in range(len(HG_LEVELS)):
            e_l = e[(2 + l) * C:(3 + l) * C]
            a = a + _dot_nt((q * e_l).astype(BF16), (k * e_l).astype(BF16)) * k_ref[l + 1]
        o = o + _dot(a.astype(BF16), v)
        st = st * dec + _dot_tn(v, (k * e_s).astype(BF16))
        return sl, o, st

    st0 = jnp.zeros((dv, q_ref.shape[1]), F32)

    def fwd(c, st):
        sl, o, st = chunk(c, st, mf_ref, kf_ref, zf_ref, lbf_ref, True)
        of_s[sl, :] = o
        return st

    lax.fori_loop(0, n, fwd, st0)

    def bwd(i, st):
        c = n - 1 - i
        sl, o, st = chunk(c, st, mb_ref, kb_ref, zb_ref, lbb_ref, False)
        tot = of_s[sl, :] + o
        ms = jnp.mean(tot * tot, axis=-1, keepdims=True)
        y = tot * lax.rsqrt(ms + EPS) * og_ref[...]
        o_ref[sl, :] = (y * _silu(g_ref[sl, :].astype(F32))).astype(o_ref.dtype)
        return st

    lax.fori_loop(0, n, bwd, st0)


def _hgrn(zf3, pb3, lb_f, lb_b, out_g):
    B, L, _ = zf3.shape
    H, dk = HGRN_HEADS, HGRN_DK
    m_f, m_b, k_f, k_b = _hgrn_consts()
    nlev = k_f.shape[0]
    full2 = lambda b, h: (0, 0)
    full3 = lambda b, h: (0, 0, 0)
    seq = lambda off: pl.BlockSpec((None, L, dk), lambda b, h: (b, 0, off + h))
    return pl.pallas_call(
        _hgrn_kernel,
        grid=(B, H),
        in_specs=[
            pl.BlockSpec((lb_f.shape[0], dk), lambda b, h: (0, h)),
            pl.BlockSpec((lb_b.shape[0], dk), lambda b, h: (0, h)),
            pl.BlockSpec((1, dk), full2),
            pl.BlockSpec(m_f.shape, full2),
            pl.BlockSpec(m_b.shape, full2),
            pl.BlockSpec((nlev, HG_C, HG_C), full3),
            pl.BlockSpec((nlev, HG_C, HG_C), full3),
            seq(0),
            seq(0),
            seq(H),
            seq(H),
            seq(2 * H),
        ],
        out_specs=pl.BlockSpec((None, L, dk), lambda b, h: (b, 0, h)),
        out_shape=jax.ShapeDtypeStruct((B, L, H * dk), BF16),
        scratch_shapes=[pltpu.VMEM((L, dk), F32)],
        compiler_params=pltpu.CompilerParams(
            dimension_semantics=("parallel", "parallel"), vmem_limit_bytes=VMEM_LIMIT),
        name="hgrn",
    )(lb_f, lb_b, out_g, jnp.asarray(m_f, BF16), jnp.asarray(m_b, BF16),
      jnp.asarray(k_f), jnp.asarray(k_b), pb3, zf3, zf3, pb3, pb3)


def _t5_bucket(rel):
    nb = REL_BUCKETS // 2
    ret = jnp.where(rel > 0, nb, 0)
    n = jnp.abs(rel)
    max_exact = nb // 2
    nf = jnp.maximum(n, 1).astype(jnp.float32)
    large = max_exact + (jnp.log(nf / max_exact) / math.log(REL_MAX_DIST / max_exact)
                         * (nb - max_exact)).astype(jnp.int32)
    large = jnp.minimum(large, nb - 1)
    return ret + jnp.where(n < max_exact, n, large)


def _rel_bias_ext(rel_bias, L):
    j = jnp.arange(2 * L - LANES, dtype=jnp.int32)
    ql = jnp.arange(LANES, dtype=jnp.int32)
    rel = j[None, :] - (L - LANES) - ql[:, None]
    bucket = _t5_bucket(rel)
    tab = rel_bias.astype(F32)
    out = jnp.zeros((tab.shape[1],) + bucket.shape, F32)
    for b in range(REL_BUCKETS):
        out = jnp.where((bucket == b)[None], tab[b][:, None, None], out)
    return out


def _attn_kernel(lam_ref, q_ref, k_ref, v_ref, bias_ref, og_ref, o_ref, *, lam_init):
    tq = q_ref.shape[0]
    L = k_ref.shape[0]
    qi = pl.program_id(2)
    lam = lam_ref[0, 0]
    q = q_ref[...]
    k = k_ref[...]
    lane = lax.broadcasted_iota(jnp.int32, q.shape, 1)
    zero = jnp.zeros_like(q)
    q0 = jnp.where(lane < DIFF_HALF, q, zero)
    q1 = jnp.where(lane >= DIFF_HALF, q, zero)
    parts = []
    for j in range(tq // LANES):
        off = pl.multiple_of(L - LANES - (qi * tq + j * LANES), LANES)
        parts.append(bias_ref[:, pl.ds(off, L)])
    bias = jnp.concatenate(parts, axis=0) if len(parts) > 1 else parts[0]

    def soft(qm):
        s = _dot_nt(qm, k) + bias
        e = jnp.exp(s - jnp.max(s, axis=-1, keepdims=True))
        return e, 1.0 / jnp.sum(e, axis=-1, keepdims=True)

    e0, r0 = soft(q0)
    e1, r1 = soft(q1)
    w = e0 * r0 - e1 * (lam * r1)
    o = _dot(w.astype(BF16), v_ref[...])
    ms = jnp.mean(o * o, axis=-1, keepdims=True)
    y = o * lax.rsqrt(ms + EPS) * og_ref[...] * (1.0 - lam_init)
    o_ref[...] = y.astype(o_ref.dtype)


def _attn(pb3, bias_ext, lam, out_g):
    B, L, _ = pb3.shape
    H, dh = DIFF_HEADS, 2 * DIFF_HALF
    tq = ATTN_TQ
    base = (3 * HGRN_HEADS * HGRN_DK) // dh
    lam_init = 0.8 - 0.6 * math.exp(-0.3 * LAYER)
    return pl.pallas_call(
        functools.partial(_attn_kernel, lam_init=lam_init),
        grid=(B, H, L // tq),
        in_specs=[
            pl.BlockSpec(memory_space=pltpu.SMEM),
            pl.BlockSpec((None, tq, dh), lambda b, h, i: (b, i, base + h)),
            pl.BlockSpec((None, L, dh), lambda b, h, i: (b, 0, base + H + h)),
            pl.BlockSpec((None, L, dh), lambda b, h, i: (b, 0, base + 2 * H + h)),
            pl.BlockSpec((None, LANES, 2 * L - LANES), lambda b, h, i: (h, 0, 0)),
            pl.BlockSpec((1, dh), lambda b, h, i: (0, 0)),
        ],
        out_specs=pl.BlockSpec((None, tq, dh), lambda b, h, i: (b, i, h)),
        out_shape=jax.ShapeDtypeStruct((B, L, H * dh), BF16),
        compiler_params=pltpu.CompilerParams(
            dimension_semantics=("parallel", "parallel", "parallel"),
            vmem_limit_bytes=VMEM_LIMIT),
        name="attn",
    )(lam, pb3, pb3, pb3, bias_ext, out_g)


def _cand_layout():
    K = PEER_TOPK
    groups = [("a", 0, 0), ("a", 0, 8), ("a", 1, 0), ("a", 2, 0), ("a", 3, 0),
              ("b", 0, 8), ("b", 0, 0), ("b", 1, 0), ("b", 2, 0)]
    seen = set()
    pos, valid = [], []
    for kind, fixed, start in groups:
        for r in range(SUBLANES):
            a, b = (fixed, start + r) if kind == "a" else (start + r, fixed)
            ok = (a + 1) * (b + 1) <= K and (a, b) not in seen
            if ok:
                seen.add((a, b))
            pos.append(a * K + b if ok else K * K + len(pos))
            valid.append(ok)
    assert len(seen) == sum(K // (a + 1) for a in range(K))
    return groups, np.array(pos, np.int32), np.array(valid, bool)


def _extract_topk(s, key, payload, k, big):
    rows = lax.broadcasted_iota(jnp.int32, (k, s.shape[1]), 0)

    def body(j, carry):
        s, vals, pay = carry
        m = jnp.max(s, axis=0, keepdims=True)
        kk = jnp.min(jnp.where(s == m, key, big), axis=0, keepdims=True)
        sel = key == kk
        p = jnp.sum(jnp.where(sel, payload, 0), axis=0, keepdims=True)
        vals = jnp.where(rows == j, m, vals)
        pay = jnp.where(rows == j, p, pay)
        s = jnp.where(sel, -jnp.inf, s)
        return s, vals, pay

    init = (s, jnp.zeros((k, s.shape[1]), F32), jnp.zeros((k, s.shape[1]), jnp.int32))
    _, vals, pay = lax.fori_loop(0, k, body, init)
    return vals, pay


def _route_kernel(x_ref, oh_ref, od_ref, woh_ref, wod_ref, g2_ref, wq_ref, sk_ref, cpos_ref, cmask_ref,
                  h_ref, hn_ref, idx_ref, gate_ref, q_s, tv_s, ti_s, gs_s, is_s, *, groups):
    K = PEER_TOPK
    tm = x_ref.shape[0]
    h = x_ref[...] + _dot(oh_ref[...], woh_ref[...]) + _dot(od_ref[...], wod_ref[...])
    h_ref[...] = h
    hn = h * lax.rsqrt(jnp.mean(h * h, axis=-1, keepdims=True) + EPS) * g2_ref[...]
    hn_ref[...] = hn
    q_s[...] = _dot(hn.astype(BF16), wq_ref[...]).astype(BF16)

    key_iota = lax.broadcasted_iota(jnp.int32, (PEER_NKEYS, tm), 0)

    def half_topk(hp, _):
        col = pl.multiple_of(hp * PEER_DKEY, PEER_DKEY)
        s = _dot_nt(sk_ref[hp], q_s[:, pl.ds(col, PEER_DKEY)])
        vals, idxs = _extract_topk(s, key_iota, key_iota, K, PEER_NKEYS)
        tv_s[hp] = vals
        ti_s[hp] = idxs
        return 0

    lax.fori_loop(0, 2 * PEER_HEADS, half_topk, 0)

    cpos = cpos_ref[...]
    cmask = cmask_ref[...]

    def head(hd, _):
        s0, s1 = tv_s[2 * hd], tv_s[2 * hd + 1]
        i0, i1 = ti_s[2 * hd] * PEER_NKEYS, ti_s[2 * hd + 1]
        cs, ci = [], []
        for kind, fixed, start in groups:
            if kind == "a":
                cs.append(s0[fixed:fixed + 1] + s1[start:start + SUBLANES])
                ci.append(i0[fixed:fixed + 1] + i1[start:start + SUBLANES])
            else:
                cs.append(s0[start:start + SUBLANES] + s1[fixed:fixed + 1])
                ci.append(i0[start:start + SUBLANES] + i1[fixed:fixed + 1])
        cand = jnp.concatenate(cs, axis=0) + cmask
        cidx = jnp.concatenate(ci, axis=0)
        best, eidx = _extract_topk(cand, cpos, cidx, K, 2 * K * K)
        ex = jnp.exp(best - best[0:1])
        gate = ex / jnp.sum(ex, axis=0, keepdims=True)
        row = pl.ds(pl.multiple_of(hd * K, K), K)
        gs_s[row, :] = gate
        is_s[row, :] = eidx.astype(F32)
        return 0

    lax.fori_loop(0, PEER_HEADS, head, 0)
    gate_ref[...] = gs_s[...].T
    idx_ref[...] = is_s[...].T.astype(jnp.int32)


def _route(x2, oh2, od2, w_out, g2, w_q, sub_keys):
    T, D = x2.shape
    tm = ROUTE_TM
    K = PEER_TOPK
    nh = oh2.shape[1]
    nq = w_q.shape[1]
    npk = PEER_HEADS * K
    groups, pos, valid = _cand_layout()
    ncand = pos.shape[0]
    cpos = jnp.asarray(np.broadcast_to(pos[:, None], (ncand, tm)))
    cmask = jnp.asarray(np.broadcast_to(np.where(valid, 0.0, -np.inf).astype(np.float32)[:, None], (ncand, tm)))
    row = lambda i: (i, 0)
    full2 = lambda i: (0, 0)
    return pl.pallas_call(
        functools.partial(_route_kernel, groups=groups),
        grid=(T // tm,),
        in_specs=[
            pl.BlockSpec((tm, D), row),
            pl.BlockSpec((tm, nh), row),
            pl.BlockSpec((tm, nh), row),
            pl.BlockSpec((nh, D), full2),
            pl.BlockSpec((nh, D), lambda i: (1, 0)),
            pl.BlockSpec((1, D), full2),
            pl.BlockSpec((D, nq), full2),
            pl.BlockSpec(sub_keys.shape, lambda i: (0, 0, 0)),
            pl.BlockSpec((ncand, tm), full2),
            pl.BlockSpec((ncand, tm), full2),
        ],
        out_specs=[
            pl.BlockSpec((tm, D), row),
            pl.BlockSpec((tm, D), row),
            pl.BlockSpec((tm, npk), row),
            pl.BlockSpec((tm, npk), row),
        ],
        out_shape=[
            jax.ShapeDtypeStruct((T, D), F32),
            jax.ShapeDtypeStruct((T, D), F32),
            jax.ShapeDtypeStruct((T, npk), jnp.int32),
            jax.ShapeDtypeStruct((T, npk), F32),
        ],
        scratch_shapes=[
            pltpu.VMEM((tm, nq), BF16),
            pltpu.VMEM((2 * PEER_HEADS, K, tm), F32),
            pltpu.VMEM((2 * PEER_HEADS, K, tm), jnp.int32),
            pltpu.VMEM((npk, tm), F32),
            pltpu.VMEM((npk, tm), F32),
        ],
        compiler_params=pltpu.CompilerParams(
            dimension_semantics=("parallel",), vmem_limit_bytes=VMEM_LIMIT),
        name="route",
    )(x2, oh2, od2, w_out, w_out, g2, w_q, sub_keys, cpos, cmask)


def _gelu(x):
    return 0.5 * x * (1.0 + lax.erf(x * (1.0 / math.sqrt(2.0))))


def _sc_mesh():
    return plsc.VectorSubcoreMesh(core_axis_name="c", subcore_axis_name="s")


def _sc_worker_id():
    return lax.axis_index("s") * SC_CORES + lax.axis_index("c")


def _sc_gather_loop(tab_hbm, idx_v, bufs, sems, n_items, compute):
    def start(item, b):
        pltpu.async_copy(tab_hbm.at[idx_v.at[item]], bufs[b], sems[b])

    def wait(b):
        pltpu.make_async_copy(tab_hbm.at[idx_v.at[0]], bufs[b], sems[b]).wait()

    start(0, 0)

    @pl.loop(0, n_items // 2)
    def _(i2):
        it = 2 * i2
        start(it + 1, 1)
        wait(0)
        compute(it, bufs[0])

        @pl.when(it + 2 < n_items)
        def _():
            start(it + 2, 0)

        wait(1)
        compute(it + 1, bufs[1])


def _peer_hidden(u, idx4, x):
    T, D = x.shape
    n_rows, R = idx4.shape
    ipt = n_rows // T
    G = SC_TOKEN_GROUP
    nit = G * ipt
    tpw = T // SC_WORKERS
    NL = SC_LANES
    RB = 8

    @functools.partial(
        pl.kernel, mesh=_sc_mesh(), compiler_params=pltpu.CompilerParams(needs_layout_passes=False),
        out_type=jax.ShapeDtypeStruct((n_rows, R), F32),
        scratch_types=[pltpu.VMEM((nit, R), jnp.int32), pltpu.VMEM((G, D), F32),
                       pltpu.VMEM((R, D), F32), pltpu.VMEM((R, D), F32),
                       pltpu.VMEM((nit, R), F32),
                       pltpu.SemaphoreType.DMA, pltpu.SemaphoreType.DMA],
        name="peer_hidden")
    def k(u_hbm, idx_hbm, x_hbm, h_hbm, idx_v, x_v, buf0, buf1, h_v, sem0, sem1):
        tok0 = _sc_worker_id() * tpw
        lane = lax.broadcasted_iota(jnp.int32, (NL,), 0)
        zero = jnp.zeros((NL,), F32)

        def compute(item, buf):
            tok = item // ipt
            hvs = [zero for _ in range(R // NL)]
            for rb in range(R // RB):
                def body(i, accs):
                    col = pl.ds(pl.multiple_of(i * NL, NL), NL)
                    xv = x_v[tok, col]
                    return tuple(accs[j] + xv * buf[rb * RB + j, col] for j in range(RB))
                accs = lax.fori_loop(0, D // NL, body, tuple(zero for _ in range(RB)))
                for j in range(RB):
                    r = rb * RB + j
                    hvs[r // NL] = jnp.where(lane == (r % NL), jnp.sum(accs[j]), hvs[r // NL])
            for q in range(R // NL):
                h_v[item, pl.ds(q * NL, NL)] = hvs[q]

        @pl.loop(0, tpw // G)
        def _(g):
            t0 = tok0 + g * G
            pltpu.sync_copy(idx_hbm.at[pl.ds(t0 * ipt, nit)], idx_v)
            pltpu.sync_copy(x_hbm.at[pl.ds(t0, G)], x_v)
            _sc_gather_loop(u_hbm, idx_v, (buf0, buf1), (sem0, sem1), nit, compute)
            pltpu.sync_copy(h_v, h_hbm.at[pl.ds(t0 * ipt, nit)])

    return k(u, idx4, x)


def _peer_combine(v, idx4, w4, T):
    D = v.shape[1]
    n_rows, R = idx4.shape
    ipt = n_rows // T
    G = SC_TOKEN_GROUP
    nit = G * ipt
    tpw = T // SC_WORKERS
    NL = SC_LANES
    half = D // 2
    nv = half // NL

    @functools.partial(
        pl.kernel, mesh=_sc_mesh(), compiler_params=pltpu.CompilerParams(needs_layout_passes=False),
        out_type=jax.ShapeDtypeStruct((T, D), F32),
        scratch_types=[pltpu.VMEM((nit, R), jnp.int32), pltpu.VMEM((nit, R), F32),
                       pltpu.VMEM((R, D), F32), pltpu.VMEM((R, D), F32),
                       pltpu.VMEM((G, D), F32),
                       pltpu.SemaphoreType.DMA, pltpu.SemaphoreType.DMA],
        name="peer_combine")
    def k(v_hbm, idx_hbm, w_hbm, o_hbm, idx_v, w_v, buf0, buf1, out_v, sem0, sem1):
        tok0 = _sc_worker_id() * tpw
        zero = jnp.zeros((NL,), F32)

        def compute(item, buf):
            tok = item // ipt
            item_vec = jnp.full((NL,), item, jnp.int32)
            for hf in range(2):
                def body(r, accs):
                    wr = plsc.load_gather(w_v, [item_vec, jnp.full((NL,), r, jnp.int32)])
                    return tuple(accs[i] + wr * buf[r, pl.ds(hf * half + i * NL, NL)] for i in range(nv))
                accs = lax.fori_loop(0, R, body, tuple(zero for _ in range(nv)))
                for i in range(nv):
                    plsc.addupdate(out_v.at[tok, pl.ds(hf * half + i * NL, NL)], accs[i])

        @pl.loop(0, tpw // G)
        def _(g):
            t0 = tok0 + g * G
            pltpu.sync_copy(idx_hbm.at[pl.ds(t0 * ipt, nit)], idx_v)
            pltpu.sync_copy(w_hbm.at[pl.ds(t0 * ipt, nit)], w_v)

            @pl.loop(0, G)
            def _(t):
                @pl.loop(0, D // NL)
                def _(i):
                    out_v[t, pl.ds(pl.multiple_of(i * NL, NL), NL)] = zero

            _sc_gather_loop(v_hbm, idx_v, (buf0, buf1), (sem0, sem1), nit, compute)
            pltpu.sync_copy(out_v, o_hbm.at[pl.ds(t0, G)])

    return k(v, idx4, w4)


def _act_kernel(h_ref, g_ref, w_ref):
    w_ref[...] = _gelu(h_ref[...]) * g_ref[...]


def _peer_act(hraw, gate):
    T, n = gate.shape
    tm = PEER_ACT_TM
    spec = pl.BlockSpec((tm, n), lambda i: (i, 0))
    return pl.pallas_call(
        _act_kernel, grid=(T // tm,), in_specs=[spec, spec], out_specs=spec,
        out_shape=jax.ShapeDtypeStruct((T, n), F32),
        compiler_params=pltpu.CompilerParams(dimension_semantics=("parallel",)),
        name="peer_act",
    )(hraw, gate)


def _final_kernel(h_ref, p_ref, g_ref, y_ref):
    y = h_ref[...] + p_ref[...]
    ms = jnp.mean(y * y, axis=-1, keepdims=True)
    y_ref[...] = y * lax.rsqrt(ms + EPS) * g_ref[...]


def _final(h, po, g):
    T, D = h.shape
    tm = FINAL_TM
    spec = pl.BlockSpec((tm, D), lambda i: (i, 0))
    return pl.pallas_call(
        _final_kernel, grid=(T // tm,),
        in_specs=[spec, spec, pl.BlockSpec((1, D), lambda i: (0, 0))], out_specs=spec,
        out_shape=jax.ShapeDtypeStruct((T, D), F32),
        compiler_params=pltpu.CompilerParams(dimension_semantics=("parallel",)),
        name="final_norm",
    )(h, po, g)


def kernel(x, norm1_g, w_in, hgrn_lb_fwd, hgrn_lb_bwd, hgrn_out_g, diff_lam_q1, diff_lam_k1,
           diff_lam_q2, diff_lam_k2, diff_out_g, rel_bias, w_out, norm2_g, peer_w_q,
           peer_sub_keys, peer_u, peer_v, final_g):
    B, L, D = x.shape
    T = B * L
    hw = HGRN_HEADS * HGRN_DK
    x2 = x.reshape(T, D)

    w = w_in[LAYER]
    scale = DIFF_HALF ** -0.5
    cols = lambda j: w[:, j * hw:(j + 1) * hw]
    w_r = jnp.concatenate([cols(1), cols(2), cols(0), cols(3), cols(4), cols(5) * scale, cols(6), cols(7)],
                          axis=1).astype(BF16)

    zf, pb = _inproj(x2, norm1_g[LAYER][None, :], w_r)
    zf3 = zf.reshape(B, L, -1)
    pb3 = pb.reshape(B, L, -1)

    o_h = _hgrn(zf3, pb3, hgrn_lb_fwd, hgrn_lb_bwd, hgrn_out_g[LAYER][None, :])

    f32 = jnp.float32
    lam_init = 0.8 - 0.6 * math.exp(-0.3 * LAYER)
    lam = (jnp.exp(jnp.sum(diff_lam_q1[LAYER].astype(f32) * diff_lam_k1[LAYER].astype(f32)))
           - jnp.exp(jnp.sum(diff_lam_q2[LAYER].astype(f32) * diff_lam_k2[LAYER].astype(f32))) + lam_init)
    o_d = _attn(pb3, _rel_bias_ext(rel_bias, L), lam.reshape(1, 1), diff_out_g[LAYER][None, :])

    sk = peer_sub_keys[LAYER].reshape(2 * PEER_HEADS, PEER_NKEYS, PEER_DKEY).astype(BF16)
    wq = peer_w_q[LAYER].reshape(D, -1).astype(BF16)
    h, hn, idx, gate = _route(x2, o_h.reshape(T, -1), o_d.reshape(T, -1), w_out[LAYER].astype(BF16),
                              norm2_g[LAYER][None, :], wq, sk)

    rows = T * PEER_HEADS * PEER_TOPK // SC_GATHER_ROWS
    idx4 = idx.reshape(rows, SC_GATHER_ROWS)
    hraw = _peer_hidden(peer_u[LAYER], idx4, hn)
    wts = _peer_act(hraw.reshape(T, -1), gate)
    po = _peer_combine(peer_v[LAYER], idx4, wts.reshape(rows, SC_GATHER_ROWS), T)
    y = _final(h, po, final_g[None, :])
    return y.reshape(B, L, D)
```

```python
import functools
import math

import numpy as np
import jax
import jax.numpy as jnp
from jax import lax
from jax.experimental import pallas as pl
from jax.experimental.pallas import tpu as pltpu
from jax.experimental.pallas import tpu_sc as plsc

F32 = jnp.float32
BF16 = jnp.bfloat16
EPS = 1e-6

HGRN_HEADS = 4
HGRN_DK = 128
DIFF_HEADS = 4
DIFF_HALF = 64
REL_BUCKETS = 32
REL_MAX_DIST = 128
PEER_HEADS = 8
PEER_NKEYS = 128
PEER_DKEY = 128
PEER_TOPK = 16
LAYER = 0

LANES = 128
SUBLANES = 8
VMEM_LIMIT = 48 * 1024 * 1024

INPROJ_TM = 512
HG_C = 64
HG_LEVELS = (64, 32, 16, 8, 4, 2)
ATTN_TQ = 256
ROUTE_TM = 256
PEER_ACT_TM = 2048
FINAL_TM = 512

SC_CORES = 2
SC_SUBCORES = 16
SC_LANES = 16
SC_WORKERS = SC_CORES * SC_SUBCORES
SC_GATHER_ROWS = 32
SC_TOKEN_GROUP = 8


def _dot(a, b):
    return jnp.dot(a, b, preferred_element_type=F32)


def _dot_nt(a, b):
    return lax.dot_general(a, b, (((1,), (1,)), ((), ())), preferred_element_type=F32)


def _dot_tn(a, b):
    return lax.dot_general(a, b, (((0,), (0,)), ((), ())), preferred_element_type=F32)


def _silu(x):
    return x * (1.0 / (1.0 + jnp.exp(-x)))


def _inproj_kernel(x_ref, g_ref, w_ref, zf_ref, pb_ref):
    x = x_ref[...]
    ms = jnp.mean(x * x, axis=-1, keepdims=True)
    xn = (x * lax.rsqrt(ms + EPS) * g_ref[...]).astype(BF16)
    nz = zf_ref.shape[1]
    zf_ref[...] = _dot(xn, w_ref[:, 0:nz])
    nb = pb_ref.shape[1]
    step = 1024
    for j in range(nb // step):
        pb_ref[:, j * step:(j + 1) * step] = _dot(
            xn, w_ref[:, nz + j * step: nz + (j + 1) * step]).astype(BF16)


def _inproj(x2, g, w):
    T, D = x2.shape
    N = w.shape[1]
    nz = 2 * HGRN_HEADS * HGRN_DK
    tm = INPROJ_TM
    return pl.pallas_call(
        _inproj_kernel,
        grid=(T // tm,),
        in_specs=[
            pl.BlockSpec((tm, D), lambda i: (i, 0)),
            pl.BlockSpec((1, D), lambda i: (0, 0)),
            pl.BlockSpec((D, N), lambda i: (0, 0)),
        ],
        out_specs=[
            pl.BlockSpec((tm, nz), lambda i: (i, 0)),
            pl.BlockSpec((tm, N - nz), lambda i: (i, 0)),
        ],
        out_shape=[
            jax.ShapeDtypeStruct((T, nz), F32),
            jax.ShapeDtypeStruct((T, N - nz), BF16),
        ],
        compiler_params=pltpu.CompilerParams(
            dimension_semantics=("parallel",), vmem_limit_bytes=VMEM_LIMIT),
        name="inproj",
    )(x2, g, w)


def _hgrn_consts():
    C = HG_C
    r = np.arange(C)
    t = r[:, None]
    u = r[None, :]
    blocks = [u <= t, u > t]
    masks = [np.eye(C, dtype=bool)]
    for B in HG_LEVELS:
        half = B // 2
        a = (r // B) * B
        m = (a + half - 1)[:, None]
        upper = (r - a) >= half
        blocks.append(np.where(upper[:, None], (u > m) & (u <= t), (u > t) & (u <= m)))
        same = a[:, None] == a[None, :]
        masks.append(same & upper[:, None] & (~upper)[None, :])
    m_f = np.concatenate(blocks, 0).astype(np.float32)
    m_b = np.concatenate([b[::-1, ::-1] for b in blocks], 0).astype(np.float32)
    k_f = np.stack(masks).astype(np.float32)
    k_b = np.ascontiguousarray(k_f.transpose(0, 2, 1))
    return m_f, m_b, k_f, k_b


def _hgrn_gates(z, tab):
    tabf = tab.astype(F32)
    e = jnp.exp(tabf - jnp.max(tabf, axis=0, keepdims=True))
    lb = jnp.sum(e[0:LAYER + 1], axis=0, keepdims=True) / jnp.sum(e, axis=0, keepdims=True)
    log_lb = jnp.log(lb)
    log_1m = jnp.log1p(-lb)
    ez = jnp.exp(-jnp.abs(z))
    l1p = jnp.log1p(ez)
    log_sig = jnp.minimum(z, 0.0) - l1p
    c = log_1m + log_sig
    hi = jnp.maximum(log_lb, c)
    lo = jnp.minimum(log_lb, c)
    log_f = hi + jnp.log1p(jnp.exp(lo - hi))
    sig_neg = jnp.where(z >= 0.0, ez, 1.0) / (1.0 + ez)
    k = (1.0 - lb) * sig_neg
    return log_f, k


def _hgrn_kernel(lbf_ref, lbb_ref, og_ref, mf_ref, mb_ref, kf_ref, kb_ref,
                 q_ref, zf_ref, zb_ref, v_ref, g_ref, o_ref, of_s):
    C = HG_C
    L = q_ref.shape[0]
    n = L // C
    dv = v_ref.shape[1]

    def chunk(c, st, m_ref, k_ref, z_ref, tab_ref, forward):
        sl = pl.ds(pl.multiple_of(c * C, C), C)
        qh = q_ref[sl, :].astype(F32)
        q = _silu(qh)
        v = v_ref[sl, :]
        log_f, k = _hgrn_gates(z_ref[sl, :], tab_ref[...])
        lf_hi = log_f.astype(BF16)
        lf_lo = (log_f - lf_hi.astype(F32)).astype(BF16)
        m = m_ref[...]
        e = jnp.exp(_dot(m, lf_hi) + _dot(m, lf_lo))
        e_b = e[0:C]
        e_s = e[C:2 * C]
        dec = e[C - 1:C] if forward else e[0:1]
        o = _dot_nt((q * e_b).astype(BF16), st.astype(BF16))
        a = _dot_nt(q.astype(BF16), k.astype(BF16)) * k_ref[0]
        for l in range(len(HG_LEVELS)):
            e_l = e[(2 + l) * C:(3 + l) * C]
            a = a + _dot_nt((q * e_l).astype(BF16), (k * e_l).astype(BF16)) * k_ref[l + 1]
        o = o + _dot(a.astype(BF16), v)
        st = st * dec + _dot_tn(v, (k * e_s).astype(BF16))
        return sl, o, st

    st0 = jnp.zeros((dv, q_ref.shape[1]), F32)

    def fwd(c, st):
        sl, o, st = chunk(c, st, mf_ref, kf_ref, zf_ref, lbf_ref, True)
        of_s[sl, :] = o
        return st

    lax.fori_loop(0, n, fwd, st0)

    def bwd(i, st):
        c = n - 1 - i
        sl, o, st = chunk(c, st, mb_ref, kb_ref, zb_ref, lbb_ref, False)
        tot = of_s[sl, :] + o
        ms = jnp.mean(tot * tot, axis=-1, keepdims=True)
        y = tot * lax.rsqrt(ms + EPS) * og_ref[...]
        o_ref[sl, :] = (y * _silu(g_ref[sl, :].astype(F32))).astype(o_ref.dtype)
        return st

    lax.fori_loop(0, n, bwd, st0)


def _hgrn(zf3, pb3, lb_f, lb_b, out_g):
    B, L, _ = zf3.shape
    H, dk = HGRN_HEADS, HGRN_DK
    m_f, m_b, k_f, k_b = _hgrn_consts()
    nlev = k_f.shape[0]
    full2 = lambda b, h: (0, 0)
    full3 = lambda b, h: (0, 0, 0)
    seq = lambda off: pl.BlockSpec((None, L, dk), lambda b, h: (b, 0, off + h))
    return pl.pallas_call(
        _hgrn_kernel,
        grid=(B, H),
        in_specs=[
            pl.BlockSpec((lb_f.shape[0], dk), lambda b, h: (0, h)),
            pl.BlockSpec((lb_b.shape[0], dk), lambda b, h: (0, h)),
            pl.BlockSpec((1, dk), full2),
            pl.BlockSpec(m_f.shape, full2),
            pl.BlockSpec(m_b.shape, full2),
            pl.BlockSpec((nlev, HG_C, HG_C), full3),
            pl.BlockSpec((nlev, HG_C, HG_C), full3),
            seq(0),
            seq(0),
            seq(H),
            seq(H),
            seq(2 * H),
        ],
        out_specs=pl.BlockSpec((None, L, dk), lambda b, h: (b, 0, h)),
        out_shape=jax.ShapeDtypeStruct((B, L, H * dk), BF16),
        scratch_shapes=[pltpu.VMEM((L, dk), F32)],
        compiler_params=pltpu.CompilerParams(
            dimension_semantics=("parallel", "parallel"), vmem_limit_bytes=VMEM_LIMIT),
        name="hgrn",
    )(lb_f, lb_b, out_g, jnp.asarray(m_f, BF16), jnp.asarray(m_b, BF16),
      jnp.asarray(k_f), jnp.asarray(k_b), pb3, zf3, zf3, pb3, pb3)


def _t5_bucket(rel):
    nb = REL_BUCKETS // 2
    ret = jnp.where(rel > 0, nb, 0)
    n = jnp.abs(rel)
    max_exact = nb // 2
    nf = jnp.maximum(n, 1).astype(jnp.float32)
    large = max_exact + (jnp.log(nf / max_exact) / math.log(REL_MAX_DIST / max_exact)
                         * (nb - max_exact)).astype(jnp.int32)
    large = jnp.minimum(large, nb - 1)
    return ret + jnp.where(n < max_exact, n, large)


def _rel_bias_ext(rel_bias, L):
    j = jnp.arange(2 * L - LANES, dtype=jnp.int32)
    ql = jnp.arange(LANES, dtype=jnp.int32)
    rel = j[None, :] - (L - LANES) - ql[:, None]
    bucket = _t5_bucket(rel)
    tab = rel_bias.astype(F32)
    out = jnp.zeros((tab.shape[1],) + bucket.shape, F32)
    for b in range(REL_BUCKETS):
        out = jnp.where((bucket == b)[None], tab[b][:, None, None], out)
    return out


def _attn_kernel(lam_ref, q_ref, k_ref, v_ref, bias_ref, og_ref, o_ref, *, lam_init):
    tq = q_ref.shape[0]
    L = k_ref.shape[0]
    qi = pl.program_id(2)
    lam = lam_ref[0, 0]
    q = q_ref[...]
    k = k_ref[...]
    lane = lax.broadcasted_iota(jnp.int32, q.shape, 1)
    zero = jnp.zeros_like(q)
    q0 = jnp.where(lane < DIFF_HALF, q, zero)
    q1 = jnp.where(lane >= DIFF_HALF, q, zero)
    parts = []
    for j in range(tq // LANES):
        off = pl.multiple_of(L - LANES - (qi * tq + j * LANES), LANES)
        parts.append(bias_ref[:, pl.ds(off, L)])
    bias = jnp.concatenate(parts, axis=0) if len(parts) > 1 else parts[0]

    def soft(qm):
        s = _dot_nt(qm, k) + bias
        e = jnp.exp(s - jnp.max(s, axis=-1, keepdims=True))
        return e, 1.0 / jnp.sum(e, axis=-1, keepdims=True)

    e0, r0 = soft(q0)
    e1, r1 = soft(q1)
    w = e0 * r0 - e1 * (lam * r1)
    o = _dot(w.astype(BF16), v_ref[...])
    ms = jnp.mean(o * o, axis=-1, keepdims=True)
    y = o * lax.rsqrt(ms + EPS) * og_ref[...] * (1.0 - lam_init)
    o_ref[...] = y.astype(o_ref.dtype)


def _attn(pb3, bias_ext, lam, out_g):
    B, L, _ = pb3.shape
    H, dh = DIFF_HEADS, 2 * DIFF_HALF
    tq = ATTN_TQ
    base = (3 * HGRN_HEADS * HGRN_DK) // dh
    lam_init = 0.8 - 0.6 * math.exp(-0.3 * LAYER)
    return pl.pallas_call(
        functools.partial(_attn_kernel, lam_init=lam_init),
        grid=(B, H, L // tq),
        in_specs=[
            pl.BlockSpec(memory_space=pltpu.SMEM),
            pl.BlockSpec((None, tq, dh), lambda b, h, i: (b, i, base + h)),
            pl.BlockSpec((None, L, dh), lambda b, h, i: (b, 0, base + H + h)),
            pl.BlockSpec((None, L, dh), lambda b, h, i: (b, 0, base + 2 * H + h)),
            pl.BlockSpec((None, LANES, 2 * L - LANES), lambda b, h, i: (h, 0, 0)),
            pl.BlockSpec((1, dh), lambda b, h, i: (0, 0)),
        ],
        out_specs=pl.BlockSpec((None, tq, dh), lambda b, h, i: (b, i, h)),
        out_shape=jax.ShapeDtypeStruct((B, L, H * dh), BF16),
        compiler_params=pltpu.CompilerParams(
            dimension_semantics=("parallel", "parallel", "parallel"),
            vmem_limit_bytes=VMEM_LIMIT),
        name="attn",
    )(lam, pb3, pb3, pb3, bias_ext, out_g)


def _cand_layout():
    K = PEER_TOPK
    groups = [("a", 0, 0), ("a", 0, 8), ("a", 1, 0), ("a", 2, 0), ("a", 3, 0),
              ("b", 0, 8), ("b", 0, 0), ("b", 1, 0), ("b", 2, 0)]
    seen = set()
    pos, valid = [], []
    for kind, fixed, start in groups:
        for r in range(SUBLANES):
            a, b = (fixed, start + r) if kind == "a" else (start + r, fixed)
            ok = (a + 1) * (b + 1) <= K and (a, b) not in seen
            if ok:
                seen.add((a, b))
            pos.append(a * K + b if ok else K * K + len(pos))
            valid.append(ok)
    assert len(seen) == sum(K // (a + 1) for a in range(K))
    return groups, np.array(pos, np.int32), np.array(valid, bool)


def _extract_topk(s, key, payload, k, big):
    rows = lax.broadcasted_iota(jnp.int32, (k, s.shape[1]), 0)

    def body(j, carry):
        s, vals, pay = carry
        m = jnp.max(s, axis=0, keepdims=True)
        kk = jnp.min(jnp.where(s == m, key, big), axis=0, keepdims=True)
        sel = key == kk
        p = jnp.sum(jnp.where(sel, payload, 0), axis=0, keepdims=True)
        vals = jnp.where(rows == j, m, vals)
        pay = jnp.where(rows == j, p, pay)
        s = jnp.where(sel, -jnp.inf, s)
        return s, vals, pay

    init = (s, jnp.zeros((k, s.shape[1]), F32), jnp.zeros((k, s.shape[1]), jnp.int32))
    _, vals, pay = lax.fori_loop(0, k, body, init)
    return vals, pay


def _route_kernel(x_ref, oh_ref, od_ref, woh_ref, wod_ref, g2_ref, wq_ref, sk_ref, cpos_ref, cmask_ref,
                  h_ref, hn_ref, idx_ref, gate_ref, q_s, tv_s, ti_s, gs_s, is_s, *, groups):
    K = PEER_TOPK
    tm = x_ref.shape[0]
    h = x_ref[...] + _dot(oh_ref[...], woh_ref[...]) + _dot(od_ref[...], wod_ref[...])
    h_ref[...] = h
    hn = h * lax.rsqrt(jnp.mean(h * h, axis=-1, keepdims=True) + EPS) * g2_ref[...]
    hn_ref[...] = hn
    q_s[...] = _dot(hn.astype(BF16), wq_ref[...]).astype(BF16)

    key_iota = lax.broadcasted_iota(jnp.int32, (PEER_NKEYS, tm), 0)

    def half_topk(hp, _):
        col = pl.multiple_of(hp * PEER_DKEY, PEER_DKEY)
        s = _dot_nt(sk_ref[hp], q_s[:, pl.ds(col, PEER_DKEY)])
        vals, idxs = _extract_topk(s, key_iota, key_iota, K, PEER_NKEYS)
        tv_s[hp] = vals
        ti_s[hp] = idxs
        return 0

    lax.fori_loop(0, 2 * PEER_HEADS, half_topk, 0)

    cpos = cpos_ref[...]
    cmask = cmask_ref[...]

    def head(hd, _):
        s0, s1 = tv_s[2 * hd], tv_s[2 * hd + 1]
        i0, i1 = ti_s[2 * hd] * PEER_NKEYS, ti_s[2 * hd + 1]
        cs, ci = [], []
        for kind, fixed, start in groups:
            if kind == "a":
                cs.append(s0[fixed:fixed + 1] + s1[start:start + SUBLANES])
                ci.append(i0[fixed:fixed + 1] + i1[start:start + SUBLANES])
            else:
                cs.append(s0[start:start + SUBLANES] + s1[fixed:fixed + 1])
                ci.append(i0[start:start + SUBLANES] + i1[fixed:fixed + 1])
        cand = jnp.concatenate(cs, axis=0) + cmask
        cidx = jnp.concatenate(ci, axis=0)
        best, eidx = _extract_topk(cand, cpos, cidx, K, 2 * K * K)
        ex = jnp.exp(best - best[0:1])
        gate = ex / jnp.sum(ex, axis=0, keepdims=True)
        row = pl.ds(pl.multiple_of(hd * K, K), K)
        gs_s[row, :] = gate
        is_s[row, :] = eidx.astype(F32)
        return 0

    lax.fori_loop(0, PEER_HEADS, head, 0)
    gate_ref[...] = gs_s[...].T
    idx_ref[...] = is_s[...].T.astype(jnp.int32)


def _route(x2, oh2, od2, w_out, g2, w_q, sub_keys):
    T, D = x2.shape
    tm = ROUTE_TM
    K = PEER_TOPK
    nh = oh2.shape[1]
    nq = w_q.shape[1]
    npk = PEER_HEADS * K
    groups, pos, valid = _cand_layout()
    ncand = pos.shape[0]
    cpos = jnp.asarray(np.broadcast_to(pos[:, None], (ncand, tm)))
    cmask = jnp.asarray(np.broadcast_to(np.where(valid, 0.0, -np.inf).astype(np.float32)[:, None], (ncand, tm)))
    row = lambda i: (i, 0)
    full2 = lambda i: (0, 0)
    return pl.pallas_call(
        functools.partial(_route_kernel, groups=groups),
        grid=(T // tm,),
        in_specs=[
            pl.BlockSpec((tm, D), row),
            pl.BlockSpec((tm, nh), row),
            pl.BlockSpec((tm, nh), row),
            pl.BlockSpec((nh, D), full2),
            pl.BlockSpec((nh, D), lambda i: (1, 0)),
            pl.BlockSpec((1, D), full2),
            pl.BlockSpec((D, nq), full2),
            pl.BlockSpec(sub_keys.shape, lambda i: (0, 0, 0)),
            pl.BlockSpec((ncand, tm), full2),
            pl.BlockSpec((ncand, tm), full2),
        ],
        out_specs=[
            pl.BlockSpec((tm, D), row),
            pl.BlockSpec((tm, D), row),
            pl.BlockSpec((tm, npk), row),
            pl.BlockSpec((tm, npk), row),
        ],
        out_shape=[
            jax.ShapeDtypeStruct((T, D), F32),
            jax.ShapeDtypeStruct((T, D), F32),
            jax.ShapeDtypeStruct((T, npk), jnp.int32),
            jax.ShapeDtypeStruct((T, npk), F32),
        ],
        scratch_shapes=[
            pltpu.VMEM((tm, nq), BF16),
            pltpu.VMEM((2 * PEER_HEADS, K, tm), F32),
            pltpu.VMEM((2 * PEER_HEADS, K, tm), jnp.int32),
            pltpu.VMEM((npk, tm), F32),
            pltpu.VMEM((npk, tm), F32),
        ],
        compiler_params=pltpu.CompilerParams(
            dimension_semantics=("parallel",), vmem_limit_bytes=VMEM_LIMIT),
        name="route",
    )(x2, oh2, od2, w_out, w_out, g2, w_q, sub_keys, cpos, cmask)


def _gelu(x):
    return 0.5 * x * (1.0 + lax.erf(x * (1.0 / math.sqrt(2.0))))


def _sc_mesh():
    return plsc.VectorSubcoreMesh(core_axis_name="c", subcore_axis_name="s")


def _sc_worker_id():
    return lax.axis_index("s") * SC_CORES + lax.axis_index("c")


def _sc_gather_loop(tab_hbm, idx_v, bufs, sems, n_items, compute):
    def start(item, b):
        pltpu.async_copy(tab_hbm.at[idx_v.at[item]], bufs[b], sems[b])

    def wait(b):
        pltpu.make_async_copy(tab_hbm.at[idx_v.at[0]], bufs[b], sems[b]).wait()

    start(0, 0)

    @pl.loop(0, n_items // 2)
    def _(i2):
        it = 2 * i2
        start(it + 1, 1)
        wait(0)
        compute(it, bufs[0])

        @pl.when(it + 2 < n_items)
        def _():
            start(it + 2, 0)

        wait(1)
        compute(it + 1, bufs[1])


def _peer_hidden(u, idx4, x):
    T, D = x.shape
    n_rows, R = idx4.shape
    ipt = n_rows // T
    G = SC_TOKEN_GROUP
    nit = G * ipt
    tpw = T // SC_WORKERS
    NL = SC_LANES
    RB = 8
    VPR = LANES // NL

    @functools.partial(
        pl.kernel, mesh=_sc_mesh(), compiler_params=pltpu.CompilerParams(needs_layout_passes=False),
        out_type=jax.ShapeDtypeStruct((n_rows, R), F32),
        scratch_types=[pltpu.VMEM((nit, R), jnp.int32), pltpu.VMEM((G, D), F32),
                       pltpu.VMEM((R, SUBLANES, LANES), F32), pltpu.VMEM((R, SUBLANES, LANES), F32),
                       pltpu.VMEM((nit, R), F32),
                       pltpu.SemaphoreType.DMA, pltpu.SemaphoreType.DMA],
        name="peer_hidden")
    def k(u_hbm, idx_hbm, x_hbm, h_hbm, idx_v, x_v, buf0, buf1, h_v, sem0, sem1):
        tok0 = _sc_worker_id() * tpw
        lane = lax.broadcasted_iota(jnp.int32, (NL,), 0)
        zero = jnp.zeros((NL,), F32)

        def compute(item, buf):
            tok = item // ipt
            hvs = [zero for _ in range(R // NL)]
            for rb in range(R // RB):
                def body(i, accs):
                    col = pl.ds(pl.multiple_of(i * NL, NL), NL)
                    xv = x_v[tok, col]
                    sub = i // VPR
                    ln = pl.ds(pl.multiple_of((i % VPR) * NL, NL), NL)
                    return tuple(accs[j] + xv * buf[rb * RB + j, sub, ln] for j in range(RB))
                accs = lax.fori_loop(0, D // NL, body, tuple(zero for _ in range(RB)))
                for j in range(RB):
                    r = rb * RB + j
                    hvs[r // NL] = jnp.where(lane == (r % NL), jnp.sum(accs[j]), hvs[r // NL])
            for q in range(R // NL):
                h_v[item, pl.ds(q * NL, NL)] = hvs[q]

        @pl.loop(0, tpw // G)
        def _(g):
            t0 = tok0 + g * G
            pltpu.sync_copy(idx_hbm.at[pl.ds(t0 * ipt, nit)], idx_v)
            pltpu.sync_copy(x_hbm.at[pl.ds(t0, G)], x_v)
            _sc_gather_loop(u_hbm, idx_v, (buf0, buf1), (sem0, sem1), nit, compute)
            pltpu.sync_copy(h_v, h_hbm.at[pl.ds(t0 * ipt, nit)])

    return k(u, idx4, x)


def _peer_combine(v, idx4, w4, T, D):
    n_rows, R = idx4.shape
    ipt = n_rows // T
    G = SC_TOKEN_GROUP
    nit = G * ipt
    tpw = T // SC_WORKERS
    NL = SC_LANES
    half = D // 2
    nv = half // NL

    @functools.partial(
        pl.kernel, mesh=_sc_mesh(), compiler_params=pltpu.CompilerParams(needs_layout_passes=False),
        out_type=jax.ShapeDtypeStruct((T, D), F32),
        scratch_types=[pltpu.VMEM((nit, R), jnp.int32), pltpu.VMEM((nit, R), F32),
                       pltpu.VMEM((R, SUBLANES, LANES), F32), pltpu.VMEM((R, SUBLANES, LANES), F32),
                       pltpu.VMEM((G, D), F32),
                       pltpu.SemaphoreType.DMA, pltpu.SemaphoreType.DMA],
        name="peer_combine")
    def k(v_hbm, idx_hbm, w_hbm, o_hbm, idx_v, w_v, buf0, buf1, out_v, sem0, sem1):
        tok0 = _sc_worker_id() * tpw
        zero = jnp.zeros((NL,), F32)

        def compute(item, buf):
            tok = item // ipt
            item_vec = jnp.full((NL,), item, jnp.int32)
            for hf in range(2):
                def body(r, accs):
                    wr = plsc.load_gather(w_v, [item_vec, jnp.full((NL,), r, jnp.int32)])
                    return tuple(
                        accs[i] + wr * buf[r, (hf * half + i * NL) // LANES, pl.ds((i * NL) % LANES, NL)]
                        for i in range(nv))
                accs = lax.fori_loop(0, R, body, tuple(zero for _ in range(nv)))
                for i in range(nv):
                    plsc.addupdate(out_v.at[tok, pl.ds(hf * half + i * NL, NL)], accs[i])

        @pl.loop(0, tpw // G)
        def _(g):
            t0 = tok0 + g * G
            pltpu.sync_copy(idx_hbm.at[pl.ds(t0 * ipt, nit)], idx_v)
            pltpu.sync_copy(w_hbm.at[pl.ds(t0 * ipt, nit)], w_v)

            @pl.loop(0, G)
            def _(t):
                @pl.loop(0, D // NL)
                def _(i):
                    out_v[t, pl.ds(pl.multiple_of(i * NL, NL), NL)] = zero

            _sc_gather_loop(v_hbm, idx_v, (buf0, buf1), (sem0, sem1), nit, compute)
            pltpu.sync_copy(out_v, o_hbm.at[pl.ds(t0, G)])

    return k(v, idx4, w4)


def _act_kernel(h_ref, g_ref, w_ref):
    w_ref[...] = _gelu(h_ref[...]) * g_ref[...]


def _peer_act(hraw, gate):
    T, n = gate.shape
    tm = PEER_ACT_TM
    spec = pl.BlockSpec((tm, n), lambda i: (i, 0))
    return pl.pallas_call(
        _act_kernel, grid=(T // tm,), in_specs=[spec, spec], out_specs=spec,
        out_shape=jax.ShapeDtypeStruct((T, n), F32),
        compiler_params=pltpu.CompilerParams(dimension_semantics=("parallel",)),
        name="peer_act",
    )(hraw, gate)


def _final_kernel(h_ref, p_ref, g_ref, y_ref):
    y = h_ref[...] + p_ref[...]
    ms = jnp.mean(y * y, axis=-1, keepdims=True)
    y_ref[...] = y * lax.rsqrt(ms + EPS) * g_ref[...]


def _final(h, po, g):
    T, D = h.shape
    tm = FINAL_TM
    spec = pl.BlockSpec((tm, D), lambda i: (i, 0))
    return pl.pallas_call(
        _final_kernel, grid=(T // tm,),
        in_specs=[spec, spec, pl.BlockSpec((1, D), lambda i: (0, 0))], out_specs=spec,
        out_shape=jax.ShapeDtypeStruct((T, D), F32),
        compiler_params=pltpu.CompilerParams(dimension_semantics=("parallel",)),
        name="final_norm",
    )(h, po, g)


def kernel(x, norm1_g, w_in, hgrn_lb_fwd, hgrn_lb_bwd, hgrn_out_g, diff_lam_q1, diff_lam_k1,
           diff_lam_q2, diff_lam_k2, diff_out_g, rel_bias, w_out, norm2_g, peer_w_q,
           peer_sub_keys, peer_u, peer_v, final_g):
    B, L, D = x.shape
    T = B * L
    hw = HGRN_HEADS * HGRN_DK
    x2 = x.reshape(T, D)

    w = w_in[LAYER]
    scale = DIFF_HALF ** -0.5
    cols = lambda j: w[:, j * hw:(j + 1) * hw]
    w_r = jnp.concatenate([cols(1), cols(2), cols(0), cols(3), cols(4), cols(5) * scale, cols(6), cols(7)],
                          axis=1).astype(BF16)

    zf, pb = _inproj(x2, norm1_g[LAYER][None, :], w_r)
    zf3 = zf.reshape(B, L, -1)
    pb3 = pb.reshape(B, L, -1)

    o_h = _hgrn(zf3, pb3, hgrn_lb_fwd, hgrn_lb_bwd, hgrn_out_g[LAYER][None, :])

    f32 = jnp.float32
    lam_init = 0.8 - 0.6 * math.exp(-0.3 * LAYER)
    lam = (jnp.exp(jnp.sum(diff_lam_q1[LAYER].astype(f32) * diff_lam_k1[LAYER].astype(f32)))
           - jnp.exp(jnp.sum(diff_lam_q2[LAYER].astype(f32) * diff_lam_k2[LAYER].astype(f32))) + lam_init)
    o_d = _attn(pb3, _rel_bias_ext(rel_bias, L), lam.reshape(1, 1), diff_out_g[LAYER][None, :])

    sk = peer_sub_keys[LAYER].reshape(2 * PEER_HEADS, PEER_NKEYS, PEER_DKEY).astype(BF16)
    wq = peer_w_q[LAYER].reshape(D, -1).astype(BF16)
    h, hn, idx, gate = _route(x2, o_h.reshape(T, -1), o_d.reshape(T, -1), w_out[LAYER].astype(BF16),
                              norm2_g[LAYER][None, :], wq, sk)

    rows = T * PEER_HEADS * PEER_TOPK // SC_GATHER_ROWS
    idx4 = idx.reshape(rows, SC_GATHER_ROWS)
    tile3 = lambda tab: tab.reshape(tab.shape[0], D // LANES, LANES)
    hraw = _peer_hidden(tile3(peer_u[LAYER]), idx4, hn)
    wts = _peer_act(hraw.reshape(T, -1), gate)
    po = _peer_combine(tile3(peer_v[LAYER]), idx4, wts.reshape(rows, SC_GATHER_ROWS), T, D)
    y = _final(h, po, final_g[None, :])
    return y.reshape(B, L, D)
```

```python
import functools
import math

import numpy as np
import jax
import jax.numpy as jnp
from jax import lax
from jax.experimental import pallas as pl
from jax.experimental.pallas import tpu as pltpu
from jax.experimental.pallas import tpu_sc as plsc

F32 = jnp.float32
BF16 = jnp.bfloat16
EPS = 1e-6

HGRN_HEADS = 4
HGRN_DK = 128
DIFF_HEADS = 4
DIFF_HALF = 64
REL_BUCKETS = 32
REL_MAX_DIST = 128
PEER_HEADS = 8
PEER_NKEYS = 128
PEER_DKEY = 128
PEER_TOPK = 16
LAYER = 0

LANES = 128
SUBLANES = 8
VMEM_LIMIT = 48 * 1024 * 1024

INPROJ_TM = 512
HG_C = 64
HG_LEVELS = (64, 32, 16, 8, 4, 2)
ATTN_TQ = 256
ROUTE_TM = 256
BATCH_CHUNKS = 4
PEER_ACT_TM = 2048
FINAL_TM = 512

SC_CORES = 2
SC_SUBCORES = 16
SC_LANES = 16
SC_WORKERS = SC_CORES * SC_SUBCORES
SC_GATHER_ROWS = 32
SC_TOKEN_GROUP = 8


def _dot(a, b):
    return jnp.dot(a, b, preferred_element_type=F32)


def _dot_nt(a, b):
    return lax.dot_general(a, b, (((1,), (1,)), ((), ())), preferred_element_type=F32)


def _dot_tn(a, b):
    return lax.dot_general(a, b, (((0,), (0,)), ((), ())), preferred_element_type=F32)


def _silu(x):
    return x * (1.0 / (1.0 + jnp.exp(-x)))


def _inproj_kernel(x_ref, g_ref, w_ref, zf_ref, pb_ref):
    x = x_ref[...]
    ms = jnp.mean(x * x, axis=-1, keepdims=True)
    xn = (x * lax.rsqrt(ms + EPS) * g_ref[...]).astype(BF16)
    nz = zf_ref.shape[1]
    zf_ref[...] = _dot(xn, w_ref[:, 0:nz])
    nb = pb_ref.shape[1]
    step = 1024
    for j in range(nb // step):
        pb_ref[:, j * step:(j + 1) * step] = _dot(
            xn, w_ref[:, nz + j * step: nz + (j + 1) * step]).astype(BF16)


def _inproj(x2, g, w):
    T, D = x2.shape
    N = w.shape[1]
    nz = 2 * HGRN_HEADS * HGRN_DK
    tm = INPROJ_TM
    return pl.pallas_call(
        _inproj_kernel,
        grid=(T // tm,),
        in_specs=[
            pl.BlockSpec((tm, D), lambda i: (i, 0)),
            pl.BlockSpec((1, D), lambda i: (0, 0)),
            pl.BlockSpec((D, N), lambda i: (0, 0)),
        ],
        out_specs=[
            pl.BlockSpec((tm, nz), lambda i: (i, 0)),
            pl.BlockSpec((tm, N - nz), lambda i: (i, 0)),
        ],
        out_shape=[
            jax.ShapeDtypeStruct((T, nz), F32),
            jax.ShapeDtypeStruct((T, N - nz), BF16),
        ],
        compiler_params=pltpu.CompilerParams(
            dimension_semantics=("parallel",), vmem_limit_bytes=VMEM_LIMIT),
        name="inproj",
    )(x2, g, w)


def _hgrn_consts():
    C = HG_C
    r = np.arange(C)
    t = r[:, None]
    u = r[None, :]
    blocks = [u <= t, u > t]
    masks = [np.eye(C, dtype=bool)]
    for B in HG_LEVELS:
        half = B // 2
        a = (r // B) * B
        m = (a + half - 1)[:, None]
        upper = (r - a) >= half
        blocks.append(np.where(upper[:, None], (u > m) & (u <= t), (u > t) & (u <= m)))
        same = a[:, None] == a[None, :]
        masks.append(same & upper[:, None] & (~upper)[None, :])
    m_f = np.concatenate(blocks, 0).astype(np.float32)
    m_b = np.concatenate([b[::-1, ::-1] for b in blocks], 0).astype(np.float32)
    k_f = np.stack(masks).astype(np.float32)
    k_b = np.ascontiguousarray(k_f.transpose(0, 2, 1))
    return m_f, m_b, k_f, k_b


def _hgrn_gates(z, tab):
    tabf = tab.astype(F32)
    e = jnp.exp(tabf - jnp.max(tabf, axis=0, keepdims=True))
    lb = jnp.sum(e[0:LAYER + 1], axis=0, keepdims=True) / jnp.sum(e, axis=0, keepdims=True)
    log_lb = jnp.log(lb)
    log_1m = jnp.log1p(-lb)
    ez = jnp.exp(-jnp.abs(z))
    l1p = jnp.log1p(ez)
    log_sig = jnp.minimum(z, 0.0) - l1p
    c = log_1m + log_sig
    hi = jnp.maximum(log_lb, c)
    lo = jnp.minimum(log_lb, c)
    log_f = hi + jnp.log1p(jnp.exp(lo - hi))
    sig_neg = jnp.where(z >= 0.0, ez, 1.0) / (1.0 + ez)
    k = (1.0 - lb) * sig_neg
    return log_f, k


def _hgrn_kernel(lbf_ref, lbb_ref, og_ref, mf_ref, mb_ref, kf_ref, kb_ref,
                 q_ref, zf_ref, zb_ref, v_ref, g_ref, o_ref, of_s):
    C = HG_C
    L = q_ref.shape[0]
    n = L // C
    dv = v_ref.shape[1]

    def chunk(c, st, m_ref, k_ref, z_ref, tab_ref, forward):
        sl = pl.ds(pl.multiple_of(c * C, C), C)
        qh = q_ref[sl, :].astype(F32)
        q = _silu(qh)
        v = v_ref[sl, :]
        log_f, k = _hgrn_gates(z_ref[sl, :], tab_ref[...])
        lf_hi = log_f.astype(BF16)
        lf_lo = (log_f - lf_hi.astype(F32)).astype(BF16)
        m = m_ref[...]
        e = jnp.exp(_dot(m, lf_hi) + _dot(m, lf_lo))
        e_b = e[0:C]
        e_s = e[C:2 * C]
        dec = e[C - 1:C] if forward else e[0:1]
        o = _dot_nt((q * e_b).astype(BF16), st.astype(BF16))
        a = _dot_nt(q.astype(BF16), k.astype(BF16)) * k_ref[0]
        for l in range(len(HG_LEVELS)):
            e_l = e[(2 + l) * C:(3 + l) * C]
            a = a + _dot_nt((q * e_l).astype(BF16), (k * e_l).astype(BF16)) * k_ref[l + 1]
        o = o + _dot(a.astype(BF16), v)
        st = st * dec + _dot_tn(v, (k * e_s).astype(BF16))
        return sl, o, st

    st0 = jnp.zeros((dv, q_ref.shape[1]), F32)

    def fwd(c, st):
        sl, o, st = chunk(c, st, mf_ref, kf_ref, zf_ref, lbf_ref, True)
        of_s[sl, :] = o
        return st

    lax.fori_loop(0, n, fwd, st0)

    def bwd(i, st):
        c = n - 1 - i
        sl, o, st = chunk(c, st, mb_ref, kb_ref, zb_ref, lbb_ref, False)
        tot = of_s[sl, :] + o
        ms = jnp.mean(tot * tot, axis=-1, keepdims=True)
        y = tot * lax.rsqrt(ms + EPS) * og_ref[...]
        o_ref[sl, :] = (y * _silu(g_ref[sl, :].astype(F32))).astype(o_ref.dtype)
        return st

    lax.fori_loop(0, n, bwd, st0)


def _hgrn(zf3, pb3, lb_f, lb_b, out_g):
    B, L, _ = zf3.shape
    H, dk = HGRN_HEADS, HGRN_DK
    m_f, m_b, k_f, k_b = _hgrn_consts()
    nlev = k_f.shape[0]
    full2 = lambda b, h: (0, 0)
    full3 = lambda b, h: (0, 0, 0)
    seq = lambda off: pl.BlockSpec((None, L, dk), lambda b, h: (b, 0, off + h))
    return pl.pallas_call(
        _hgrn_kernel,
        grid=(B, H),
        in_specs=[
            pl.BlockSpec((lb_f.shape[0], dk), lambda b, h: (0, h)),
            pl.BlockSpec((lb_b.shape[0], dk), lambda b, h: (0, h)),
            pl.BlockSpec((1, dk), full2),
            pl.BlockSpec(m_f.shape, full2),
            pl.BlockSpec(m_b.shape, full2),
            pl.BlockSpec((nlev, HG_C, HG_C), full3),
            pl.BlockSpec((nlev, HG_C, HG_C), full3),
            seq(0),
            seq(0),
            seq(H),
            seq(H),
            seq(2 * H),
        ],
        out_specs=pl.BlockSpec((None, L, dk), lambda b, h: (b, 0, h)),
        out_shape=jax.ShapeDtypeStruct((B, L, H * dk), BF16),
        scratch_shapes=[pltpu.VMEM((L, dk), F32)],
        compiler_params=pltpu.CompilerParams(
            dimension_semantics=("parallel", "parallel"), vmem_limit_bytes=VMEM_LIMIT),
        name="hgrn",
    )(lb_f, lb_b, out_g, jnp.asarray(m_f, BF16), jnp.asarray(m_b, BF16),
      jnp.asarray(k_f), jnp.asarray(k_b), pb3, zf3, zf3, pb3, pb3)


def _t5_bucket(rel):
    nb = REL_BUCKETS // 2
    ret = jnp.where(rel > 0, nb, 0)
    n = jnp.abs(rel)
    max_exact = nb // 2
    nf = jnp.maximum(n, 1).astype(jnp.float32)
    large = max_exact + (jnp.log(nf / max_exact) / math.log(REL_MAX_DIST / max_exact)
                         * (nb - max_exact)).astype(jnp.int32)
    large = jnp.minimum(large, nb - 1)
    return ret + jnp.where(n < max_exact, n, large)


def _rel_bias_ext(rel_bias, L):
    j = jnp.arange(2 * L - LANES, dtype=jnp.int32)
    ql = jnp.arange(LANES, dtype=jnp.int32)
    rel = j[None, :] - (L - LANES) - ql[:, None]
    bucket = _t5_bucket(rel)
    tab = rel_bias.astype(F32)
    out = jnp.zeros((tab.shape[1],) + bucket.shape, F32)
    for b in range(REL_BUCKETS):
        out = jnp.where((bucket == b)[None], tab[b][:, None, None], out)
    return out


def _attn_kernel(lam_ref, q_ref, k_ref, v_ref, bias_ref, og_ref, o_ref, *, lam_init):
    tq = q_ref.shape[0]
    L = k_ref.shape[0]
    qi = pl.program_id(2)
    lam = lam_ref[0, 0]
    q = q_ref[...]
    k = k_ref[...]
    lane = lax.broadcasted_iota(jnp.int32, q.shape, 1)
    zero = jnp.zeros_like(q)
    q0 = jnp.where(lane < DIFF_HALF, q, zero)
    q1 = jnp.where(lane >= DIFF_HALF, q, zero)
    parts = []
    for j in range(tq // LANES):
        off = pl.multiple_of(L - LANES - (qi * tq + j * LANES), LANES)
        parts.append(bias_ref[:, pl.ds(off, L)])
    bias = jnp.concatenate(parts, axis=0) if len(parts) > 1 else parts[0]

    def soft(qm):
        s = _dot_nt(qm, k) + bias
        e = jnp.exp(s - jnp.max(s, axis=-1, keepdims=True))
        return e, 1.0 / jnp.sum(e, axis=-1, keepdims=True)

    e0, r0 = soft(q0)
    e1, r1 = soft(q1)
    w = e0 * r0 - e1 * (lam * r1)
    o = _dot(w.astype(BF16), v_ref[...])
    ms = jnp.mean(o * o, axis=-1, keepdims=True)
    y = o * lax.rsqrt(ms + EPS) * og_ref[...] * (1.0 - lam_init)
    o_ref[...] = y.astype(o_ref.dtype)


def _attn(pb3, bias_ext, lam, out_g):
    B, L, _ = pb3.shape
    H, dh = DIFF_HEADS, 2 * DIFF_HALF
    tq = ATTN_TQ
    base = (3 * HGRN_HEADS * HGRN_DK) // dh
    lam_init = 0.8 - 0.6 * math.exp(-0.3 * LAYER)
    return pl.pallas_call(
        functools.partial(_attn_kernel, lam_init=lam_init),
        grid=(B, H, L // tq),
        in_specs=[
            pl.BlockSpec(memory_space=pltpu.SMEM),
            pl.BlockSpec((None, tq, dh), lambda b, h, i: (b, i, base + h)),
            pl.BlockSpec((None, L, dh), lambda b, h, i: (b, 0, base + H + h)),
            pl.BlockSpec((None, L, dh), lambda b, h, i: (b, 0, base + 2 * H + h)),
            pl.BlockSpec((None, LANES, 2 * L - LANES), lambda b, h, i: (h, 0, 0)),
            pl.BlockSpec((1, dh), lambda b, h, i: (0, 0)),
        ],
        out_specs=pl.BlockSpec((None, tq, dh), lambda b, h, i: (b, i, h)),
        out_shape=jax.ShapeDtypeStruct((B, L, H * dh), BF16),
        compiler_params=pltpu.CompilerParams(
            dimension_semantics=("parallel", "parallel", "parallel"),
            vmem_limit_bytes=VMEM_LIMIT),
        name="attn",
    )(lam, pb3, pb3, pb3, bias_ext, out_g)


def _cand_layout():
    K = PEER_TOPK
    groups = [("a", 0, 0), ("a", 0, 8), ("a", 1, 0), ("a", 2, 0), ("a", 3, 0),
              ("b", 0, 8), ("b", 0, 0), ("b", 1, 0), ("b", 2, 0)]
    seen = set()
    pos, valid = [], []
    for kind, fixed, start in groups:
        for r in range(SUBLANES):
            a, b = (fixed, start + r) if kind == "a" else (start + r, fixed)
            ok = (a + 1) * (b + 1) <= K and (a, b) not in seen
            if ok:
                seen.add((a, b))
            pos.append(a * K + b if ok else K * K + len(pos))
            valid.append(ok)
    assert len(seen) == sum(K // (a + 1) for a in range(K))
    return groups, np.array(pos, np.int32), np.array(valid, bool)


def _extract_topk(s, key, payload, k, big):
    rows = lax.broadcasted_iota(jnp.int32, (k, s.shape[1]), 0)

    def body(j, carry):
        s, vals, pay = carry
        m = jnp.max(s, axis=0, keepdims=True)
        kk = jnp.min(jnp.where(s == m, key, big), axis=0, keepdims=True)
        sel = key == kk
        p = jnp.sum(jnp.where(sel, payload, 0), axis=0, keepdims=True)
        vals = jnp.where(rows == j, m, vals)
        pay = jnp.where(rows == j, p, pay)
        s = jnp.where(sel, -jnp.inf, s)
        return s, vals, pay

    init = (s, jnp.zeros((k, s.shape[1]), F32), jnp.zeros((k, s.shape[1]), jnp.int32))
    _, vals, pay = lax.fori_loop(0, k, body, init)
    return vals, pay


def _route_kernel(x_ref, oh_ref, od_ref, woh_ref, wod_ref, g2_ref, wq_ref, sk_ref, cpos_ref, cmask_ref,
                  h_ref, hn_ref, idx_ref, gate_ref, q_s, tv_s, ti_s, gs_s, is_s, *, groups):
    K = PEER_TOPK
    tm = x_ref.shape[0]
    h = x_ref[...] + _dot(oh_ref[...], woh_ref[...]) + _dot(od_ref[...], wod_ref[...])
    h_ref[...] = h
    hn = h * lax.rsqrt(jnp.mean(h * h, axis=-1, keepdims=True) + EPS) * g2_ref[...]
    hn_ref[...] = hn
    q_s[...] = _dot(hn.astype(BF16), wq_ref[...]).astype(BF16)

    key_iota = lax.broadcasted_iota(jnp.int32, (PEER_NKEYS, tm), 0)

    def half_topk(hp, _):
        col = pl.multiple_of(hp * PEER_DKEY, PEER_DKEY)
        s = _dot_nt(sk_ref[hp], q_s[:, pl.ds(col, PEER_DKEY)])
        vals, idxs = _extract_topk(s, key_iota, key_iota, K, PEER_NKEYS)
        tv_s[hp] = vals
        ti_s[hp] = idxs
        return 0

    lax.fori_loop(0, 2 * PEER_HEADS, half_topk, 0)

    cpos = cpos_ref[...]
    cmask = cmask_ref[...]

    def head(hd, _):
        s0, s1 = tv_s[2 * hd], tv_s[2 * hd + 1]
        i0, i1 = ti_s[2 * hd] * PEER_NKEYS, ti_s[2 * hd + 1]
        cs, ci = [], []
        for kind, fixed, start in groups:
            if kind == "a":
                cs.append(s0[fixed:fixed + 1] + s1[start:start + SUBLANES])
                ci.append(i0[fixed:fixed + 1] + i1[start:start + SUBLANES])
            else:
                cs.append(s0[start:start + SUBLANES] + s1[fixed:fixed + 1])
                ci.append(i0[start:start + SUBLANES] + i1[fixed:fixed + 1])
        cand = jnp.concatenate(cs, axis=0) + cmask
        cidx = jnp.concatenate(ci, axis=0)
        best, eidx = _extract_topk(cand, cpos, cidx, K, 2 * K * K)
        ex = jnp.exp(best - best[0:1])
        gate = ex / jnp.sum(ex, axis=0, keepdims=True)
        row = pl.ds(pl.multiple_of(hd * K, K), K)
        gs_s[row, :] = gate
        is_s[row, :] = eidx.astype(F32)
        return 0

    lax.fori_loop(0, PEER_HEADS, head, 0)
    gate_ref[...] = gs_s[...].T
    idx_ref[...] = is_s[...].T.astype(jnp.int32)


def _route(x2, oh2, od2, w_out, g2, w_q, sub_keys):
    T, D = x2.shape
    tm = ROUTE_TM
    K = PEER_TOPK
    nh = oh2.shape[1]
    nq = w_q.shape[1]
    npk = PEER_HEADS * K
    groups, pos, valid = _cand_layout()
    ncand = pos.shape[0]
    cpos = jnp.asarray(np.broadcast_to(pos[:, None], (ncand, tm)))
    cmask = jnp.asarray(np.broadcast_to(np.where(valid, 0.0, -np.inf).astype(np.float32)[:, None], (ncand, tm)))
    row = lambda i: (i, 0)
    full2 = lambda i: (0, 0)
    return pl.pallas_call(
        functools.partial(_route_kernel, groups=groups),
        grid=(T // tm,),
        in_specs=[
            pl.BlockSpec((tm, D), row),
            pl.BlockSpec((tm, nh), row),
            pl.BlockSpec((tm, nh), row),
            pl.BlockSpec((nh, D), full2),
            pl.BlockSpec((nh, D), lambda i: (1, 0)),
            pl.BlockSpec((1, D), full2),
            pl.BlockSpec((D, nq), full2),
            pl.BlockSpec(sub_keys.shape, lambda i: (0, 0, 0)),
            pl.BlockSpec((ncand, tm), full2),
            pl.BlockSpec((ncand, tm), full2),
        ],
        out_specs=[
            pl.BlockSpec((tm, D), row),
            pl.BlockSpec((tm, D), row),
            pl.BlockSpec((tm, npk), row),
            pl.BlockSpec((tm, npk), row),
        ],
        out_shape=[
            jax.ShapeDtypeStruct((T, D), F32),
            jax.ShapeDtypeStruct((T, D), F32),
            jax.ShapeDtypeStruct((T, npk), jnp.int32),
            jax.ShapeDtypeStruct((T, npk), F32),
        ],
        scratch_shapes=[
            pltpu.VMEM((tm, nq), BF16),
            pltpu.VMEM((2 * PEER_HEADS, K, tm), F32),
            pltpu.VMEM((2 * PEER_HEADS, K, tm), jnp.int32),
            pltpu.VMEM((npk, tm), F32),
            pltpu.VMEM((npk, tm), F32),
        ],
        compiler_params=pltpu.CompilerParams(
            dimension_semantics=("parallel",), vmem_limit_bytes=VMEM_LIMIT),
        name="route",
    )(x2, oh2, od2, w_out, w_out, g2, w_q, sub_keys, cpos, cmask)


def _gelu(x):
    return 0.5 * x * (1.0 + lax.erf(x * (1.0 / math.sqrt(2.0))))


def _sc_mesh():
    return plsc.VectorSubcoreMesh(core_axis_name="c", subcore_axis_name="s")


def _sc_worker_id():
    return lax.axis_index("s") * SC_CORES + lax.axis_index("c")


def _sc_gather_loop(tab_hbm, idx_v, bufs, sems, n_items, compute):
    def start(item, b):
        pltpu.async_copy(tab_hbm.at[idx_v.at[item]], bufs[b], sems[b])

    def wait(b):
        pltpu.make_async_copy(tab_hbm.at[idx_v.at[0]], bufs[b], sems[b]).wait()

    start(0, 0)

    @pl.loop(0, n_items // 2)
    def _(i2):
        it = 2 * i2
        start(it + 1, 1)
        wait(0)
        compute(it, bufs[0])

        @pl.when(it + 2 < n_items)
        def _():
            start(it + 2, 0)

        wait(1)
        compute(it + 1, bufs[1])


def _peer_hidden(u, idx4, x):
    T, D = x.shape
    n_rows, R = idx4.shape
    ipt = n_rows // T
    G = SC_TOKEN_GROUP
    nit = G * ipt
    tpw = T // SC_WORKERS
    NL = SC_LANES
    RB = 8
    VPR = LANES // NL

    @functools.partial(
        pl.kernel, mesh=_sc_mesh(), compiler_params=pltpu.CompilerParams(needs_layout_passes=False),
        out_type=jax.ShapeDtypeStruct((n_rows, R), F32),
        scratch_types=[pltpu.VMEM((nit, R), jnp.int32), pltpu.VMEM((G, D), F32),
                       pltpu.VMEM((R, SUBLANES, LANES), F32), pltpu.VMEM((R, SUBLANES, LANES), F32),
                       pltpu.VMEM((nit, R), F32),
                       pltpu.SemaphoreType.DMA, pltpu.SemaphoreType.DMA],
        name="peer_hidden")
    def k(u_hbm, idx_hbm, x_hbm, h_hbm, idx_v, x_v, buf0, buf1, h_v, sem0, sem1):
        tok0 = _sc_worker_id() * tpw
        lane = lax.broadcasted_iota(jnp.int32, (NL,), 0)
        zero = jnp.zeros((NL,), F32)

        def compute(item, buf):
            tok = item // ipt
            hvs = [zero for _ in range(R // NL)]
            for rb in range(R // RB):
                def body(i, accs):
                    col = pl.ds(pl.multiple_of(i * NL, NL), NL)
                    xv = x_v[tok, col]
                    sub = i // VPR
                    ln = pl.ds(pl.multiple_of((i % VPR) * NL, NL), NL)
                    return tuple(accs[j] + xv * buf[rb * RB + j, sub, ln] for j in range(RB))
                accs = lax.fori_loop(0, D // NL, body, tuple(zero for _ in range(RB)))
                for j in range(RB):
                    r = rb * RB + j
                    hvs[r // NL] = jnp.where(lane == (r % NL), jnp.sum(accs[j]), hvs[r // NL])
            for q in range(R // NL):
                h_v[item, pl.ds(q * NL, NL)] = hvs[q]

        @pl.loop(0, tpw // G)
        def _(g):
            t0 = tok0 + g * G
            pltpu.sync_copy(idx_hbm.at[pl.ds(t0 * ipt, nit)], idx_v)
            pltpu.sync_copy(x_hbm.at[pl.ds(t0, G)], x_v)
            _sc_gather_loop(u_hbm, idx_v, (buf0, buf1), (sem0, sem1), nit, compute)
            pltpu.sync_copy(h_v, h_hbm.at[pl.ds(t0 * ipt, nit)])

    return k(u, idx4, x)


def _peer_combine(v, idx4, w4, T, D):
    n_rows, R = idx4.shape
    ipt = n_rows // T
    G = SC_TOKEN_GROUP
    nit = G * ipt
    tpw = T // SC_WORKERS
    NL = SC_LANES
    half = D // 2
    nv = half // NL

    @functools.partial(
        pl.kernel, mesh=_sc_mesh(), compiler_params=pltpu.CompilerParams(needs_layout_passes=False),
        out_type=jax.ShapeDtypeStruct((T, D), F32),
        scratch_types=[pltpu.VMEM((nit, R), jnp.int32), pltpu.VMEM((nit, R), F32),
                       pltpu.VMEM((R, SUBLANES, LANES), F32), pltpu.VMEM((R, SUBLANES, LANES), F32),
                       pltpu.VMEM((G, D), F32),
                       pltpu.SemaphoreType.DMA, pltpu.SemaphoreType.DMA],
        name="peer_combine")
    def k(v_hbm, idx_hbm, w_hbm, o_hbm, idx_v, w_v, buf0, buf1, out_v, sem0, sem1):
        tok0 = _sc_worker_id() * tpw
        zero = jnp.zeros((NL,), F32)

        def compute(item, buf):
            tok = item // ipt
            item_vec = jnp.full((NL,), item, jnp.int32)
            for hf in range(2):
                def body(r, accs):
                    wr = plsc.load_gather(w_v, [item_vec, jnp.full((NL,), r, jnp.int32)])
                    return tuple(
                        accs[i] + wr * buf[r, (hf * half + i * NL) // LANES, pl.ds((i * NL) % LANES, NL)]
                        for i in range(nv))
                accs = lax.fori_loop(0, R, body, tuple(zero for _ in range(nv)))
                for i in range(nv):
                    plsc.addupdate(out_v.at[tok, pl.ds(hf * half + i * NL, NL)], accs[i])

        @pl.loop(0, tpw // G)
        def _(g):
            t0 = tok0 + g * G
            pltpu.sync_copy(idx_hbm.at[pl.ds(t0 * ipt, nit)], idx_v)
            pltpu.sync_copy(w_hbm.at[pl.ds(t0 * ipt, nit)], w_v)

            @pl.loop(0, G)
            def _(t):
                @pl.loop(0, D // NL)
                def _(i):
                    out_v[t, pl.ds(pl.multiple_of(i * NL, NL), NL)] = zero

            _sc_gather_loop(v_hbm, idx_v, (buf0, buf1), (sem0, sem1), nit, compute)
            pltpu.sync_copy(out_v, o_hbm.at[pl.ds(t0, G)])

    return k(v, idx4, w4)


def _act_kernel(h_ref, g_ref, w_ref):
    w_ref[...] = _gelu(h_ref[...]) * g_ref[...]


def _peer_act(hraw, gate):
    T, n = gate.shape
    tm = PEER_ACT_TM
    spec = pl.BlockSpec((tm, n), lambda i: (i, 0))
    return pl.pallas_call(
        _act_kernel, grid=(T // tm,), in_specs=[spec, spec], out_specs=spec,
        out_shape=jax.ShapeDtypeStruct((T, n), F32),
        compiler_params=pltpu.CompilerParams(dimension_semantics=("parallel",)),
        name="peer_act",
    )(hraw, gate)


def _final_kernel(h_ref, p_ref, g_ref, y_ref):
    y = h_ref[...] + p_ref[...]
    ms = jnp.mean(y * y, axis=-1, keepdims=True)
    y_ref[...] = y * lax.rsqrt(ms + EPS) * g_ref[...]


def _final(h, po, g):
    T, D = h.shape
    tm = FINAL_TM
    spec = pl.BlockSpec((tm, D), lambda i: (i, 0))
    return pl.pallas_call(
        _final_kernel, grid=(T // tm,),
        in_specs=[spec, spec, pl.BlockSpec((1, D), lambda i: (0, 0))], out_specs=spec,
        out_shape=jax.ShapeDtypeStruct((T, D), F32),
        compiler_params=pltpu.CompilerParams(dimension_semantics=("parallel",)),
        name="final_norm",
    )(h, po, g)


def kernel(x, norm1_g, w_in, hgrn_lb_fwd, hgrn_lb_bwd, hgrn_out_g, diff_lam_q1, diff_lam_k1,
           diff_lam_q2, diff_lam_k2, diff_out_g, rel_bias, w_out, norm2_g, peer_w_q,
           peer_sub_keys, peer_u, peer_v, final_g):
    B, L, D = x.shape
    hw = HGRN_HEADS * HGRN_DK

    w = w_in[LAYER]
    scale = DIFF_HALF ** -0.5
    cols = lambda j: w[:, j * hw:(j + 1) * hw]
    w_r = jnp.concatenate([cols(1), cols(2), cols(0), cols(3), cols(4), cols(5) * scale, cols(6), cols(7)],
                          axis=1).astype(BF16)
    f32 = jnp.float32
    lam_init = 0.8 - 0.6 * math.exp(-0.3 * LAYER)
    lam = (jnp.exp(jnp.sum(diff_lam_q1[LAYER].astype(f32) * diff_lam_k1[LAYER].astype(f32)))
           - jnp.exp(jnp.sum(diff_lam_q2[LAYER].astype(f32) * diff_lam_k2[LAYER].astype(f32))) + lam_init)
    lam = lam.reshape(1, 1)
    bias_ext = _rel_bias_ext(rel_bias, L)
    sk = peer_sub_keys[LAYER].reshape(2 * PEER_HEADS, PEER_NKEYS, PEER_DKEY).astype(BF16)
    wq = peer_w_q[LAYER].reshape(D, -1).astype(BF16)
    wo = w_out[LAYER].astype(BF16)
    tile3 = lambda tab: tab.reshape(tab.shape[0], D // LANES, LANES)
    u3, v3 = tile3(peer_u[LAYER]), tile3(peer_v[LAYER])

    bc = B // BATCH_CHUNKS
    tc = bc * L
    rows = tc * PEER_HEADS * PEER_TOPK // SC_GATHER_ROWS
    outs = []
    for c in range(BATCH_CHUNKS):
        x2 = x[c * bc:(c + 1) * bc].reshape(tc, D)
        zf, pb = _inproj(x2, norm1_g[LAYER][None, :], w_r)
        zf3 = zf.reshape(bc, L, -1)
        pb3 = pb.reshape(bc, L, -1)
        o_h = _hgrn(zf3, pb3, hgrn_lb_fwd, hgrn_lb_bwd, hgrn_out_g[LAYER][None, :])
        o_d = _attn(pb3, bias_ext, lam, diff_out_g[LAYER][None, :])
        h, hn, idx, gate = _route(x2, o_h.reshape(tc, -1), o_d.reshape(tc, -1), wo,
                                  norm2_g[LAYER][None, :], wq, sk)
        idx4 = idx.reshape(rows, SC_GATHER_ROWS)
        hraw = _peer_hidden(u3, idx4, hn)
        wts = _peer_act(hraw.reshape(tc, -1), gate)
        po = _peer_combine(v3, idx4, wts.reshape(rows, SC_GATHER_ROWS), tc, D)
        outs.append(_final(h, po, final_g[None, :]).reshape(bc, L, D))
    return jnp.concatenate(outs, axis=0)
```

```python
import functools
import math

import numpy as np
import jax
import jax.numpy as jnp
from jax import lax
from jax.experimental import pallas as pl
from jax.experimental.pallas import tpu as pltpu
from jax.experimental.pallas import tpu_sc as plsc

F32 = jnp.float32
BF16 = jnp.bfloat16
EPS = 1e-6

HGRN_HEADS = 4
HGRN_DK = 128
DIFF_HEADS = 4
DIFF_HALF = 64
REL_BUCKETS = 32
REL_MAX_DIST = 128
PEER_HEADS = 8
PEER_NKEYS = 128
PEER_DKEY = 128
PEER_TOPK = 16
LAYER = 0

LANES = 128
SUBLANES = 8
VMEM_LIMIT = 48 * 1024 * 1024

INPROJ_TM = 512
HG_C = 64
HG_LEVELS = (64, 32, 16, 8, 4, 2)
ATTN_TQ = 256
ROUTE_TM = 256
BATCH_CHUNKS = 4
PEER_ACT_TM = 2048
FINAL_TM = 512

SC_CORES = 2
SC_SUBCORES = 16
SC_LANES = 16
SC_WORKERS = SC_CORES * SC_SUBCORES
SC_GATHER_ROWS = 64
SC_TOKEN_GROUP = 8


def _dot(a, b):
    return jnp.dot(a, b, preferred_element_type=F32)


def _dot_nt(a, b):
    return lax.dot_general(a, b, (((1,), (1,)), ((), ())), preferred_element_type=F32)


def _dot_tn(a, b):
    return lax.dot_general(a, b, (((0,), (0,)), ((), ())), preferred_element_type=F32)


def _silu(x):
    return x * (1.0 / (1.0 + jnp.exp(-x)))


BF16_HI_MASK = -65536


def _pack_bf16_pair(hi, lo):
    bits = lambda a: lax.bitcast_convert_type(a.astype(BF16).astype(F32), jnp.int32)
    return (bits(hi) & BF16_HI_MASK) | lax.shift_right_logical(bits(lo), 16)


def _inproj_kernel(x_ref, g_ref, w_ref, zf_ref, pb_ref):
    x = x_ref[...]
    ms = jnp.mean(x * x, axis=-1, keepdims=True)
    xn = (x * lax.rsqrt(ms + EPS) * g_ref[...]).astype(BF16)
    nz = zf_ref.shape[1]
    zf_ref[...] = _dot(xn, w_ref[:, 0:nz])
    nb = pb_ref.shape[1]
    step = 1024
    for j in range(nb // step):
        pb_ref[:, j * step:(j + 1) * step] = _dot(
            xn, w_ref[:, nz + j * step: nz + (j + 1) * step]).astype(BF16)


def _inproj(x2, g, w):
    T, D = x2.shape
    N = w.shape[1]
    nz = 2 * HGRN_HEADS * HGRN_DK
    tm = INPROJ_TM
    return pl.pallas_call(
        _inproj_kernel,
        grid=(T // tm,),
        in_specs=[
            pl.BlockSpec((tm, D), lambda i: (i, 0)),
            pl.BlockSpec((1, D), lambda i: (0, 0)),
            pl.BlockSpec((D, N), lambda i: (0, 0)),
        ],
        out_specs=[
            pl.BlockSpec((tm, nz), lambda i: (i, 0)),
            pl.BlockSpec((tm, N - nz), lambda i: (i, 0)),
        ],
        out_shape=[
            jax.ShapeDtypeStruct((T, nz), F32),
            jax.ShapeDtypeStruct((T, N - nz), BF16),
        ],
        compiler_params=pltpu.CompilerParams(
            dimension_semantics=("parallel",), vmem_limit_bytes=VMEM_LIMIT),
        name="inproj",
    )(x2, g, w)


def _hgrn_consts():
    C = HG_C
    r = np.arange(C)
    t = r[:, None]
    u = r[None, :]
    blocks = [u <= t, u > t]
    masks = [np.eye(C, dtype=bool)]
    for B in HG_LEVELS:
        half = B // 2
        a = (r // B) * B
        m = (a + half - 1)[:, None]
        upper = (r - a) >= half
        blocks.append(np.where(upper[:, None], (u > m) & (u <= t), (u > t) & (u <= m)))
        same = a[:, None] == a[None, :]
        masks.append(same & upper[:, None] & (~upper)[None, :])
    m_f = np.concatenate(blocks, 0).astype(np.float32)
    m_b = np.concatenate([b[::-1, ::-1] for b in blocks], 0).astype(np.float32)
    k_f = np.stack(masks).astype(np.float32)
    k_b = np.ascontiguousarray(k_f.transpose(0, 2, 1))
    return m_f, m_b, k_f, k_b


def _hgrn_gates(z, tab):
    tabf = tab.astype(F32)
    e = jnp.exp(tabf - jnp.max(tabf, axis=0, keepdims=True))
    lb = jnp.sum(e[0:LAYER + 1], axis=0, keepdims=True) / jnp.sum(e, axis=0, keepdims=True)
    log_lb = jnp.log(lb)
    log_1m = jnp.log1p(-lb)
    ez = jnp.exp(-jnp.abs(z))
    l1p = jnp.log1p(ez)
    log_sig = jnp.minimum(z, 0.0) - l1p
    c = log_1m + log_sig
    hi = jnp.maximum(log_lb, c)
    lo = jnp.minimum(log_lb, c)
    log_f = hi + jnp.log1p(jnp.exp(lo - hi))
    sig_neg = jnp.where(z >= 0.0, ez, 1.0) / (1.0 + ez)
    k = (1.0 - lb) * sig_neg
    return log_f, k


def _hgrn_kernel(lbf_ref, lbb_ref, og_ref, mf_ref, mb_ref, kf_ref, kb_ref,
                 q_ref, zf_ref, zb_ref, v_ref, g_ref, o_ref, of_s):
    C = HG_C
    L = q_ref.shape[0]
    n = L // C
    dv = v_ref.shape[1]

    def chunk(c, st, m_ref, k_ref, z_ref, tab_ref, forward):
        sl = pl.ds(pl.multiple_of(c * C, C), C)
        qh = q_ref[sl, :].astype(F32)
        q = _silu(qh)
        v = v_ref[sl, :]
        log_f, k = _hgrn_gates(z_ref[sl, :], tab_ref[...])
        lf_hi = log_f.astype(BF16)
        lf_lo = (log_f - lf_hi.astype(F32)).astype(BF16)
        m = m_ref[...]
        e = jnp.exp(_dot(m, lf_hi) + _dot(m, lf_lo))
        e_b = e[0:C]
        e_s = e[C:2 * C]
        dec = e[C - 1:C] if forward else e[0:1]
        o = _dot_nt((q * e_b).astype(BF16), st.astype(BF16))
        a = _dot_nt(q.astype(BF16), k.astype(BF16)) * k_ref[0]
        for l in range(len(HG_LEVELS)):
            e_l = e[(2 + l) * C:(3 + l) * C]
            a = a + _dot_nt((q * e_l).astype(BF16), (k * e_l).astype(BF16)) * k_ref[l + 1]
        o = o + _dot(a.astype(BF16), v)
        st = st * dec + _dot_tn(v, (k * e_s).astype(BF16))
        return sl, o, st

    st0 = jnp.zeros((dv, q_ref.shape[1]), F32)

    def fwd(c, st):
        sl, o, st = chunk(c, st, mf_ref, kf_ref, zf_ref, lbf_ref, True)
        of_s[sl, :] = o
        return st

    lax.fori_loop(0, n, fwd, st0)

    def bwd(i, st):
        c = n - 1 - i
        sl, o, st = chunk(c, st, mb_ref, kb_ref, zb_ref, lbb_ref, False)
        tot = of_s[sl, :] + o
        ms = jnp.mean(tot * tot, axis=-1, keepdims=True)
        y = tot * lax.rsqrt(ms + EPS) * og_ref[...]
        o_ref[sl, :] = (y * _silu(g_ref[sl, :].astype(F32))).astype(o_ref.dtype)
        return st

    lax.fori_loop(0, n, bwd, st0)


def _hgrn(zf3, pb3, lb_f, lb_b, out_g):
    B, L, _ = zf3.shape
    H, dk = HGRN_HEADS, HGRN_DK
    m_f, m_b, k_f, k_b = _hgrn_consts()
    nlev = k_f.shape[0]
    full2 = lambda b, h: (0, 0)
    full3 = lambda b, h: (0, 0, 0)
    seq = lambda off: pl.BlockSpec((None, L, dk), lambda b, h: (b, 0, off + h))
    return pl.pallas_call(
        _hgrn_kernel,
        grid=(B, H),
        in_specs=[
            pl.BlockSpec((lb_f.shape[0], dk), lambda b, h: (0, h)),
            pl.BlockSpec((lb_b.shape[0], dk), lambda b, h: (0, h)),
            pl.BlockSpec((1, dk), full2),
            pl.BlockSpec(m_f.shape, full2),
            pl.BlockSpec(m_b.shape, full2),
            pl.BlockSpec((nlev, HG_C, HG_C), full3),
            pl.BlockSpec((nlev, HG_C, HG_C), full3),
            seq(0),
            seq(0),
            seq(H),
            seq(H),
            seq(2 * H),
        ],
        out_specs=pl.BlockSpec((None, L, dk), lambda b, h: (b, 0, h)),
        out_shape=jax.ShapeDtypeStruct((B, L, H * dk), BF16),
        scratch_shapes=[pltpu.VMEM((L, dk), F32)],
        compiler_params=pltpu.CompilerParams(
            dimension_semantics=("parallel", "parallel"), vmem_limit_bytes=VMEM_LIMIT),
        name="hgrn",
    )(lb_f, lb_b, out_g, jnp.asarray(m_f, BF16), jnp.asarray(m_b, BF16),
      jnp.asarray(k_f), jnp.asarray(k_b), pb3, zf3, zf3, pb3, pb3)


def _t5_bucket(rel):
    nb = REL_BUCKETS // 2
    ret = jnp.where(rel > 0, nb, 0)
    n = jnp.abs(rel)
    max_exact = nb // 2
    nf = jnp.maximum(n, 1).astype(jnp.float32)
    large = max_exact + (jnp.log(nf / max_exact) / math.log(REL_MAX_DIST / max_exact)
                         * (nb - max_exact)).astype(jnp.int32)
    large = jnp.minimum(large, nb - 1)
    return ret + jnp.where(n < max_exact, n, large)


def _rel_bias_ext(rel_bias, L):
    j = jnp.arange(2 * L - LANES, dtype=jnp.int32)
    ql = jnp.arange(LANES, dtype=jnp.int32)
    rel = j[None, :] - (L - LANES) - ql[:, None]
    bucket = _t5_bucket(rel)
    tab = rel_bias.astype(F32)
    out = jnp.zeros((tab.shape[1],) + bucket.shape, F32)
    for b in range(REL_BUCKETS):
        out = jnp.where((bucket == b)[None], tab[b][:, None, None], out)
    return out


def _attn_kernel(lam_ref, q_ref, k_ref, v_ref, bias_ref, og_ref, o_ref, *, lam_init):
    tq = q_ref.shape[0]
    L = k_ref.shape[0]
    qi = pl.program_id(2)
    lam = lam_ref[0, 0]
    q = q_ref[...]
    k = k_ref[...]
    lane = lax.broadcasted_iota(jnp.int32, q.shape, 1)
    zero = jnp.zeros_like(q)
    q0 = jnp.where(lane < DIFF_HALF, q, zero)
    q1 = jnp.where(lane >= DIFF_HALF, q, zero)
    parts = []
    for j in range(tq // LANES):
        off = pl.multiple_of(L - LANES - (qi * tq + j * LANES), LANES)
        parts.append(bias_ref[:, pl.ds(off, L)])
    bias = jnp.concatenate(parts, axis=0) if len(parts) > 1 else parts[0]

    def soft(qm):
        s = _dot_nt(qm, k) + bias
        e = jnp.exp(s - jnp.max(s, axis=-1, keepdims=True))
        return e, 1.0 / jnp.sum(e, axis=-1, keepdims=True)

    e0, r0 = soft(q0)
    e1, r1 = soft(q1)
    w = e0 * r0 - e1 * (lam * r1)
    o = _dot(w.astype(BF16), v_ref[...])
    ms = jnp.mean(o * o, axis=-1, keepdims=True)
    y = o * lax.rsqrt(ms + EPS) * og_ref[...] * (1.0 - lam_init)
    o_ref[...] = y.astype(o_ref.dtype)


def _attn(pb3, bias_ext, lam, out_g):
    B, L, _ = pb3.shape
    H, dh = DIFF_HEADS, 2 * DIFF_HALF
    tq = ATTN_TQ
    base = (3 * HGRN_HEADS * HGRN_DK) // dh
    lam_init = 0.8 - 0.6 * math.exp(-0.3 * LAYER)
    return pl.pallas_call(
        functools.partial(_attn_kernel, lam_init=lam_init),
        grid=(B, H, L // tq),
        in_specs=[
            pl.BlockSpec(memory_space=pltpu.SMEM),
            pl.BlockSpec((None, tq, dh), lambda b, h, i: (b, i, base + h)),
            pl.BlockSpec((None, L, dh), lambda b, h, i: (b, 0, base + H + h)),
            pl.BlockSpec((None, L, dh), lambda b, h, i: (b, 0, base + 2 * H + h)),
            pl.BlockSpec((None, LANES, 2 * L - LANES), lambda b, h, i: (h, 0, 0)),
            pl.BlockSpec((1, dh), lambda b, h, i: (0, 0)),
        ],
        out_specs=pl.BlockSpec((None, tq, dh), lambda b, h, i: (b, i, h)),
        out_shape=jax.ShapeDtypeStruct((B, L, H * dh), BF16),
        compiler_params=pltpu.CompilerParams(
            dimension_semantics=("parallel", "parallel", "parallel"),
            vmem_limit_bytes=VMEM_LIMIT),
        name="attn",
    )(lam, pb3, pb3, pb3, bias_ext, out_g)


def _cand_layout():
    K = PEER_TOPK
    groups = [("a", 0, 0), ("a", 0, 8), ("a", 1, 0), ("a", 2, 0), ("a", 3, 0),
              ("b", 0, 8), ("b", 0, 0), ("b", 1, 0), ("b", 2, 0)]
    seen = set()
    pos, valid = [], []
    for kind, fixed, start in groups:
        for r in range(SUBLANES):
            a, b = (fixed, start + r) if kind == "a" else (start + r, fixed)
            ok = (a + 1) * (b + 1) <= K and (a, b) not in seen
            if ok:
                seen.add((a, b))
            pos.append(a * K + b if ok else K * K + len(pos))
            valid.append(ok)
    assert len(seen) == sum(K // (a + 1) for a in range(K))
    return groups, np.array(pos, np.int32), np.array(valid, bool)


def _extract_topk(s, key, payload, k, big):
    rows = lax.broadcasted_iota(jnp.int32, (k, s.shape[1]), 0)

    def body(j, carry):
        s, vals, pay = carry
        m = jnp.max(s, axis=0, keepdims=True)
        kk = jnp.min(jnp.where(s == m, key, big), axis=0, keepdims=True)
        sel = key == kk
        p = jnp.sum(jnp.where(sel, payload, 0), axis=0, keepdims=True)
        vals = jnp.where(rows == j, m, vals)
        pay = jnp.where(rows == j, p, pay)
        s = jnp.where(sel, -jnp.inf, s)
        return s, vals, pay

    init = (s, jnp.zeros((k, s.shape[1]), F32), jnp.zeros((k, s.shape[1]), jnp.int32))
    _, vals, pay = lax.fori_loop(0, k, body, init)
    return vals, pay


def _route_kernel(x_ref, oh_ref, od_ref, woh_ref, wod_ref, g2_ref, wq_ref, sk_ref, cpos_ref, cmask_ref,
                  h_ref, hn_ref, idx_ref, gate_ref, q_s, tv_s, ti_s, gs_s, is_s, *, groups):
    K = PEER_TOPK
    tm = x_ref.shape[0]
    h = x_ref[...] + _dot(oh_ref[...], woh_ref[...]) + _dot(od_ref[...], wod_ref[...])
    h_ref[...] = h
    hn = h * lax.rsqrt(jnp.mean(h * h, axis=-1, keepdims=True) + EPS) * g2_ref[...]
    half = hn.shape[1] // 2
    hn_ref[...] = _pack_bf16_pair(hn[:, :half], hn[:, half:])
    q_s[...] = _dot(hn.astype(BF16), wq_ref[...]).astype(BF16)

    key_iota = lax.broadcasted_iota(jnp.int32, (PEER_NKEYS, tm), 0)

    def half_topk(hp, _):
        col = pl.multiple_of(hp * PEER_DKEY, PEER_DKEY)
        s = _dot_nt(sk_ref[hp], q_s[:, pl.ds(col, PEER_DKEY)])
        vals, idxs = _extract_topk(s, key_iota, key_iota, K, PEER_NKEYS)
        tv_s[hp] = vals
        ti_s[hp] = idxs
        return 0

    lax.fori_loop(0, 2 * PEER_HEADS, half_topk, 0)

    cpos = cpos_ref[...]
    cmask = cmask_ref[...]

    def head(hd, _):
        s0, s1 = tv_s[2 * hd], tv_s[2 * hd + 1]
        i0, i1 = ti_s[2 * hd] * PEER_NKEYS, ti_s[2 * hd + 1]
        cs, ci = [], []
        for kind, fixed, start in groups:
            if kind == "a":
                cs.append(s0[fixed:fixed + 1] + s1[start:start + SUBLANES])
                ci.append(i0[fixed:fixed + 1] + i1[start:start + SUBLANES])
            else:
                cs.append(s0[start:start + SUBLANES] + s1[fixed:fixed + 1])
                ci.append(i0[start:start + SUBLANES] + i1[fixed:fixed + 1])
        cand = jnp.concatenate(cs, axis=0) + cmask
        cidx = jnp.concatenate(ci, axis=0)
        best, eidx = _extract_topk(cand, cpos, cidx, K, 2 * K * K)
        ex = jnp.exp(best - best[0:1])
        gate = ex / jnp.sum(ex, axis=0, keepdims=True)
        row = pl.ds(pl.multiple_of(hd * K, K), K)
        gs_s[row, :] = gate
        is_s[row, :] = eidx.astype(F32)
        return 0

    lax.fori_loop(0, PEER_HEADS, head, 0)
    gate_ref[...] = gs_s[...].T
    idx_ref[...] = is_s[...].T.astype(jnp.int32)


def _route(x2, oh2, od2, w_out, g2, w_q, sub_keys):
    T, D = x2.shape
    tm = ROUTE_TM
    K = PEER_TOPK
    nh = oh2.shape[1]
    nq = w_q.shape[1]
    npk = PEER_HEADS * K
    groups, pos, valid = _cand_layout()
    ncand = pos.shape[0]
    cpos = jnp.asarray(np.broadcast_to(pos[:, None], (ncand, tm)))
    cmask = jnp.asarray(np.broadcast_to(np.where(valid, 0.0, -np.inf).astype(np.float32)[:, None], (ncand, tm)))
    row = lambda i: (i, 0)
    full2 = lambda i: (0, 0)
    return pl.pallas_call(
        functools.partial(_route_kernel, groups=groups),
        grid=(T // tm,),
        in_specs=[
            pl.BlockSpec((tm, D), row),
            pl.BlockSpec((tm, nh), row),
            pl.BlockSpec((tm, nh), row),
            pl.BlockSpec((nh, D), full2),
            pl.BlockSpec((nh, D), lambda i: (1, 0)),
            pl.BlockSpec((1, D), full2),
            pl.BlockSpec((D, nq), full2),
            pl.BlockSpec(sub_keys.shape, lambda i: (0, 0, 0)),
            pl.BlockSpec((ncand, tm), full2),
            pl.BlockSpec((ncand, tm), full2),
        ],
        out_specs=[
            pl.BlockSpec((tm, D), row),
            pl.BlockSpec((tm, D // 2), row),
            pl.BlockSpec((tm, npk), row),
            pl.BlockSpec((tm, npk), row),
        ],
        out_shape=[
            jax.ShapeDtypeStruct((T, D), F32),
            jax.ShapeDtypeStruct((T, D // 2), jnp.int32),
            jax.ShapeDtypeStruct((T, npk), jnp.int32),
            jax.ShapeDtypeStruct((T, npk), F32),
        ],
        scratch_shapes=[
            pltpu.VMEM((tm, nq), BF16),
            pltpu.VMEM((2 * PEER_HEADS, K, tm), F32),
            pltpu.VMEM((2 * PEER_HEADS, K, tm), jnp.int32),
            pltpu.VMEM((npk, tm), F32),
            pltpu.VMEM((npk, tm), F32),
        ],
        compiler_params=pltpu.CompilerParams(
            dimension_semantics=("parallel",), vmem_limit_bytes=VMEM_LIMIT),
        name="route",
    )(x2, oh2, od2, w_out, w_out, g2, w_q, sub_keys, cpos, cmask)


def _gelu(x):
    return 0.5 * x * (1.0 + lax.erf(x * (1.0 / math.sqrt(2.0))))


def _sc_mesh():
    return plsc.VectorSubcoreMesh(core_axis_name="c", subcore_axis_name="s")


def _sc_worker_id():
    return lax.axis_index("s") * SC_CORES + lax.axis_index("c")


def _sc_widen_pair(s):
    si = plsc.bitcast(s, jnp.int32)
    return plsc.bitcast(si & BF16_HI_MASK, F32), plsc.bitcast(si << 16, F32)


def _sc_gather_loop(tab_hbm, idx_v, bufs, sems, n_items, compute):
    def start(item, b):
        pltpu.async_copy(tab_hbm.at[idx_v.at[item]], bufs[b], sems[b])

    def wait(b):
        pltpu.make_async_copy(tab_hbm.at[idx_v.at[0]], bufs[b], sems[b]).wait()

    start(0, 0)

    @pl.loop(0, n_items // 2)
    def _(i2):
        it = 2 * i2
        start(it + 1, 1)
        wait(0)
        compute(it, bufs[0])

        @pl.when(it + 2 < n_items)
        def _():
            start(it + 2, 0)

        wait(1)
        compute(it + 1, bufs[1])


def _peer_hidden(u, idx4, x):
    T, DW = x.shape
    n_rows, R = idx4.shape
    ipt = n_rows // T
    G = SC_TOKEN_GROUP
    nit = G * ipt
    tpw = T // SC_WORKERS
    NL = SC_LANES
    RB = 8
    NV = 4
    GW = NV * NL
    tile = (R, DW // LANES, LANES)

    @functools.partial(
        pl.kernel, mesh=_sc_mesh(), compiler_params=pltpu.CompilerParams(needs_layout_passes=False),
        out_type=jax.ShapeDtypeStruct((n_rows, R), F32),
        scratch_types=[pltpu.VMEM((nit, R), jnp.int32), pltpu.VMEM((G, DW), jnp.int32),
                       pltpu.VMEM(tile, jnp.int32), pltpu.VMEM(tile, jnp.int32),
                       pltpu.VMEM((nit, R), F32),
                       pltpu.SemaphoreType.DMA, pltpu.SemaphoreType.DMA],
        name="peer_hidden")
    def k(u_hbm, idx_hbm, x_hbm, h_hbm, idx_v, x_v, buf0, buf1, h_v, sem0, sem1):
        tok0 = _sc_worker_id() * tpw
        lane = lax.broadcasted_iota(jnp.int32, (NL,), 0)
        zero = jnp.zeros((NL,), F32)

        def compute(item, buf):
            tok = item // ipt
            hvs = [zero for _ in range(R // NL)]
            for rb in range(R // RB):
                def body(g, accs):
                    sub = g // (LANES // GW)
                    base = (g % (LANES // GW)) * GW
                    xs = [plsc.bitcast(x_v[tok, pl.ds(pl.multiple_of(g * GW + jj * NL, NL), NL)], BF16)
                          for jj in range(NV)]
                    out = []
                    for j in range(RB):
                        ps = [plsc.bitcast(buf[rb * RB + j, sub, pl.ds(pl.multiple_of(base + jj * NL, NL), NL)],
                                           BF16) * xs[jj] for jj in range(NV)]
                        hi, lo = _sc_widen_pair((ps[0] + ps[1]) + (ps[2] + ps[3]))
                        out.append((accs[j] + hi) + lo)
                    return tuple(out)
                accs = lax.fori_loop(0, DW // GW, body, tuple(zero for _ in range(RB)))
                for j in range(RB):
                    r = rb * RB + j
                    hvs[r // NL] = jnp.where(lane == (r % NL), jnp.sum(accs[j]), hvs[r // NL])
            for q in range(R // NL):
                h_v[item, pl.ds(q * NL, NL)] = hvs[q]

        @pl.loop(0, tpw // G)
        def _(g):
            t0 = tok0 + g * G
            pltpu.sync_copy(idx_hbm.at[pl.ds(t0 * ipt, nit)], idx_v)
            pltpu.sync_copy(x_hbm.at[pl.ds(t0, G)], x_v)
            _sc_gather_loop(u_hbm, idx_v, (buf0, buf1), (sem0, sem1), nit, compute)
            pltpu.sync_copy(h_v, h_hbm.at[pl.ds(t0 * ipt, nit)])

    return k(u, idx4, x)


def _peer_combine(v, idx4, w4, T, D):
    n_rows, R = idx4.shape
    ipt = n_rows // T
    G = SC_TOKEN_GROUP
    nit = G * ipt
    tpw = T // SC_WORKERS
    NL = SC_LANES
    DW = D // 2
    half = DW // 2
    nv = half // NL
    RG = 4
    tile = (R, DW // LANES, LANES)

    @functools.partial(
        pl.kernel, mesh=_sc_mesh(), compiler_params=pltpu.CompilerParams(needs_layout_passes=False),
        out_type=jax.ShapeDtypeStruct((T, D), F32),
        scratch_types=[pltpu.VMEM((nit, R), jnp.int32), pltpu.VMEM((nit, R), jnp.int32),
                       pltpu.VMEM(tile, jnp.int32), pltpu.VMEM(tile, jnp.int32),
                       pltpu.VMEM((G, D), F32),
                       pltpu.SemaphoreType.DMA, pltpu.SemaphoreType.DMA],
        name="peer_combine")
    def k(v_hbm, idx_hbm, w_hbm, o_hbm, idx_v, w_v, buf0, buf1, out_v, sem0, sem1):
        tok0 = _sc_worker_id() * tpw
        zero = jnp.zeros((NL,), F32)

        def compute(item, buf):
            tok = item // ipt
            item_vec = jnp.full((NL,), item, jnp.int32)
            for hf in range(2):
                def body(rg, accs):
                    ws = [plsc.bitcast(plsc.load_gather(
                        w_v, [item_vec, jnp.full((NL,), rg * RG + rr, jnp.int32)]), BF16) for rr in range(RG)]
                    his, los = [], []
                    for i in range(nv):
                        word = hf * half + i * NL
                        ps = [plsc.bitcast(buf[rg * RG + rr, word // LANES, pl.ds(word % LANES, NL)], BF16) * ws[rr]
                              for rr in range(RG)]
                        hi, lo = _sc_widen_pair((ps[0] + ps[1]) + (ps[2] + ps[3]))
                        his.append(accs[i] + hi)
                        los.append(accs[nv + i] + lo)
                    return tuple(his + los)
                accs = lax.fori_loop(0, R // RG, body, tuple(zero for _ in range(2 * nv)))
                for i in range(nv):
                    word = hf * half + i * NL
                    plsc.addupdate(out_v.at[tok, pl.ds(word, NL)], accs[i])
                    plsc.addupdate(out_v.at[tok, pl.ds(DW + word, NL)], accs[nv + i])

        @pl.loop(0, tpw // G)
        def _(g):
            t0 = tok0 + g * G
            pltpu.sync_copy(idx_hbm.at[pl.ds(t0 * ipt, nit)], idx_v)
            pltpu.sync_copy(w_hbm.at[pl.ds(t0 * ipt, nit)], w_v)

            @pl.loop(0, G)
            def _(t):
                @pl.loop(0, D // NL)
                def _(i):
                    out_v[t, pl.ds(pl.multiple_of(i * NL, NL), NL)] = zero

            _sc_gather_loop(v_hbm, idx_v, (buf0, buf1), (sem0, sem1), nit, compute)
            pltpu.sync_copy(out_v, o_hbm.at[pl.ds(t0, G)])

    return k(v, idx4, w4)


def _act_kernel(h_ref, g_ref, w_ref):
    w = _gelu(h_ref[...]) * g_ref[...]
    w_ref[...] = _pack_bf16_pair(w, w)


def _peer_act(hraw, gate):
    T, n = gate.shape
    tm = PEER_ACT_TM
    spec = pl.BlockSpec((tm, n), lambda i: (i, 0))
    return pl.pallas_call(
        _act_kernel, grid=(T // tm,), in_specs=[spec, spec], out_specs=spec,
        out_shape=jax.ShapeDtypeStruct((T, n), jnp.int32),
        compiler_params=pltpu.CompilerParams(dimension_semantics=("parallel",)),
        name="peer_act",
    )(hraw, gate)


def _final_kernel(h_ref, p_ref, g_ref, y_ref):
    y = h_ref[...] + p_ref[...]
    ms = jnp.mean(y * y, axis=-1, keepdims=True)
    y_ref[...] = y * lax.rsqrt(ms + EPS) * g_ref[...]


def _final(h, po, g):
    T, D = h.shape
    tm = FINAL_TM
    spec = pl.BlockSpec((tm, D), lambda i: (i, 0))
    return pl.pallas_call(
        _final_kernel, grid=(T // tm,),
        in_specs=[spec, spec, pl.BlockSpec((1, D), lambda i: (0, 0))], out_specs=spec,
        out_shape=jax.ShapeDtypeStruct((T, D), F32),
        compiler_params=pltpu.CompilerParams(dimension_semantics=("parallel",)),
        name="final_norm",
    )(h, po, g)


def kernel(x, norm1_g, w_in, hgrn_lb_fwd, hgrn_lb_bwd, hgrn_out_g, diff_lam_q1, diff_lam_k1,
           diff_lam_q2, diff_lam_k2, diff_out_g, rel_bias, w_out, norm2_g, peer_w_q,
           peer_sub_keys, peer_u, peer_v, final_g):
    B, L, D = x.shape
    hw = HGRN_HEADS * HGRN_DK

    w = w_in[LAYER]
    scale = DIFF_HALF ** -0.5
    cols = lambda j: w[:, j * hw:(j + 1) * hw]
    w_r = jnp.concatenate([cols(1), cols(2), cols(0), cols(3), cols(4), cols(5) * scale, cols(6), cols(7)],
                          axis=1).astype(BF16)
    f32 = jnp.float32
    lam_init = 0.8 - 0.6 * math.exp(-0.3 * LAYER)
    lam = (jnp.exp(jnp.sum(diff_lam_q1[LAYER].astype(f32) * diff_lam_k1[LAYER].astype(f32)))
           - jnp.exp(jnp.sum(diff_lam_q2[LAYER].astype(f32) * diff_lam_k2[LAYER].astype(f32))) + lam_init)
    lam = lam.reshape(1, 1)
    bias_ext = _rel_bias_ext(rel_bias, L)
    sk = peer_sub_keys[LAYER].reshape(2 * PEER_HEADS, PEER_NKEYS, PEER_DKEY).astype(BF16)
    wq = peer_w_q[LAYER].reshape(D, -1).astype(BF16)
    wo = w_out[LAYER].astype(BF16)
    def pack_table(tab):
        words = _pack_bf16_pair(tab[:, :D // 2], tab[:, D // 2:])
        return words.reshape(tab.shape[0], D // 2 // LANES, LANES)
    u3, v3 = pack_table(peer_u[LAYER]), pack_table(peer_v[LAYER])

    bc = B // BATCH_CHUNKS
    tc = bc * L
    rows = tc * PEER_HEADS * PEER_TOPK // SC_GATHER_ROWS
    outs = []
    for c in range(BATCH_CHUNKS):
        x2 = x[c * bc:(c + 1) * bc].reshape(tc, D)
        zf, pb = _inproj(x2, norm1_g[LAYER][None, :], w_r)
        zf3 = zf.reshape(bc, L, -1)
        pb3 = pb.reshape(bc, L, -1)
        o_h = _hgrn(zf3, pb3, hgrn_lb_fwd, hgrn_lb_bwd, hgrn_out_g[LAYER][None, :])
        o_d = _attn(pb3, bias_ext, lam, diff_out_g[LAYER][None, :])
        h, hn, idx, gate = _route(x2, o_h.reshape(tc, -1), o_d.reshape(tc, -1), wo,
                                  norm2_g[LAYER][None, :], wq, sk)
        idx4 = idx.reshape(rows, SC_GATHER_ROWS)
        hraw = _peer_hidden(u3, idx4, hn)
        wts = _peer_act(hraw.reshape(tc, -1), gate)
        po = _peer_combine(v3, idx4, wts.reshape(rows, SC_GATHER_ROWS), tc, D)
        outs.append(_final(h, po, final_g[None, :]).reshape(bc, L, D))
    return jnp.concatenate(outs, axis=0)
```

```python
import functools
import math

import numpy as np
import jax
import jax.numpy as jnp
from jax import lax
from jax.experimental import pallas as pl
from jax.experimental.pallas import tpu as pltpu
from jax.experimental.pallas import tpu_sc as plsc

F32 = jnp.float32
BF16 = jnp.bfloat16
EPS = 1e-6

HGRN_HEADS = 4
HGRN_DK = 128
DIFF_HEADS = 4
DIFF_HALF = 64
REL_BUCKETS = 32
REL_MAX_DIST = 128
PEER_HEADS = 8
PEER_NKEYS = 128
PEER_DKEY = 128
PEER_TOPK = 16
LAYER = 0

LANES = 128
SUBLANES = 8
VMEM_LIMIT = 48 * 1024 * 1024

INPROJ_TM = 512
HG_C = 64
HG_LEVELS = (64, 32, 16, 8, 4, 2)
ATTN_TQ = 256
ROUTE_TM = 256
BATCH_CHUNKS = 8
CHUNK_LAG = 2
PEER_ACT_TM = 2048
FINAL_TM = 512

SC_CORES = 2
SC_SUBCORES = 16
SC_LANES = 16
SC_WORKERS = SC_CORES * SC_SUBCORES
SC_GATHER_ROWS = 64
SC_TOKEN_GROUP = 8


def _dot(a, b):
    return jnp.dot(a, b, preferred_element_type=F32)


def _dot_nt(a, b):
    return lax.dot_general(a, b, (((1,), (1,)), ((), ())), preferred_element_type=F32)


def _dot_tn(a, b):
    return lax.dot_general(a, b, (((0,), (0,)), ((), ())), preferred_element_type=F32)


def _silu(x):
    return x * (1.0 / (1.0 + jnp.exp(-x)))


BF16_HI_MASK = -65536


def _pack_bf16_pair(hi, lo):
    bits = lambda a: lax.bitcast_convert_type(a.astype(BF16).astype(F32), jnp.int32)
    return (bits(hi) & BF16_HI_MASK) | lax.shift_right_logical(bits(lo), 16)


def _inproj_kernel(x_ref, g_ref, w_ref, zf_ref, pb_ref):
    x = x_ref[...]
    ms = jnp.mean(x * x, axis=-1, keepdims=True)
    xn = (x * lax.rsqrt(ms + EPS) * g_ref[...]).astype(BF16)
    nz = zf_ref.shape[1]
    zf_ref[...] = _dot(xn, w_ref[:, 0:nz])
    nb = pb_ref.shape[1]
    step = 1024
    for j in range(nb // step):
        pb_ref[:, j * step:(j + 1) * step] = _dot(
            xn, w_ref[:, nz + j * step: nz + (j + 1) * step]).astype(BF16)


def _inproj(x2, g, w):
    T, D = x2.shape
    N = w.shape[1]
    nz = 2 * HGRN_HEADS * HGRN_DK
    tm = INPROJ_TM
    return pl.pallas_call(
        _inproj_kernel,
        grid=(T // tm,),
        in_specs=[
            pl.BlockSpec((tm, D), lambda i: (i, 0)),
            pl.BlockSpec((1, D), lambda i: (0, 0)),
            pl.BlockSpec((D, N), lambda i: (0, 0)),
        ],
        out_specs=[
            pl.BlockSpec((tm, nz), lambda i: (i, 0)),
            pl.BlockSpec((tm, N - nz), lambda i: (i, 0)),
        ],
        out_shape=[
            jax.ShapeDtypeStruct((T, nz), F32),
            jax.ShapeDtypeStruct((T, N - nz), BF16),
        ],
        compiler_params=pltpu.CompilerParams(
            dimension_semantics=("parallel",), vmem_limit_bytes=VMEM_LIMIT),
        name="inproj",
    )(x2, g, w)


def _hgrn_consts():
    C = HG_C
    r = np.arange(C)
    t = r[:, None]
    u = r[None, :]
    blocks = [u <= t, u > t]
    masks = [np.eye(C, dtype=bool)]
    for B in HG_LEVELS:
        half = B // 2
        a = (r // B) * B
        m = (a + half - 1)[:, None]
        upper = (r - a) >= half
        blocks.append(np.where(upper[:, None], (u > m) & (u <= t), (u > t) & (u <= m)))
        same = a[:, None] == a[None, :]
        masks.append(same & upper[:, None] & (~upper)[None, :])
    m_f = np.concatenate(blocks, 0).astype(np.float32)
    m_b = np.concatenate([b[::-1, ::-1] for b in blocks], 0).astype(np.float32)
    k_f = np.stack(masks).astype(np.float32)
    k_b = np.ascontiguousarray(k_f.transpose(0, 2, 1))
    return m_f, m_b, k_f, k_b


def _hgrn_gates(z, tab):
    tabf = tab.astype(F32)
    e = jnp.exp(tabf - jnp.max(tabf, axis=0, keepdims=True))
    lb = jnp.sum(e[0:LAYER + 1], axis=0, keepdims=True) / jnp.sum(e, axis=0, keepdims=True)
    log_lb = jnp.log(lb)
    log_1m = jnp.log1p(-lb)
    ez = jnp.exp(-jnp.abs(z))
    l1p = jnp.log1p(ez)
    log_sig = jnp.minimum(z, 0.0) - l1p
    c = log_1m + log_sig
    hi = jnp.maximum(log_lb, c)
    lo = jnp.minimum(log_lb, c)
    log_f = hi + jnp.log1p(jnp.exp(lo - hi))
    sig_neg = jnp.where(z >= 0.0, ez, 1.0) / (1.0 + ez)
    k = (1.0 - lb) * sig_neg
    return log_f, k


def _hgrn_kernel(lbf_ref, lbb_ref, og_ref, mf_ref, mb_ref, kf_ref, kb_ref,
                 q_ref, zf_ref, zb_ref, v_ref, g_ref, o_ref, of_s):
    C = HG_C
    L = q_ref.shape[0]
    n = L // C
    dv = v_ref.shape[1]

    def chunk(c, st, m_ref, k_ref, z_ref, tab_ref, forward):
        sl = pl.ds(pl.multiple_of(c * C, C), C)
        qh = q_ref[sl, :].astype(F32)
        q = _silu(qh)
        v = v_ref[sl, :]
        log_f, k = _hgrn_gates(z_ref[sl, :], tab_ref[...])
        lf_hi = log_f.astype(BF16)
        lf_lo = (log_f - lf_hi.astype(F32)).astype(BF16)
        m = m_ref[...]
        e = jnp.exp(_dot(m, lf_hi) + _dot(m, lf_lo))
        e_b = e[0:C]
        e_s = e[C:2 * C]
        dec = e[C - 1:C] if forward else e[0:1]
        o = _dot_nt((q * e_b).astype(BF16), st.astype(BF16))
        a = _dot_nt(q.astype(BF16), k.astype(BF16)) * k_ref[0]
        for l in range(len(HG_LEVELS)):
            e_l = e[(2 + l) * C:(3 + l) * C]
            a = a + _dot_nt((q * e_l).astype(BF16), (k * e_l).astype(BF16)) * k_ref[l + 1]
        o = o + _dot(a.astype(BF16), v)
        st = st * dec + _dot_tn(v, (k * e_s).astype(BF16))
        return sl, o, st

    st0 = jnp.zeros((dv, q_ref.shape[1]), F32)

    def fwd(c, st):
        sl, o, st = chunk(c, st, mf_ref, kf_ref, zf_ref, lbf_ref, True)
        of_s[sl, :] = o
        return st

    lax.fori_loop(0, n, fwd, st0)

    def bwd(i, st):
        c = n - 1 - i
        sl, o, st = chunk(c, st, mb_ref, kb_ref, zb_ref, lbb_ref, False)
        tot = of_s[sl, :] + o
        ms = jnp.mean(tot * tot, axis=-1, keepdims=True)
        y = tot * lax.rsqrt(ms + EPS) * og_ref[...]
        o_ref[sl, :] = (y * _silu(g_ref[sl, :].astype(F32))).astype(o_ref.dtype)
        return st

    lax.fori_loop(0, n, bwd, st0)


def _hgrn(zf3, pb3, lb_f, lb_b, out_g):
    B, L, _ = zf3.shape
    H, dk = HGRN_HEADS, HGRN_DK
    m_f, m_b, k_f, k_b = _hgrn_consts()
    nlev = k_f.shape[0]
    full2 = lambda b, h: (0, 0)
    full3 = lambda b, h: (0, 0, 0)
    seq = lambda off: pl.BlockSpec((None, L, dk), lambda b, h: (b, 0, off + h))
    return pl.pallas_call(
        _hgrn_kernel,
        grid=(B, H),
        in_specs=[
            pl.BlockSpec((lb_f.shape[0], dk), lambda b, h: (0, h)),
            pl.BlockSpec((lb_b.shape[0], dk), lambda b, h: (0, h)),
            pl.BlockSpec((1, dk), full2),
            pl.BlockSpec(m_f.shape, full2),
            pl.BlockSpec(m_b.shape, full2),
            pl.BlockSpec((nlev, HG_C, HG_C), full3),
            pl.BlockSpec((nlev, HG_C, HG_C), full3),
            seq(0),
            seq(0),
            seq(H),
            seq(H),
            seq(2 * H),
        ],
        out_specs=pl.BlockSpec((None, L, dk), lambda b, h: (b, 0, h)),
        out_shape=jax.ShapeDtypeStruct((B, L, H * dk), BF16),
        scratch_shapes=[pltpu.VMEM((L, dk), F32)],
        compiler_params=pltpu.CompilerParams(
            dimension_semantics=("parallel", "parallel"), vmem_limit_bytes=VMEM_LIMIT),
        name="hgrn",
    )(lb_f, lb_b, out_g, jnp.asarray(m_f, BF16), jnp.asarray(m_b, BF16),
      jnp.asarray(k_f), jnp.asarray(k_b), pb3, zf3, zf3, pb3, pb3)


def _t5_bucket(rel):
    nb = REL_BUCKETS // 2
    ret = jnp.where(rel > 0, nb, 0)
    n = jnp.abs(rel)
    max_exact = nb // 2
    nf = jnp.maximum(n, 1).astype(jnp.float32)
    large = max_exact + (jnp.log(nf / max_exact) / math.log(REL_MAX_DIST / max_exact)
                         * (nb - max_exact)).astype(jnp.int32)
    large = jnp.minimum(large, nb - 1)
    return ret + jnp.where(n < max_exact, n, large)


def _rel_bias_ext(rel_bias, L):
    j = jnp.arange(2 * L - LANES, dtype=jnp.int32)
    ql = jnp.arange(LANES, dtype=jnp.int32)
    rel = j[None, :] - (L - LANES) - ql[:, None]
    bucket = _t5_bucket(rel)
    tab = rel_bias.astype(F32)
    out = jnp.zeros((tab.shape[1],) + bucket.shape, F32)
    for b in range(REL_BUCKETS):
        out = jnp.where((bucket == b)[None], tab[b][:, None, None], out)
    return out


def _attn_kernel(lam_ref, q_ref, k_ref, v_ref, bias_ref, og_ref, o_ref, *, lam_init):
    tq = q_ref.shape[0]
    L = k_ref.shape[0]
    qi = pl.program_id(2)
    lam = lam_ref[0, 0]
    q = q_ref[...]
    k = k_ref[...]
    lane = lax.broadcasted_iota(jnp.int32, q.shape, 1)
    zero = jnp.zeros_like(q)
    q0 = jnp.where(lane < DIFF_HALF, q, zero)
    q1 = jnp.where(lane >= DIFF_HALF, q, zero)
    parts = []
    for j in range(tq // LANES):
        off = pl.multiple_of(L - LANES - (qi * tq + j * LANES), LANES)
        parts.append(bias_ref[:, pl.ds(off, L)])
    bias = jnp.concatenate(parts, axis=0) if len(parts) > 1 else parts[0]

    def soft(qm):
        s = _dot_nt(qm, k) + bias
        e = jnp.exp(s - jnp.max(s, axis=-1, keepdims=True))
        return e, 1.0 / jnp.sum(e, axis=-1, keepdims=True)

    e0, r0 = soft(q0)
    e1, r1 = soft(q1)
    w = e0 * r0 - e1 * (lam * r1)
    o = _dot(w.astype(BF16), v_ref[...])
    ms = jnp.mean(o * o, axis=-1, keepdims=True)
    y = o * lax.rsqrt(ms + EPS) * og_ref[...] * (1.0 - lam_init)
    o_ref[...] = y.astype(o_ref.dtype)


def _attn(pb3, bias_ext, lam, out_g):
    B, L, _ = pb3.shape
    H, dh = DIFF_HEADS, 2 * DIFF_HALF
    tq = ATTN_TQ
    base = (3 * HGRN_HEADS * HGRN_DK) // dh
    lam_init = 0.8 - 0.6 * math.exp(-0.3 * LAYER)
    return pl.pallas_call(
        functools.partial(_attn_kernel, lam_init=lam_init),
        grid=(B, H, L // tq),
        in_specs=[
            pl.BlockSpec(memory_space=pltpu.SMEM),
            pl.BlockSpec((None, tq, dh), lambda b, h, i: (b, i, base + h)),
            pl.BlockSpec((None, L, dh), lambda b, h, i: (b, 0, base + H + h)),
            pl.BlockSpec((None, L, dh), lambda b, h, i: (b, 0, base + 2 * H + h)),
            pl.BlockSpec((None, LANES, 2 * L - LANES), lambda b, h, i: (h, 0, 0)),
            pl.BlockSpec((1, dh), lambda b, h, i: (0, 0)),
        ],
        out_specs=pl.BlockSpec((None, tq, dh), lambda b, h, i: (b, i, h)),
        out_shape=jax.ShapeDtypeStruct((B, L, H * dh), BF16),
        compiler_params=pltpu.CompilerParams(
            dimension_semantics=("parallel", "parallel", "parallel"),
            vmem_limit_bytes=VMEM_LIMIT),
        name="attn",
    )(lam, pb3, pb3, pb3, bias_ext, out_g)


def _cand_layout():
    K = PEER_TOPK
    groups = [("a", 0, 0), ("a", 0, 8), ("a", 1, 0), ("a", 2, 0), ("a", 3, 0),
              ("b", 0, 8), ("b", 0, 0), ("b", 1, 0), ("b", 2, 0)]
    seen = set()
    pos, valid = [], []
    for kind, fixed, start in groups:
        for r in range(SUBLANES):
            a, b = (fixed, start + r) if kind == "a" else (start + r, fixed)
            ok = (a + 1) * (b + 1) <= K and (a, b) not in seen
            if ok:
                seen.add((a, b))
            pos.append(a * K + b if ok else K * K + len(pos))
            valid.append(ok)
    assert len(seen) == sum(K // (a + 1) for a in range(K))
    return groups, np.array(pos, np.int32), np.array(valid, bool)


def _extract_topk(s, key, payload, k, big):
    rows = lax.broadcasted_iota(jnp.int32, (k, s.shape[1]), 0)

    def body(j, carry):
        s, vals, pay = carry
        m = jnp.max(s, axis=0, keepdims=True)
        kk = jnp.min(jnp.where(s == m, key, big), axis=0, keepdims=True)
        sel = key == kk
        p = jnp.sum(jnp.where(sel, payload, 0), axis=0, keepdims=True)
        vals = jnp.where(rows == j, m, vals)
        pay = jnp.where(rows == j, p, pay)
        s = jnp.where(sel, -jnp.inf, s)
        return s, vals, pay

    init = (s, jnp.zeros((k, s.shape[1]), F32), jnp.zeros((k, s.shape[1]), jnp.int32))
    _, vals, pay = lax.fori_loop(0, k, body, init)
    return vals, pay


def _route_kernel(x_ref, oh_ref, od_ref, woh_ref, wod_ref, g2_ref, wq_ref, sk_ref, cpos_ref, cmask_ref,
                  h_ref, hn_ref, idx_ref, gate_ref, q_s, tv_s, ti_s, gs_s, is_s, *, groups):
    K = PEER_TOPK
    tm = x_ref.shape[0]
    h = x_ref[...] + _dot(oh_ref[...], woh_ref[...]) + _dot(od_ref[...], wod_ref[...])
    h_ref[...] = h
    hn = h * lax.rsqrt(jnp.mean(h * h, axis=-1, keepdims=True) + EPS) * g2_ref[...]
    half = hn.shape[1] // 2
    hn_ref[...] = _pack_bf16_pair(hn[:, :half], hn[:, half:])
    q_s[...] = _dot(hn.astype(BF16), wq_ref[...]).astype(BF16)

    key_iota = lax.broadcasted_iota(jnp.int32, (PEER_NKEYS, tm), 0)

    def half_topk(hp, _):
        col = pl.multiple_of(hp * PEER_DKEY, PEER_DKEY)
        s = _dot_nt(sk_ref[hp], q_s[:, pl.ds(col, PEER_DKEY)])
        vals, idxs = _extract_topk(s, key_iota, key_iota, K, PEER_NKEYS)
        tv_s[hp] = vals
        ti_s[hp] = idxs
        return 0

    lax.fori_loop(0, 2 * PEER_HEADS, half_topk, 0)

    cpos = cpos_ref[...]
    cmask = cmask_ref[...]

    def head(hd, _):
        s0, s1 = tv_s[2 * hd], tv_s[2 * hd + 1]
        i0, i1 = ti_s[2 * hd] * PEER_NKEYS, ti_s[2 * hd + 1]
        cs, ci = [], []
        for kind, fixed, start in groups:
            if kind == "a":
                cs.append(s0[fixed:fixed + 1] + s1[start:start + SUBLANES])
                ci.append(i0[fixed:fixed + 1] + i1[start:start + SUBLANES])
            else:
                cs.append(s0[start:start + SUBLANES] + s1[fixed:fixed + 1])
                ci.append(i0[start:start + SUBLANES] + i1[fixed:fixed + 1])
        cand = jnp.concatenate(cs, axis=0) + cmask
        cidx = jnp.concatenate(ci, axis=0)
        best, eidx = _extract_topk(cand, cpos, cidx, K, 2 * K * K)
        ex = jnp.exp(best - best[0:1])
        gate = ex / jnp.sum(ex, axis=0, keepdims=True)
        row = pl.ds(pl.multiple_of(hd * K, K), K)
        gs_s[row, :] = gate
        is_s[row, :] = eidx.astype(F32)
        return 0

    lax.fori_loop(0, PEER_HEADS, head, 0)
    gate_ref[...] = gs_s[...].T
    idx_ref[...] = is_s[...].T.astype(jnp.int32)


def _route(x2, oh2, od2, w_out, g2, w_q, sub_keys):
    T, D = x2.shape
    tm = ROUTE_TM
    K = PEER_TOPK
    nh = oh2.shape[1]
    nq = w_q.shape[1]
    npk = PEER_HEADS * K
    groups, pos, valid = _cand_layout()
    ncand = pos.shape[0]
    cpos = jnp.asarray(np.broadcast_to(pos[:, None], (ncand, tm)))
    cmask = jnp.asarray(np.broadcast_to(np.where(valid, 0.0, -np.inf).astype(np.float32)[:, None], (ncand, tm)))
    row = lambda i: (i, 0)
    full2 = lambda i: (0, 0)
    return pl.pallas_call(
        functools.partial(_route_kernel, groups=groups),
        grid=(T // tm,),
        in_specs=[
            pl.BlockSpec((tm, D), row),
            pl.BlockSpec((tm, nh), row),
            pl.BlockSpec((tm, nh), row),
            pl.BlockSpec((nh, D), full2),
            pl.BlockSpec((nh, D), lambda i: (1, 0)),
            pl.BlockSpec((1, D), full2),
            pl.BlockSpec((D, nq), full2),
            pl.BlockSpec(sub_keys.shape, lambda i: (0, 0, 0)),
            pl.BlockSpec((ncand, tm), full2),
            pl.BlockSpec((ncand, tm), full2),
        ],
        out_specs=[
            pl.BlockSpec((tm, D), row),
            pl.BlockSpec((tm, D // 2), row),
            pl.BlockSpec((tm, npk), row),
            pl.BlockSpec((tm, npk), row),
        ],
        out_shape=[
            jax.ShapeDtypeStruct((T, D), F32),
            jax.ShapeDtypeStruct((T, D // 2), jnp.int32),
            jax.ShapeDtypeStruct((T, npk), jnp.int32),
            jax.ShapeDtypeStruct((T, npk), F32),
        ],
        scratch_shapes=[
            pltpu.VMEM((tm, nq), BF16),
            pltpu.VMEM((2 * PEER_HEADS, K, tm), F32),
            pltpu.VMEM((2 * PEER_HEADS, K, tm), jnp.int32),
            pltpu.VMEM((npk, tm), F32),
            pltpu.VMEM((npk, tm), F32),
        ],
        compiler_params=pltpu.CompilerParams(
            dimension_semantics=("parallel",), vmem_limit_bytes=VMEM_LIMIT),
        name="route",
    )(x2, oh2, od2, w_out, w_out, g2, w_q, sub_keys, cpos, cmask)


def _gelu(x):
    return 0.5 * x * (1.0 + lax.erf(x * (1.0 / math.sqrt(2.0))))


def _sc_mesh():
    return plsc.VectorSubcoreMesh(core_axis_name="c", subcore_axis_name="s")


def _sc_worker_id():
    return lax.axis_index("s") * SC_CORES + lax.axis_index("c")


def _sc_widen_pair(s):
    si = plsc.bitcast(s, jnp.int32)
    return plsc.bitcast(si & BF16_HI_MASK, F32), plsc.bitcast(si << 16, F32)


def _sc_gather_loop(tab_hbm, idx_v, bufs, sems, n_items, compute):
    def start(item, b):
        pltpu.async_copy(tab_hbm.at[idx_v.at[item]], bufs[b], sems[b])

    def wait(b):
        pltpu.make_async_copy(tab_hbm.at[idx_v.at[0]], bufs[b], sems[b]).wait()

    start(0, 0)

    @pl.loop(0, n_items // 2)
    def _(i2):
        it = 2 * i2
        start(it + 1, 1)
        wait(0)
        compute(it, bufs[0])

        @pl.when(it + 2 < n_items)
        def _():
            start(it + 2, 0)

        wait(1)
        compute(it + 1, bufs[1])


def _peer_hidden(u, idx4, x):
    T, DW = x.shape
    n_rows, R = idx4.shape
    ipt = n_rows // T
    G = SC_TOKEN_GROUP
    nit = G * ipt
    tpw = T // SC_WORKERS
    NL = SC_LANES
    RB = 8
    NV = 4
    GW = NV * NL
    tile = (R, DW // LANES, LANES)

    @functools.partial(
        pl.kernel, mesh=_sc_mesh(), compiler_params=pltpu.CompilerParams(needs_layout_passes=False),
        out_type=jax.ShapeDtypeStruct((n_rows, R), F32),
        scratch_types=[pltpu.VMEM((nit, R), jnp.int32), pltpu.VMEM((G, DW), jnp.int32),
                       pltpu.VMEM(tile, jnp.int32), pltpu.VMEM(tile, jnp.int32),
                       pltpu.VMEM((nit, R), F32),
                       pltpu.SemaphoreType.DMA, pltpu.SemaphoreType.DMA],
        name="peer_hidden")
    def k(u_hbm, idx_hbm, x_hbm, h_hbm, idx_v, x_v, buf0, buf1, h_v, sem0, sem1):
        tok0 = _sc_worker_id() * tpw
        lane = lax.broadcasted_iota(jnp.int32, (NL,), 0)
        zero = jnp.zeros((NL,), F32)

        def compute(item, buf):
            tok = item // ipt
            hvs = [zero for _ in range(R // NL)]
            for rb in range(R // RB):
                def body(g, accs):
                    sub = g // (LANES // GW)
                    base = (g % (LANES // GW)) * GW
                    xs = [plsc.bitcast(x_v[tok, pl.ds(pl.multiple_of(g * GW + jj * NL, NL), NL)], BF16)
                          for jj in range(NV)]
                    out = []
                    for j in range(RB):
                        ps = [plsc.bitcast(buf[rb * RB + j, sub, pl.ds(pl.multiple_of(base + jj * NL, NL), NL)],
                                           BF16) * xs[jj] for jj in range(NV)]
                        hi, lo = _sc_widen_pair((ps[0] + ps[1]) + (ps[2] + ps[3]))
                        out.append((accs[j] + hi) + lo)
                    return tuple(out)
                accs = lax.fori_loop(0, DW // GW, body, tuple(zero for _ in range(RB)))
                for j in range(RB):
                    r = rb * RB + j
                    hvs[r // NL] = jnp.where(lane == (r % NL), jnp.sum(accs[j]), hvs[r // NL])
            for q in range(R // NL):
                h_v[item, pl.ds(q * NL, NL)] = hvs[q]

        @pl.loop(0, tpw // G)
        def _(g):
            t0 = tok0 + g * G
            pltpu.sync_copy(idx_hbm.at[pl.ds(t0 * ipt, nit)], idx_v)
            pltpu.sync_copy(x_hbm.at[pl.ds(t0, G)], x_v)
            _sc_gather_loop(u_hbm, idx_v, (buf0, buf1), (sem0, sem1), nit, compute)
            pltpu.sync_copy(h_v, h_hbm.at[pl.ds(t0 * ipt, nit)])

    return k(u, idx4, x)


def _peer_combine(v, idx4, w4, T, D):
    n_rows, R = idx4.shape
    ipt = n_rows // T
    G = SC_TOKEN_GROUP
    nit = G * ipt
    tpw = T // SC_WORKERS
    NL = SC_LANES
    DW = D // 2
    half = DW // 2
    nv = half // NL
    RG = 4
    tile = (R, DW // LANES, LANES)

    @functools.partial(
        pl.kernel, mesh=_sc_mesh(), compiler_params=pltpu.CompilerParams(needs_layout_passes=False),
        out_type=jax.ShapeDtypeStruct((T, D), F32),
        scratch_types=[pltpu.VMEM((nit, R), jnp.int32), pltpu.VMEM((nit, R), jnp.int32),
                       pltpu.VMEM(tile, jnp.int32), pltpu.VMEM(tile, jnp.int32),
                       pltpu.VMEM((G, D), F32),
                       pltpu.SemaphoreType.DMA, pltpu.SemaphoreType.DMA],
        name="peer_combine")
    def k(v_hbm, idx_hbm, w_hbm, o_hbm, idx_v, w_v, buf0, buf1, out_v, sem0, sem1):
        tok0 = _sc_worker_id() * tpw
        zero = jnp.zeros((NL,), F32)

        def compute(item, buf):
            tok = item // ipt
            item_vec = jnp.full((NL,), item, jnp.int32)
            for hf in range(2):
                def body(rg, accs):
                    ws = [plsc.bitcast(plsc.load_gather(
                        w_v, [item_vec, jnp.full((NL,), rg * RG + rr, jnp.int32)]), BF16) for rr in range(RG)]
                    his, los = [], []
                    for i in range(nv):
                        word = hf * half + i * NL
                        ps = [plsc.bitcast(buf[rg * RG + rr, word // LANES, pl.ds(word % LANES, NL)], BF16) * ws[rr]
                              for rr in range(RG)]
                        hi, lo = _sc_widen_pair((ps[0] + ps[1]) + (ps[2] + ps[3]))
                        his.append(accs[i] + hi)
                        los.append(accs[nv + i] + lo)
                    return tuple(his + los)
                accs = lax.fori_loop(0, R // RG, body, tuple(zero for _ in range(2 * nv)))
                for i in range(nv):
                    word = hf * half + i * NL
                    plsc.addupdate(out_v.at[tok, pl.ds(word, NL)], accs[i])
                    plsc.addupdate(out_v.at[tok, pl.ds(DW + word, NL)], accs[nv + i])

        @pl.loop(0, tpw // G)
        def _(g):
            t0 = tok0 + g * G
            pltpu.sync_copy(idx_hbm.at[pl.ds(t0 * ipt, nit)], idx_v)
            pltpu.sync_copy(w_hbm.at[pl.ds(t0 * ipt, nit)], w_v)

            @pl.loop(0, G)
            def _(t):
                @pl.loop(0, D // NL)
                def _(i):
                    out_v[t, pl.ds(pl.multiple_of(i * NL, NL), NL)] = zero

            _sc_gather_loop(v_hbm, idx_v, (buf0, buf1), (sem0, sem1), nit, compute)
            pltpu.sync_copy(out_v, o_hbm.at[pl.ds(t0, G)])

    return k(v, idx4, w4)


def _act_kernel(h_ref, g_ref, w_ref):
    w = _gelu(h_ref[...]) * g_ref[...]
    w_ref[...] = _pack_bf16_pair(w, w)


def _peer_act(hraw, gate):
    T, n = gate.shape
    tm = PEER_ACT_TM
    spec = pl.BlockSpec((tm, n), lambda i: (i, 0))
    return pl.pallas_call(
        _act_kernel, grid=(T // tm,), in_specs=[spec, spec], out_specs=spec,
        out_shape=jax.ShapeDtypeStruct((T, n), jnp.int32),
        compiler_params=pltpu.CompilerParams(dimension_semantics=("parallel",)),
        name="peer_act",
    )(hraw, gate)


def _final_kernel(h_ref, p_ref, g_ref, y_ref):
    y = h_ref[...] + p_ref[...]
    ms = jnp.mean(y * y, axis=-1, keepdims=True)
    y_ref[...] = y * lax.rsqrt(ms + EPS) * g_ref[...]


def _final(h, po, g):
    T, D = h.shape
    tm = FINAL_TM
    spec = pl.BlockSpec((tm, D), lambda i: (i, 0))
    return pl.pallas_call(
        _final_kernel, grid=(T // tm,),
        in_specs=[spec, spec, pl.BlockSpec((1, D), lambda i: (0, 0))], out_specs=spec,
        out_shape=jax.ShapeDtypeStruct((T, D), F32),
        compiler_params=pltpu.CompilerParams(dimension_semantics=("parallel",)),
        name="final_norm",
    )(h, po, g)


def kernel(x, norm1_g, w_in, hgrn_lb_fwd, hgrn_lb_bwd, hgrn_out_g, diff_lam_q1, diff_lam_k1,
           diff_lam_q2, diff_lam_k2, diff_out_g, rel_bias, w_out, norm2_g, peer_w_q,
           peer_sub_keys, peer_u, peer_v, final_g):
    B, L, D = x.shape
    hw = HGRN_HEADS * HGRN_DK

    w = w_in[LAYER]
    scale = DIFF_HALF ** -0.5
    cols = lambda j: w[:, j * hw:(j + 1) * hw]
    w_r = jnp.concatenate([cols(1), cols(2), cols(0), cols(3), cols(4), cols(5) * scale, cols(6), cols(7)],
                          axis=1).astype(BF16)
    f32 = jnp.float32
    lam_init = 0.8 - 0.6 * math.exp(-0.3 * LAYER)
    lam = (jnp.exp(jnp.sum(diff_lam_q1[LAYER].astype(f32) * diff_lam_k1[LAYER].astype(f32)))
           - jnp.exp(jnp.sum(diff_lam_q2[LAYER].astype(f32) * diff_lam_k2[LAYER].astype(f32))) + lam_init)
    lam = lam.reshape(1, 1)
    bias_ext = _rel_bias_ext(rel_bias, L)
    sk = peer_sub_keys[LAYER].reshape(2 * PEER_HEADS, PEER_NKEYS, PEER_DKEY).astype(BF16)
    wq = peer_w_q[LAYER].reshape(D, -1).astype(BF16)
    wo = w_out[LAYER].astype(BF16)
    def pack_table(tab):
        words = _pack_bf16_pair(tab[:, :D // 2], tab[:, D // 2:])
        return words.reshape(tab.shape[0], D // 2 // LANES, LANES)
    u3, v3 = pack_table(peer_u[LAYER]), pack_table(peer_v[LAYER])

    bc = B // BATCH_CHUNKS
    tc = bc * L
    rows = tc * PEER_HEADS * PEER_TOPK // SC_GATHER_ROWS
    outs = []
    pending = {}
    for c in range(BATCH_CHUNKS + CHUNK_LAG):
        x2 = x[c * bc:(c + 1) * bc].reshape(tc, D) if c < BATCH_CHUNKS else None
        if c >= CHUNK_LAG:
            h, idx4, gate, hraw = pending.pop(c - CHUNK_LAG)
            wts = _peer_act(hraw.reshape(tc, -1), gate)
            if x2 is not None:
                x2, wts = lax.optimization_barrier((x2, wts))
            po = _peer_combine(v3, idx4, wts.reshape(rows, SC_GATHER_ROWS), tc, D)
            outs.append(_final(h, po, final_g[None, :]).reshape(bc, L, D))
        if x2 is not None:
            zf, pb = _inproj(x2, norm1_g[LAYER][None, :], w_r)
            zf3 = zf.reshape(bc, L, -1)
            pb3 = pb.reshape(bc, L, -1)
            o_h = _hgrn(zf3, pb3, hgrn_lb_fwd, hgrn_lb_bwd, hgrn_out_g[LAYER][None, :])
            o_d = _attn(pb3, bias_ext, lam, diff_out_g[LAYER][None, :])
            h, hn, idx, gate = _route(x2, o_h.reshape(tc, -1), o_d.reshape(tc, -1), wo,
                                      norm2_g[LAYER][None, :], wq, sk)
            idx4 = idx.reshape(rows, SC_GATHER_ROWS)
            pending[c] = (h, idx4, gate, _peer_hidden(u3, idx4, hn))
    return jnp.concatenate(outs, axis=0)
```

```python
import functools
import math

import numpy as np
import jax
import jax.numpy as jnp
from jax import lax
from jax.experimental import pallas as pl
from jax.experimental.pallas import tpu as pltpu
from jax.experimental.pallas import tpu_sc as plsc

F32 = jnp.float32
BF16 = jnp.bfloat16
EPS = 1e-6

HGRN_HEADS = 4
HGRN_DK = 128
DIFF_HEADS = 4
DIFF_HALF = 64
REL_BUCKETS = 32
REL_MAX_DIST = 128
PEER_HEADS = 8
PEER_NKEYS = 128
PEER_DKEY = 128
PEER_TOPK = 16
LAYER = 0

LANES = 128
SUBLANES = 8
VMEM_LIMIT = 48 * 1024 * 1024

INPROJ_TM = 512
HG_C = 64
HG_LEVELS = (64, 32, 16, 8, 4, 2)
ATTN_TQ = 256
ROUTE_TM = 256
BATCH_CHUNKS = 8
CHUNK_LAG = 2
PEER_ACT_TM = 2048
FINAL_TM = 512

SC_CORES = 2
SC_SUBCORES = 16
SC_LANES = 16
SC_WORKERS = SC_CORES * SC_SUBCORES
SC_GATHER_ROWS = 64
SC_TOKEN_GROUP = 8


def _dot(a, b):
    return jnp.dot(a, b, preferred_element_type=F32)


def _dot_nt(a, b):
    return lax.dot_general(a, b, (((1,), (1,)), ((), ())), preferred_element_type=F32)


def _dot_tn(a, b):
    return lax.dot_general(a, b, (((0,), (0,)), ((), ())), preferred_element_type=F32)


def _silu(x):
    return x * (1.0 / (1.0 + jnp.exp(-x)))


BF16_HI_MASK = -65536


def _pack_bf16_pair(hi, lo):
    bits = lambda a: lax.bitcast_convert_type(a.astype(BF16).astype(F32), jnp.int32)
    return (bits(hi) & BF16_HI_MASK) | lax.shift_right_logical(bits(lo), 16)


def _inproj_kernel(x_ref, g_ref, w_ref, zf_ref, pb_ref):
    x = x_ref[...]
    ms = jnp.mean(x * x, axis=-1, keepdims=True)
    xn = (x * lax.rsqrt(ms + EPS) * g_ref[...]).astype(BF16)
    nz = zf_ref.shape[1]
    zf_ref[...] = _dot(xn, w_ref[:, 0:nz])
    nb = pb_ref.shape[1]
    step = 1024
    for j in range(nb // step):
        pb_ref[:, j * step:(j + 1) * step] = _dot(
            xn, w_ref[:, nz + j * step: nz + (j + 1) * step]).astype(BF16)


def _inproj(x2, g, w):
    T, D = x2.shape
    N = w.shape[1]
    nz = 2 * HGRN_HEADS * HGRN_DK
    tm = INPROJ_TM
    return pl.pallas_call(
        _inproj_kernel,
        grid=(T // tm,),
        in_specs=[
            pl.BlockSpec((tm, D), lambda i: (i, 0)),
            pl.BlockSpec((1, D), lambda i: (0, 0)),
            pl.BlockSpec((D, N), lambda i: (0, 0)),
        ],
        out_specs=[
            pl.BlockSpec((tm, nz), lambda i: (i, 0)),
            pl.BlockSpec((tm, N - nz), lambda i: (i, 0)),
        ],
        out_shape=[
            jax.ShapeDtypeStruct((T, nz), F32),
            jax.ShapeDtypeStruct((T, N - nz), BF16),
        ],
        compiler_params=pltpu.CompilerParams(
            dimension_semantics=("parallel",), vmem_limit_bytes=VMEM_LIMIT),
        name="inproj",
    )(x2, g, w)


def _hgrn_consts():
    C = HG_C
    r = np.arange(C)
    t = r[:, None]
    u = r[None, :]
    blocks = [u <= t, u > t]
    masks = [np.eye(C, dtype=bool)]
    for B in HG_LEVELS:
        half = B // 2
        a = (r // B) * B
        m = (a + half - 1)[:, None]
        upper = (r - a) >= half
        blocks.append(np.where(upper[:, None], (u > m) & (u <= t), (u > t) & (u <= m)))
        same = a[:, None] == a[None, :]
        masks.append(same & upper[:, None] & (~upper)[None, :])
    m_f = np.concatenate(blocks, 0).astype(np.float32)
    m_b = np.concatenate([b[::-1, ::-1] for b in blocks], 0).astype(np.float32)
    k_f = np.stack(masks).astype(np.float32)
    k_b = np.ascontiguousarray(k_f.transpose(0, 2, 1))
    return m_f, m_b, k_f, k_b


def _hgrn_gates(z, tab):
    tabf = tab.astype(F32)
    e = jnp.exp(tabf - jnp.max(tabf, axis=0, keepdims=True))
    lb = jnp.sum(e[0:LAYER + 1], axis=0, keepdims=True) / jnp.sum(e, axis=0, keepdims=True)
    log_lb = jnp.log(lb)
    log_1m = jnp.log1p(-lb)
    ez = jnp.exp(-jnp.abs(z))
    l1p = jnp.log1p(ez)
    log_sig = jnp.minimum(z, 0.0) - l1p
    c = log_1m + log_sig
    hi = jnp.maximum(log_lb, c)
    lo = jnp.minimum(log_lb, c)
    log_f = hi + jnp.log1p(jnp.exp(lo - hi))
    sig_neg = jnp.where(z >= 0.0, ez, 1.0) / (1.0 + ez)
    k = (1.0 - lb) * sig_neg
    return log_f, k


def _hgrn_kernel(lbf_ref, lbb_ref, og_ref, mf_ref, mb_ref, kf_ref, kb_ref,
                 q_ref, zf_ref, zb_ref, v_ref, g_ref, o_ref, of_s):
    C = HG_C
    L = q_ref.shape[0]
    n = L // C
    dv = v_ref.shape[1]

    def chunk(c, st, m_ref, k_ref, z_ref, tab_ref, forward):
        sl = pl.ds(pl.multiple_of(c * C, C), C)
        qh = q_ref[sl, :].astype(F32)
        q = _silu(qh)
        v = v_ref[sl, :]
        log_f, k = _hgrn_gates(z_ref[sl, :], tab_ref[...])
        lf_hi = log_f.astype(BF16)
        lf_lo = (log_f - lf_hi.astype(F32)).astype(BF16)
        m = m_ref[...]
        e = jnp.exp(_dot(m, lf_hi) + _dot(m, lf_lo))
        e_b = e[0:C]
        e_s = e[C:2 * C]
        dec = e[C - 1:C] if forward else e[0:1]
        o = _dot_nt((q * e_b).astype(BF16), st.astype(BF16))
        a = _dot_nt(q.astype(BF16), k.astype(BF16)) * k_ref[0]
        for l in range(len(HG_LEVELS)):
            e_l = e[(2 + l) * C:(3 + l) * C]
            a = a + _dot_nt((q * e_l).astype(BF16), (k * e_l).astype(BF16)) * k_ref[l + 1]
        o = o + _dot(a.astype(BF16), v)
        st = st * dec + _dot_tn(v, (k * e_s).astype(BF16))
        return sl, o, st

    st0 = jnp.zeros((dv, q_ref.shape[1]), F32)

    def fwd(c, st):
        sl, o, st = chunk(c, st, mf_ref, kf_ref, zf_ref, lbf_ref, True)
        of_s[sl, :] = o
        return st

    lax.fori_loop(0, n, fwd, st0)

    def bwd(i, st):
        c = n - 1 - i
        sl, o, st = chunk(c, st, mb_ref, kb_ref, zb_ref, lbb_ref, False)
        tot = of_s[sl, :] + o
        ms = jnp.mean(tot * tot, axis=-1, keepdims=True)
        y = tot * lax.rsqrt(ms + EPS) * og_ref[...]
        o_ref[sl, :] = (y * _silu(g_ref[sl, :].astype(F32))).astype(o_ref.dtype)
        return st

    lax.fori_loop(0, n, bwd, st0)


def _hgrn(zf3, pb3, lb_f, lb_b, out_g):
    B, L, _ = zf3.shape
    H, dk = HGRN_HEADS, HGRN_DK
    m_f, m_b, k_f, k_b = _hgrn_consts()
    nlev = k_f.shape[0]
    full2 = lambda b, h: (0, 0)
    full3 = lambda b, h: (0, 0, 0)
    seq = lambda off: pl.BlockSpec((None, L, dk), lambda b, h: (b, 0, off + h))
    return pl.pallas_call(
        _hgrn_kernel,
        grid=(B, H),
        in_specs=[
            pl.BlockSpec((lb_f.shape[0], dk), lambda b, h: (0, h)),
            pl.BlockSpec((lb_b.shape[0], dk), lambda b, h: (0, h)),
            pl.BlockSpec((1, dk), full2),
            pl.BlockSpec(m_f.shape, full2),
            pl.BlockSpec(m_b.shape, full2),
            pl.BlockSpec((nlev, HG_C, HG_C), full3),
            pl.BlockSpec((nlev, HG_C, HG_C), full3),
            seq(0),
            seq(0),
            seq(H),
            seq(H),
            seq(2 * H),
        ],
        out_specs=pl.BlockSpec((None, L, dk), lambda b, h: (b, 0, h)),
        out_shape=jax.ShapeDtypeStruct((B, L, H * dk), BF16),
        scratch_shapes=[pltpu.VMEM((L, dk), F32)],
        compiler_params=pltpu.CompilerParams(
            dimension_semantics=("parallel", "parallel"), vmem_limit_bytes=VMEM_LIMIT),
        name="hgrn",
    )(lb_f, lb_b, out_g, jnp.asarray(m_f, BF16), jnp.asarray(m_b, BF16),
      jnp.asarray(k_f), jnp.asarray(k_b), pb3, zf3, zf3, pb3, pb3)


def _t5_bucket(rel):
    nb = REL_BUCKETS // 2
    ret = jnp.where(rel > 0, nb, 0)
    n = jnp.abs(rel)
    max_exact = nb // 2
    nf = jnp.maximum(n, 1).astype(jnp.float32)
    large = max_exact + (jnp.log(nf / max_exact) / math.log(REL_MAX_DIST / max_exact)
                         * (nb - max_exact)).astype(jnp.int32)
    large = jnp.minimum(large, nb - 1)
    return ret + jnp.where(n < max_exact, n, large)


def _rel_bias_ext(rel_bias, L):
    j = jnp.arange(2 * L - LANES, dtype=jnp.int32)
    ql = jnp.arange(LANES, dtype=jnp.int32)
    rel = j[None, :] - (L - LANES) - ql[:, None]
    bucket = _t5_bucket(rel)
    tab = rel_bias.astype(F32)
    out = jnp.zeros((tab.shape[1],) + bucket.shape, F32)
    for b in range(REL_BUCKETS):
        out = jnp.where((bucket == b)[None], tab[b][:, None, None], out)
    return out


def _attn_kernel(lam_ref, q_ref, k_ref, v_ref, bias_ref, og_ref, o_ref, *, lam_init):
    tq = q_ref.shape[0]
    L = k_ref.shape[0]
    qi = pl.program_id(2)
    lam = lam_ref[0, 0]
    q = q_ref[...]
    k = k_ref[...]
    lane = lax.broadcasted_iota(jnp.int32, q.shape, 1)
    zero = jnp.zeros_like(q)
    q0 = jnp.where(lane < DIFF_HALF, q, zero)
    q1 = jnp.where(lane >= DIFF_HALF, q, zero)
    parts = []
    for j in range(tq // LANES):
        off = pl.multiple_of(L - LANES - (qi * tq + j * LANES), LANES)
        parts.append(bias_ref[:, pl.ds(off, L)])
    bias = jnp.concatenate(parts, axis=0) if len(parts) > 1 else parts[0]

    def soft(qm):
        s = _dot_nt(qm, k) + bias
        e = jnp.exp(s - jnp.max(s, axis=-1, keepdims=True))
        return e, 1.0 / jnp.sum(e, axis=-1, keepdims=True)

    e0, r0 = soft(q0)
    e1, r1 = soft(q1)
    w = e0 * r0 - e1 * (lam * r1)
    o = _dot(w.astype(BF16), v_ref[...])
    ms = jnp.mean(o * o, axis=-1, keepdims=True)
    y = o * lax.rsqrt(ms + EPS) * og_ref[...] * (1.0 - lam_init)
    o_ref[...] = y.astype(o_ref.dtype)


def _attn(pb3, bias_ext, lam, out_g):
    B, L, _ = pb3.shape
    H, dh = DIFF_HEADS, 2 * DIFF_HALF
    tq = ATTN_TQ
    base = (3 * HGRN_HEADS * HGRN_DK) // dh
    lam_init = 0.8 - 0.6 * math.exp(-0.3 * LAYER)
    return pl.pallas_call(
        functools.partial(_attn_kernel, lam_init=lam_init),
        grid=(B, H, L // tq),
        in_specs=[
            pl.BlockSpec(memory_space=pltpu.SMEM),
            pl.BlockSpec((None, tq, dh), lambda b, h, i: (b, i, base + h)),
            pl.BlockSpec((None, L, dh), lambda b, h, i: (b, 0, base + H + h)),
            pl.BlockSpec((None, L, dh), lambda b, h, i: (b, 0, base + 2 * H + h)),
            pl.BlockSpec((None, LANES, 2 * L - LANES), lambda b, h, i: (h, 0, 0)),
            pl.BlockSpec((1, dh), lambda b, h, i: (0, 0)),
        ],
        out_specs=pl.BlockSpec((None, tq, dh), lambda b, h, i: (b, i, h)),
        out_shape=jax.ShapeDtypeStruct((B, L, H * dh), BF16),
        compiler_params=pltpu.CompilerParams(
            dimension_semantics=("parallel", "parallel", "parallel"),
            vmem_limit_bytes=VMEM_LIMIT),
        name="attn",
    )(lam, pb3, pb3, pb3, bias_ext, out_g)


def _cand_layout():
    K = PEER_TOPK
    groups = [("a", 0, 0), ("a", 0, 8), ("a", 1, 0), ("a", 2, 0), ("a", 3, 0),
              ("b", 0, 8), ("b", 0, 0), ("b", 1, 0), ("b", 2, 0)]
    seen = set()
    pos, valid = [], []
    for kind, fixed, start in groups:
        for r in range(SUBLANES):
            a, b = (fixed, start + r) if kind == "a" else (start + r, fixed)
            ok = (a + 1) * (b + 1) <= K and (a, b) not in seen
            if ok:
                seen.add((a, b))
            pos.append(a * K + b if ok else K * K + len(pos))
            valid.append(ok)
    assert len(seen) == sum(K // (a + 1) for a in range(K))
    return groups, np.array(pos, np.int32), np.array(valid, bool)


def _extract_topk(s, key, payload, k, big):
    rows = lax.broadcasted_iota(jnp.int32, (k, s.shape[1]), 0)

    def body(j, carry):
        s, vals, pay = carry
        m = jnp.max(s, axis=0, keepdims=True)
        kk = jnp.min(jnp.where(s == m, key, big), axis=0, keepdims=True)
        sel = key == kk
        p = jnp.sum(jnp.where(sel, payload, 0), axis=0, keepdims=True)
        vals = jnp.where(rows == j, m, vals)
        pay = jnp.where(rows == j, p, pay)
        s = jnp.where(sel, -jnp.inf, s)
        return s, vals, pay

    init = (s, jnp.zeros((k, s.shape[1]), F32), jnp.zeros((k, s.shape[1]), jnp.int32))
    _, vals, pay = lax.fori_loop(0, k, body, init)
    return vals, pay


def _route_kernel(x_ref, oh_ref, od_ref, woh_ref, wod_ref, g2_ref, wq_ref, sk_ref, cpos_ref, cmask_ref,
                  h_ref, hn_ref, idx_ref, gate_ref, q_s, tv_s, ti_s, gs_s, is_s, *, groups):
    K = PEER_TOPK
    tm = x_ref.shape[0]
    h = x_ref[...] + _dot(oh_ref[...], woh_ref[...]) + _dot(od_ref[...], wod_ref[...])
    h_ref[...] = h
    hn = h * lax.rsqrt(jnp.mean(h * h, axis=-1, keepdims=True) + EPS) * g2_ref[...]
    half = hn.shape[1] // 2
    hn_ref[...] = _pack_bf16_pair(hn[:, :half], hn[:, half:])
    q_s[...] = _dot(hn.astype(BF16), wq_ref[...]).astype(BF16)

    key_iota = lax.broadcasted_iota(jnp.int32, (PEER_NKEYS, tm), 0)

    def half_topk(hp, _):
        col = pl.multiple_of(hp * PEER_DKEY, PEER_DKEY)
        s = _dot_nt(sk_ref[hp], q_s[:, pl.ds(col, PEER_DKEY)])
        vals, idxs = _extract_topk(s, key_iota, key_iota, K, PEER_NKEYS)
        tv_s[hp] = vals
        ti_s[hp] = idxs
        return 0

    lax.fori_loop(0, 2 * PEER_HEADS, half_topk, 0)

    cpos = cpos_ref[...]
    cmask = cmask_ref[...]

    def head(hd, _):
        s0, s1 = tv_s[2 * hd], tv_s[2 * hd + 1]
        i0, i1 = ti_s[2 * hd] * PEER_NKEYS, ti_s[2 * hd + 1]
        cs, ci = [], []
        for kind, fixed, start in groups:
            if kind == "a":
                cs.append(s0[fixed:fixed + 1] + s1[start:start + SUBLANES])
                ci.append(i0[fixed:fixed + 1] + i1[start:start + SUBLANES])
            else:
                cs.append(s0[start:start + SUBLANES] + s1[fixed:fixed + 1])
                ci.append(i0[start:start + SUBLANES] + i1[fixed:fixed + 1])
        cand = jnp.concatenate(cs, axis=0) + cmask
        cidx = jnp.concatenate(ci, axis=0)
        best, eidx = _extract_topk(cand, cpos, cidx, K, 2 * K * K)
        ex = jnp.exp(best - best[0:1])
        gate = ex / jnp.sum(ex, axis=0, keepdims=True)
        row = pl.ds(pl.multiple_of(hd * K, K), K)
        gs_s[row, :] = gate
        is_s[row, :] = eidx.astype(F32)
        return 0

    lax.fori_loop(0, PEER_HEADS, head, 0)
    gate_ref[...] = gs_s[...].T
    idx_ref[...] = is_s[...].T.astype(jnp.int32)


def _route(x2, oh2, od2, w_out, g2, w_q, sub_keys):
    T, D = x2.shape
    tm = ROUTE_TM
    K = PEER_TOPK
    nh = oh2.shape[1]
    nq = w_q.shape[1]
    npk = PEER_HEADS * K
    groups, pos, valid = _cand_layout()
    ncand = pos.shape[0]
    cpos = jnp.asarray(np.broadcast_to(pos[:, None], (ncand, tm)))
    cmask = jnp.asarray(np.broadcast_to(np.where(valid, 0.0, -np.inf).astype(np.float32)[:, None], (ncand, tm)))
    row = lambda i: (i, 0)
    full2 = lambda i: (0, 0)
    return pl.pallas_call(
        functools.partial(_route_kernel, groups=groups),
        grid=(T // tm,),
        in_specs=[
            pl.BlockSpec((tm, D), row),
            pl.BlockSpec((tm, nh), row),
            pl.BlockSpec((tm, nh), row),
            pl.BlockSpec((nh, D), full2),
            pl.BlockSpec((nh, D), lambda i: (1, 0)),
            pl.BlockSpec((1, D), full2),
            pl.BlockSpec((D, nq), full2),
            pl.BlockSpec(sub_keys.shape, lambda i: (0, 0, 0)),
            pl.BlockSpec((ncand, tm), full2),
            pl.BlockSpec((ncand, tm), full2),
        ],
        out_specs=[
            pl.BlockSpec((tm, D), row),
            pl.BlockSpec((tm, D // 2), row),
            pl.BlockSpec((tm, npk), row),
            pl.BlockSpec((tm, npk), row),
        ],
        out_shape=[
            jax.ShapeDtypeStruct((T, D), F32),
            jax.ShapeDtypeStruct((T, D // 2), jnp.int32),
            jax.ShapeDtypeStruct((T, npk), jnp.int32),
            jax.ShapeDtypeStruct((T, npk), F32),
        ],
        scratch_shapes=[
            pltpu.VMEM((tm, nq), BF16),
            pltpu.VMEM((2 * PEER_HEADS, K, tm), F32),
            pltpu.VMEM((2 * PEER_HEADS, K, tm), jnp.int32),
            pltpu.VMEM((npk, tm), F32),
            pltpu.VMEM((npk, tm), F32),
        ],
        compiler_params=pltpu.CompilerParams(
            dimension_semantics=("parallel",), vmem_limit_bytes=VMEM_LIMIT),
        name="route",
    )(x2, oh2, od2, w_out, w_out, g2, w_q, sub_keys, cpos, cmask)


def _gelu(x):
    return 0.5 * x * (1.0 + lax.erf(x * (1.0 / math.sqrt(2.0))))


def _sc_mesh():
    return plsc.VectorSubcoreMesh(core_axis_name="c", subcore_axis_name="s")


def _sc_worker_id():
    return lax.axis_index("s") * SC_CORES + lax.axis_index("c")


def _sc_widen_pair(s):
    si = plsc.bitcast(s, jnp.int32)
    return plsc.bitcast(si & BF16_HI_MASK, F32), plsc.bitcast(si << 16, F32)


def _sc_gather_loop(tab_hbm, idx_v, bufs, sems, n_items, compute):
    def start(item, b):
        pltpu.async_copy(tab_hbm.at[idx_v.at[item]], bufs[b], sems[b])

    def wait(b):
        pltpu.make_async_copy(tab_hbm.at[idx_v.at[0]], bufs[b], sems[b]).wait()

    start(0, 0)

    @pl.loop(0, n_items // 2)
    def _(i2):
        it = 2 * i2
        start(it + 1, 1)
        wait(0)
        compute(it, bufs[0])

        @pl.when(it + 2 < n_items)
        def _():
            start(it + 2, 0)

        wait(1)
        compute(it + 1, bufs[1])


def _peer_hidden(u, idx4, x):
    T, DW = x.shape
    n_rows, R = idx4.shape
    ipt = n_rows // T
    G = SC_TOKEN_GROUP
    nit = G * ipt
    tpw = T // SC_WORKERS
    NL = SC_LANES
    RB = 8
    NV = 4
    GW = NV * NL
    tile = (R, DW // LANES, LANES)

    @functools.partial(
        pl.kernel, mesh=_sc_mesh(), compiler_params=pltpu.CompilerParams(needs_layout_passes=False),
        out_type=jax.ShapeDtypeStruct((n_rows, R), F32),
        scratch_types=[pltpu.VMEM((nit, R), jnp.int32), pltpu.VMEM((G, DW), jnp.int32),
                       pltpu.VMEM(tile, jnp.int32), pltpu.VMEM(tile, jnp.int32),
                       pltpu.VMEM((nit, R), F32),
                       pltpu.SemaphoreType.DMA, pltpu.SemaphoreType.DMA],
        name="peer_hidden")
    def k(u_hbm, idx_hbm, x_hbm, h_hbm, idx_v, x_v, buf0, buf1, h_v, sem0, sem1):
        tok0 = _sc_worker_id() * tpw
        lane = lax.broadcasted_iota(jnp.int32, (NL,), 0)
        zero = jnp.zeros((NL,), F32)

        def compute(item, buf):
            tok = item // ipt
            hvs = [zero for _ in range(R // NL)]
            for rb in range(R // RB):
                def body(g, accs):
                    sub = g // (LANES // GW)
                    base = (g % (LANES // GW)) * GW
                    xs = [plsc.bitcast(x_v[tok, pl.ds(pl.multiple_of(g * GW + jj * NL, NL), NL)], BF16)
                          for jj in range(NV)]
                    out = []
                    for j in range(RB):
                        ps = [plsc.bitcast(buf[rb * RB + j, sub, pl.ds(pl.multiple_of(base + jj * NL, NL), NL)],
                                           BF16) * xs[jj] for jj in range(NV)]
                        hi, lo = _sc_widen_pair((ps[0] + ps[1]) + (ps[2] + ps[3]))
                        out.append((accs[j] + hi) + lo)
                    return tuple(out)
                accs = lax.fori_loop(0, DW // GW, body, tuple(zero for _ in range(RB)))
                for j in range(RB):
                    r = rb * RB + j
                    hvs[r // NL] = jnp.where(lane == (r % NL), jnp.sum(accs[j]), hvs[r // NL])
            for q in range(R // NL):
                h_v[item, pl.ds(q * NL, NL)] = hvs[q]

        @pl.loop(0, tpw // G)
        def _(g):
            t0 = tok0 + g * G
            pltpu.sync_copy(idx_hbm.at[pl.ds(t0 * ipt, nit)], idx_v)
            pltpu.sync_copy(x_hbm.at[pl.ds(t0, G)], x_v)
            _sc_gather_loop(u_hbm, idx_v, (buf0, buf1), (sem0, sem1), nit, compute)
            pltpu.sync_copy(h_v, h_hbm.at[pl.ds(t0 * ipt, nit)])

    return k(u, idx4, x)


def _peer_combine(v, idx4, w4, T, D):
    n_rows, R = idx4.shape
    ipt = n_rows // T
    G = SC_TOKEN_GROUP
    nit = G * ipt
    tpw = T // SC_WORKERS
    NL = SC_LANES
    DW = D // 2
    half = DW // 2
    nv = half // NL
    RG = 4
    tile = (R, DW // LANES, LANES)

    @functools.partial(
        pl.kernel, mesh=_sc_mesh(), compiler_params=pltpu.CompilerParams(needs_layout_passes=False),
        out_type=jax.ShapeDtypeStruct((T, D), F32),
        scratch_types=[pltpu.VMEM((nit, R), jnp.int32), pltpu.VMEM((nit, R), jnp.int32),
                       pltpu.VMEM(tile, jnp.int32), pltpu.VMEM(tile, jnp.int32),
                       pltpu.VMEM((G, D), F32),
                       pltpu.SemaphoreType.DMA, pltpu.SemaphoreType.DMA],
        name="peer_combine")
    def k(v_hbm, idx_hbm, w_hbm, o_hbm, idx_v, w_v, buf0, buf1, out_v, sem0, sem1):
        tok0 = _sc_worker_id() * tpw
        zero = jnp.zeros((NL,), F32)

        def compute(item, buf):
            tok = item // ipt
            item_vec = jnp.full((NL,), item, jnp.int32)
            for hf in range(2):
                def body(rg, accs):
                    ws = [plsc.bitcast(plsc.load_gather(
                        w_v, [item_vec, jnp.full((NL,), rg * RG + rr, jnp.int32)]), BF16) for rr in range(RG)]
                    his, los = [], []
                    for i in range(nv):
                        word = hf * half + i * NL
                        ps = [plsc.bitcast(buf[rg * RG + rr, word // LANES, pl.ds(word % LANES, NL)], BF16) * ws[rr]
                              for rr in range(RG)]
                        hi, lo = _sc_widen_pair((ps[0] + ps[1]) + (ps[2] + ps[3]))
                        his.append(accs[i] + hi)
                        los.append(accs[nv + i] + lo)
                    return tuple(his + los)
                accs = lax.fori_loop(0, R // RG, body, tuple(zero for _ in range(2 * nv)))
                for i in range(nv):
                    word = hf * half + i * NL
                    plsc.addupdate(out_v.at[tok, pl.ds(word, NL)], accs[i])
                    plsc.addupdate(out_v.at[tok, pl.ds(DW + word, NL)], accs[nv + i])

        @pl.loop(0, tpw // G)
        def _(g):
            t0 = tok0 + g * G
            pltpu.sync_copy(idx_hbm.at[pl.ds(t0 * ipt, nit)], idx_v)
            pltpu.sync_copy(w_hbm.at[pl.ds(t0 * ipt, nit)], w_v)

            @pl.loop(0, G)
            def _(t):
                @pl.loop(0, D // NL)
                def _(i):
                    out_v[t, pl.ds(pl.multiple_of(i * NL, NL), NL)] = zero

            _sc_gather_loop(v_hbm, idx_v, (buf0, buf1), (sem0, sem1), nit, compute)
            pltpu.sync_copy(out_v, o_hbm.at[pl.ds(t0, G)])

    return k(v, idx4, w4)


def _act_kernel(h_ref, g_ref, w_ref):
    w = _gelu(h_ref[...]) * g_ref[...]
    w_ref[...] = _pack_bf16_pair(w, w)


def _peer_act(hraw, gate):
    T, n = gate.shape
    tm = PEER_ACT_TM
    spec = pl.BlockSpec((tm, n), lambda i: (i, 0))
    return pl.pallas_call(
        _act_kernel, grid=(T // tm,), in_specs=[spec, spec], out_specs=spec,
        out_shape=jax.ShapeDtypeStruct((T, n), jnp.int32),
        compiler_params=pltpu.CompilerParams(dimension_semantics=("parallel",)),
        name="peer_act",
    )(hraw, gate)


def _final_kernel(h_ref, p_ref, g_ref, y_ref):
    y = h_ref[...] + p_ref[...]
    ms = jnp.mean(y * y, axis=-1, keepdims=True)
    y_ref[...] = y * lax.rsqrt(ms + EPS) * g_ref[...]


def _final(h, po, g):
    T, D = h.shape
    tm = FINAL_TM
    spec = pl.BlockSpec((tm, D), lambda i: (i, 0))
    return pl.pallas_call(
        _final_kernel, grid=(T // tm,),
        in_specs=[spec, spec, pl.BlockSpec((1, D), lambda i: (0, 0))], out_specs=spec,
        out_shape=jax.ShapeDtypeStruct((T, D), F32),
        compiler_params=pltpu.CompilerParams(dimension_semantics=("parallel",)),
        name="final_norm",
    )(h, po, g)


def kernel(x, norm1_g, w_in, hgrn_lb_fwd, hgrn_lb_bwd, hgrn_out_g, diff_lam_q1, diff_lam_k1,
           diff_lam_q2, diff_lam_k2, diff_out_g, rel_bias, w_out, norm2_g, peer_w_q,
           peer_sub_keys, peer_u, peer_v, final_g):
    B, L, D = x.shape
    hw = HGRN_HEADS * HGRN_DK

    w = w_in[LAYER]
    scale = DIFF_HALF ** -0.5
    cols = lambda j: w[:, j * hw:(j + 1) * hw]
    w_r = jnp.concatenate([cols(1), cols(2), cols(0), cols(3), cols(4), cols(5) * scale, cols(6), cols(7)],
                          axis=1).astype(BF16)
    f32 = jnp.float32
    lam_init = 0.8 - 0.6 * math.exp(-0.3 * LAYER)
    lam = (jnp.exp(jnp.sum(diff_lam_q1[LAYER].astype(f32) * diff_lam_k1[LAYER].astype(f32)))
           - jnp.exp(jnp.sum(diff_lam_q2[LAYER].astype(f32) * diff_lam_k2[LAYER].astype(f32))) + lam_init)
    lam = lam.reshape(1, 1)
    bias_ext = _rel_bias_ext(rel_bias, L)
    sk = peer_sub_keys[LAYER].reshape(2 * PEER_HEADS, PEER_NKEYS, PEER_DKEY).astype(BF16)
    wq = peer_w_q[LAYER].reshape(D, -1).astype(BF16)
    wo = w_out[LAYER].astype(BF16)
    def pack_table(tab):
        words = _pack_bf16_pair(tab[:, :D // 2], tab[:, D // 2:])
        return words.reshape(tab.shape[0], D // 2 // LANES, LANES)
    u3, v3 = pack_table(peer_u[LAYER]), pack_table(peer_v[LAYER])

    bc = B // BATCH_CHUNKS
    tc = bc * L
    rows = tc * PEER_HEADS * PEER_TOPK // SC_GATHER_ROWS
    outs = []
    pending = {}
    combined = None
    for c in range(BATCH_CHUNKS + CHUNK_LAG + 1):
        x2 = x[c * bc:(c + 1) * bc].reshape(tc, D) if c < BATCH_CHUNKS else None
        if combined is not None:
            h_prev, po = combined
            if x2 is not None:
                x2, po = lax.optimization_barrier((x2, po))
            outs.append(_final(h_prev, po, final_g[None, :]).reshape(bc, L, D))
            combined = None
        if CHUNK_LAG <= c < BATCH_CHUNKS + CHUNK_LAG:
            h, idx4, gate, hraw = pending.pop(c - CHUNK_LAG)
            wts = _peer_act(hraw.reshape(tc, -1), gate)
            if x2 is not None:
                x2, wts = lax.optimization_barrier((x2, wts))
            combined = (h, _peer_combine(v3, idx4, wts.reshape(rows, SC_GATHER_ROWS), tc, D))
        if x2 is not None:
            zf, pb = _inproj(x2, norm1_g[LAYER][None, :], w_r)
            zf3 = zf.reshape(bc, L, -1)
            pb3 = pb.reshape(bc, L, -1)
            o_h = _hgrn(zf3, pb3, hgrn_lb_fwd, hgrn_lb_bwd, hgrn_out_g[LAYER][None, :])
            o_d = _attn(pb3, bias_ext, lam, diff_out_g[LAYER][None, :])
            h, hn, idx, gate = _route(x2, o_h.reshape(tc, -1), o_d.reshape(tc, -1), wo,
                                      norm2_g[LAYER][None, :], wq, sk)
            idx4 = idx.reshape(rows, SC_GATHER_ROWS)
            pending[c] = (h, idx4, gate, _peer_hidden(u3, idx4, hn))
    return jnp.concatenate(outs, axis=0)
```

```python
import functools
import math

import numpy as np
import jax
import jax.numpy as jnp
from jax import lax
from jax.experimental import pallas as pl
from jax.experimental.pallas import tpu as pltpu
from jax.experimental.pallas import tpu_sc as plsc

F32 = jnp.float32
BF16 = jnp.bfloat16
EPS = 1e-6

HGRN_HEADS = 4
HGRN_DK = 128
DIFF_HEADS = 4
DIFF_HALF = 64
REL_BUCKETS = 32
REL_MAX_DIST = 128
PEER_HEADS = 8
PEER_NKEYS = 128
PEER_DKEY = 128
PEER_TOPK = 16
LAYER = 0

LANES = 128
SUBLANES = 8
VMEM_LIMIT = 48 * 1024 * 1024

INPROJ_TM = 512
HG_C = 64
HG_LEVELS = (64, 32, 16, 8, 4, 2)
ATTN_TQ = 256
ROUTE_TM = 256
BATCH_CHUNKS = 16
CHUNK_LAG = 2
PEER_ACT_TM = 2048
FINAL_TM = 512

SC_CORES = 2
SC_SUBCORES = 16
SC_LANES = 16
SC_WORKERS = SC_CORES * SC_SUBCORES
SC_GATHER_ROWS = 64
SC_TOKEN_GROUP = 8


def _dot(a, b):
    return jnp.dot(a, b, preferred_element_type=F32)


def _dot_nt(a, b):
    return lax.dot_general(a, b, (((1,), (1,)), ((), ())), preferred_element_type=F32)


def _dot_tn(a, b):
    return lax.dot_general(a, b, (((0,), (0,)), ((), ())), preferred_element_type=F32)


def _silu(x):
    return x * (1.0 / (1.0 + jnp.exp(-x)))


BF16_HI_MASK = -65536


def _pack_bf16_pair(hi, lo):
    bits = lambda a: lax.bitcast_convert_type(a.astype(BF16).astype(F32), jnp.int32)
    return (bits(hi) & BF16_HI_MASK) | lax.shift_right_logical(bits(lo), 16)


def _inproj_kernel(x_ref, g_ref, w_ref, zf_ref, pb_ref):
    x = x_ref[...]
    ms = jnp.mean(x * x, axis=-1, keepdims=True)
    xn = (x * lax.rsqrt(ms + EPS) * g_ref[...]).astype(BF16)
    nz = zf_ref.shape[1]
    zf_ref[...] = _dot(xn, w_ref[:, 0:nz])
    nb = pb_ref.shape[1]
    step = 1024
    for j in range(nb // step):
        pb_ref[:, j * step:(j + 1) * step] = _dot(
            xn, w_ref[:, nz + j * step: nz + (j + 1) * step]).astype(BF16)


def _inproj(x2, g, w):
    T, D = x2.shape
    N = w.shape[1]
    nz = 2 * HGRN_HEADS * HGRN_DK
    tm = INPROJ_TM
    return pl.pallas_call(
        _inproj_kernel,
        grid=(T // tm,),
        in_specs=[
            pl.BlockSpec((tm, D), lambda i: (i, 0)),
            pl.BlockSpec((1, D), lambda i: (0, 0)),
            pl.BlockSpec((D, N), lambda i: (0, 0)),
        ],
        out_specs=[
            pl.BlockSpec((tm, nz), lambda i: (i, 0)),
            pl.BlockSpec((tm, N - nz), lambda i: (i, 0)),
        ],
        out_shape=[
            jax.ShapeDtypeStruct((T, nz), F32),
            jax.ShapeDtypeStruct((T, N - nz), BF16),
        ],
        compiler_params=pltpu.CompilerParams(
            dimension_semantics=("parallel",), vmem_limit_bytes=VMEM_LIMIT),
        name="inproj",
    )(x2, g, w)


def _hgrn_consts():
    C = HG_C
    r = np.arange(C)
    t = r[:, None]
    u = r[None, :]
    blocks = [u <= t, u > t]
    masks = [np.eye(C, dtype=bool)]
    for B in HG_LEVELS:
        half = B // 2
        a = (r // B) * B
        m = (a + half - 1)[:, None]
        upper = (r - a) >= half
        blocks.append(np.where(upper[:, None], (u > m) & (u <= t), (u > t) & (u <= m)))
        same = a[:, None] == a[None, :]
        masks.append(same & upper[:, None] & (~upper)[None, :])
    m_f = np.concatenate(blocks, 0).astype(np.float32)
    m_b = np.concatenate([b[::-1, ::-1] for b in blocks], 0).astype(np.float32)
    k_f = np.stack(masks).astype(np.float32)
    k_b = np.ascontiguousarray(k_f.transpose(0, 2, 1))
    return m_f, m_b, k_f, k_b


def _hgrn_gates(z, tab):
    tabf = tab.astype(F32)
    e = jnp.exp(tabf - jnp.max(tabf, axis=0, keepdims=True))
    lb = jnp.sum(e[0:LAYER + 1], axis=0, keepdims=True) / jnp.sum(e, axis=0, keepdims=True)
    log_lb = jnp.log(lb)
    log_1m = jnp.log1p(-lb)
    ez = jnp.exp(-jnp.abs(z))
    l1p = jnp.log1p(ez)
    log_sig = jnp.minimum(z, 0.0) - l1p
    c = log_1m + log_sig
    hi = jnp.maximum(log_lb, c)
    lo = jnp.minimum(log_lb, c)
    log_f = hi + jnp.log1p(jnp.exp(lo - hi))
    sig_neg = jnp.where(z >= 0.0, ez, 1.0) / (1.0 + ez)
    k = (1.0 - lb) * sig_neg
    return log_f, k


def _hgrn_kernel(lbf_ref, lbb_ref, og_ref, mf_ref, mb_ref, kf_ref, kb_ref,
                 q_ref, zf_ref, zb_ref, v_ref, g_ref, o_ref, of_s):
    C = HG_C
    L = q_ref.shape[0]
    n = L // C
    dv = v_ref.shape[1]

    def chunk(c, st, m_ref, k_ref, z_ref, tab_ref, forward):
        sl = pl.ds(pl.multiple_of(c * C, C), C)
        qh = q_ref[sl, :].astype(F32)
        q = _silu(qh)
        v = v_ref[sl, :]
        log_f, k = _hgrn_gates(z_ref[sl, :], tab_ref[...])
        lf_hi = log_f.astype(BF16)
        lf_lo = (log_f - lf_hi.astype(F32)).astype(BF16)
        m = m_ref[...]
        e = jnp.exp(_dot(m, lf_hi) + _dot(m, lf_lo))
        e_b = e[0:C]
        e_s = e[C:2 * C]
        dec = e[C - 1:C] if forward else e[0:1]
        o = _dot_nt((q * e_b).astype(BF16), st.astype(BF16))
        a = _dot_nt(q.astype(BF16), k.astype(BF16)) * k_ref[0]
        for l in range(len(HG_LEVELS)):
            e_l = e[(2 + l) * C:(3 + l) * C]
            a = a + _dot_nt((q * e_l).astype(BF16), (k * e_l).astype(BF16)) * k_ref[l + 1]
        o = o + _dot(a.astype(BF16), v)
        st = st * dec + _dot_tn(v, (k * e_s).astype(BF16))
        return sl, o, st

    st0 = jnp.zeros((dv, q_ref.shape[1]), F32)

    def fwd(c, st):
        sl, o, st = chunk(c, st, mf_ref, kf_ref, zf_ref, lbf_ref, True)
        of_s[sl, :] = o
        return st

    lax.fori_loop(0, n, fwd, st0)

    def bwd(i, st):
        c = n - 1 - i
        sl, o, st = chunk(c, st, mb_ref, kb_ref, zb_ref, lbb_ref, False)
        tot = of_s[sl, :] + o
        ms = jnp.mean(tot * tot, axis=-1, keepdims=True)
        y = tot * lax.rsqrt(ms + EPS) * og_ref[...]
        o_ref[sl, :] = (y * _silu(g_ref[sl, :].astype(F32))).astype(o_ref.dtype)
        return st

    lax.fori_loop(0, n, bwd, st0)


def _hgrn(zf3, pb3, lb_f, lb_b, out_g):
    B, L, _ = zf3.shape
    H, dk = HGRN_HEADS, HGRN_DK
    m_f, m_b, k_f, k_b = _hgrn_consts()
    nlev = k_f.shape[0]
    full2 = lambda b, h: (0, 0)
    full3 = lambda b, h: (0, 0, 0)
    seq = lambda off: pl.BlockSpec((None, L, dk), lambda b, h: (b, 0, off + h))
    return pl.pallas_call(
        _hgrn_kernel,
        grid=(B, H),
        in_specs=[
            pl.BlockSpec((lb_f.shape[0], dk), lambda b, h: (0, h)),
            pl.BlockSpec((lb_b.shape[0], dk), lambda b, h: (0, h)),
            pl.BlockSpec((1, dk), full2),
            pl.BlockSpec(m_f.shape, full2),
            pl.BlockSpec(m_b.shape, full2),
            pl.BlockSpec((nlev, HG_C, HG_C), full3),
            pl.BlockSpec((nlev, HG_C, HG_C), full3),
            seq(0),
            seq(0),
            seq(H),
            seq(H),
            seq(2 * H),
        ],
        out_specs=pl.BlockSpec((None, L, dk), lambda b, h: (b, 0, h)),
        out_shape=jax.ShapeDtypeStruct((B, L, H * dk), BF16),
        scratch_shapes=[pltpu.VMEM((L, dk), F32)],
        compiler_params=pltpu.CompilerParams(
            dimension_semantics=("parallel", "parallel"), vmem_limit_bytes=VMEM_LIMIT),
        name="hgrn",
    )(lb_f, lb_b, out_g, jnp.asarray(m_f, BF16), jnp.asarray(m_b, BF16),
      jnp.asarray(k_f), jnp.asarray(k_b), pb3, zf3, zf3, pb3, pb3)


def _t5_bucket(rel):
    nb = REL_BUCKETS // 2
    ret = jnp.where(rel > 0, nb, 0)
    n = jnp.abs(rel)
    max_exact = nb // 2
    nf = jnp.maximum(n, 1).astype(jnp.float32)
    large = max_exact + (jnp.log(nf / max_exact) / math.log(REL_MAX_DIST / max_exact)
                         * (nb - max_exact)).astype(jnp.int32)
    large = jnp.minimum(large, nb - 1)
    return ret + jnp.where(n < max_exact, n, large)


def _rel_bias_ext(rel_bias, L):
    j = jnp.arange(2 * L - LANES, dtype=jnp.int32)
    ql = jnp.arange(LANES, dtype=jnp.int32)
    rel = j[None, :] - (L - LANES) - ql[:, None]
    bucket = _t5_bucket(rel)
    tab = rel_bias.astype(F32)
    out = jnp.zeros((tab.shape[1],) + bucket.shape, F32)
    for b in range(REL_BUCKETS):
        out = jnp.where((bucket == b)[None], tab[b][:, None, None], out)
    return out


def _attn_kernel(lam_ref, q_ref, k_ref, v_ref, bias_ref, og_ref, o_ref, *, lam_init):
    tq = q_ref.shape[0]
    L = k_ref.shape[0]
    qi = pl.program_id(2)
    lam = lam_ref[0, 0]
    q = q_ref[...]
    k = k_ref[...]
    lane = lax.broadcasted_iota(jnp.int32, q.shape, 1)
    zero = jnp.zeros_like(q)
    q0 = jnp.where(lane < DIFF_HALF, q, zero)
    q1 = jnp.where(lane >= DIFF_HALF, q, zero)
    parts = []
    for j in range(tq // LANES):
        off = pl.multiple_of(L - LANES - (qi * tq + j * LANES), LANES)
        parts.append(bias_ref[:, pl.ds(off, L)])
    bias = jnp.concatenate(parts, axis=0) if len(parts) > 1 else parts[0]

    def soft(qm):
        s = _dot_nt(qm, k) + bias
        e = jnp.exp(s - jnp.max(s, axis=-1, keepdims=True))
        return e, 1.0 / jnp.sum(e, axis=-1, keepdims=True)

    e0, r0 = soft(q0)
    e1, r1 = soft(q1)
    w = e0 * r0 - e1 * (lam * r1)
    o = _dot(w.astype(BF16), v_ref[...])
    ms = jnp.mean(o * o, axis=-1, keepdims=True)
    y = o * lax.rsqrt(ms + EPS) * og_ref[...] * (1.0 - lam_init)
    o_ref[...] = y.astype(o_ref.dtype)


def _attn(pb3, bias_ext, lam, out_g):
    B, L, _ = pb3.shape
    H, dh = DIFF_HEADS, 2 * DIFF_HALF
    tq = ATTN_TQ
    base = (3 * HGRN_HEADS * HGRN_DK) // dh
    lam_init = 0.8 - 0.6 * math.exp(-0.3 * LAYER)
    return pl.pallas_call(
        functools.partial(_attn_kernel, lam_init=lam_init),
        grid=(B, H, L // tq),
        in_specs=[
            pl.BlockSpec(memory_space=pltpu.SMEM),
            pl.BlockSpec((None, tq, dh), lambda b, h, i: (b, i, base + h)),
            pl.BlockSpec((None, L, dh), lambda b, h, i: (b, 0, base + H + h)),
            pl.BlockSpec((None, L, dh), lambda b, h, i: (b, 0, base + 2 * H + h)),
            pl.BlockSpec((None, LANES, 2 * L - LANES), lambda b, h, i: (h, 0, 0)),
            pl.BlockSpec((1, dh), lambda b, h, i: (0, 0)),
        ],
        out_specs=pl.BlockSpec((None, tq, dh), lambda b, h, i: (b, i, h)),
        out_shape=jax.ShapeDtypeStruct((B, L, H * dh), BF16),
        compiler_params=pltpu.CompilerParams(
            dimension_semantics=("parallel", "parallel", "parallel"),
            vmem_limit_bytes=VMEM_LIMIT),
        name="attn",
    )(lam, pb3, pb3, pb3, bias_ext, out_g)


def _cand_layout():
    K = PEER_TOPK
    groups = [("a", 0, 0), ("a", 0, 8), ("a", 1, 0), ("a", 2, 0), ("a", 3, 0),
              ("b", 0, 8), ("b", 0, 0), ("b", 1, 0), ("b", 2, 0)]
    seen = set()
    pos, valid = [], []
    for kind, fixed, start in groups:
        for r in range(SUBLANES):
            a, b = (fixed, start + r) if kind == "a" else (start + r, fixed)
            ok = (a + 1) * (b + 1) <= K and (a, b) not in seen
            if ok:
                seen.add((a, b))
            pos.append(a * K + b if ok else K * K + len(pos))
            valid.append(ok)
    assert len(seen) == sum(K // (a + 1) for a in range(K))
    return groups, np.array(pos, np.int32), np.array(valid, bool)


def _extract_topk(s, key, payload, k, big):
    rows = lax.broadcasted_iota(jnp.int32, (k, s.shape[1]), 0)

    def body(j, carry):
        s, vals, pay = carry
        m = jnp.max(s, axis=0, keepdims=True)
        kk = jnp.min(jnp.where(s == m, key, big), axis=0, keepdims=True)
        sel = key == kk
        p = jnp.sum(jnp.where(sel, payload, 0), axis=0, keepdims=True)
        vals = jnp.where(rows == j, m, vals)
        pay = jnp.where(rows == j, p, pay)
        s = jnp.where(sel, -jnp.inf, s)
        return s, vals, pay

    init = (s, jnp.zeros((k, s.shape[1]), F32), jnp.zeros((k, s.shape[1]), jnp.int32))
    _, vals, pay = lax.fori_loop(0, k, body, init)
    return vals, pay


def _route_kernel(x_ref, oh_ref, od_ref, woh_ref, wod_ref, g2_ref, wq_ref, sk_ref, cpos_ref, cmask_ref,
                  h_ref, hn_ref, idx_ref, gate_ref, q_s, tv_s, ti_s, gs_s, is_s, *, groups):
    K = PEER_TOPK
    tm = x_ref.shape[0]
    h = x_ref[...] + _dot(oh_ref[...], woh_ref[...]) + _dot(od_ref[...], wod_ref[...])
    h_ref[...] = h
    hn = h * lax.rsqrt(jnp.mean(h * h, axis=-1, keepdims=True) + EPS) * g2_ref[...]
    half = hn.shape[1] // 2
    hn_ref[...] = _pack_bf16_pair(hn[:, :half], hn[:, half:])
    q_s[...] = _dot(hn.astype(BF16), wq_ref[...]).astype(BF16)

    key_iota = lax.broadcasted_iota(jnp.int32, (PEER_NKEYS, tm), 0)

    def half_topk(hp, _):
        col = pl.multiple_of(hp * PEER_DKEY, PEER_DKEY)
        s = _dot_nt(sk_ref[hp], q_s[:, pl.ds(col, PEER_DKEY)])
        vals, idxs = _extract_topk(s, key_iota, key_iota, K, PEER_NKEYS)
        tv_s[hp] = vals
        ti_s[hp] = idxs
        return 0

    lax.fori_loop(0, 2 * PEER_HEADS, half_topk, 0)

    cpos = cpos_ref[...]
    cmask = cmask_ref[...]

    def head(hd, _):
        s0, s1 = tv_s[2 * hd], tv_s[2 * hd + 1]
        i0, i1 = ti_s[2 * hd] * PEER_NKEYS, ti_s[2 * hd + 1]
        cs, ci = [], []
        for kind, fixed, start in groups:
            if kind == "a":
                cs.append(s0[fixed:fixed + 1] + s1[start:start + SUBLANES])
                ci.append(i0[fixed:fixed + 1] + i1[start:start + SUBLANES])
            else:
                cs.append(s0[start:start + SUBLANES] + s1[fixed:fixed + 1])
                ci.append(i0[start:start + SUBLANES] + i1[fixed:fixed + 1])
        cand = jnp.concatenate(cs, axis=0) + cmask
        cidx = jnp.concatenate(ci, axis=0)
        best, eidx = _extract_topk(cand, cpos, cidx, K, 2 * K * K)
        ex = jnp.exp(best - best[0:1])
        gate = ex / jnp.sum(ex, axis=0, keepdims=True)
        row = pl.ds(pl.multiple_of(hd * K, K), K)
        gs_s[row, :] = gate
        is_s[row, :] = eidx.astype(F32)
        return 0

    lax.fori_loop(0, PEER_HEADS, head, 0)
    gate_ref[...] = gs_s[...].T
    idx_ref[...] = is_s[...].T.astype(jnp.int32)


def _route(x2, oh2, od2, w_out, g2, w_q, sub_keys):
    T, D = x2.shape
    tm = ROUTE_TM
    K = PEER_TOPK
    nh = oh2.shape[1]
    nq = w_q.shape[1]
    npk = PEER_HEADS * K
    groups, pos, valid = _cand_layout()
    ncand = pos.shape[0]
    cpos = jnp.asarray(np.broadcast_to(pos[:, None], (ncand, tm)))
    cmask = jnp.asarray(np.broadcast_to(np.where(valid, 0.0, -np.inf).astype(np.float32)[:, None], (ncand, tm)))
    row = lambda i: (i, 0)
    full2 = lambda i: (0, 0)
    return pl.pallas_call(
        functools.partial(_route_kernel, groups=groups),
        grid=(T // tm,),
        in_specs=[
            pl.BlockSpec((tm, D), row),
            pl.BlockSpec((tm, nh), row),
            pl.BlockSpec((tm, nh), row),
            pl.BlockSpec((nh, D), full2),
            pl.BlockSpec((nh, D), lambda i: (1, 0)),
            pl.BlockSpec((1, D), full2),
            pl.BlockSpec((D, nq), full2),
            pl.BlockSpec(sub_keys.shape, lambda i: (0, 0, 0)),
            pl.BlockSpec((ncand, tm), full2),
            pl.BlockSpec((ncand, tm), full2),
        ],
        out_specs=[
            pl.BlockSpec((tm, D), row),
            pl.BlockSpec((tm, D // 2), row),
            pl.BlockSpec((tm, npk), row),
            pl.BlockSpec((tm, npk), row),
        ],
        out_shape=[
            jax.ShapeDtypeStruct((T, D), F32),
            jax.ShapeDtypeStruct((T, D // 2), jnp.int32),
            jax.ShapeDtypeStruct((T, npk), jnp.int32),
            jax.ShapeDtypeStruct((T, npk), F32),
        ],
        scratch_shapes=[
            pltpu.VMEM((tm, nq), BF16),
            pltpu.VMEM((2 * PEER_HEADS, K, tm), F32),
            pltpu.VMEM((2 * PEER_HEADS, K, tm), jnp.int32),
            pltpu.VMEM((npk, tm), F32),
            pltpu.VMEM((npk, tm), F32),
        ],
        compiler_params=pltpu.CompilerParams(
            dimension_semantics=("parallel",), vmem_limit_bytes=VMEM_LIMIT),
        name="route",
    )(x2, oh2, od2, w_out, w_out, g2, w_q, sub_keys, cpos, cmask)


def _gelu(x):
    return 0.5 * x * (1.0 + lax.erf(x * (1.0 / math.sqrt(2.0))))


def _sc_mesh():
    return plsc.VectorSubcoreMesh(core_axis_name="c", subcore_axis_name="s")


def _sc_worker_id():
    return lax.axis_index("s") * SC_CORES + lax.axis_index("c")


def _sc_widen_pair(s):
    si = plsc.bitcast(s, jnp.int32)
    return plsc.bitcast(si & BF16_HI_MASK, F32), plsc.bitcast(si << 16, F32)


def _sc_gather_loop(tab_hbm, idx_v, bufs, sems, n_items, compute):
    def start(item, b):
        pltpu.async_copy(tab_hbm.at[idx_v.at[item]], bufs[b], sems[b])

    def wait(b):
        pltpu.make_async_copy(tab_hbm.at[idx_v.at[0]], bufs[b], sems[b]).wait()

    start(0, 0)

    @pl.loop(0, n_items // 2)
    def _(i2):
        it = 2 * i2
        start(it + 1, 1)
        wait(0)
        compute(it, bufs[0])

        @pl.when(it + 2 < n_items)
        def _():
            start(it + 2, 0)

        wait(1)
        compute(it + 1, bufs[1])


def _peer_hidden(u, idx4, x):
    T, DW = x.shape
    n_rows, R = idx4.shape
    ipt = n_rows // T
    G = SC_TOKEN_GROUP
    nit = G * ipt
    tpw = T // SC_WORKERS
    NL = SC_LANES
    RB = 8
    NV = 4
    GW = NV * NL
    tile = (R, DW // LANES, LANES)

    @functools.partial(
        pl.kernel, mesh=_sc_mesh(), compiler_params=pltpu.CompilerParams(needs_layout_passes=False),
        out_type=jax.ShapeDtypeStruct((n_rows, R), F32),
        scratch_types=[pltpu.VMEM((nit, R), jnp.int32), pltpu.VMEM((G, DW), jnp.int32),
                       pltpu.VMEM(tile, jnp.int32), pltpu.VMEM(tile, jnp.int32),
                       pltpu.VMEM((nit, R), F32),
                       pltpu.SemaphoreType.DMA, pltpu.SemaphoreType.DMA],
        name="peer_hidden")
    def k(u_hbm, idx_hbm, x_hbm, h_hbm, idx_v, x_v, buf0, buf1, h_v, sem0, sem1):
        tok0 = _sc_worker_id() * tpw
        lane = lax.broadcasted_iota(jnp.int32, (NL,), 0)
        zero = jnp.zeros((NL,), F32)

        def compute(item, buf):
            tok = item // ipt
            hvs = [zero for _ in range(R // NL)]
            for rb in range(R // RB):
                def body(g, accs):
                    sub = g // (LANES // GW)
                    base = (g % (LANES // GW)) * GW
                    xs = [plsc.bitcast(x_v[tok, pl.ds(pl.multiple_of(g * GW + jj * NL, NL), NL)], BF16)
                          for jj in range(NV)]
                    out = []
                    for j in range(RB):
                        ps = [plsc.bitcast(buf[rb * RB + j, sub, pl.ds(pl.multiple_of(base + jj * NL, NL), NL)],
                                           BF16) * xs[jj] for jj in range(NV)]
                        hi, lo = _sc_widen_pair((ps[0] + ps[1]) + (ps[2] + ps[3]))
                        out.append((accs[j] + hi) + lo)
                    return tuple(out)
                accs = lax.fori_loop(0, DW // GW, body, tuple(zero for _ in range(RB)))
                for j in range(RB):
                    r = rb * RB + j
                    hvs[r // NL] = jnp.where(lane == (r % NL), jnp.sum(accs[j]), hvs[r // NL])
            for q in range(R // NL):
                h_v[item, pl.ds(q * NL, NL)] = hvs[q]

        @pl.loop(0, tpw // G)
        def _(g):
            t0 = tok0 + g * G
            pltpu.sync_copy(idx_hbm.at[pl.ds(t0 * ipt, nit)], idx_v)
            pltpu.sync_copy(x_hbm.at[pl.ds(t0, G)], x_v)
            _sc_gather_loop(u_hbm, idx_v, (buf0, buf1), (sem0, sem1), nit, compute)
            pltpu.sync_copy(h_v, h_hbm.at[pl.ds(t0 * ipt, nit)])

    return k(u, idx4, x)


def _peer_combine(v, idx4, w4, T, D):
    n_rows, R = idx4.shape
    ipt = n_rows // T
    G = SC_TOKEN_GROUP
    nit = G * ipt
    tpw = T // SC_WORKERS
    NL = SC_LANES
    DW = D // 2
    half = DW // 2
    nv = half // NL
    RG = 4
    tile = (R, DW // LANES, LANES)

    @functools.partial(
        pl.kernel, mesh=_sc_mesh(), compiler_params=pltpu.CompilerParams(needs_layout_passes=False),
        out_type=jax.ShapeDtypeStruct((T, D), F32),
        scratch_types=[pltpu.VMEM((nit, R), jnp.int32), pltpu.VMEM((nit, R), jnp.int32),
                       pltpu.VMEM(tile, jnp.int32), pltpu.VMEM(tile, jnp.int32),
                       pltpu.VMEM((G, D), F32),
                       pltpu.SemaphoreType.DMA, pltpu.SemaphoreType.DMA],
        name="peer_combine")
    def k(v_hbm, idx_hbm, w_hbm, o_hbm, idx_v, w_v, buf0, buf1, out_v, sem0, sem1):
        tok0 = _sc_worker_id() * tpw
        zero = jnp.zeros((NL,), F32)

        def compute(item, buf):
            tok = item // ipt
            item_vec = jnp.full((NL,), item, jnp.int32)
            for hf in range(2):
                def body(rg, accs):
                    ws = [plsc.bitcast(plsc.load_gather(
                        w_v, [item_vec, jnp.full((NL,), rg * RG + rr, jnp.int32)]), BF16) for rr in range(RG)]
                    his, los = [], []
                    for i in range(nv):
                        word = hf * half + i * NL
                        ps = [plsc.bitcast(buf[rg * RG + rr, word // LANES, pl.ds(word % LANES, NL)], BF16) * ws[rr]
                              for rr in range(RG)]
                        hi, lo = _sc_widen_pair((ps[0] + ps[1]) + (ps[2] + ps[3]))
                        his.append(accs[i] + hi)
                        los.append(accs[nv + i] + lo)
                    return tuple(his + los)
                accs = lax.fori_loop(0, R // RG, body, tuple(zero for _ in range(2 * nv)))
                for i in range(nv):
                    word = hf * half + i * NL
                    plsc.addupdate(out_v.at[tok, pl.ds(word, NL)], accs[i])
                    plsc.addupdate(out_v.at[tok, pl.ds(DW + word, NL)], accs[nv + i])

        @pl.loop(0, tpw // G)
        def _(g):
            t0 = tok0 + g * G
            pltpu.sync_copy(idx_hbm.at[pl.ds(t0 * ipt, nit)], idx_v)
            pltpu.sync_copy(w_hbm.at[pl.ds(t0 * ipt, nit)], w_v)

            @pl.loop(0, G)
            def _(t):
                @pl.loop(0, D // NL)
                def _(i):
                    out_v[t, pl.ds(pl.multiple_of(i * NL, NL), NL)] = zero

            _sc_gather_loop(v_hbm, idx_v, (buf0, buf1), (sem0, sem1), nit, compute)
            pltpu.sync_copy(out_v, o_hbm.at[pl.ds(t0, G)])

    return k(v, idx4, w4)


def _act_kernel(h_ref, g_ref, w_ref):
    w = _gelu(h_ref[...]) * g_ref[...]
    w_ref[...] = _pack_bf16_pair(w, w)


def _peer_act(hraw, gate):
    T, n = gate.shape
    tm = PEER_ACT_TM
    spec = pl.BlockSpec((tm, n), lambda i: (i, 0))
    return pl.pallas_call(
        _act_kernel, grid=(T // tm,), in_specs=[spec, spec], out_specs=spec,
        out_shape=jax.ShapeDtypeStruct((T, n), jnp.int32),
        compiler_params=pltpu.CompilerParams(dimension_semantics=("parallel",)),
        name="peer_act",
    )(hraw, gate)


def _final_kernel(h_ref, p_ref, g_ref, y_ref):
    y = h_ref[...] + p_ref[...]
    ms = jnp.mean(y * y, axis=-1, keepdims=True)
    y_ref[...] = y * lax.rsqrt(ms + EPS) * g_ref[...]


def _final(h, po, g):
    T, D = h.shape
    tm = FINAL_TM
    spec = pl.BlockSpec((tm, D), lambda i: (i, 0))
    return pl.pallas_call(
        _final_kernel, grid=(T // tm,),
        in_specs=[spec, spec, pl.BlockSpec((1, D), lambda i: (0, 0))], out_specs=spec,
        out_shape=jax.ShapeDtypeStruct((T, D), F32),
        compiler_params=pltpu.CompilerParams(dimension_semantics=("parallel",)),
        name="final_norm",
    )(h, po, g)


def kernel(x, norm1_g, w_in, hgrn_lb_fwd, hgrn_lb_bwd, hgrn_out_g, diff_lam_q1, diff_lam_k1,
           diff_lam_q2, diff_lam_k2, diff_out_g, rel_bias, w_out, norm2_g, peer_w_q,
           peer_sub_keys, peer_u, peer_v, final_g):
    B, L, D = x.shape
    hw = HGRN_HEADS * HGRN_DK

    w = w_in[LAYER]
    scale = DIFF_HALF ** -0.5
    cols = lambda j: w[:, j * hw:(j + 1) * hw]
    w_r = jnp.concatenate([cols(1), cols(2), cols(0), cols(3), cols(4), cols(5) * scale, cols(6), cols(7)],
                          axis=1).astype(BF16)
    f32 = jnp.float32
    lam_init = 0.8 - 0.6 * math.exp(-0.3 * LAYER)
    lam = (jnp.exp(jnp.sum(diff_lam_q1[LAYER].astype(f32) * diff_lam_k1[LAYER].astype(f32)))
           - jnp.exp(jnp.sum(diff_lam_q2[LAYER].astype(f32) * diff_lam_k2[LAYER].astype(f32))) + lam_init)
    lam = lam.reshape(1, 1)
    bias_ext = _rel_bias_ext(rel_bias, L)
    sk = peer_sub_keys[LAYER].reshape(2 * PEER_HEADS, PEER_NKEYS, PEER_DKEY).astype(BF16)
    wq = peer_w_q[LAYER].reshape(D, -1).astype(BF16)
    wo = w_out[LAYER].astype(BF16)
    def pack_table(tab):
        words = _pack_bf16_pair(tab[:, :D // 2], tab[:, D // 2:])
        return words.reshape(tab.shape[0], D // 2 // LANES, LANES)
    u3, v3 = pack_table(peer_u[LAYER]), pack_table(peer_v[LAYER])

    bc = B // BATCH_CHUNKS
    tc = bc * L
    rows = tc * PEER_HEADS * PEER_TOPK // SC_GATHER_ROWS
    outs = []
    pending = {}
    combined = None
    for c in range(BATCH_CHUNKS + CHUNK_LAG + 1):
        x2 = x[c * bc:(c + 1) * bc].reshape(tc, D) if c < BATCH_CHUNKS else None
        if combined is not None:
            h_prev, po = combined
            if x2 is not None:
                x2, po = lax.optimization_barrier((x2, po))
            outs.append(_final(h_prev, po, final_g[None, :]).reshape(bc, L, D))
            combined = None
        if CHUNK_LAG <= c < BATCH_CHUNKS + CHUNK_LAG:
            h, idx4, gate, hraw = pending.pop(c - CHUNK_LAG)
            wts = _peer_act(hraw.reshape(tc, -1), gate)
            if x2 is not None:
                x2, wts = lax.optimization_barrier((x2, wts))
            combined = (h, _peer_combine(v3, idx4, wts.reshape(rows, SC_GATHER_ROWS), tc, D))
        if x2 is not None:
            zf, pb = _inproj(x2, norm1_g[LAYER][None, :], w_r)
            zf3 = zf.reshape(bc, L, -1)
            pb3 = pb.reshape(bc, L, -1)
            o_h = _hgrn(zf3, pb3, hgrn_lb_fwd, hgrn_lb_bwd, hgrn_out_g[LAYER][None, :])
            o_d = _attn(pb3, bias_ext, lam, diff_out_g[LAYER][None, :])
            h, hn, idx, gate = _route(x2, o_h.reshape(tc, -1), o_d.reshape(tc, -1), wo,
                                      norm2_g[LAYER][None, :], wq, sk)
            idx4 = idx.reshape(rows, SC_GATHER_ROWS)
            pending[c] = (h, idx4, gate, _peer_hidden(u3, idx4, hn))
    return jnp.concatenate(outs, axis=0)
```

```python
import functools
import math

import numpy as np
import jax
import jax.numpy as jnp
from jax import lax
from jax.experimental import pallas as pl
from jax.experimental.pallas import tpu as pltpu
from jax.experimental.pallas import tpu_sc as plsc

F32 = jnp.float32
BF16 = jnp.bfloat16
EPS = 1e-6

HGRN_HEADS = 4
HGRN_DK = 128
DIFF_HEADS = 4
DIFF_HALF = 64
REL_BUCKETS = 32
REL_MAX_DIST = 128
PEER_HEADS = 8
PEER_NKEYS = 128
PEER_DKEY = 128
PEER_TOPK = 16
LAYER = 0

LANES = 128
SUBLANES = 8
VMEM_LIMIT = 48 * 1024 * 1024

INPROJ_TM = 512
HG_C = 64
HG_LEVELS = (64, 32, 16, 8, 4, 2)
ATTN_TQ = 256
ROUTE_TM = 256
BATCH_CHUNKS = 16
CHUNK_LAG = 2
PEER_ACT_TM = 2048
FINAL_TM = 512

SC_CORES = 2
SC_SUBCORES = 16
SC_LANES = 16
SC_WORKERS = SC_CORES * SC_SUBCORES
SC_GATHER_ROWS = 64
SC_TOKEN_GROUP = 8


def _dot(a, b):
    return jnp.dot(a, b, preferred_element_type=F32)


def _dot_nt(a, b):
    return lax.dot_general(a, b, (((1,), (1,)), ((), ())), preferred_element_type=F32)


def _dot_tn(a, b):
    return lax.dot_general(a, b, (((0,), (0,)), ((), ())), preferred_element_type=F32)


def _silu(x):
    return x * (1.0 / (1.0 + jnp.exp(-x)))


BF16_HI_MASK = -65536


def _pack_bf16_pair(hi, lo):
    bits = lambda a: lax.bitcast_convert_type(a.astype(BF16).astype(F32), jnp.int32)
    return (bits(hi) & BF16_HI_MASK) | lax.shift_right_logical(bits(lo), 16)


def _inproj_kernel(x_ref, g_ref, w_ref, zf_ref, pb_ref):
    x = x_ref[...]
    ms = jnp.mean(x * x, axis=-1, keepdims=True)
    xn = (x * lax.rsqrt(ms + EPS) * g_ref[...]).astype(BF16)
    nz = zf_ref.shape[1]
    zf_ref[...] = _dot(xn, w_ref[:, 0:nz])
    nb = pb_ref.shape[1]
    step = 1024
    for j in range(nb // step):
        pb_ref[:, j * step:(j + 1) * step] = _dot(
            xn, w_ref[:, nz + j * step: nz + (j + 1) * step]).astype(BF16)


def _inproj(x2, g, w):
    T, D = x2.shape
    N = w.shape[1]
    nz = 2 * HGRN_HEADS * HGRN_DK
    tm = INPROJ_TM
    return pl.pallas_call(
        _inproj_kernel,
        grid=(T // tm,),
        in_specs=[
            pl.BlockSpec((tm, D), lambda i: (i, 0)),
            pl.BlockSpec((1, D), lambda i: (0, 0)),
            pl.BlockSpec((D, N), lambda i: (0, 0)),
        ],
        out_specs=[
            pl.BlockSpec((tm, nz), lambda i: (i, 0)),
            pl.BlockSpec((tm, N - nz), lambda i: (i, 0)),
        ],
        out_shape=[
            jax.ShapeDtypeStruct((T, nz), F32),
            jax.ShapeDtypeStruct((T, N - nz), BF16),
        ],
        compiler_params=pltpu.CompilerParams(
            dimension_semantics=("parallel",), vmem_limit_bytes=VMEM_LIMIT),
        name="inproj",
    )(x2, g, w)


def _hgrn_consts():
    C = HG_C
    r = np.arange(C)
    t = r[:, None]
    u = r[None, :]
    blocks = [u <= t, u > t]
    masks = [np.eye(C, dtype=bool)]
    for B in HG_LEVELS:
        half = B // 2
        a = (r // B) * B
        m = (a + half - 1)[:, None]
        upper = (r - a) >= half
        blocks.append(np.where(upper[:, None], (u > m) & (u <= t), (u > t) & (u <= m)))
        same = a[:, None] == a[None, :]
        masks.append(same & upper[:, None] & (~upper)[None, :])
    m_f = np.concatenate(blocks, 0).astype(np.float32)
    m_b = np.concatenate([b[::-1, ::-1] for b in blocks], 0).astype(np.float32)
    k_f = np.stack(masks).astype(np.float32)
    k_b = np.ascontiguousarray(k_f.transpose(0, 2, 1))
    return m_f, m_b, k_f, k_b


def _hgrn_gates(z, tab):
    tabf = tab.astype(F32)
    e = jnp.exp(tabf - jnp.max(tabf, axis=0, keepdims=True))
    lb = jnp.sum(e[0:LAYER + 1], axis=0, keepdims=True) / jnp.sum(e, axis=0, keepdims=True)
    log_lb = jnp.log(lb)
    log_1m = jnp.log1p(-lb)
    ez = jnp.exp(-jnp.abs(z))
    l1p = jnp.log1p(ez)
    log_sig = jnp.minimum(z, 0.0) - l1p
    c = log_1m + log_sig
    hi = jnp.maximum(log_lb, c)
    lo = jnp.minimum(log_lb, c)
    log_f = hi + jnp.log1p(jnp.exp(lo - hi))
    sig_neg = jnp.where(z >= 0.0, ez, 1.0) / (1.0 + ez)
    k = (1.0 - lb) * sig_neg
    return log_f, k


def _hgrn_kernel(lbf_ref, lbb_ref, og_ref, mf_ref, mb_ref, kf_ref, kb_ref,
                 q_ref, zf_ref, zb_ref, v_ref, g_ref, o_ref, of_s, ob_s):
    C = HG_C
    L = q_ref.shape[0]
    n = L // C
    dv = v_ref.shape[1]

    def chunk(c, st, m_ref, k_ref, z_ref, tab_ref, forward):
        sl = pl.ds(pl.multiple_of(c * C, C), C)
        qh = q_ref[sl, :].astype(F32)
        q = _silu(qh)
        v = v_ref[sl, :]
        log_f, k = _hgrn_gates(z_ref[sl, :], tab_ref[...])
        lf_hi = log_f.astype(BF16)
        lf_lo = (log_f - lf_hi.astype(F32)).astype(BF16)
        m = m_ref[...]
        e = jnp.exp(_dot(m, lf_hi) + _dot(m, lf_lo))
        e_b = e[0:C]
        e_s = e[C:2 * C]
        dec = e[C - 1:C] if forward else e[0:1]
        o = _dot_nt((q * e_b).astype(BF16), st.astype(BF16))
        a = _dot_nt(q.astype(BF16), k.astype(BF16)) * k_ref[0]
        for l in range(len(HG_LEVELS)):
            e_l = e[(2 + l) * C:(3 + l) * C]
            a = a + _dot_nt((q * e_l).astype(BF16), (k * e_l).astype(BF16)) * k_ref[l + 1]
        o = o + _dot(a.astype(BF16), v)
        st = st * dec + _dot_tn(v, (k * e_s).astype(BF16))
        return sl, o, st

    st0 = jnp.zeros((dv, q_ref.shape[1]), F32)

    def both(i, carry):
        st_f, st_b = carry
        sl_f, o_f, st_f = chunk(i, st_f, mf_ref, kf_ref, zf_ref, lbf_ref, True)
        sl_b, o_b, st_b = chunk(n - 1 - i, st_b, mb_ref, kb_ref, zb_ref, lbb_ref, False)
        of_s[sl_f, :] = o_f
        ob_s[sl_b, :] = o_b
        return st_f, st_b

    lax.fori_loop(0, n, both, (st0, st0))

    def finish(c, _):
        sl = pl.ds(pl.multiple_of(c * C, C), C)
        tot = of_s[sl, :] + ob_s[sl, :]
        ms = jnp.mean(tot * tot, axis=-1, keepdims=True)
        y = tot * lax.rsqrt(ms + EPS) * og_ref[...]
        o_ref[sl, :] = (y * _silu(g_ref[sl, :].astype(F32))).astype(o_ref.dtype)
        return 0

    lax.fori_loop(0, n, finish, 0)


def _hgrn(zf3, pb3, lb_f, lb_b, out_g):
    B, L, _ = zf3.shape
    H, dk = HGRN_HEADS, HGRN_DK
    m_f, m_b, k_f, k_b = _hgrn_consts()
    nlev = k_f.shape[0]
    full2 = lambda b, h: (0, 0)
    full3 = lambda b, h: (0, 0, 0)
    seq = lambda off: pl.BlockSpec((None, L, dk), lambda b, h: (b, 0, off + h))
    return pl.pallas_call(
        _hgrn_kernel,
        grid=(B, H),
        in_specs=[
            pl.BlockSpec((lb_f.shape[0], dk), lambda b, h: (0, h)),
            pl.BlockSpec((lb_b.shape[0], dk), lambda b, h: (0, h)),
            pl.BlockSpec((1, dk), full2),
            pl.BlockSpec(m_f.shape, full2),
            pl.BlockSpec(m_b.shape, full2),
            pl.BlockSpec((nlev, HG_C, HG_C), full3),
            pl.BlockSpec((nlev, HG_C, HG_C), full3),
            seq(0),
            seq(0),
            seq(H),
            seq(H),
            seq(2 * H),
        ],
        out_specs=pl.BlockSpec((None, L, dk), lambda b, h: (b, 0, h)),
        out_shape=jax.ShapeDtypeStruct((B, L, H * dk), BF16),
        scratch_shapes=[pltpu.VMEM((L, dk), F32), pltpu.VMEM((L, dk), F32)],
        compiler_params=pltpu.CompilerParams(
            dimension_semantics=("parallel", "parallel"), vmem_limit_bytes=VMEM_LIMIT),
        name="hgrn",
    )(lb_f, lb_b, out_g, jnp.asarray(m_f, BF16), jnp.asarray(m_b, BF16),
      jnp.asarray(k_f), jnp.asarray(k_b), pb3, zf3, zf3, pb3, pb3)


def _t5_bucket(rel):
    nb = REL_BUCKETS // 2
    ret = jnp.where(rel > 0, nb, 0)
    n = jnp.abs(rel)
    max_exact = nb // 2
    nf = jnp.maximum(n, 1).astype(jnp.float32)
    large = max_exact + (jnp.log(nf / max_exact) / math.log(REL_MAX_DIST / max_exact)
                         * (nb - max_exact)).astype(jnp.int32)
    large = jnp.minimum(large, nb - 1)
    return ret + jnp.where(n < max_exact, n, large)


def _rel_bias_ext(rel_bias, L):
    j = jnp.arange(2 * L - LANES, dtype=jnp.int32)
    ql = jnp.arange(LANES, dtype=jnp.int32)
    rel = j[None, :] - (L - LANES) - ql[:, None]
    bucket = _t5_bucket(rel)
    tab = rel_bias.astype(F32)
    out = jnp.zeros((tab.shape[1],) + bucket.shape, F32)
    for b in range(REL_BUCKETS):
        out = jnp.where((bucket == b)[None], tab[b][:, None, None], out)
    return out


def _attn_kernel(lam_ref, q_ref, k_ref, v_ref, bias_ref, og_ref, o_ref, *, lam_init):
    tq = q_ref.shape[0]
    L = k_ref.shape[0]
    qi = pl.program_id(2)
    lam = lam_ref[0, 0]
    q = q_ref[...]
    k = k_ref[...]
    lane = lax.broadcasted_iota(jnp.int32, q.shape, 1)
    zero = jnp.zeros_like(q)
    q0 = jnp.where(lane < DIFF_HALF, q, zero)
    q1 = jnp.where(lane >= DIFF_HALF, q, zero)
    parts = []
    for j in range(tq // LANES):
        off = pl.multiple_of(L - LANES - (qi * tq + j * LANES), LANES)
        parts.append(bias_ref[:, pl.ds(off, L)])
    bias = jnp.concatenate(parts, axis=0) if len(parts) > 1 else parts[0]

    def soft(qm):
        s = _dot_nt(qm, k) + bias
        e = jnp.exp(s - jnp.max(s, axis=-1, keepdims=True))
        return e, 1.0 / jnp.sum(e, axis=-1, keepdims=True)

    e0, r0 = soft(q0)
    e1, r1 = soft(q1)
    w = e0 * r0 - e1 * (lam * r1)
    o = _dot(w.astype(BF16), v_ref[...])
    ms = jnp.mean(o * o, axis=-1, keepdims=True)
    y = o * lax.rsqrt(ms + EPS) * og_ref[...] * (1.0 - lam_init)
    o_ref[...] = y.astype(o_ref.dtype)


def _attn(pb3, bias_ext, lam, out_g):
    B, L, _ = pb3.shape
    H, dh = DIFF_HEADS, 2 * DIFF_HALF
    tq = ATTN_TQ
    base = (3 * HGRN_HEADS * HGRN_DK) // dh
    lam_init = 0.8 - 0.6 * math.exp(-0.3 * LAYER)
    return pl.pallas_call(
        functools.partial(_attn_kernel, lam_init=lam_init),
        grid=(B, H, L // tq),
        in_specs=[
            pl.BlockSpec(memory_space=pltpu.SMEM),
            pl.BlockSpec((None, tq, dh), lambda b, h, i: (b, i, base + h)),
            pl.BlockSpec((None, L, dh), lambda b, h, i: (b, 0, base + H + h)),
            pl.BlockSpec((None, L, dh), lambda b, h, i: (b, 0, base + 2 * H + h)),
            pl.BlockSpec((None, LANES, 2 * L - LANES), lambda b, h, i: (h, 0, 0)),
            pl.BlockSpec((1, dh), lambda b, h, i: (0, 0)),
        ],
        out_specs=pl.BlockSpec((None, tq, dh), lambda b, h, i: (b, i, h)),
        out_shape=jax.ShapeDtypeStruct((B, L, H * dh), BF16),
        compiler_params=pltpu.CompilerParams(
            dimension_semantics=("parallel", "parallel", "parallel"),
            vmem_limit_bytes=VMEM_LIMIT),
        name="attn",
    )(lam, pb3, pb3, pb3, bias_ext, out_g)


def _cand_layout():
    K = PEER_TOPK
    groups = [("a", 0, 0), ("a", 0, 8), ("a", 1, 0), ("a", 2, 0), ("a", 3, 0),
              ("b", 0, 8), ("b", 0, 0), ("b", 1, 0), ("b", 2, 0)]
    seen = set()
    pos, valid = [], []
    for kind, fixed, start in groups:
        for r in range(SUBLANES):
            a, b = (fixed, start + r) if kind == "a" else (start + r, fixed)
            ok = (a + 1) * (b + 1) <= K and (a, b) not in seen
            if ok:
                seen.add((a, b))
            pos.append(a * K + b if ok else K * K + len(pos))
            valid.append(ok)
    assert len(seen) == sum(K // (a + 1) for a in range(K))
    return groups, np.array(pos, np.int32), np.array(valid, bool)


def _extract_topk(s, key, payload, k, big):
    n, tm = s.shape
    S = SUBLANES
    s3 = s.reshape(n // S, S, tm)
    key3 = key.reshape(n // S, S, tm)
    pay3 = None if payload is key else payload.reshape(n // S, S, tm)
    slot = lax.broadcasted_iota(jnp.int32, (k // S, S, tm), 0) * S + lax.broadcasted_iota(
        jnp.int32, (k // S, S, tm), 1)

    def all_reduce(x3, op):
        r = x3[0]
        for g in range(1, x3.shape[0]):
            r = op(r, x3[g])
        for sh in (S // 2, S // 4, S // 8):
            r = op(r, pltpu.roll(r, sh, axis=0))
        return r

    def body(j, carry):
        s3, vals, pay = carry
        m = all_reduce(s3, jnp.maximum)
        kk = all_reduce(jnp.where(s3 == m[None], key3, big), jnp.minimum)
        sel = key3 == kk[None]
        p = kk if pay3 is None else all_reduce(jnp.where(sel, pay3, 0), jnp.add)
        vals = jnp.where(slot == j, m[None], vals)
        pay = jnp.where(slot == j, p[None], pay)
        s3 = jnp.where(sel, -jnp.inf, s3)
        return s3, vals, pay

    init = (s3, jnp.zeros((k // S, S, tm), F32), jnp.zeros((k // S, S, tm), jnp.int32))
    _, vals, pay = lax.fori_loop(0, k, body, init)
    return vals.reshape(k, tm), pay.reshape(k, tm)


def _route_kernel(x_ref, oh_ref, od_ref, woh_ref, wod_ref, g2_ref, wq_ref, sk_ref, cpos_ref, cmask_ref,
                  h_ref, hn_ref, idx_ref, gate_ref, q_s, tv_s, ti_s, gs_s, is_s, *, groups):
    K = PEER_TOPK
    tm = x_ref.shape[0]
    h = x_ref[...] + _dot(oh_ref[...], woh_ref[...]) + _dot(od_ref[...], wod_ref[...])
    h_ref[...] = h
    hn = h * lax.rsqrt(jnp.mean(h * h, axis=-1, keepdims=True) + EPS) * g2_ref[...]
    half = hn.shape[1] // 2
    hn_ref[...] = _pack_bf16_pair(hn[:, :half], hn[:, half:])
    q_s[...] = _dot(hn.astype(BF16), wq_ref[...]).astype(BF16)

    key_iota = lax.broadcasted_iota(jnp.int32, (PEER_NKEYS, tm), 0)

    def half_topk(hp, _):
        col = pl.multiple_of(hp * PEER_DKEY, PEER_DKEY)
        s = _dot_nt(sk_ref[hp], q_s[:, pl.ds(col, PEER_DKEY)])
        vals, idxs = _extract_topk(s, key_iota, key_iota, K, PEER_NKEYS)
        tv_s[hp] = vals
        ti_s[hp] = idxs
        return 0

    lax.fori_loop(0, 2 * PEER_HEADS, half_topk, 0)

    cpos = cpos_ref[...]
    cmask = cmask_ref[...]

    def head(hd, _):
        s0, s1 = tv_s[2 * hd], tv_s[2 * hd + 1]
        i0, i1 = ti_s[2 * hd] * PEER_NKEYS, ti_s[2 * hd + 1]
        cs, ci = [], []
        for kind, fixed, start in groups:
            if kind == "a":
                cs.append(s0[fixed:fixed + 1] + s1[start:start + SUBLANES])
                ci.append(i0[fixed:fixed + 1] + i1[start:start + SUBLANES])
            else:
                cs.append(s0[start:start + SUBLANES] + s1[fixed:fixed + 1])
                ci.append(i0[start:start + SUBLANES] + i1[fixed:fixed + 1])
        cand = jnp.concatenate(cs, axis=0) + cmask
        cidx = jnp.concatenate(ci, axis=0)
        best, eidx = _extract_topk(cand, cpos, cidx, K, 2 * K * K)
        ex = jnp.exp(best - best[0:1])
        gate = ex / jnp.sum(ex, axis=0, keepdims=True)
        row = pl.ds(pl.multiple_of(hd * K, K), K)
        gs_s[row, :] = gate
        is_s[row, :] = eidx.astype(F32)
        return 0

    lax.fori_loop(0, PEER_HEADS, head, 0)
    gate_ref[...] = gs_s[...].T
    idx_ref[...] = is_s[...].T.astype(jnp.int32)


def _route(x2, oh2, od2, w_out, g2, w_q, sub_keys):
    T, D = x2.shape
    tm = ROUTE_TM
    K = PEER_TOPK
    nh = oh2.shape[1]
    nq = w_q.shape[1]
    npk = PEER_HEADS * K
    groups, pos, valid = _cand_layout()
    ncand = pos.shape[0]
    cpos = jnp.asarray(np.broadcast_to(pos[:, None], (ncand, tm)))
    cmask = jnp.asarray(np.broadcast_to(np.where(valid, 0.0, -np.inf).astype(np.float32)[:, None], (ncand, tm)))
    row = lambda i: (i, 0)
    full2 = lambda i: (0, 0)
    return pl.pallas_call(
        functools.partial(_route_kernel, groups=groups),
        grid=(T // tm,),
        in_specs=[
            pl.BlockSpec((tm, D), row),
            pl.BlockSpec((tm, nh), row),
            pl.BlockSpec((tm, nh), row),
            pl.BlockSpec((nh, D), full2),
            pl.BlockSpec((nh, D), lambda i: (1, 0)),
            pl.BlockSpec((1, D), full2),
            pl.BlockSpec((D, nq), full2),
            pl.BlockSpec(sub_keys.shape, lambda i: (0, 0, 0)),
            pl.BlockSpec((ncand, tm), full2),
            pl.BlockSpec((ncand, tm), full2),
        ],
        out_specs=[
            pl.BlockSpec((tm, D), row),
            pl.BlockSpec((tm, D // 2), row),
            pl.BlockSpec((tm, npk), row),
            pl.BlockSpec((tm, npk), row),
        ],
        out_shape=[
            jax.ShapeDtypeStruct((T, D), F32),
            jax.ShapeDtypeStruct((T, D // 2), jnp.int32),
            jax.ShapeDtypeStruct((T, npk), jnp.int32),
            jax.ShapeDtypeStruct((T, npk), F32),
        ],
        scratch_shapes=[
            pltpu.VMEM((tm, nq), BF16),
            pltpu.VMEM((2 * PEER_HEADS, K, tm), F32),
            pltpu.VMEM((2 * PEER_HEADS, K, tm), jnp.int32),
            pltpu.VMEM((npk, tm), F32),
            pltpu.VMEM((npk, tm), F32),
        ],
        compiler_params=pltpu.CompilerParams(
            dimension_semantics=("parallel",), vmem_limit_bytes=VMEM_LIMIT),
        name="route",
    )(x2, oh2, od2, w_out, w_out, g2, w_q, sub_keys, cpos, cmask)


def _gelu(x):
    return 0.5 * x * (1.0 + lax.erf(x * (1.0 / math.sqrt(2.0))))


def _sc_mesh():
    return plsc.VectorSubcoreMesh(core_axis_name="c", subcore_axis_name="s")


def _sc_worker_id():
    return lax.axis_index("s") * SC_CORES + lax.axis_index("c")


def _sc_widen_pair(s):
    si = plsc.bitcast(s, jnp.int32)
    return plsc.bitcast(si & BF16_HI_MASK, F32), plsc.bitcast(si << 16, F32)


def _sc_gather_loop(tab_hbm, idx_v, bufs, sems, n_items, compute):
    def start(item, b):
        pltpu.async_copy(tab_hbm.at[idx_v.at[item]], bufs[b], sems[b])

    def wait(b):
        pltpu.make_async_copy(tab_hbm.at[idx_v.at[0]], bufs[b], sems[b]).wait()

    start(0, 0)

    @pl.loop(0, n_items // 2)
    def _(i2):
        it = 2 * i2
        start(it + 1, 1)
        wait(0)
        compute(it, bufs[0])

        @pl.when(it + 2 < n_items)
        def _():
            start(it + 2, 0)

        wait(1)
        compute(it + 1, bufs[1])


def _peer_hidden(u, idx4, x):
    T, DW = x.shape
    n_rows, R = idx4.shape
    ipt = n_rows // T
    G = SC_TOKEN_GROUP
    nit = G * ipt
    tpw = T // SC_WORKERS
    NL = SC_LANES
    RB = 8
    NV = 4
    GW = NV * NL
    tile = (R, DW // LANES, LANES)

    @functools.partial(
        pl.kernel, mesh=_sc_mesh(), compiler_params=pltpu.CompilerParams(needs_layout_passes=False),
        out_type=jax.ShapeDtypeStruct((n_rows, R), F32),
        scratch_types=[pltpu.VMEM((nit, R), jnp.int32), pltpu.VMEM((G, DW), jnp.int32),
                       pltpu.VMEM(tile, jnp.int32), pltpu.VMEM(tile, jnp.int32),
                       pltpu.VMEM((nit, R), F32),
                       pltpu.SemaphoreType.DMA, pltpu.SemaphoreType.DMA],
        name="peer_hidden")
    def k(u_hbm, idx_hbm, x_hbm, h_hbm, idx_v, x_v, buf0, buf1, h_v, sem0, sem1):
        tok0 = _sc_worker_id() * tpw
        lane = lax.broadcasted_iota(jnp.int32, (NL,), 0)
        zero = jnp.zeros((NL,), F32)

        def compute(item, buf):
            tok = item // ipt
            hvs = [zero for _ in range(R // NL)]
            for rb in range(R // RB):
                def body(g, accs):
                    sub = g // (LANES // GW)
                    base = (g % (LANES // GW)) * GW
                    xs = [plsc.bitcast(x_v[tok, pl.ds(pl.multiple_of(g * GW + jj * NL, NL), NL)], BF16)
                          for jj in range(NV)]
                    out = []
                    for j in range(RB):
                        ps = [plsc.bitcast(buf[rb * RB + j, sub, pl.ds(pl.multiple_of(base + jj * NL, NL), NL)],
                                           BF16) * xs[jj] for jj in range(NV)]
                        hi, lo = _sc_widen_pair((ps[0] + ps[1]) + (ps[2] + ps[3]))
                        out.append((accs[j] + hi) + lo)
                    return tuple(out)
                accs = lax.fori_loop(0, DW // GW, body, tuple(zero for _ in range(RB)))
                for j in range(RB):
                    r = rb * RB + j
                    hvs[r // NL] = jnp.where(lane == (r % NL), jnp.sum(accs[j]), hvs[r // NL])
            for q in range(R // NL):
                h_v[item, pl.ds(q * NL, NL)] = hvs[q]

        @pl.loop(0, tpw // G)
        def _(g):
            t0 = tok0 + g * G
            pltpu.sync_copy(idx_hbm.at[pl.ds(t0 * ipt, nit)], idx_v)
            pltpu.sync_copy(x_hbm.at[pl.ds(t0, G)], x_v)
            _sc_gather_loop(u_hbm, idx_v, (buf0, buf1), (sem0, sem1), nit, compute)
            pltpu.sync_copy(h_v, h_hbm.at[pl.ds(t0 * ipt, nit)])

    return k(u, idx4, x)


def _peer_combine(v, idx4, w4, T, D):
    n_rows, R = idx4.shape
    ipt = n_rows // T
    G = SC_TOKEN_GROUP
    nit = G * ipt
    tpw = T // SC_WORKERS
    NL = SC_LANES
    DW = D // 2
    half = DW // 2
    nv = half // NL
    RG = 4
    tile = (R, DW // LANES, LANES)

    @functools.partial(
        pl.kernel, mesh=_sc_mesh(), compiler_params=pltpu.CompilerParams(needs_layout_passes=False),
        out_type=jax.ShapeDtypeStruct((T, D), F32),
        scratch_types=[pltpu.VMEM((nit, R), jnp.int32), pltpu.VMEM((nit, R), jnp.int32),
                       pltpu.VMEM(tile, jnp.int32), pltpu.VMEM(tile, jnp.int32),
                       pltpu.VMEM((G, D), F32),
                       pltpu.SemaphoreType.DMA, pltpu.SemaphoreType.DMA],
        name="peer_combine")
    def k(v_hbm, idx_hbm, w_hbm, o_hbm, idx_v, w_v, buf0, buf1, out_v, sem0, sem1):
        tok0 = _sc_worker_id() * tpw
        zero = jnp.zeros((NL,), F32)

        def compute(item, buf):
            tok = item // ipt
            item_vec = jnp.full((NL,), item, jnp.int32)
            for hf in range(2):
                def body(rg, accs):
                    ws = [plsc.bitcast(plsc.load_gather(
                        w_v, [item_vec, jnp.full((NL,), rg * RG + rr, jnp.int32)]), BF16) for rr in range(RG)]
                    his, los = [], []
                    for i in range(nv):
                        word = hf * half + i * NL
                        ps = [plsc.bitcast(buf[rg * RG + rr, word // LANES, pl.ds(word % LANES, NL)], BF16) * ws[rr]
                              for rr in range(RG)]
                        hi, lo = _sc_widen_pair((ps[0] + ps[1]) + (ps[2] + ps[3]))
                        his.append(accs[i] + hi)
                        los.append(accs[nv + i] + lo)
                    return tuple(his + los)
                accs = lax.fori_loop(0, R // RG, body, tuple(zero for _ in range(2 * nv)))
                for i in range(nv):
                    word = hf * half + i * NL
                    plsc.addupdate(out_v.at[tok, pl.ds(word, NL)], accs[i])
                    plsc.addupdate(out_v.at[tok, pl.ds(DW + word, NL)], accs[nv + i])

        @pl.loop(0, tpw // G)
        def _(g):
            t0 = tok0 + g * G
            pltpu.sync_copy(idx_hbm.at[pl.ds(t0 * ipt, nit)], idx_v)
            pltpu.sync_copy(w_hbm.at[pl.ds(t0 * ipt, nit)], w_v)

            @pl.loop(0, G)
            def _(t):
                @pl.loop(0, D // NL)
                def _(i):
                    out_v[t, pl.ds(pl.multiple_of(i * NL, NL), NL)] = zero

            _sc_gather_loop(v_hbm, idx_v, (buf0, buf1), (sem0, sem1), nit, compute)
            pltpu.sync_copy(out_v, o_hbm.at[pl.ds(t0, G)])

    return k(v, idx4, w4)


def _act_kernel(h_ref, g_ref, w_ref):
    w = _gelu(h_ref[...]) * g_ref[...]
    w_ref[...] = _pack_bf16_pair(w, w)


def _peer_act(hraw, gate):
    T, n = gate.shape
    tm = PEER_ACT_TM
    spec = pl.BlockSpec((tm, n), lambda i: (i, 0))
    return pl.pallas_call(
        _act_kernel, grid=(T // tm,), in_specs=[spec, spec], out_specs=spec,
        out_shape=jax.ShapeDtypeStruct((T, n), jnp.int32),
        compiler_params=pltpu.CompilerParams(dimension_semantics=("parallel",)),
        name="peer_act",
    )(hraw, gate)


def _final_kernel(h_ref, p_ref, g_ref, y_ref):
    y = h_ref[...] + p_ref[...]
    ms = jnp.mean(y * y, axis=-1, keepdims=True)
    y_ref[...] = y * lax.rsqrt(ms + EPS) * g_ref[...]


def _final(h, po, g):
    T, D = h.shape
    tm = FINAL_TM
    spec = pl.BlockSpec((tm, D), lambda i: (i, 0))
    return pl.pallas_call(
        _final_kernel, grid=(T // tm,),
        in_specs=[spec, spec, pl.BlockSpec((1, D), lambda i: (0, 0))], out_specs=spec,
        out_shape=jax.ShapeDtypeStruct((T, D), F32),
        compiler_params=pltpu.CompilerParams(dimension_semantics=("parallel",)),
        name="final_norm",
    )(h, po, g)


def kernel(x, norm1_g, w_in, hgrn_lb_fwd, hgrn_lb_bwd, hgrn_out_g, diff_lam_q1, diff_lam_k1,
           diff_lam_q2, diff_lam_k2, diff_out_g, rel_bias, w_out, norm2_g, peer_w_q,
           peer_sub_keys, peer_u, peer_v, final_g):
    B, L, D = x.shape
    hw = HGRN_HEADS * HGRN_DK

    w = w_in[LAYER]
    scale = DIFF_HALF ** -0.5
    cols = lambda j: w[:, j * hw:(j + 1) * hw]
    w_r = jnp.concatenate([cols(1), cols(2), cols(0), cols(3), cols(4), cols(5) * scale, cols(6), cols(7)],
                          axis=1).astype(BF16)
    f32 = jnp.float32
    lam_init = 0.8 - 0.6 * math.exp(-0.3 * LAYER)
    lam = (jnp.exp(jnp.sum(diff_lam_q1[LAYER].astype(f32) * diff_lam_k1[LAYER].astype(f32)))
           - jnp.exp(jnp.sum(diff_lam_q2[LAYER].astype(f32) * diff_lam_k2[LAYER].astype(f32))) + lam_init)
    lam = lam.reshape(1, 1)
    bias_ext = _rel_bias_ext(rel_bias, L)
    sk = peer_sub_keys[LAYER].reshape(2 * PEER_HEADS, PEER_NKEYS, PEER_DKEY).astype(BF16)
    wq = peer_w_q[LAYER].reshape(D, -1).astype(BF16)
    wo = w_out[LAYER].astype(BF16)
    def pack_table(tab):
        words = _pack_bf16_pair(tab[:, :D // 2], tab[:, D // 2:])
        return words.reshape(tab.shape[0], D // 2 // LANES, LANES)
    u3, v3 = pack_table(peer_u[LAYER]), pack_table(peer_v[LAYER])

    bc = B // BATCH_CHUNKS
    tc = bc * L
    rows = tc * PEER_HEADS * PEER_TOPK // SC_GATHER_ROWS
    outs = []
    pending = {}
    combined = None
    for c in range(BATCH_CHUNKS + CHUNK_LAG + 1):
        x2 = x[c * bc:(c + 1) * bc].reshape(tc, D) if c < BATCH_CHUNKS else None
        if combined is not None:
            h_prev, po = combined
            if x2 is not None:
                x2, po = lax.optimization_barrier((x2, po))
            outs.append(_final(h_prev, po, final_g[None, :]).reshape(bc, L, D))
            combined = None
        if CHUNK_LAG <= c < BATCH_CHUNKS + CHUNK_LAG:
            h, idx4, gate, hraw = pending.pop(c - CHUNK_LAG)
            wts = _peer_act(hraw.reshape(tc, -1), gate)
            if x2 is not None:
                x2, wts = lax.optimization_barrier((x2, wts))
            combined = (h, _peer_combine(v3, idx4, wts.reshape(rows, SC_GATHER_ROWS), tc, D))
        if x2 is not None:
            zf, pb = _inproj(x2, norm1_g[LAYER][None, :], w_r)
            zf3 = zf.reshape(bc, L, -1)
            pb3 = pb.reshape(bc, L, -1)
            o_h = _hgrn(zf3, pb3, hgrn_lb_fwd, hgrn_lb_bwd, hgrn_out_g[LAYER][None, :])
            o_d = _attn(pb3, bias_ext, lam, diff_out_g[LAYER][None, :])
            h, hn, idx, gate = _route(x2, o_h.reshape(tc, -1), o_d.reshape(tc, -1), wo,
                                      norm2_g[LAYER][None, :], wq, sk)
            idx4 = idx.reshape(rows, SC_GATHER_ROWS)
            pending[c] = (h, idx4, gate, _peer_hidden(u3, idx4, hn))
    return jnp.concatenate(outs, axis=0)
```

```python
import functools
import math

import numpy as np
import jax
import jax.numpy as jnp
from jax import lax
from jax.experimental import pallas as pl
from jax.experimental.pallas import tpu as pltpu
from jax.experimental.pallas import tpu_sc as plsc

F32 = jnp.float32
BF16 = jnp.bfloat16
EPS = 1e-6

HGRN_HEADS = 4
HGRN_DK = 128
DIFF_HEADS = 4
DIFF_HALF = 64
REL_BUCKETS = 32
REL_MAX_DIST = 128
PEER_HEADS = 8
PEER_NKEYS = 128
PEER_DKEY = 128
PEER_TOPK = 16
LAYER = 0

LANES = 128
SUBLANES = 8
VMEM_LIMIT = 48 * 1024 * 1024

INPROJ_TM = 512
HG_C = 64
HG_LEVELS = (64, 32, 16, 8, 4, 2)
ATTN_TQ = 256
ROUTE_TM = 256
BATCH_CHUNKS = 16
CHUNK_LAG = 2
PEER_ACT_TM = 2048
FINAL_TM = 512

SC_CORES = 2
SC_SUBCORES = 16
SC_LANES = 16
SC_WORKERS = SC_CORES * SC_SUBCORES
SC_GATHER_ROWS = 64
SC_TOKEN_GROUP = 32


def _dot(a, b):
    return jnp.dot(a, b, preferred_element_type=F32)


def _dot_nt(a, b):
    return lax.dot_general(a, b, (((1,), (1,)), ((), ())), preferred_element_type=F32)


def _dot_tn(a, b):
    return lax.dot_general(a, b, (((0,), (0,)), ((), ())), preferred_element_type=F32)


def _silu(x):
    return x * (1.0 / (1.0 + jnp.exp(-x)))


BF16_HI_MASK = -65536


def _pack_bf16_pair(hi, lo):
    bits = lambda a: lax.bitcast_convert_type(a.astype(BF16).astype(F32), jnp.int32)
    return (bits(hi) & BF16_HI_MASK) | lax.shift_right_logical(bits(lo), 16)


def _inproj_kernel(x_ref, g_ref, w_ref, zf_ref, pb_ref):
    x = x_ref[...]
    ms = jnp.mean(x * x, axis=-1, keepdims=True)
    xn = (x * lax.rsqrt(ms + EPS) * g_ref[...]).astype(BF16)
    nz = zf_ref.shape[1]
    zf_ref[...] = _dot(xn, w_ref[:, 0:nz])
    nb = pb_ref.shape[1]
    step = 1024
    for j in range(nb // step):
        pb_ref[:, j * step:(j + 1) * step] = _dot(
            xn, w_ref[:, nz + j * step: nz + (j + 1) * step]).astype(BF16)


def _inproj(x2, g, w):
    T, D = x2.shape
    N = w.shape[1]
    nz = 2 * HGRN_HEADS * HGRN_DK
    tm = INPROJ_TM
    return pl.pallas_call(
        _inproj_kernel,
        grid=(T // tm,),
        in_specs=[
            pl.BlockSpec((tm, D), lambda i: (i, 0)),
            pl.BlockSpec((1, D), lambda i: (0, 0)),
            pl.BlockSpec((D, N), lambda i: (0, 0)),
        ],
        out_specs=[
            pl.BlockSpec((tm, nz), lambda i: (i, 0)),
            pl.BlockSpec((tm, N - nz), lambda i: (i, 0)),
        ],
        out_shape=[
            jax.ShapeDtypeStruct((T, nz), F32),
            jax.ShapeDtypeStruct((T, N - nz), BF16),
        ],
        compiler_params=pltpu.CompilerParams(
            dimension_semantics=("parallel",), vmem_limit_bytes=VMEM_LIMIT),
        name="inproj",
    )(x2, g, w)


def _hgrn_consts():
    C = HG_C
    r = np.arange(C)
    t = r[:, None]
    u = r[None, :]
    blocks = [u <= t, u > t]
    masks = [np.eye(C, dtype=bool)]
    for B in HG_LEVELS:
        half = B // 2
        a = (r // B) * B
        m = (a + half - 1)[:, None]
        upper = (r - a) >= half
        blocks.append(np.where(upper[:, None], (u > m) & (u <= t), (u > t) & (u <= m)))
        same = a[:, None] == a[None, :]
        masks.append(same & upper[:, None] & (~upper)[None, :])
    m_f = np.concatenate(blocks, 0).astype(np.float32)
    m_b = np.concatenate([b[::-1, ::-1] for b in blocks], 0).astype(np.float32)
    k_f = np.stack(masks).astype(np.float32)
    k_b = np.ascontiguousarray(k_f.transpose(0, 2, 1))
    return m_f, m_b, k_f, k_b


def _hgrn_gates(z, tab):
    tabf = tab.astype(F32)
    e = jnp.exp(tabf - jnp.max(tabf, axis=0, keepdims=True))
    lb = jnp.sum(e[0:LAYER + 1], axis=0, keepdims=True) / jnp.sum(e, axis=0, keepdims=True)
    log_lb = jnp.log(lb)
    log_1m = jnp.log1p(-lb)
    ez = jnp.exp(-jnp.abs(z))
    l1p = jnp.log1p(ez)
    log_sig = jnp.minimum(z, 0.0) - l1p
    c = log_1m + log_sig
    hi = jnp.maximum(log_lb, c)
    lo = jnp.minimum(log_lb, c)
    log_f = hi + jnp.log1p(jnp.exp(lo - hi))
    sig_neg = jnp.where(z >= 0.0, ez, 1.0) / (1.0 + ez)
    k = (1.0 - lb) * sig_neg
    return log_f, k


def _hgrn_kernel(lbf_ref, lbb_ref, og_ref, mf_ref, mb_ref, kf_ref, kb_ref,
                 q_ref, zf_ref, zb_ref, v_ref, g_ref, o_ref, of_s, ob_s):
    C = HG_C
    L = q_ref.shape[0]
    n = L // C
    dv = v_ref.shape[1]

    def chunk(c, st, m_ref, k_ref, z_ref, tab_ref, forward):
        sl = pl.ds(pl.multiple_of(c * C, C), C)
        qh = q_ref[sl, :].astype(F32)
        q = _silu(qh)
        v = v_ref[sl, :]
        log_f, k = _hgrn_gates(z_ref[sl, :], tab_ref[...])
        lf_hi = log_f.astype(BF16)
        lf_lo = (log_f - lf_hi.astype(F32)).astype(BF16)
        m = m_ref[...]
        e = jnp.exp(_dot(m, lf_hi) + _dot(m, lf_lo))
        e_b = e[0:C]
        e_s = e[C:2 * C]
        dec = e[C - 1:C] if forward else e[0:1]
        o = _dot_nt((q * e_b).astype(BF16), st.astype(BF16))
        a = _dot_nt(q.astype(BF16), k.astype(BF16)) * k_ref[0]
        for l in range(len(HG_LEVELS)):
            e_l = e[(2 + l) * C:(3 + l) * C]
            a = a + _dot_nt((q * e_l).astype(BF16), (k * e_l).astype(BF16)) * k_ref[l + 1]
        o = o + _dot(a.astype(BF16), v)
        st = st * dec + _dot_tn(v, (k * e_s).astype(BF16))
        return sl, o, st

    st0 = jnp.zeros((dv, q_ref.shape[1]), F32)

    def both(i, carry):
        st_f, st_b = carry
        sl_f, o_f, st_f = chunk(i, st_f, mf_ref, kf_ref, zf_ref, lbf_ref, True)
        sl_b, o_b, st_b = chunk(n - 1 - i, st_b, mb_ref, kb_ref, zb_ref, lbb_ref, False)
        of_s[sl_f, :] = o_f
        ob_s[sl_b, :] = o_b
        return st_f, st_b

    lax.fori_loop(0, n, both, (st0, st0))

    def finish(c, _):
        sl = pl.ds(pl.multiple_of(c * C, C), C)
        tot = of_s[sl, :] + ob_s[sl, :]
        ms = jnp.mean(tot * tot, axis=-1, keepdims=True)
        y = tot * lax.rsqrt(ms + EPS) * og_ref[...]
        o_ref[sl, :] = (y * _silu(g_ref[sl, :].astype(F32))).astype(o_ref.dtype)
        return 0

    lax.fori_loop(0, n, finish, 0)


def _hgrn(zf3, pb3, lb_f, lb_b, out_g):
    B, L, _ = zf3.shape
    H, dk = HGRN_HEADS, HGRN_DK
    m_f, m_b, k_f, k_b = _hgrn_consts()
    nlev = k_f.shape[0]
    full2 = lambda b, h: (0, 0)
    full3 = lambda b, h: (0, 0, 0)
    seq = lambda off: pl.BlockSpec((None, L, dk), lambda b, h: (b, 0, off + h))
    return pl.pallas_call(
        _hgrn_kernel,
        grid=(B, H),
        in_specs=[
            pl.BlockSpec((lb_f.shape[0], dk), lambda b, h: (0, h)),
            pl.BlockSpec((lb_b.shape[0], dk), lambda b, h: (0, h)),
            pl.BlockSpec((1, dk), full2),
            pl.BlockSpec(m_f.shape, full2),
            pl.BlockSpec(m_b.shape, full2),
            pl.BlockSpec((nlev, HG_C, HG_C), full3),
            pl.BlockSpec((nlev, HG_C, HG_C), full3),
            seq(0),
            seq(0),
            seq(H),
            seq(H),
            seq(2 * H),
        ],
        out_specs=pl.BlockSpec((None, L, dk), lambda b, h: (b, 0, h)),
        out_shape=jax.ShapeDtypeStruct((B, L, H * dk), BF16),
        scratch_shapes=[pltpu.VMEM((L, dk), F32), pltpu.VMEM((L, dk), F32)],
        compiler_params=pltpu.CompilerParams(
            dimension_semantics=("parallel", "parallel"), vmem_limit_bytes=VMEM_LIMIT),
        name="hgrn",
    )(lb_f, lb_b, out_g, jnp.asarray(m_f, BF16), jnp.asarray(m_b, BF16),
      jnp.asarray(k_f), jnp.asarray(k_b), pb3, zf3, zf3, pb3, pb3)


def _t5_bucket(rel):
    nb = REL_BUCKETS // 2
    ret = jnp.where(rel > 0, nb, 0)
    n = jnp.abs(rel)
    max_exact = nb // 2
    nf = jnp.maximum(n, 1).astype(jnp.float32)
    large = max_exact + (jnp.log(nf / max_exact) / math.log(REL_MAX_DIST / max_exact)
                         * (nb - max_exact)).astype(jnp.int32)
    large = jnp.minimum(large, nb - 1)
    return ret + jnp.where(n < max_exact, n, large)


def _rel_bias_ext(rel_bias, L):
    j = jnp.arange(2 * L - LANES, dtype=jnp.int32)
    ql = jnp.arange(LANES, dtype=jnp.int32)
    rel = j[None, :] - (L - LANES) - ql[:, None]
    bucket = _t5_bucket(rel)
    tab = rel_bias.astype(F32)
    out = jnp.zeros((tab.shape[1],) + bucket.shape, F32)
    for b in range(REL_BUCKETS):
        out = jnp.where((bucket == b)[None], tab[b][:, None, None], out)
    return out


def _attn_kernel(lam_ref, q_ref, k_ref, v_ref, bias_ref, og_ref, o_ref, *, lam_init):
    tq = q_ref.shape[0]
    L = k_ref.shape[0]
    qi = pl.program_id(2)
    lam = lam_ref[0, 0]
    q = q_ref[...]
    k = k_ref[...]
    lane = lax.broadcasted_iota(jnp.int32, q.shape, 1)
    zero = jnp.zeros_like(q)
    q0 = jnp.where(lane < DIFF_HALF, q, zero)
    q1 = jnp.where(lane >= DIFF_HALF, q, zero)
    parts = []
    for j in range(tq // LANES):
        off = pl.multiple_of(L - LANES - (qi * tq + j * LANES), LANES)
        parts.append(bias_ref[:, pl.ds(off, L)])
    bias = jnp.concatenate(parts, axis=0) if len(parts) > 1 else parts[0]

    def soft(qm):
        s = _dot_nt(qm, k) + bias
        e = jnp.exp(s - jnp.max(s, axis=-1, keepdims=True))
        return e, 1.0 / jnp.sum(e, axis=-1, keepdims=True)

    e0, r0 = soft(q0)
    e1, r1 = soft(q1)
    w = e0 * r0 - e1 * (lam * r1)
    o = _dot(w.astype(BF16), v_ref[...])
    ms = jnp.mean(o * o, axis=-1, keepdims=True)
    y = o * lax.rsqrt(ms + EPS) * og_ref[...] * (1.0 - lam_init)
    o_ref[...] = y.astype(o_ref.dtype)


def _attn(pb3, bias_ext, lam, out_g):
    B, L, _ = pb3.shape
    H, dh = DIFF_HEADS, 2 * DIFF_HALF
    tq = ATTN_TQ
    base = (3 * HGRN_HEADS * HGRN_DK) // dh
    lam_init = 0.8 - 0.6 * math.exp(-0.3 * LAYER)
    return pl.pallas_call(
        functools.partial(_attn_kernel, lam_init=lam_init),
        grid=(B, H, L // tq),
        in_specs=[
            pl.BlockSpec(memory_space=pltpu.SMEM),
            pl.BlockSpec((None, tq, dh), lambda b, h, i: (b, i, base + h)),
            pl.BlockSpec((None, L, dh), lambda b, h, i: (b, 0, base + H + h)),
            pl.BlockSpec((None, L, dh), lambda b, h, i: (b, 0, base + 2 * H + h)),
            pl.BlockSpec((None, LANES, 2 * L - LANES), lambda b, h, i: (h, 0, 0)),
            pl.BlockSpec((1, dh), lambda b, h, i: (0, 0)),
        ],
        out_specs=pl.BlockSpec((None, tq, dh), lambda b, h, i: (b, i, h)),
        out_shape=jax.ShapeDtypeStruct((B, L, H * dh), BF16),
        compiler_params=pltpu.CompilerParams(
            dimension_semantics=("parallel", "parallel", "parallel"),
            vmem_limit_bytes=VMEM_LIMIT),
        name="attn",
    )(lam, pb3, pb3, pb3, bias_ext, out_g)


def _cand_layout():
    K = PEER_TOPK
    groups = [("a", 0, 0), ("a", 0, 8), ("a", 1, 0), ("a", 2, 0), ("a", 3, 0),
              ("b", 0, 8), ("b", 0, 0), ("b", 1, 0), ("b", 2, 0)]
    seen = set()
    pos, valid = [], []
    for kind, fixed, start in groups:
        for r in range(SUBLANES):
            a, b = (fixed, start + r) if kind == "a" else (start + r, fixed)
            ok = (a + 1) * (b + 1) <= K and (a, b) not in seen
            if ok:
                seen.add((a, b))
            pos.append(a * K + b if ok else K * K + len(pos))
            valid.append(ok)
    assert len(seen) == sum(K // (a + 1) for a in range(K))
    return groups, np.array(pos, np.int32), np.array(valid, bool)


def _extract_topk(s, key, payload, k, big):
    n, tm = s.shape
    S = SUBLANES
    s3 = s.reshape(n // S, S, tm)
    key3 = key.reshape(n // S, S, tm)
    pay3 = None if payload is key else payload.reshape(n // S, S, tm)
    slot = lax.broadcasted_iota(jnp.int32, (k // S, S, tm), 0) * S + lax.broadcasted_iota(
        jnp.int32, (k // S, S, tm), 1)

    def all_reduce(x3, op):
        r = x3[0]
        for g in range(1, x3.shape[0]):
            r = op(r, x3[g])
        for sh in (S // 2, S // 4, S // 8):
            r = op(r, pltpu.roll(r, sh, axis=0))
        return r

    def body(j, carry):
        s3, vals, pay = carry
        m = all_reduce(s3, jnp.maximum)
        kk = all_reduce(jnp.where(s3 == m[None], key3, big), jnp.minimum)
        sel = key3 == kk[None]
        p = kk if pay3 is None else all_reduce(jnp.where(sel, pay3, 0), jnp.add)
        vals = jnp.where(slot == j, m[None], vals)
        pay = jnp.where(slot == j, p[None], pay)
        s3 = jnp.where(sel, -jnp.inf, s3)
        return s3, vals, pay

    init = (s3, jnp.zeros((k // S, S, tm), F32), jnp.zeros((k // S, S, tm), jnp.int32))
    _, vals, pay = lax.fori_loop(0, k, body, init)
    return vals.reshape(k, tm), pay.reshape(k, tm)


def _route_kernel(x_ref, oh_ref, od_ref, woh_ref, wod_ref, g2_ref, wq_ref, sk_ref, cpos_ref, cmask_ref,
                  h_ref, hn_ref, idx_ref, gate_ref, q_s, tv_s, ti_s, gs_s, is_s, *, groups):
    K = PEER_TOPK
    tm = x_ref.shape[0]
    h = x_ref[...] + _dot(oh_ref[...], woh_ref[...]) + _dot(od_ref[...], wod_ref[...])
    h_ref[...] = h
    hn = h * lax.rsqrt(jnp.mean(h * h, axis=-1, keepdims=True) + EPS) * g2_ref[...]
    half = hn.shape[1] // 2
    hn_ref[...] = _pack_bf16_pair(hn[:, :half], hn[:, half:])
    q_s[...] = _dot(hn.astype(BF16), wq_ref[...]).astype(BF16)

    key_iota = lax.broadcasted_iota(jnp.int32, (PEER_NKEYS, tm), 0)

    def half_topk(hp, _):
        col = pl.multiple_of(hp * PEER_DKEY, PEER_DKEY)
        s = _dot_nt(sk_ref[hp], q_s[:, pl.ds(col, PEER_DKEY)])
        vals, idxs = _extract_topk(s, key_iota, key_iota, K, PEER_NKEYS)
        tv_s[hp] = vals
        ti_s[hp] = idxs
        return 0

    lax.fori_loop(0, 2 * PEER_HEADS, half_topk, 0)

    cpos = cpos_ref[...]
    cmask = cmask_ref[...]

    def head(hd, _):
        s0, s1 = tv_s[2 * hd], tv_s[2 * hd + 1]
        i0, i1 = ti_s[2 * hd] * PEER_NKEYS, ti_s[2 * hd + 1]
        cs, ci = [], []
        for kind, fixed, start in groups:
            if kind == "a":
                cs.append(s0[fixed:fixed + 1] + s1[start:start + SUBLANES])
                ci.append(i0[fixed:fixed + 1] + i1[start:start + SUBLANES])
            else:
                cs.append(s0[start:start + SUBLANES] + s1[fixed:fixed + 1])
                ci.append(i0[start:start + SUBLANES] + i1[fixed:fixed + 1])
        cand = jnp.concatenate(cs, axis=0) + cmask
        cidx = jnp.concatenate(ci, axis=0)
        best, eidx = _extract_topk(cand, cpos, cidx, K, 2 * K * K)
        ex = jnp.exp(best - best[0:1])
        gate = ex / jnp.sum(ex, axis=0, keepdims=True)
        row = pl.ds(pl.multiple_of(hd * K, K), K)
        gs_s[row, :] = gate
        is_s[row, :] = eidx.astype(F32)
        return 0

    lax.fori_loop(0, PEER_HEADS, head, 0)
    gate_ref[...] = gs_s[...].T
    idx_ref[...] = is_s[...].T.astype(jnp.int32)


def _route(x2, oh2, od2, w_out, g2, w_q, sub_keys):
    T, D = x2.shape
    tm = ROUTE_TM
    K = PEER_TOPK
    nh = oh2.shape[1]
    nq = w_q.shape[1]
    npk = PEER_HEADS * K
    groups, pos, valid = _cand_layout()
    ncand = pos.shape[0]
    cpos = jnp.asarray(np.broadcast_to(pos[:, None], (ncand, tm)))
    cmask = jnp.asarray(np.broadcast_to(np.where(valid, 0.0, -np.inf).astype(np.float32)[:, None], (ncand, tm)))
    row = lambda i: (i, 0)
    full2 = lambda i: (0, 0)
    return pl.pallas_call(
        functools.partial(_route_kernel, groups=groups),
        grid=(T // tm,),
        in_specs=[
            pl.BlockSpec((tm, D), row),
            pl.BlockSpec((tm, nh), row),
            pl.BlockSpec((tm, nh), row),
            pl.BlockSpec((nh, D), full2),
            pl.BlockSpec((nh, D), lambda i: (1, 0)),
            pl.BlockSpec((1, D), full2),
            pl.BlockSpec((D, nq), full2),
            pl.BlockSpec(sub_keys.shape, lambda i: (0, 0, 0)),
            pl.BlockSpec((ncand, tm), full2),
            pl.BlockSpec((ncand, tm), full2),
        ],
        out_specs=[
            pl.BlockSpec((tm, D), row),
            pl.BlockSpec((tm, D // 2), row),
            pl.BlockSpec((tm, npk), row),
            pl.BlockSpec((tm, npk), row),
        ],
        out_shape=[
            jax.ShapeDtypeStruct((T, D), F32),
            jax.ShapeDtypeStruct((T, D // 2), jnp.int32),
            jax.ShapeDtypeStruct((T, npk), jnp.int32),
            jax.ShapeDtypeStruct((T, npk), F32),
        ],
        scratch_shapes=[
            pltpu.VMEM((tm, nq), BF16),
            pltpu.VMEM((2 * PEER_HEADS, K, tm), F32),
            pltpu.VMEM((2 * PEER_HEADS, K, tm), jnp.int32),
            pltpu.VMEM((npk, tm), F32),
            pltpu.VMEM((npk, tm), F32),
        ],
        compiler_params=pltpu.CompilerParams(
            dimension_semantics=("parallel",), vmem_limit_bytes=VMEM_LIMIT),
        name="route",
    )(x2, oh2, od2, w_out, w_out, g2, w_q, sub_keys, cpos, cmask)


def _gelu(x):
    return 0.5 * x * (1.0 + lax.erf(x * (1.0 / math.sqrt(2.0))))


def _sc_mesh():
    return plsc.VectorSubcoreMesh(core_axis_name="c", subcore_axis_name="s")


def _sc_worker_id():
    return lax.axis_index("s") * SC_CORES + lax.axis_index("c")


def _sc_widen_pair(s):
    si = plsc.bitcast(s, jnp.int32)
    return plsc.bitcast(si & BF16_HI_MASK, F32), plsc.bitcast(si << 16, F32)


def _sc_gather_loop(tab_hbm, idx_v, bufs, sems, n_items, compute):
    def start(item, b):
        pltpu.async_copy(tab_hbm.at[idx_v.at[item]], bufs[b], sems[b])

    def wait(b):
        pltpu.make_async_copy(tab_hbm.at[idx_v.at[0]], bufs[b], sems[b]).wait()

    start(0, 0)

    @pl.loop(0, n_items // 2)
    def _(i2):
        it = 2 * i2
        start(it + 1, 1)
        wait(0)
        compute(it, bufs[0])

        @pl.when(it + 2 < n_items)
        def _():
            start(it + 2, 0)

        wait(1)
        compute(it + 1, bufs[1])


def _peer_hidden(u, idx4, x):
    T, DW = x.shape
    n_rows, R = idx4.shape
    ipt = n_rows // T
    G = SC_TOKEN_GROUP
    nit = G * ipt
    tpw = T // SC_WORKERS
    NL = SC_LANES
    RB = 8
    NV = 4
    GW = NV * NL
    tile = (R, DW // LANES, LANES)

    @functools.partial(
        pl.kernel, mesh=_sc_mesh(), compiler_params=pltpu.CompilerParams(needs_layout_passes=False),
        out_type=jax.ShapeDtypeStruct((n_rows, R), F32),
        scratch_types=[pltpu.VMEM((nit, R), jnp.int32), pltpu.VMEM((G, DW), jnp.int32),
                       pltpu.VMEM(tile, jnp.int32), pltpu.VMEM(tile, jnp.int32),
                       pltpu.VMEM((nit, R), F32),
                       pltpu.SemaphoreType.DMA, pltpu.SemaphoreType.DMA],
        name="peer_hidden")
    def k(u_hbm, idx_hbm, x_hbm, h_hbm, idx_v, x_v, buf0, buf1, h_v, sem0, sem1):
        tok0 = _sc_worker_id() * tpw
        lane = lax.broadcasted_iota(jnp.int32, (NL,), 0)
        zero = jnp.zeros((NL,), F32)

        def compute(item, buf):
            tok = item // ipt
            hvs = [zero for _ in range(R // NL)]
            for rb in range(R // RB):
                def body(g, accs):
                    sub = g // (LANES // GW)
                    base = (g % (LANES // GW)) * GW
                    xs = [plsc.bitcast(x_v[tok, pl.ds(pl.multiple_of(g * GW + jj * NL, NL), NL)], BF16)
                          for jj in range(NV)]
                    out = []
                    for j in range(RB):
                        ps = [plsc.bitcast(buf[rb * RB + j, sub, pl.ds(pl.multiple_of(base + jj * NL, NL), NL)],
                                           BF16) * xs[jj] for jj in range(NV)]
                        hi, lo = _sc_widen_pair((ps[0] + ps[1]) + (ps[2] + ps[3]))
                        out.append((accs[j] + hi) + lo)
                    return tuple(out)
                accs = lax.fori_loop(0, DW // GW, body, tuple(zero for _ in range(RB)))
                for j in range(RB):
                    r = rb * RB + j
                    hvs[r // NL] = jnp.where(lane == (r % NL), jnp.sum(accs[j]), hvs[r // NL])
            for q in range(R // NL):
                h_v[item, pl.ds(q * NL, NL)] = hvs[q]

        @pl.loop(0, tpw // G)
        def _(g):
            t0 = tok0 + g * G
            pltpu.sync_copy(idx_hbm.at[pl.ds(t0 * ipt, nit)], idx_v)
            pltpu.sync_copy(x_hbm.at[pl.ds(t0, G)], x_v)
            _sc_gather_loop(u_hbm, idx_v, (buf0, buf1), (sem0, sem1), nit, compute)
            pltpu.sync_copy(h_v, h_hbm.at[pl.ds(t0 * ipt, nit)])

    return k(u, idx4, x)


def _peer_combine(v, idx4, w4, T, D):
    n_rows, R = idx4.shape
    ipt = n_rows // T
    G = SC_TOKEN_GROUP
    nit = G * ipt
    tpw = T // SC_WORKERS
    NL = SC_LANES
    DW = D // 2
    half = DW // 2
    nv = half // NL
    RG = 4
    tile = (R, DW // LANES, LANES)

    @functools.partial(
        pl.kernel, mesh=_sc_mesh(), compiler_params=pltpu.CompilerParams(needs_layout_passes=False),
        out_type=jax.ShapeDtypeStruct((T, D), F32),
        scratch_types=[pltpu.VMEM((nit, R), jnp.int32), pltpu.VMEM((nit, R), jnp.int32),
                       pltpu.VMEM(tile, jnp.int32), pltpu.VMEM(tile, jnp.int32),
                       pltpu.VMEM((G, D), F32),
                       pltpu.SemaphoreType.DMA, pltpu.SemaphoreType.DMA],
        name="peer_combine")
    def k(v_hbm, idx_hbm, w_hbm, o_hbm, idx_v, w_v, buf0, buf1, out_v, sem0, sem1):
        tok0 = _sc_worker_id() * tpw
        zero = jnp.zeros((NL,), F32)

        def compute(item, buf):
            tok = item // ipt
            item_vec = jnp.full((NL,), item, jnp.int32)
            for hf in range(2):
                def body(rg, accs):
                    ws = [plsc.bitcast(plsc.load_gather(
                        w_v, [item_vec, jnp.full((NL,), rg * RG + rr, jnp.int32)]), BF16) for rr in range(RG)]
                    his, los = [], []
                    for i in range(nv):
                        word = hf * half + i * NL
                        ps = [plsc.bitcast(buf[rg * RG + rr, word // LANES, pl.ds(word % LANES, NL)], BF16) * ws[rr]
                              for rr in range(RG)]
                        hi, lo = _sc_widen_pair((ps[0] + ps[1]) + (ps[2] + ps[3]))
                        his.append(accs[i] + hi)
                        los.append(accs[nv + i] + lo)
                    return tuple(his + los)
                accs = lax.fori_loop(0, R // RG, body, tuple(zero for _ in range(2 * nv)))
                for i in range(nv):
                    word = hf * half + i * NL
                    plsc.addupdate(out_v.at[tok, pl.ds(word, NL)], accs[i])
                    plsc.addupdate(out_v.at[tok, pl.ds(DW + word, NL)], accs[nv + i])

        @pl.loop(0, tpw // G)
        def _(g):
            t0 = tok0 + g * G
            pltpu.sync_copy(idx_hbm.at[pl.ds(t0 * ipt, nit)], idx_v)
            pltpu.sync_copy(w_hbm.at[pl.ds(t0 * ipt, nit)], w_v)

            @pl.loop(0, G)
            def _(t):
                @pl.loop(0, D // NL)
                def _(i):
                    out_v[t, pl.ds(pl.multiple_of(i * NL, NL), NL)] = zero

            _sc_gather_loop(v_hbm, idx_v, (buf0, buf1), (sem0, sem1), nit, compute)
            pltpu.sync_copy(out_v, o_hbm.at[pl.ds(t0, G)])

    return k(v, idx4, w4)


def _act_kernel(h_ref, g_ref, w_ref):
    w = _gelu(h_ref[...]) * g_ref[...]
    w_ref[...] = _pack_bf16_pair(w, w)


def _peer_act(hraw, gate):
    T, n = gate.shape
    tm = PEER_ACT_TM
    spec = pl.BlockSpec((tm, n), lambda i: (i, 0))
    return pl.pallas_call(
        _act_kernel, grid=(T // tm,), in_specs=[spec, spec], out_specs=spec,
        out_shape=jax.ShapeDtypeStruct((T, n), jnp.int32),
        compiler_params=pltpu.CompilerParams(dimension_semantics=("parallel",)),
        name="peer_act",
    )(hraw, gate)


def _final_kernel(h_ref, p_ref, g_ref, y_ref):
    y = h_ref[...] + p_ref[...]
    ms = jnp.mean(y * y, axis=-1, keepdims=True)
    y_ref[...] = y * lax.rsqrt(ms + EPS) * g_ref[...]


def _final(h, po, g):
    T, D = h.shape
    tm = FINAL_TM
    spec = pl.BlockSpec((tm, D), lambda i: (i, 0))
    return pl.pallas_call(
        _final_kernel, grid=(T // tm,),
        in_specs=[spec, spec, pl.BlockSpec((1, D), lambda i: (0, 0))], out_specs=spec,
        out_shape=jax.ShapeDtypeStruct((T, D), F32),
        compiler_params=pltpu.CompilerParams(dimension_semantics=("parallel",)),
        name="final_norm",
    )(h, po, g)


def kernel(x, norm1_g, w_in, hgrn_lb_fwd, hgrn_lb_bwd, hgrn_out_g, diff_lam_q1, diff_lam_k1,
           diff_lam_q2, diff_lam_k2, diff_out_g, rel_bias, w_out, norm2_g, peer_w_q,
           peer_sub_keys, peer_u, peer_v, final_g):
    B, L, D = x.shape
    hw = HGRN_HEADS * HGRN_DK

    w = w_in[LAYER]
    scale = DIFF_HALF ** -0.5
    cols = lambda j: w[:, j * hw:(j + 1) * hw]
    w_r = jnp.concatenate([cols(1), cols(2), cols(0), cols(3), cols(4), cols(5) * scale, cols(6), cols(7)],
                          axis=1).astype(BF16)
    f32 = jnp.float32
    lam_init = 0.8 - 0.6 * math.exp(-0.3 * LAYER)
    lam = (jnp.exp(jnp.sum(diff_lam_q1[LAYER].astype(f32) * diff_lam_k1[LAYER].astype(f32)))
           - jnp.exp(jnp.sum(diff_lam_q2[LAYER].astype(f32) * diff_lam_k2[LAYER].astype(f32))) + lam_init)
    lam = lam.reshape(1, 1)
    bias_ext = _rel_bias_ext(rel_bias, L)
    sk = peer_sub_keys[LAYER].reshape(2 * PEER_HEADS, PEER_NKEYS, PEER_DKEY).astype(BF16)
    wq = peer_w_q[LAYER].reshape(D, -1).astype(BF16)
    wo = w_out[LAYER].astype(BF16)
    def pack_table(tab):
        words = _pack_bf16_pair(tab[:, :D // 2], tab[:, D // 2:])
        return words.reshape(tab.shape[0], D // 2 // LANES, LANES)
    u3, v3 = pack_table(peer_u[LAYER]), pack_table(peer_v[LAYER])

    bc = B // BATCH_CHUNKS
    tc = bc * L
    rows = tc * PEER_HEADS * PEER_TOPK // SC_GATHER_ROWS
    outs = []
    pending = {}
    combined = None
    for c in range(BATCH_CHUNKS + CHUNK_LAG + 1):
        x2 = x[c * bc:(c + 1) * bc].reshape(tc, D) if c < BATCH_CHUNKS else None
        if combined is not None:
            h_prev, po = combined
            if x2 is not None:
                x2, po = lax.optimization_barrier((x2, po))
            outs.append(_final(h_prev, po, final_g[None, :]).reshape(bc, L, D))
            combined = None
        if CHUNK_LAG <= c < BATCH_CHUNKS + CHUNK_LAG:
            h, idx4, gate, hraw = pending.pop(c - CHUNK_LAG)
            wts = _peer_act(hraw.reshape(tc, -1), gate)
            if x2 is not None:
                x2, wts = lax.optimization_barrier((x2, wts))
            combined = (h, _peer_combine(v3, idx4, wts.reshape(rows, SC_GATHER_ROWS), tc, D))
        if x2 is not None:
            zf, pb = _inproj(x2, norm1_g[LAYER][None, :], w_r)
            zf3 = zf.reshape(bc, L, -1)
            pb3 = pb.reshape(bc, L, -1)
            o_h = _hgrn(zf3, pb3, hgrn_lb_fwd, hgrn_lb_bwd, hgrn_out_g[LAYER][None, :])
            o_d = _attn(pb3, bias_ext, lam, diff_out_g[LAYER][None, :])
            h, hn, idx, gate = _route(x2, o_h.reshape(tc, -1), o_d.reshape(tc, -1), wo,
                                      norm2_g[LAYER][None, :], wq, sk)
            idx4 = idx.reshape(rows, SC_GATHER_ROWS)
            pending[c] = (h, idx4, gate, _peer_hidden(u3, idx4, hn))
    return jnp.concatenate(outs, axis=0)
```

```python
import functools
import math

import numpy as np
import jax
import jax.numpy as jnp
from jax import lax
from jax.experimental import pallas as pl
from jax.experimental.pallas import tpu as pltpu
from jax.experimental.pallas import tpu_sc as plsc

F32 = jnp.float32
BF16 = jnp.bfloat16
EPS = 1e-6

HGRN_HEADS = 4
HGRN_DK = 128
DIFF_HEADS = 4
DIFF_HALF = 64
REL_BUCKETS = 32
REL_MAX_DIST = 128
PEER_HEADS = 8
PEER_NKEYS = 128
PEER_DKEY = 128
PEER_TOPK = 16
LAYER = 0

LANES = 128
SUBLANES = 8
VMEM_LIMIT = 48 * 1024 * 1024

INPROJ_TM = 512
HG_C = 64
HG_LEVELS = (64, 32, 16, 8, 4, 2)
ATTN_TQ = 256
ROUTE_TM = 256
BATCH_CHUNKS = 16
CHUNK_LAG = 2
PEER_ACT_TM = 2048
FINAL_TM = 512

SC_CORES = 2
SC_SUBCORES = 16
SC_LANES = 16
SC_WORKERS = SC_CORES * SC_SUBCORES
SC_GATHER_ROWS = 64
SC_TOKEN_GROUP = 16
SC_COMBINE_GROUP = 8
SC_RING = 3


def _dot(a, b):
    return jnp.dot(a, b, preferred_element_type=F32)


def _dot_nt(a, b):
    return lax.dot_general(a, b, (((1,), (1,)), ((), ())), preferred_element_type=F32)


def _dot_tn(a, b):
    return lax.dot_general(a, b, (((0,), (0,)), ((), ())), preferred_element_type=F32)


def _silu(x):
    return x * (1.0 / (1.0 + jnp.exp(-x)))


BF16_HI_MASK = -65536


def _pack_bf16_pair(hi, lo):
    bits = lambda a: lax.bitcast_convert_type(a.astype(BF16).astype(F32), jnp.int32)
    return (bits(hi) & BF16_HI_MASK) | lax.shift_right_logical(bits(lo), 16)


def _inproj_kernel(x_ref, g_ref, w_ref, zf_ref, pb_ref):
    x = x_ref[...]
    ms = jnp.mean(x * x, axis=-1, keepdims=True)
    xn = (x * lax.rsqrt(ms + EPS) * g_ref[...]).astype(BF16)
    nz = zf_ref.shape[1]
    zf_ref[...] = _dot(xn, w_ref[:, 0:nz])
    nb = pb_ref.shape[1]
    step = 1024
    for j in range(nb // step):
        pb_ref[:, j * step:(j + 1) * step] = _dot(
            xn, w_ref[:, nz + j * step: nz + (j + 1) * step]).astype(BF16)


def _inproj(x2, g, w):
    T, D = x2.shape
    N = w.shape[1]
    nz = 2 * HGRN_HEADS * HGRN_DK
    tm = INPROJ_TM
    return pl.pallas_call(
        _inproj_kernel,
        grid=(T // tm,),
        in_specs=[
            pl.BlockSpec((tm, D), lambda i: (i, 0)),
            pl.BlockSpec((1, D), lambda i: (0, 0)),
            pl.BlockSpec((D, N), lambda i: (0, 0)),
        ],
        out_specs=[
            pl.BlockSpec((tm, nz), lambda i: (i, 0)),
            pl.BlockSpec((tm, N - nz), lambda i: (i, 0)),
        ],
        out_shape=[
            jax.ShapeDtypeStruct((T, nz), F32),
            jax.ShapeDtypeStruct((T, N - nz), BF16),
        ],
        compiler_params=pltpu.CompilerParams(
            dimension_semantics=("parallel",), vmem_limit_bytes=VMEM_LIMIT),
        name="inproj",
    )(x2, g, w)


def _hgrn_consts():
    C = HG_C
    r = np.arange(C)
    t = r[:, None]
    u = r[None, :]
    blocks = [u <= t, u > t]
    masks = [np.eye(C, dtype=bool)]
    for B in HG_LEVELS:
        half = B // 2
        a = (r // B) * B
        m = (a + half - 1)[:, None]
        upper = (r - a) >= half
        blocks.append(np.where(upper[:, None], (u > m) & (u <= t), (u > t) & (u <= m)))
        same = a[:, None] == a[None, :]
        masks.append(same & upper[:, None] & (~upper)[None, :])
    m_f = np.concatenate(blocks, 0).astype(np.float32)
    m_b = np.concatenate([b[::-1, ::-1] for b in blocks], 0).astype(np.float32)
    k_f = np.stack(masks).astype(np.float32)
    k_b = np.ascontiguousarray(k_f.transpose(0, 2, 1))
    return m_f, m_b, k_f, k_b


def _hgrn_gates(z, tab):
    tabf = tab.astype(F32)
    e = jnp.exp(tabf - jnp.max(tabf, axis=0, keepdims=True))
    lb = jnp.sum(e[0:LAYER + 1], axis=0, keepdims=True) / jnp.sum(e, axis=0, keepdims=True)
    log_lb = jnp.log(lb)
    log_1m = jnp.log1p(-lb)
    ez = jnp.exp(-jnp.abs(z))
    l1p = jnp.log1p(ez)
    log_sig = jnp.minimum(z, 0.0) - l1p
    c = log_1m + log_sig
    hi = jnp.maximum(log_lb, c)
    lo = jnp.minimum(log_lb, c)
    log_f = hi + jnp.log1p(jnp.exp(lo - hi))
    sig_neg = jnp.where(z >= 0.0, ez, 1.0) / (1.0 + ez)
    k = (1.0 - lb) * sig_neg
    return log_f, k


def _hgrn_kernel(lbf_ref, lbb_ref, og_ref, mf_ref, mb_ref, kf_ref, kb_ref,
                 q_ref, zf_ref, zb_ref, v_ref, g_ref, o_ref, of_s, ob_s):
    C = HG_C
    L = q_ref.shape[0]
    n = L // C
    dv = v_ref.shape[1]

    def chunk(c, st, m_ref, k_ref, z_ref, tab_ref, forward):
        sl = pl.ds(pl.multiple_of(c * C, C), C)
        qh = q_ref[sl, :].astype(F32)
        q = _silu(qh)
        v = v_ref[sl, :]
        log_f, k = _hgrn_gates(z_ref[sl, :], tab_ref[...])
        lf_hi = log_f.astype(BF16)
        lf_lo = (log_f - lf_hi.astype(F32)).astype(BF16)
        m = m_ref[...]
        e = jnp.exp(_dot(m, lf_hi) + _dot(m, lf_lo))
        e_b = e[0:C]
        e_s = e[C:2 * C]
        dec = e[C - 1:C] if forward else e[0:1]
        o = _dot_nt((q * e_b).astype(BF16), st.astype(BF16))
        a = _dot_nt(q.astype(BF16), k.astype(BF16)) * k_ref[0]
        for l in range(len(HG_LEVELS)):
            e_l = e[(2 + l) * C:(3 + l) * C]
            a = a + _dot_nt((q * e_l).astype(BF16), (k * e_l).astype(BF16)) * k_ref[l + 1]
        o = o + _dot(a.astype(BF16), v)
        st = st * dec + _dot_tn(v, (k * e_s).astype(BF16))
        return sl, o, st

    st0 = jnp.zeros((dv, q_ref.shape[1]), F32)

    def both(i, carry):
        st_f, st_b = carry
        sl_f, o_f, st_f = chunk(i, st_f, mf_ref, kf_ref, zf_ref, lbf_ref, True)
        sl_b, o_b, st_b = chunk(n - 1 - i, st_b, mb_ref, kb_ref, zb_ref, lbb_ref, False)
        of_s[sl_f, :] = o_f
        ob_s[sl_b, :] = o_b
        return st_f, st_b

    lax.fori_loop(0, n, both, (st0, st0))

    def finish(c, _):
        sl = pl.ds(pl.multiple_of(c * C, C), C)
        tot = of_s[sl, :] + ob_s[sl, :]
        ms = jnp.mean(tot * tot, axis=-1, keepdims=True)
        y = tot * lax.rsqrt(ms + EPS) * og_ref[...]
        o_ref[sl, :] = (y * _silu(g_ref[sl, :].astype(F32))).astype(o_ref.dtype)
        return 0

    lax.fori_loop(0, n, finish, 0)


def _hgrn(zf3, pb3, lb_f, lb_b, out_g):
    B, L, _ = zf3.shape
    H, dk = HGRN_HEADS, HGRN_DK
    m_f, m_b, k_f, k_b = _hgrn_consts()
    nlev = k_f.shape[0]
    full2 = lambda b, h: (0, 0)
    full3 = lambda b, h: (0, 0, 0)
    seq = lambda off: pl.BlockSpec((None, L, dk), lambda b, h: (b, 0, off + h))
    return pl.pallas_call(
        _hgrn_kernel,
        grid=(B, H),
        in_specs=[
            pl.BlockSpec((lb_f.shape[0], dk), lambda b, h: (0, h)),
            pl.BlockSpec((lb_b.shape[0], dk), lambda b, h: (0, h)),
            pl.BlockSpec((1, dk), full2),
            pl.BlockSpec(m_f.shape, full2),
            pl.BlockSpec(m_b.shape, full2),
            pl.BlockSpec((nlev, HG_C, HG_C), full3),
            pl.BlockSpec((nlev, HG_C, HG_C), full3),
            seq(0),
            seq(0),
            seq(H),
            seq(H),
            seq(2 * H),
        ],
        out_specs=pl.BlockSpec((None, L, dk), lambda b, h: (b, 0, h)),
        out_shape=jax.ShapeDtypeStruct((B, L, H * dk), BF16),
        scratch_shapes=[pltpu.VMEM((L, dk), F32), pltpu.VMEM((L, dk), F32)],
        compiler_params=pltpu.CompilerParams(
            dimension_semantics=("parallel", "parallel"), vmem_limit_bytes=VMEM_LIMIT),
        name="hgrn",
    )(lb_f, lb_b, out_g, jnp.asarray(m_f, BF16), jnp.asarray(m_b, BF16),
      jnp.asarray(k_f), jnp.asarray(k_b), pb3, zf3, zf3, pb3, pb3)


def _t5_bucket(rel):
    nb = REL_BUCKETS // 2
    ret = jnp.where(rel > 0, nb, 0)
    n = jnp.abs(rel)
    max_exact = nb // 2
    nf = jnp.maximum(n, 1).astype(jnp.float32)
    large = max_exact + (jnp.log(nf / max_exact) / math.log(REL_MAX_DIST / max_exact)
                         * (nb - max_exact)).astype(jnp.int32)
    large = jnp.minimum(large, nb - 1)
    return ret + jnp.where(n < max_exact, n, large)


def _rel_bias_ext(rel_bias, L):
    j = jnp.arange(2 * L - LANES, dtype=jnp.int32)
    ql = jnp.arange(LANES, dtype=jnp.int32)
    rel = j[None, :] - (L - LANES) - ql[:, None]
    bucket = _t5_bucket(rel)
    tab = rel_bias.astype(F32)
    out = jnp.zeros((tab.shape[1],) + bucket.shape, F32)
    for b in range(REL_BUCKETS):
        out = jnp.where((bucket == b)[None], tab[b][:, None, None], out)
    return out


def _attn_kernel(lam_ref, q_ref, k_ref, v_ref, bias_ref, og_ref, o_ref, *, lam_init):
    tq = q_ref.shape[0]
    L = k_ref.shape[0]
    qi = pl.program_id(2)
    lam = lam_ref[0, 0]
    q = q_ref[...]
    k = k_ref[...]
    lane = lax.broadcasted_iota(jnp.int32, q.shape, 1)
    zero = jnp.zeros_like(q)
    q0 = jnp.where(lane < DIFF_HALF, q, zero)
    q1 = jnp.where(lane >= DIFF_HALF, q, zero)
    parts = []
    for j in range(tq // LANES):
        off = pl.multiple_of(L - LANES - (qi * tq + j * LANES), LANES)
        parts.append(bias_ref[:, pl.ds(off, L)])
    bias = jnp.concatenate(parts, axis=0) if len(parts) > 1 else parts[0]

    def soft(qm):
        s = _dot_nt(qm, k) + bias
        e = jnp.exp(s - jnp.max(s, axis=-1, keepdims=True))
        return e, 1.0 / jnp.sum(e, axis=-1, keepdims=True)

    e0, r0 = soft(q0)
    e1, r1 = soft(q1)
    w = e0 * r0 - e1 * (lam * r1)
    o = _dot(w.astype(BF16), v_ref[...])
    ms = jnp.mean(o * o, axis=-1, keepdims=True)
    y = o * lax.rsqrt(ms + EPS) * og_ref[...] * (1.0 - lam_init)
    o_ref[...] = y.astype(o_ref.dtype)


def _attn(pb3, bias_ext, lam, out_g):
    B, L, _ = pb3.shape
    H, dh = DIFF_HEADS, 2 * DIFF_HALF
    tq = ATTN_TQ
    base = (3 * HGRN_HEADS * HGRN_DK) // dh
    lam_init = 0.8 - 0.6 * math.exp(-0.3 * LAYER)
    return pl.pallas_call(
        functools.partial(_attn_kernel, lam_init=lam_init),
        grid=(B, H, L // tq),
        in_specs=[
            pl.BlockSpec(memory_space=pltpu.SMEM),
            pl.BlockSpec((None, tq, dh), lambda b, h, i: (b, i, base + h)),
            pl.BlockSpec((None, L, dh), lambda b, h, i: (b, 0, base + H + h)),
            pl.BlockSpec((None, L, dh), lambda b, h, i: (b, 0, base + 2 * H + h)),
            pl.BlockSpec((None, LANES, 2 * L - LANES), lambda b, h, i: (h, 0, 0)),
            pl.BlockSpec((1, dh), lambda b, h, i: (0, 0)),
        ],
        out_specs=pl.BlockSpec((None, tq, dh), lambda b, h, i: (b, i, h)),
        out_shape=jax.ShapeDtypeStruct((B, L, H * dh), BF16),
        compiler_params=pltpu.CompilerParams(
            dimension_semantics=("parallel", "parallel", "parallel"),
            vmem_limit_bytes=VMEM_LIMIT),
        name="attn",
    )(lam, pb3, pb3, pb3, bias_ext, out_g)


def _cand_layout():
    K = PEER_TOPK
    groups = [("a", 0, 0), ("a", 0, 8), ("a", 1, 0), ("a", 2, 0), ("a", 3, 0),
              ("b", 0, 8), ("b", 0, 0), ("b", 1, 0), ("b", 2, 0)]
    seen = set()
    pos, valid = [], []
    for kind, fixed, start in groups:
        for r in range(SUBLANES):
            a, b = (fixed, start + r) if kind == "a" else (start + r, fixed)
            ok = (a + 1) * (b + 1) <= K and (a, b) not in seen
            if ok:
                seen.add((a, b))
            pos.append(a * K + b if ok else K * K + len(pos))
            valid.append(ok)
    assert len(seen) == sum(K // (a + 1) for a in range(K))
    return groups, np.array(pos, np.int32), np.array(valid, bool)


def _extract_topk(s, key, payload, k, big):
    n, tm = s.shape
    S = SUBLANES
    s3 = s.reshape(n // S, S, tm)
    key3 = key.reshape(n // S, S, tm)
    pay3 = None if payload is key else payload.reshape(n // S, S, tm)
    slot = lax.broadcasted_iota(jnp.int32, (k // S, S, tm), 0) * S + lax.broadcasted_iota(
        jnp.int32, (k // S, S, tm), 1)

    def all_reduce(x3, op):
        r = x3[0]
        for g in range(1, x3.shape[0]):
            r = op(r, x3[g])
        for sh in (S // 2, S // 4, S // 8):
            r = op(r, pltpu.roll(r, sh, axis=0))
        return r

    def body(j, carry):
        s3, vals, pay = carry
        m = all_reduce(s3, jnp.maximum)
        kk = all_reduce(jnp.where(s3 == m[None], key3, big), jnp.minimum)
        sel = key3 == kk[None]
        p = kk if pay3 is None else all_reduce(jnp.where(sel, pay3, 0), jnp.add)
        vals = jnp.where(slot == j, m[None], vals)
        pay = jnp.where(slot == j, p[None], pay)
        s3 = jnp.where(sel, -jnp.inf, s3)
        return s3, vals, pay

    init = (s3, jnp.zeros((k // S, S, tm), F32), jnp.zeros((k // S, S, tm), jnp.int32))
    _, vals, pay = lax.fori_loop(0, k, body, init)
    return vals.reshape(k, tm), pay.reshape(k, tm)


def _route_kernel(x_ref, oh_ref, od_ref, woh_ref, wod_ref, g2_ref, wq_ref, sk_ref, cpos_ref, cmask_ref,
                  h_ref, hn_ref, idx_ref, gate_ref, q_s, tv_s, ti_s, gs_s, is_s, *, groups):
    K = PEER_TOPK
    tm = x_ref.shape[0]
    h = x_ref[...] + _dot(oh_ref[...], woh_ref[...]) + _dot(od_ref[...], wod_ref[...])
    h_ref[...] = h
    hn = h * lax.rsqrt(jnp.mean(h * h, axis=-1, keepdims=True) + EPS) * g2_ref[...]
    half = hn.shape[1] // 2
    hn_ref[...] = _pack_bf16_pair(hn[:, :half], hn[:, half:])
    q_s[...] = _dot(hn.astype(BF16), wq_ref[...]).astype(BF16)

    key_iota = lax.broadcasted_iota(jnp.int32, (PEER_NKEYS, tm), 0)

    def half_topk(hp, _):
        col = pl.multiple_of(hp * PEER_DKEY, PEER_DKEY)
        s = _dot_nt(sk_ref[hp], q_s[:, pl.ds(col, PEER_DKEY)])
        vals, idxs = _extract_topk(s, key_iota, key_iota, K, PEER_NKEYS)
        tv_s[hp] = vals
        ti_s[hp] = idxs
        return 0

    lax.fori_loop(0, 2 * PEER_HEADS, half_topk, 0)

    cpos = cpos_ref[...]
    cmask = cmask_ref[...]

    def head(hd, _):
        s0, s1 = tv_s[2 * hd], tv_s[2 * hd + 1]
        i0, i1 = ti_s[2 * hd] * PEER_NKEYS, ti_s[2 * hd + 1]
        cs, ci = [], []
        for kind, fixed, start in groups:
            if kind == "a":
                cs.append(s0[fixed:fixed + 1] + s1[start:start + SUBLANES])
                ci.append(i0[fixed:fixed + 1] + i1[start:start + SUBLANES])
            else:
                cs.append(s0[start:start + SUBLANES] + s1[fixed:fixed + 1])
                ci.append(i0[start:start + SUBLANES] + i1[fixed:fixed + 1])
        cand = jnp.concatenate(cs, axis=0) + cmask
        cidx = jnp.concatenate(ci, axis=0)
        best, eidx = _extract_topk(cand, cpos, cidx, K, 2 * K * K)
        ex = jnp.exp(best - best[0:1])
        gate = ex / jnp.sum(ex, axis=0, keepdims=True)
        row = pl.ds(pl.multiple_of(hd * K, K), K)
        gs_s[row, :] = gate
        is_s[row, :] = eidx.astype(F32)
        return 0

    lax.fori_loop(0, PEER_HEADS, head, 0)
    gate_ref[...] = gs_s[...].T
    idx_ref[...] = is_s[...].T.astype(jnp.int32)


def _route(x2, oh2, od2, w_out, g2, w_q, sub_keys):
    T, D = x2.shape
    tm = ROUTE_TM
    K = PEER_TOPK
    nh = oh2.shape[1]
    nq = w_q.shape[1]
    npk = PEER_HEADS * K
    groups, pos, valid = _cand_layout()
    ncand = pos.shape[0]
    cpos = jnp.asarray(np.broadcast_to(pos[:, None], (ncand, tm)))
    cmask = jnp.asarray(np.broadcast_to(np.where(valid, 0.0, -np.inf).astype(np.float32)[:, None], (ncand, tm)))
    row = lambda i: (i, 0)
    full2 = lambda i: (0, 0)
    return pl.pallas_call(
        functools.partial(_route_kernel, groups=groups),
        grid=(T // tm,),
        in_specs=[
            pl.BlockSpec((tm, D), row),
            pl.BlockSpec((tm, nh), row),
            pl.BlockSpec((tm, nh), row),
            pl.BlockSpec((nh, D), full2),
            pl.BlockSpec((nh, D), lambda i: (1, 0)),
            pl.BlockSpec((1, D), full2),
            pl.BlockSpec((D, nq), full2),
            pl.BlockSpec(sub_keys.shape, lambda i: (0, 0, 0)),
            pl.BlockSpec((ncand, tm), full2),
            pl.BlockSpec((ncand, tm), full2),
        ],
        out_specs=[
            pl.BlockSpec((tm, D), row),
            pl.BlockSpec((tm, D // 2), row),
            pl.BlockSpec((tm, npk), row),
            pl.BlockSpec((tm, npk), row),
        ],
        out_shape=[
            jax.ShapeDtypeStruct((T, D), F32),
            jax.ShapeDtypeStruct((T, D // 2), jnp.int32),
            jax.ShapeDtypeStruct((T, npk), jnp.int32),
            jax.ShapeDtypeStruct((T, npk), F32),
        ],
        scratch_shapes=[
            pltpu.VMEM((tm, nq), BF16),
            pltpu.VMEM((2 * PEER_HEADS, K, tm), F32),
            pltpu.VMEM((2 * PEER_HEADS, K, tm), jnp.int32),
            pltpu.VMEM((npk, tm), F32),
            pltpu.VMEM((npk, tm), F32),
        ],
        compiler_params=pltpu.CompilerParams(
            dimension_semantics=("parallel",), vmem_limit_bytes=VMEM_LIMIT),
        name="route",
    )(x2, oh2, od2, w_out, w_out, g2, w_q, sub_keys, cpos, cmask)


def _gelu(x):
    return 0.5 * x * (1.0 + lax.erf(x * (1.0 / math.sqrt(2.0))))


def _sc_mesh():
    return plsc.VectorSubcoreMesh(core_axis_name="c", subcore_axis_name="s")


def _sc_worker_id():
    return lax.axis_index("s") * SC_CORES + lax.axis_index("c")


def _sc_widen_pair(s):
    si = plsc.bitcast(s, jnp.int32)
    return plsc.bitcast(si & BF16_HI_MASK, F32), plsc.bitcast(si << 16, F32)


def _sc_gather_loop(tab_hbm, idx_v, bufs, sems, n_items, compute):
    nb = len(bufs)

    def start(item, b):
        pltpu.async_copy(tab_hbm.at[idx_v.at[item]], bufs[b], sems[b])

    def wait(b):
        pltpu.make_async_copy(tab_hbm.at[idx_v.at[0]], bufs[b], sems[b]).wait()

    for p in range(nb - 1):
        start(p, p)

    @pl.loop(0, pl.cdiv(n_items, nb))
    def _(i):
        for b in range(nb):
            it = i * nb + b

            @pl.when(it < n_items)
            def _():
                @pl.when(it + nb - 1 < n_items)
                def _():
                    start(it + nb - 1, (b + nb - 1) % nb)

                wait(b)
                compute(it, bufs[b])


def _peer_hidden(u, idx4, x):
    T, DW = x.shape
    n_rows, R = idx4.shape
    ipt = n_rows // T
    G = SC_TOKEN_GROUP
    nit = G * ipt
    tpw = T // SC_WORKERS
    NL = SC_LANES
    RB = 8
    NV = 4
    GW = NV * NL
    tile = (R, DW // LANES, LANES)

    @functools.partial(
        pl.kernel, mesh=_sc_mesh(), compiler_params=pltpu.CompilerParams(needs_layout_passes=False),
        out_type=jax.ShapeDtypeStruct((n_rows, R), F32),
        scratch_types=[pltpu.VMEM((nit, R), jnp.int32), pltpu.VMEM((G, DW), jnp.int32),
                       [pltpu.VMEM(tile, jnp.int32)] * SC_RING, [pltpu.SemaphoreType.DMA] * SC_RING,
                       pltpu.VMEM((nit, R), F32)],
        name="peer_hidden")
    def k(u_hbm, idx_hbm, x_hbm, h_hbm, idx_v, x_v, bufs, sems, h_v):
        tok0 = _sc_worker_id() * tpw
        lane = lax.broadcasted_iota(jnp.int32, (NL,), 0)
        zero = jnp.zeros((NL,), F32)

        def compute(item, buf):
            tok = item // ipt
            hvs = [zero for _ in range(R // NL)]
            for rb in range(R // RB):
                def body(g, accs):
                    sub = g // (LANES // GW)
                    base = (g % (LANES // GW)) * GW
                    xs = [plsc.bitcast(x_v[tok, pl.ds(pl.multiple_of(g * GW + jj * NL, NL), NL)], BF16)
                          for jj in range(NV)]
                    out = []
                    for j in range(RB):
                        ps = [plsc.bitcast(buf[rb * RB + j, sub, pl.ds(pl.multiple_of(base + jj * NL, NL), NL)],
                                           BF16) * xs[jj] for jj in range(NV)]
                        hi, lo = _sc_widen_pair((ps[0] + ps[1]) + (ps[2] + ps[3]))
                        out.append((accs[j] + hi) + lo)
                    return tuple(out)
                accs = lax.fori_loop(0, DW // GW, body, tuple(zero for _ in range(RB)))
                for j in range(RB):
                    r = rb * RB + j
                    hvs[r // NL] = jnp.where(lane == (r % NL), jnp.sum(accs[j]), hvs[r // NL])
            for q in range(R // NL):
                h_v[item, pl.ds(q * NL, NL)] = hvs[q]

        @pl.loop(0, tpw // G)
        def _(g):
            t0 = tok0 + g * G
            pltpu.sync_copy(idx_hbm.at[pl.ds(t0 * ipt, nit)], idx_v)
            pltpu.sync_copy(x_hbm.at[pl.ds(t0, G)], x_v)
            _sc_gather_loop(u_hbm, idx_v, bufs, sems, nit, compute)
            pltpu.sync_copy(h_v, h_hbm.at[pl.ds(t0 * ipt, nit)])

    return k(u, idx4, x)


def _peer_combine(v, idx4, w4, T, D):
    n_rows, R = idx4.shape
    ipt = n_rows // T
    G = SC_COMBINE_GROUP
    nit = G * ipt
    tpw = T // SC_WORKERS
    NL = SC_LANES
    DW = D // 2
    half = DW // 2
    nv = half // NL
    RG = 4
    tile = (R, DW // LANES, LANES)

    @functools.partial(
        pl.kernel, mesh=_sc_mesh(), compiler_params=pltpu.CompilerParams(needs_layout_passes=False),
        out_type=jax.ShapeDtypeStruct((T, D), F32),
        scratch_types=[pltpu.VMEM((nit, R), jnp.int32), pltpu.VMEM((nit, R), jnp.int32),
                       [pltpu.VMEM(tile, jnp.int32)] * SC_RING, [pltpu.SemaphoreType.DMA] * SC_RING,
                       pltpu.VMEM((G, D), F32)],
        name="peer_combine")
    def k(v_hbm, idx_hbm, w_hbm, o_hbm, idx_v, w_v, bufs, sems, out_v):
        tok0 = _sc_worker_id() * tpw
        zero = jnp.zeros((NL,), F32)

        def compute(item, buf):
            tok = item // ipt
            item_vec = jnp.full((NL,), item, jnp.int32)
            for hf in range(2):
                def body(rg, accs):
                    ws = [plsc.bitcast(plsc.load_gather(
                        w_v, [item_vec, jnp.full((NL,), rg * RG + rr, jnp.int32)]), BF16) for rr in range(RG)]
                    his, los = [], []
                    for i in range(nv):
                        word = hf * half + i * NL
                        ps = [plsc.bitcast(buf[rg * RG + rr, word // LANES, pl.ds(word % LANES, NL)], BF16) * ws[rr]
                              for rr in range(RG)]
                        hi, lo = _sc_widen_pair((ps[0] + ps[1]) + (ps[2] + ps[3]))
                        his.append(accs[i] + hi)
                        los.append(accs[nv + i] + lo)
                    return tuple(his + los)
                accs = lax.fori_loop(0, R // RG, body, tuple(zero for _ in range(2 * nv)))
                for i in range(nv):
                    word = hf * half + i * NL
                    plsc.addupdate(out_v.at[tok, pl.ds(word, NL)], accs[i])
                    plsc.addupdate(out_v.at[tok, pl.ds(DW + word, NL)], accs[nv + i])

        @pl.loop(0, tpw // G)
        def _(g):
            t0 = tok0 + g * G
            pltpu.sync_copy(idx_hbm.at[pl.ds(t0 * ipt, nit)], idx_v)
            pltpu.sync_copy(w_hbm.at[pl.ds(t0 * ipt, nit)], w_v)

            @pl.loop(0, G)
            def _(t):
                @pl.loop(0, D // NL)
                def _(i):
                    out_v[t, pl.ds(pl.multiple_of(i * NL, NL), NL)] = zero

            _sc_gather_loop(v_hbm, idx_v, bufs, sems, nit, compute)
            pltpu.sync_copy(out_v, o_hbm.at[pl.ds(t0, G)])

    return k(v, idx4, w4)


def _act_kernel(h_ref, g_ref, w_ref):
    w = _gelu(h_ref[...]) * g_ref[...]
    w_ref[...] = _pack_bf16_pair(w, w)


def _peer_act(hraw, gate):
    T, n = gate.shape
    tm = PEER_ACT_TM
    spec = pl.BlockSpec((tm, n), lambda i: (i, 0))
    return pl.pallas_call(
        _act_kernel, grid=(T // tm,), in_specs=[spec, spec], out_specs=spec,
        out_shape=jax.ShapeDtypeStruct((T, n), jnp.int32),
        compiler_params=pltpu.CompilerParams(dimension_semantics=("parallel",)),
        name="peer_act",
    )(hraw, gate)


def _final_kernel(h_ref, p_ref, g_ref, y_ref):
    y = h_ref[...] + p_ref[...]
    ms = jnp.mean(y * y, axis=-1, keepdims=True)
    y_ref[...] = y * lax.rsqrt(ms + EPS) * g_ref[...]


def _final(h, po, g):
    T, D = h.shape
    tm = FINAL_TM
    spec = pl.BlockSpec((tm, D), lambda i: (i, 0))
    return pl.pallas_call(
        _final_kernel, grid=(T // tm,),
        in_specs=[spec, spec, pl.BlockSpec((1, D), lambda i: (0, 0))], out_specs=spec,
        out_shape=jax.ShapeDtypeStruct((T, D), F32),
        compiler_params=pltpu.CompilerParams(dimension_semantics=("parallel",)),
        name="final_norm",
    )(h, po, g)


def kernel(x, norm1_g, w_in, hgrn_lb_fwd, hgrn_lb_bwd, hgrn_out_g, diff_lam_q1, diff_lam_k1,
           diff_lam_q2, diff_lam_k2, diff_out_g, rel_bias, w_out, norm2_g, peer_w_q,
           peer_sub_keys, peer_u, peer_v, final_g):
    B, L, D = x.shape
    hw = HGRN_HEADS * HGRN_DK

    w = w_in[LAYER]
    scale = DIFF_HALF ** -0.5
    cols = lambda j: w[:, j * hw:(j + 1) * hw]
    w_r = jnp.concatenate([cols(1), cols(2), cols(0), cols(3), cols(4), cols(5) * scale, cols(6), cols(7)],
                          axis=1).astype(BF16)
    f32 = jnp.float32
    lam_init = 0.8 - 0.6 * math.exp(-0.3 * LAYER)
    lam = (jnp.exp(jnp.sum(diff_lam_q1[LAYER].astype(f32) * diff_lam_k1[LAYER].astype(f32)))
           - jnp.exp(jnp.sum(diff_lam_q2[LAYER].astype(f32) * diff_lam_k2[LAYER].astype(f32))) + lam_init)
    lam = lam.reshape(1, 1)
    bias_ext = _rel_bias_ext(rel_bias, L)
    sk = peer_sub_keys[LAYER].reshape(2 * PEER_HEADS, PEER_NKEYS, PEER_DKEY).astype(BF16)
    wq = peer_w_q[LAYER].reshape(D, -1).astype(BF16)
    wo = w_out[LAYER].astype(BF16)
    def pack_table(tab):
        words = _pack_bf16_pair(tab[:, :D // 2], tab[:, D // 2:])
        return words.reshape(tab.shape[0], D // 2 // LANES, LANES)
    u3, v3 = pack_table(peer_u[LAYER]), pack_table(peer_v[LAYER])

    bc = B // BATCH_CHUNKS
    tc = bc * L
    rows = tc * PEER_HEADS * PEER_TOPK // SC_GATHER_ROWS
    outs = []
    pending = {}
    combined = None
    for c in range(BATCH_CHUNKS + CHUNK_LAG + 1):
        x2 = x[c * bc:(c + 1) * bc].reshape(tc, D) if c < BATCH_CHUNKS else None
        if combined is not None:
            h_prev, po = combined
            if x2 is not None:
                x2, po = lax.optimization_barrier((x2, po))
            outs.append(_final(h_prev, po, final_g[None, :]).reshape(bc, L, D))
            combined = None
        if CHUNK_LAG <= c < BATCH_CHUNKS + CHUNK_LAG:
            h, idx4, gate, hraw = pending.pop(c - CHUNK_LAG)
            wts = _peer_act(hraw.reshape(tc, -1), gate)
            if x2 is not None:
                x2, wts = lax.optimization_barrier((x2, wts))
            combined = (h, _peer_combine(v3, idx4, wts.reshape(rows, SC_GATHER_ROWS), tc, D))
        if x2 is not None:
            zf, pb = _inproj(x2, norm1_g[LAYER][None, :], w_r)
            zf3 = zf.reshape(bc, L, -1)
            pb3 = pb.reshape(bc, L, -1)
            o_h = _hgrn(zf3, pb3, hgrn_lb_fwd, hgrn_lb_bwd, hgrn_out_g[LAYER][None, :])
            o_d = _attn(pb3, bias_ext, lam, diff_out_g[LAYER][None, :])
            h, hn, idx, gate = _route(x2, o_h.reshape(tc, -1), o_d.reshape(tc, -1), wo,
                                      norm2_g[LAYER][None, :], wq, sk)
            idx4 = idx.reshape(rows, SC_GATHER_ROWS)
            pending[c] = (h, idx4, gate, _peer_hidden(u3, idx4, hn))
    return jnp.concatenate(outs, axis=0)
```

```python
import functools
import math

import numpy as np
import jax
import jax.numpy as jnp
from jax import lax
from jax.experimental import pallas as pl
from jax.experimental.pallas import tpu as pltpu
from jax.experimental.pallas import tpu_sc as plsc

F32 = jnp.float32
BF16 = jnp.bfloat16
EPS = 1e-6

HGRN_HEADS = 4
HGRN_DK = 128
DIFF_HEADS = 4
DIFF_HALF = 64
REL_BUCKETS = 32
REL_MAX_DIST = 128
PEER_HEADS = 8
PEER_NKEYS = 128
PEER_DKEY = 128
PEER_TOPK = 16
LAYER = 0

LANES = 128
SUBLANES = 8
VMEM_LIMIT = 48 * 1024 * 1024

INPROJ_TM = 512
HG_C = 64
HG_LEVELS = (64, 32, 16, 8, 4, 2)
ATTN_TQ = 256
ROUTE_TM = 256
BATCH_CHUNKS = 16
PEER_ACT_TM = 2048
FINAL_TM = 512

SC_CORES = 2
SC_SUBCORES = 16
SC_LANES = 16
SC_WORKERS = SC_CORES * SC_SUBCORES
SC_GATHER_ROWS = 64
SC_TOKEN_GROUP = 32
SC_COMBINE_GROUP = 32
SC_RING = 2


def _dot(a, b):
    return jnp.dot(a, b, preferred_element_type=F32)


def _dot_nt(a, b):
    return lax.dot_general(a, b, (((1,), (1,)), ((), ())), preferred_element_type=F32)


def _dot_tn(a, b):
    return lax.dot_general(a, b, (((0,), (0,)), ((), ())), preferred_element_type=F32)


def _silu(x):
    return x * (1.0 / (1.0 + jnp.exp(-x)))


BF16_HI_MASK = -65536


def _pack_bf16_pair(hi, lo):
    bits = lambda a: lax.bitcast_convert_type(a.astype(BF16).astype(F32), jnp.int32)
    return (bits(hi) & BF16_HI_MASK) | lax.shift_right_logical(bits(lo), 16)


def _inproj_kernel(x_ref, g_ref, w_ref, zf_ref, pb_ref):
    x = x_ref[...]
    ms = jnp.mean(x * x, axis=-1, keepdims=True)
    xn = (x * lax.rsqrt(ms + EPS) * g_ref[...]).astype(BF16)
    nz = zf_ref.shape[1]
    zf_ref[...] = _dot(xn, w_ref[:, 0:nz])
    nb = pb_ref.shape[1]
    step = 1024
    for j in range(nb // step):
        pb_ref[:, j * step:(j + 1) * step] = _dot(
            xn, w_ref[:, nz + j * step: nz + (j + 1) * step]).astype(BF16)


def _inproj(x2, g, w):
    T, D = x2.shape
    N = w.shape[1]
    nz = 2 * HGRN_HEADS * HGRN_DK
    tm = INPROJ_TM
    return pl.pallas_call(
        _inproj_kernel,
        grid=(T // tm,),
        in_specs=[
            pl.BlockSpec((tm, D), lambda i: (i, 0)),
            pl.BlockSpec((1, D), lambda i: (0, 0)),
            pl.BlockSpec((D, N), lambda i: (0, 0)),
        ],
        out_specs=[
            pl.BlockSpec((tm, nz), lambda i: (i, 0)),
            pl.BlockSpec((tm, N - nz), lambda i: (i, 0)),
        ],
        out_shape=[
            jax.ShapeDtypeStruct((T, nz), F32),
            jax.ShapeDtypeStruct((T, N - nz), BF16),
        ],
        compiler_params=pltpu.CompilerParams(
            dimension_semantics=("parallel",), vmem_limit_bytes=VMEM_LIMIT),
        name="inproj",
    )(x2, g, w)


def _hgrn_consts():
    C = HG_C
    r = np.arange(C)
    t = r[:, None]
    u = r[None, :]
    blocks = [u <= t, u > t]
    masks = [np.eye(C, dtype=bool)]
    for B in HG_LEVELS:
        half = B // 2
        a = (r // B) * B
        m = (a + half - 1)[:, None]
        upper = (r - a) >= half
        blocks.append(np.where(upper[:, None], (u > m) & (u <= t), (u > t) & (u <= m)))
        same = a[:, None] == a[None, :]
        masks.append(same & upper[:, None] & (~upper)[None, :])
    m_f = np.concatenate(blocks, 0).astype(np.float32)
    m_b = np.concatenate([b[::-1, ::-1] for b in blocks], 0).astype(np.float32)
    k_f = np.stack(masks).astype(np.float32)
    k_b = np.ascontiguousarray(k_f.transpose(0, 2, 1))
    return m_f, m_b, k_f, k_b


def _hgrn_gates(z, tab):
    tabf = tab.astype(F32)
    e = jnp.exp(tabf - jnp.max(tabf, axis=0, keepdims=True))
    lb = jnp.sum(e[0:LAYER + 1], axis=0, keepdims=True) / jnp.sum(e, axis=0, keepdims=True)
    log_lb = jnp.log(lb)
    log_1m = jnp.log1p(-lb)
    ez = jnp.exp(-jnp.abs(z))
    l1p = jnp.log1p(ez)
    log_sig = jnp.minimum(z, 0.0) - l1p
    c = log_1m + log_sig
    hi = jnp.maximum(log_lb, c)
    lo = jnp.minimum(log_lb, c)
    log_f = hi + jnp.log1p(jnp.exp(lo - hi))
    sig_neg = jnp.where(z >= 0.0, ez, 1.0) / (1.0 + ez)
    k = (1.0 - lb) * sig_neg
    return log_f, k


def _hgrn_kernel(lbf_ref, lbb_ref, og_ref, mf_ref, mb_ref, kf_ref, kb_ref,
                 q_ref, zf_ref, zb_ref, v_ref, g_ref, o_ref, of_s, ob_s):
    C = HG_C
    L = q_ref.shape[0]
    n = L // C
    dv = v_ref.shape[1]

    def chunk(c, st, m_ref, k_ref, z_ref, tab_ref, forward):
        sl = pl.ds(pl.multiple_of(c * C, C), C)
        qh = q_ref[sl, :].astype(F32)
        q = _silu(qh)
        v = v_ref[sl, :]
        log_f, k = _hgrn_gates(z_ref[sl, :], tab_ref[...])
        lf_hi = log_f.astype(BF16)
        lf_lo = (log_f - lf_hi.astype(F32)).astype(BF16)
        m = m_ref[...]
        e = jnp.exp(_dot(m, lf_hi) + _dot(m, lf_lo))
        e_b = e[0:C]
        e_s = e[C:2 * C]
        dec = e[C - 1:C] if forward else e[0:1]
        o = _dot_nt((q * e_b).astype(BF16), st.astype(BF16))
        a = _dot_nt(q.astype(BF16), k.astype(BF16)) * k_ref[0]
        for l in range(len(HG_LEVELS)):
            e_l = e[(2 + l) * C:(3 + l) * C]
            a = a + _dot_nt((q * e_l).astype(BF16), (k * e_l).astype(BF16)) * k_ref[l + 1]
        o = o + _dot(a.astype(BF16), v)
        st = st * dec + _dot_tn(v, (k * e_s).astype(BF16))
        return sl, o, st

    st0 = jnp.zeros((dv, q_ref.shape[1]), F32)

    def both(i, carry):
        st_f, st_b = carry
        sl_f, o_f, st_f = chunk(i, st_f, mf_ref, kf_ref, zf_ref, lbf_ref, True)
        sl_b, o_b, st_b = chunk(n - 1 - i, st_b, mb_ref, kb_ref, zb_ref, lbb_ref, False)
        of_s[sl_f, :] = o_f
        ob_s[sl_b, :] = o_b
        return st_f, st_b

    lax.fori_loop(0, n, both, (st0, st0))

    def finish(c, _):
        sl = pl.ds(pl.multiple_of(c * C, C), C)
        tot = of_s[sl, :] + ob_s[sl, :]
        ms = jnp.mean(tot * tot, axis=-1, keepdims=True)
        y = tot * lax.rsqrt(ms + EPS) * og_ref[...]
        o_ref[sl, :] = (y * _silu(g_ref[sl, :].astype(F32))).astype(o_ref.dtype)
        return 0

    lax.fori_loop(0, n, finish, 0)


def _hgrn(zf3, pb3, lb_f, lb_b, out_g):
    B, L, _ = zf3.shape
    H, dk = HGRN_HEADS, HGRN_DK
    m_f, m_b, k_f, k_b = _hgrn_consts()
    nlev = k_f.shape[0]
    full2 = lambda b, h: (0, 0)
    full3 = lambda b, h: (0, 0, 0)
    seq = lambda off: pl.BlockSpec((None, L, dk), lambda b, h: (b, 0, off + h))
    return pl.pallas_call(
        _hgrn_kernel,
        grid=(B, H),
        in_specs=[
            pl.BlockSpec((lb_f.shape[0], dk), lambda b, h: (0, h)),
            pl.BlockSpec((lb_b.shape[0], dk), lambda b, h: (0, h)),
            pl.BlockSpec((1, dk), full2),
            pl.BlockSpec(m_f.shape, full2),
            pl.BlockSpec(m_b.shape, full2),
            pl.BlockSpec((nlev, HG_C, HG_C), full3),
            pl.BlockSpec((nlev, HG_C, HG_C), full3),
            seq(0),
            seq(0),
            seq(H),
            seq(H),
            seq(2 * H),
        ],
        out_specs=pl.BlockSpec((None, L, dk), lambda b, h: (b, 0, h)),
        out_shape=jax.ShapeDtypeStruct((B, L, H * dk), BF16),
        scratch_shapes=[pltpu.VMEM((L, dk), F32), pltpu.VMEM((L, dk), F32)],
        compiler_params=pltpu.CompilerParams(
            dimension_semantics=("parallel", "parallel"), vmem_limit_bytes=VMEM_LIMIT),
        name="hgrn",
    )(lb_f, lb_b, out_g, jnp.asarray(m_f, BF16), jnp.asarray(m_b, BF16),
      jnp.asarray(k_f), jnp.asarray(k_b), pb3, zf3, zf3, pb3, pb3)


def _t5_bucket(rel):
    nb = REL_BUCKETS // 2
    ret = jnp.where(rel > 0, nb, 0)
    n = jnp.abs(rel)
    max_exact = nb // 2
    nf = jnp.maximum(n, 1).astype(jnp.float32)
    large = max_exact + (jnp.log(nf / max_exact) / math.log(REL_MAX_DIST / max_exact)
                         * (nb - max_exact)).astype(jnp.int32)
    large = jnp.minimum(large, nb - 1)
    return ret + jnp.where(n < max_exact, n, large)


def _rel_bias_ext(rel_bias, L):
    j = jnp.arange(2 * L - LANES, dtype=jnp.int32)
    ql = jnp.arange(LANES, dtype=jnp.int32)
    rel = j[None, :] - (L - LANES) - ql[:, None]
    bucket = _t5_bucket(rel)
    tab = rel_bias.astype(F32)
    out = jnp.zeros((tab.shape[1],) + bucket.shape, F32)
    for b in range(REL_BUCKETS):
        out = jnp.where((bucket == b)[None], tab[b][:, None, None], out)
    return out


def _attn_kernel(lam_ref, q_ref, k_ref, v_ref, bias_ref, og_ref, o_ref, *, lam_init):
    tq = q_ref.shape[0]
    L = k_ref.shape[0]
    qi = pl.program_id(2)
    lam = lam_ref[0, 0]
    q = q_ref[...]
    k = k_ref[...]
    lane = lax.broadcasted_iota(jnp.int32, q.shape, 1)
    zero = jnp.zeros_like(q)
    q0 = jnp.where(lane < DIFF_HALF, q, zero)
    q1 = jnp.where(lane >= DIFF_HALF, q, zero)
    parts = []
    for j in range(tq // LANES):
        off = pl.multiple_of(L - LANES - (qi * tq + j * LANES), LANES)
        parts.append(bias_ref[:, pl.ds(off, L)])
    bias = jnp.concatenate(parts, axis=0) if len(parts) > 1 else parts[0]

    def soft(qm):
        s = _dot_nt(qm, k) + bias
        e = jnp.exp(s - jnp.max(s, axis=-1, keepdims=True))
        return e, 1.0 / jnp.sum(e, axis=-1, keepdims=True)

    e0, r0 = soft(q0)
    e1, r1 = soft(q1)
    w = e0 * r0 - e1 * (lam * r1)
    o = _dot(w.astype(BF16), v_ref[...])
    ms = jnp.mean(o * o, axis=-1, keepdims=True)
    y = o * lax.rsqrt(ms + EPS) * og_ref[...] * (1.0 - lam_init)
    o_ref[...] = y.astype(o_ref.dtype)


def _attn(pb3, bias_ext, lam, out_g):
    B, L, _ = pb3.shape
    H, dh = DIFF_HEADS, 2 * DIFF_HALF
    tq = ATTN_TQ
    base = (3 * HGRN_HEADS * HGRN_DK) // dh
    lam_init = 0.8 - 0.6 * math.exp(-0.3 * LAYER)
    return pl.pallas_call(
        functools.partial(_attn_kernel, lam_init=lam_init),
        grid=(B, H, L // tq),
        in_specs=[
            pl.BlockSpec(memory_space=pltpu.SMEM),
            pl.BlockSpec((None, tq, dh), lambda b, h, i: (b, i, base + h)),
            pl.BlockSpec((None, L, dh), lambda b, h, i: (b, 0, base + H + h)),
            pl.BlockSpec((None, L, dh), lambda b, h, i: (b, 0, base + 2 * H + h)),
            pl.BlockSpec((None, LANES, 2 * L - LANES), lambda b, h, i: (h, 0, 0)),
            pl.BlockSpec((1, dh), lambda b, h, i: (0, 0)),
        ],
        out_specs=pl.BlockSpec((None, tq, dh), lambda b, h, i: (b, i, h)),
        out_shape=jax.ShapeDtypeStruct((B, L, H * dh), BF16),
        compiler_params=pltpu.CompilerParams(
            dimension_semantics=("parallel", "parallel", "parallel"),
            vmem_limit_bytes=VMEM_LIMIT),
        name="attn",
    )(lam, pb3, pb3, pb3, bias_ext, out_g)


def _cand_layout():
    K = PEER_TOPK
    groups = [("a", 0, 0), ("a", 0, 8), ("a", 1, 0), ("a", 2, 0), ("a", 3, 0),
              ("b", 0, 8), ("b", 0, 0), ("b", 1, 0), ("b", 2, 0)]
    seen = set()
    pos, valid = [], []
    for kind, fixed, start in groups:
        for r in range(SUBLANES):
            a, b = (fixed, start + r) if kind == "a" else (start + r, fixed)
            ok = (a + 1) * (b + 1) <= K and (a, b) not in seen
            if ok:
                seen.add((a, b))
            pos.append(a * K + b if ok else K * K + len(pos))
            valid.append(ok)
    assert len(seen) == sum(K // (a + 1) for a in range(K))
    return groups, np.array(pos, np.int32), np.array(valid, bool)


def _extract_topk(s, key, payload, k, big):
    n, tm = s.shape
    S = SUBLANES
    s3 = s.reshape(n // S, S, tm)
    key3 = key.reshape(n // S, S, tm)
    pay3 = None if payload is key else payload.reshape(n // S, S, tm)
    slot = lax.broadcasted_iota(jnp.int32, (k // S, S, tm), 0) * S + lax.broadcasted_iota(
        jnp.int32, (k // S, S, tm), 1)

    def all_reduce(x3, op):
        r = x3[0]
        for g in range(1, x3.shape[0]):
            r = op(r, x3[g])
        for sh in (S // 2, S // 4, S // 8):
            r = op(r, pltpu.roll(r, sh, axis=0))
        return r

    def body(j, carry):
        s3, vals, pay = carry
        m = all_reduce(s3, jnp.maximum)
        kk = all_reduce(jnp.where(s3 == m[None], key3, big), jnp.minimum)
        sel = key3 == kk[None]
        p = kk if pay3 is None else all_reduce(jnp.where(sel, pay3, 0), jnp.add)
        vals = jnp.where(slot == j, m[None], vals)
        pay = jnp.where(slot == j, p[None], pay)
        s3 = jnp.where(sel, -jnp.inf, s3)
        return s3, vals, pay

    init = (s3, jnp.zeros((k // S, S, tm), F32), jnp.zeros((k // S, S, tm), jnp.int32))
    _, vals, pay = lax.fori_loop(0, k, body, init)
    return vals.reshape(k, tm), pay.reshape(k, tm)


def _route_kernel(x_ref, oh_ref, od_ref, woh_ref, wod_ref, g2_ref, wq_ref, sk_ref, cpos_ref, cmask_ref,
                  h_ref, hn_ref, idx_ref, gate_ref, q_s, tv_s, ti_s, gs_s, is_s, *, groups):
    K = PEER_TOPK
    tm = x_ref.shape[0]
    h = x_ref[...] + _dot(oh_ref[...], woh_ref[...]) + _dot(od_ref[...], wod_ref[...])
    h_ref[...] = h
    hn = h * lax.rsqrt(jnp.mean(h * h, axis=-1, keepdims=True) + EPS) * g2_ref[...]
    half = hn.shape[1] // 2
    hn_ref[...] = _pack_bf16_pair(hn[:, :half], hn[:, half:])
    q_s[...] = _dot(hn.astype(BF16), wq_ref[...]).astype(BF16)

    key_iota = lax.broadcasted_iota(jnp.int32, (PEER_NKEYS, tm), 0)

    def half_topk(hp, _):
        col = pl.multiple_of(hp * PEER_DKEY, PEER_DKEY)
        s = _dot_nt(sk_ref[hp], q_s[:, pl.ds(col, PEER_DKEY)])
        vals, idxs = _extract_topk(s, key_iota, key_iota, K, PEER_NKEYS)
        tv_s[hp] = vals
        ti_s[hp] = idxs
        return 0

    lax.fori_loop(0, 2 * PEER_HEADS, half_topk, 0)

    cpos = cpos_ref[...]
    cmask = cmask_ref[...]

    def head(hd, _):
        s0, s1 = tv_s[2 * hd], tv_s[2 * hd + 1]
        i0, i1 = ti_s[2 * hd] * PEER_NKEYS, ti_s[2 * hd + 1]
        cs, ci = [], []
        for kind, fixed, start in groups:
            if kind == "a":
                cs.append(s0[fixed:fixed + 1] + s1[start:start + SUBLANES])
                ci.append(i0[fixed:fixed + 1] + i1[start:start + SUBLANES])
            else:
                cs.append(s0[start:start + SUBLANES] + s1[fixed:fixed + 1])
                ci.append(i0[start:start + SUBLANES] + i1[fixed:fixed + 1])
        cand = jnp.concatenate(cs, axis=0) + cmask
        cidx = jnp.concatenate(ci, axis=0)
        best, eidx = _extract_topk(cand, cpos, cidx, K, 2 * K * K)
        ex = jnp.exp(best - best[0:1])
        gate = ex / jnp.sum(ex, axis=0, keepdims=True)
        row = pl.ds(pl.multiple_of(hd * K, K), K)
        gs_s[row, :] = gate
        is_s[row, :] = eidx.astype(F32)
        return 0

    lax.fori_loop(0, PEER_HEADS, head, 0)
    gate_ref[...] = gs_s[...].T
    idx_ref[...] = is_s[...].T.astype(jnp.int32)


def _route(x2, oh2, od2, w_out, g2, w_q, sub_keys):
    T, D = x2.shape
    tm = ROUTE_TM
    K = PEER_TOPK
    nh = oh2.shape[1]
    nq = w_q.shape[1]
    npk = PEER_HEADS * K
    groups, pos, valid = _cand_layout()
    ncand = pos.shape[0]
    cpos = jnp.asarray(np.broadcast_to(pos[:, None], (ncand, tm)))
    cmask = jnp.asarray(np.broadcast_to(np.where(valid, 0.0, -np.inf).astype(np.float32)[:, None], (ncand, tm)))
    row = lambda i: (i, 0)
    full2 = lambda i: (0, 0)
    return pl.pallas_call(
        functools.partial(_route_kernel, groups=groups),
        grid=(T // tm,),
        in_specs=[
            pl.BlockSpec((tm, D), row),
            pl.BlockSpec((tm, nh), row),
            pl.BlockSpec((tm, nh), row),
            pl.BlockSpec((nh, D), full2),
            pl.BlockSpec((nh, D), lambda i: (1, 0)),
            pl.BlockSpec((1, D), full2),
            pl.BlockSpec((D, nq), full2),
            pl.BlockSpec(sub_keys.shape, lambda i: (0, 0, 0)),
            pl.BlockSpec((ncand, tm), full2),
            pl.BlockSpec((ncand, tm), full2),
        ],
        out_specs=[
            pl.BlockSpec((tm, D), row),
            pl.BlockSpec((tm, D // 2), row),
            pl.BlockSpec((tm, npk), row),
            pl.BlockSpec((tm, npk), row),
        ],
        out_shape=[
            jax.ShapeDtypeStruct((T, D), F32),
            jax.ShapeDtypeStruct((T, D // 2), jnp.int32),
            jax.ShapeDtypeStruct((T, npk), jnp.int32),
            jax.ShapeDtypeStruct((T, npk), F32),
        ],
        scratch_shapes=[
            pltpu.VMEM((tm, nq), BF16),
            pltpu.VMEM((2 * PEER_HEADS, K, tm), F32),
            pltpu.VMEM((2 * PEER_HEADS, K, tm), jnp.int32),
            pltpu.VMEM((npk, tm), F32),
            pltpu.VMEM((npk, tm), F32),
        ],
        compiler_params=pltpu.CompilerParams(
            dimension_semantics=("parallel",), vmem_limit_bytes=VMEM_LIMIT),
        name="route",
    )(x2, oh2, od2, w_out, w_out, g2, w_q, sub_keys, cpos, cmask)


def _gelu(x):
    return 0.5 * x * (1.0 + lax.erf(x * (1.0 / math.sqrt(2.0))))


def _sc_mesh():
    return plsc.VectorSubcoreMesh(core_axis_name="c", subcore_axis_name="s")


def _sc_worker_id():
    return lax.axis_index("s") * SC_CORES + lax.axis_index("c")


def _sc_widen_pair(s):
    si = plsc.bitcast(s, jnp.int32)
    return plsc.bitcast(si & BF16_HI_MASK, F32), plsc.bitcast(si << 16, F32)


def _sc_gather_loop(tab_hbm, idx_v, bufs, sems, n_items, compute):
    nb = len(bufs)

    def start(item, b):
        pltpu.async_copy(tab_hbm.at[idx_v.at[item]], bufs[b], sems[b])

    def wait(b):
        pltpu.make_async_copy(tab_hbm.at[idx_v.at[0]], bufs[b], sems[b]).wait()

    for p in range(nb - 1):
        start(p, p)

    @pl.loop(0, pl.cdiv(n_items, nb))
    def _(i):
        for b in range(nb):
            it = i * nb + b

            @pl.when(it < n_items)
            def _():
                @pl.when(it + nb - 1 < n_items)
                def _():
                    start(it + nb - 1, (b + nb - 1) % nb)

                wait(b)
                compute(it, bufs[b])


def _peer_hidden(u, idx4, x):
    T, DW = x.shape
    n_rows, R = idx4.shape
    ipt = n_rows // T
    G = SC_TOKEN_GROUP
    nit = G * ipt
    tpw = T // SC_WORKERS
    NL = SC_LANES
    RB = 8
    NV = 4
    GW = NV * NL
    tile = (R, DW // LANES, LANES)

    @functools.partial(
        pl.kernel, mesh=_sc_mesh(), compiler_params=pltpu.CompilerParams(needs_layout_passes=False),
        out_type=jax.ShapeDtypeStruct((n_rows, R), F32),
        scratch_types=[pltpu.VMEM((nit, R), jnp.int32), pltpu.VMEM((G, DW), jnp.int32),
                       [pltpu.VMEM(tile, jnp.int32)] * SC_RING, [pltpu.SemaphoreType.DMA] * SC_RING,
                       pltpu.VMEM((nit, R), F32)],
        name="peer_hidden")
    def k(u_hbm, idx_hbm, x_hbm, h_hbm, idx_v, x_v, bufs, sems, h_v):
        tok0 = _sc_worker_id() * tpw
        lane = lax.broadcasted_iota(jnp.int32, (NL,), 0)
        zero = jnp.zeros((NL,), F32)

        def compute(item, buf):
            tok = item // ipt
            hvs = [zero for _ in range(R // NL)]
            for rb in range(R // RB):
                def body(g, accs):
                    sub = g // (LANES // GW)
                    base = (g % (LANES // GW)) * GW
                    xs = [plsc.bitcast(x_v[tok, pl.ds(pl.multiple_of(g * GW + jj * NL, NL), NL)], BF16)
                          for jj in range(NV)]
                    out = []
                    for j in range(RB):
                        ps = [plsc.bitcast(buf[rb * RB + j, sub, pl.ds(pl.multiple_of(base + jj * NL, NL), NL)],
                                           BF16) * xs[jj] for jj in range(NV)]
                        hi, lo = _sc_widen_pair((ps[0] + ps[1]) + (ps[2] + ps[3]))
                        out.append((accs[j] + hi) + lo)
                    return tuple(out)
                accs = lax.fori_loop(0, DW // GW, body, tuple(zero for _ in range(RB)))
                for j in range(RB):
                    r = rb * RB + j
                    hvs[r // NL] = jnp.where(lane == (r % NL), jnp.sum(accs[j]), hvs[r // NL])
            for q in range(R // NL):
                h_v[item, pl.ds(q * NL, NL)] = hvs[q]

        @pl.loop(0, tpw // G)
        def _(g):
            t0 = tok0 + g * G
            pltpu.sync_copy(idx_hbm.at[pl.ds(t0 * ipt, nit)], idx_v)
            pltpu.sync_copy(x_hbm.at[pl.ds(t0, G)], x_v)
            _sc_gather_loop(u_hbm, idx_v, bufs, sems, nit, compute)
            pltpu.sync_copy(h_v, h_hbm.at[pl.ds(t0 * ipt, nit)])

    return k(u, idx4, x)


def _peer_combine(v, idx4, w4, T, D):
    n_rows, R = idx4.shape
    ipt = n_rows // T
    G = SC_COMBINE_GROUP
    nit = G * ipt
    tpw = T // SC_WORKERS
    NL = SC_LANES
    DW = D // 2
    half = DW // 2
    nv = half // NL
    RG = 4
    tile = (R, DW // LANES, LANES)

    @functools.partial(
        pl.kernel, mesh=_sc_mesh(), compiler_params=pltpu.CompilerParams(needs_layout_passes=False),
        out_type=jax.ShapeDtypeStruct((T, D), F32),
        scratch_types=[pltpu.VMEM((nit, R), jnp.int32), pltpu.VMEM((nit, R), jnp.int32),
                       [pltpu.VMEM(tile, jnp.int32)] * SC_RING, [pltpu.SemaphoreType.DMA] * SC_RING,
                       pltpu.VMEM((G, D), F32)],
        name="peer_combine")
    def k(v_hbm, idx_hbm, w_hbm, o_hbm, idx_v, w_v, bufs, sems, out_v):
        tok0 = _sc_worker_id() * tpw
        zero = jnp.zeros((NL,), F32)

        def compute(item, buf):
            tok = item // ipt
            item_vec = jnp.full((NL,), item, jnp.int32)
            for hf in range(2):
                def body(rg, accs):
                    ws = [plsc.bitcast(plsc.load_gather(
                        w_v, [item_vec, jnp.full((NL,), rg * RG + rr, jnp.int32)]), BF16) for rr in range(RG)]
                    his, los = [], []
                    for i in range(nv):
                        word = hf * half + i * NL
                        ps = [plsc.bitcast(buf[rg * RG + rr, word // LANES, pl.ds(word % LANES, NL)], BF16) * ws[rr]
                              for rr in range(RG)]
                        hi, lo = _sc_widen_pair((ps[0] + ps[1]) + (ps[2] + ps[3]))
                        his.append(accs[i] + hi)
                        los.append(accs[nv + i] + lo)
                    return tuple(his + los)
                accs = lax.fori_loop(0, R // RG, body, tuple(zero for _ in range(2 * nv)))
                for i in range(nv):
                    word = hf * half + i * NL
                    plsc.addupdate(out_v.at[tok, pl.ds(word, NL)], accs[i])
                    plsc.addupdate(out_v.at[tok, pl.ds(DW + word, NL)], accs[nv + i])

        @pl.loop(0, tpw // G)
        def _(g):
            t0 = tok0 + g * G
            pltpu.sync_copy(idx_hbm.at[pl.ds(t0 * ipt, nit)], idx_v)
            pltpu.sync_copy(w_hbm.at[pl.ds(t0 * ipt, nit)], w_v)

            @pl.loop(0, G)
            def _(t):
                @pl.loop(0, D // NL)
                def _(i):
                    out_v[t, pl.ds(pl.multiple_of(i * NL, NL), NL)] = zero

            _sc_gather_loop(v_hbm, idx_v, bufs, sems, nit, compute)
            pltpu.sync_copy(out_v, o_hbm.at[pl.ds(t0, G)])

    return k(v, idx4, w4)


def _act_kernel(h_ref, g_ref, w_ref):
    w = _gelu(h_ref[...]) * g_ref[...]
    w_ref[...] = _pack_bf16_pair(w, w)


def _peer_act(hraw, gate):
    T, n = gate.shape
    tm = PEER_ACT_TM
    spec = pl.BlockSpec((tm, n), lambda i: (i, 0))
    return pl.pallas_call(
        _act_kernel, grid=(T // tm,), in_specs=[spec, spec], out_specs=spec,
        out_shape=jax.ShapeDtypeStruct((T, n), jnp.int32),
        compiler_params=pltpu.CompilerParams(dimension_semantics=("parallel",)),
        name="peer_act",
    )(hraw, gate)


def _final_kernel(h_ref, p_ref, g_ref, y_ref):
    y = h_ref[...] + p_ref[...]
    ms = jnp.mean(y * y, axis=-1, keepdims=True)
    y_ref[...] = y * lax.rsqrt(ms + EPS) * g_ref[...]


def _final(h, po, g):
    T, D = h.shape
    tm = FINAL_TM
    spec = pl.BlockSpec((tm, D), lambda i: (i, 0))
    return pl.pallas_call(
        _final_kernel, grid=(T // tm,),
        in_specs=[spec, spec, pl.BlockSpec((1, D), lambda i: (0, 0))], out_specs=spec,
        out_shape=jax.ShapeDtypeStruct((T, D), F32),
        compiler_params=pltpu.CompilerParams(dimension_semantics=("parallel",)),
        name="final_norm",
    )(h, po, g)


def kernel(x, norm1_g, w_in, hgrn_lb_fwd, hgrn_lb_bwd, hgrn_out_g, diff_lam_q1, diff_lam_k1,
           diff_lam_q2, diff_lam_k2, diff_out_g, rel_bias, w_out, norm2_g, peer_w_q,
           peer_sub_keys, peer_u, peer_v, final_g):
    B, L, D = x.shape
    hw = HGRN_HEADS * HGRN_DK

    w = w_in[LAYER]
    scale = DIFF_HALF ** -0.5
    cols = lambda j: w[:, j * hw:(j + 1) * hw]
    w_r = jnp.concatenate([cols(1), cols(2), cols(0), cols(3), cols(4), cols(5) * scale, cols(6), cols(7)],
                          axis=1).astype(BF16)
    f32 = jnp.float32
    lam_init = 0.8 - 0.6 * math.exp(-0.3 * LAYER)
    lam = (jnp.exp(jnp.sum(diff_lam_q1[LAYER].astype(f32) * diff_lam_k1[LAYER].astype(f32)))
           - jnp.exp(jnp.sum(diff_lam_q2[LAYER].astype(f32) * diff_lam_k2[LAYER].astype(f32))) + lam_init)
    lam = lam.reshape(1, 1)
    bias_ext = _rel_bias_ext(rel_bias, L)
    sk = peer_sub_keys[LAYER].reshape(2 * PEER_HEADS, PEER_NKEYS, PEER_DKEY).astype(BF16)
    wq = peer_w_q[LAYER].reshape(D, -1).astype(BF16)
    wo = w_out[LAYER].astype(BF16)
    def pack_table(tab):
        words = _pack_bf16_pair(tab[:, :D // 2], tab[:, D // 2:])
        return words.reshape(tab.shape[0], D // 2 // LANES, LANES)
    u3, v3 = pack_table(peer_u[LAYER]), pack_table(peer_v[LAYER])

    bc = B // BATCH_CHUNKS
    tc = bc * L
    rows = tc * PEER_HEADS * PEER_TOPK // SC_GATHER_ROWS
    outs = []
    hidden = {}
    combined = {}
    for c in range(BATCH_CHUNKS + 2):
        live = c < BATCH_CHUNKS
        tied = {}
        if live:
            x2 = x[c * bc:(c + 1) * bc].reshape(tc, D)
            zf, pb = _inproj(x2, norm1_g[LAYER][None, :], w_r)
            zf3 = zf.reshape(bc, L, -1)
            pb3 = pb.reshape(bc, L, -1)
            tied["o_h"] = _hgrn(zf3, pb3, hgrn_lb_fwd, hgrn_lb_bwd, hgrn_out_g[LAYER][None, :])
            tied["o_d"] = _attn(pb3, bias_ext, lam, diff_out_g[LAYER][None, :])
        if c - 1 in hidden:
            h1, idx4_1, gate1, hraw1 = hidden.pop(c - 1)
            tied["wts"] = _peer_act(hraw1.reshape(tc, -1), gate1)
        if c - 2 in combined:
            h2, tied["po"] = combined.pop(c - 2)
        tied = lax.optimization_barrier(tied)
        if "wts" in tied:
            combined[c - 1] = (h1, _peer_combine(v3, idx4_1, tied["wts"].reshape(rows, SC_GATHER_ROWS), tc, D))
        if "po" in tied:
            outs.append(_final(h2, tied["po"], final_g[None, :]).reshape(bc, L, D))
        if live:
            h, hn, idx, gate = _route(x2, tied["o_h"].reshape(tc, -1), tied["o_d"].reshape(tc, -1), wo,
                                      norm2_g[LAYER][None, :], wq, sk)
            idx4 = idx.reshape(rows, SC_GATHER_ROWS)
            hidden[c] = (h, idx4, gate, _peer_hidden(u3, idx4, hn))
    return jnp.concatenate(outs, axis=0)
```

```python
import functools
import math

import numpy as np
import jax
import jax.numpy as jnp
from jax import lax
from jax.experimental import pallas as pl
from jax.experimental.pallas import tpu as pltpu
from jax.experimental.pallas import tpu_sc as plsc

F32 = jnp.float32
BF16 = jnp.bfloat16
EPS = 1e-6

HGRN_HEADS = 4
HGRN_DK = 128
DIFF_HEADS = 4
DIFF_HALF = 64
REL_BUCKETS = 32
REL_MAX_DIST = 128
PEER_HEADS = 8
PEER_NKEYS = 128
PEER_DKEY = 128
PEER_TOPK = 16
LAYER = 0

LANES = 128
SUBLANES = 8
VMEM_LIMIT = 48 * 1024 * 1024

INPROJ_TM = 512
HG_C = 64
HG_LEVELS = (64, 32, 16, 8, 4, 2)
ATTN_TQ = 256
ROUTE_TM = 256
BATCH_CHUNKS = 16
CHUNK_LAG = 2
PEER_ACT_TM = 2048
FINAL_TM = 512

SC_CORES = 2
SC_SUBCORES = 16
SC_LANES = 16
SC_WORKERS = SC_CORES * SC_SUBCORES
SC_GATHER_ROWS = 64
SC_TOKEN_GROUP = 32
SC_COMBINE_GROUP = 32
SC_RING = 2


def _dot(a, b):
    return jnp.dot(a, b, preferred_element_type=F32)


def _dot_nt(a, b):
    return lax.dot_general(a, b, (((1,), (1,)), ((), ())), preferred_element_type=F32)


def _dot_tn(a, b):
    return lax.dot_general(a, b, (((0,), (0,)), ((), ())), preferred_element_type=F32)


def _silu(x):
    return x * (1.0 / (1.0 + jnp.exp(-x)))


BF16_HI_MASK = -65536


def _pack_bf16_pair(hi, lo):
    bits = lambda a: lax.bitcast_convert_type(a.astype(BF16).astype(F32), jnp.int32)
    return (bits(hi) & BF16_HI_MASK) | lax.shift_right_logical(bits(lo), 16)


def _inproj_kernel(x_ref, g_ref, w_ref, zf_ref, pb_ref):
    x = x_ref[...]
    ms = jnp.mean(x * x, axis=-1, keepdims=True)
    xn = (x * lax.rsqrt(ms + EPS) * g_ref[...]).astype(BF16)
    nz = zf_ref.shape[1]
    zf_ref[...] = _dot(xn, w_ref[:, 0:nz])
    nb = pb_ref.shape[1]
    step = 1024
    for j in range(nb // step):
        pb_ref[:, j * step:(j + 1) * step] = _dot(
            xn, w_ref[:, nz + j * step: nz + (j + 1) * step]).astype(BF16)


def _inproj(x2, g, w):
    T, D = x2.shape
    N = w.shape[1]
    nz = 2 * HGRN_HEADS * HGRN_DK
    tm = INPROJ_TM
    return pl.pallas_call(
        _inproj_kernel,
        grid=(T // tm,),
        in_specs=[
            pl.BlockSpec((tm, D), lambda i: (i, 0)),
            pl.BlockSpec((1, D), lambda i: (0, 0)),
            pl.BlockSpec((D, N), lambda i: (0, 0)),
        ],
        out_specs=[
            pl.BlockSpec((tm, nz), lambda i: (i, 0)),
            pl.BlockSpec((tm, N - nz), lambda i: (i, 0)),
        ],
        out_shape=[
            jax.ShapeDtypeStruct((T, nz), F32),
            jax.ShapeDtypeStruct((T, N - nz), BF16),
        ],
        compiler_params=pltpu.CompilerParams(
            dimension_semantics=("parallel",), vmem_limit_bytes=VMEM_LIMIT),
        name="inproj",
    )(x2, g, w)


def _hgrn_consts():
    C = HG_C
    r = np.arange(C)
    t = r[:, None]
    u = r[None, :]
    blocks = [u <= t, u > t]
    masks = [np.eye(C, dtype=bool)]
    for B in HG_LEVELS:
        half = B // 2
        a = (r // B) * B
        m = (a + half - 1)[:, None]
        upper = (r - a) >= half
        blocks.append(np.where(upper[:, None], (u > m) & (u <= t), (u > t) & (u <= m)))
        same = a[:, None] == a[None, :]
        masks.append(same & upper[:, None] & (~upper)[None, :])
    m_f = np.concatenate(blocks, 0).astype(np.float32)
    m_b = np.concatenate([b[::-1, ::-1] for b in blocks], 0).astype(np.float32)
    k_f = np.stack(masks).astype(np.float32)
    k_b = np.ascontiguousarray(k_f.transpose(0, 2, 1))
    return m_f, m_b, k_f, k_b


def _hgrn_gates(z, tab):
    tabf = tab.astype(F32)
    e = jnp.exp(tabf - jnp.max(tabf, axis=0, keepdims=True))
    lb = jnp.sum(e[0:LAYER + 1], axis=0, keepdims=True) / jnp.sum(e, axis=0, keepdims=True)
    log_lb = jnp.log(lb)
    log_1m = jnp.log1p(-lb)
    ez = jnp.exp(-jnp.abs(z))
    l1p = jnp.log1p(ez)
    log_sig = jnp.minimum(z, 0.0) - l1p
    c = log_1m + log_sig
    hi = jnp.maximum(log_lb, c)
    lo = jnp.minimum(log_lb, c)
    log_f = hi + jnp.log1p(jnp.exp(lo - hi))
    sig_neg = jnp.where(z >= 0.0, ez, 1.0) / (1.0 + ez)
    k = (1.0 - lb) * sig_neg
    return log_f, k


def _hgrn_kernel(lbf_ref, lbb_ref, og_ref, mf_ref, mb_ref, kf_ref, kb_ref,
                 q_ref, zf_ref, zb_ref, v_ref, g_ref, o_ref, of_s, ob_s):
    C = HG_C
    L = q_ref.shape[0]
    n = L // C
    dv = v_ref.shape[1]

    def chunk(c, st, m_ref, k_ref, z_ref, tab_ref, forward):
        sl = pl.ds(pl.multiple_of(c * C, C), C)
        qh = q_ref[sl, :].astype(F32)
        q = _silu(qh)
        v = v_ref[sl, :]
        log_f, k = _hgrn_gates(z_ref[sl, :], tab_ref[...])
        lf_hi = log_f.astype(BF16)
        lf_lo = (log_f - lf_hi.astype(F32)).astype(BF16)
        m = m_ref[...]
        e = jnp.exp(_dot(m, lf_hi) + _dot(m, lf_lo))
        e_b = e[0:C]
        e_s = e[C:2 * C]
        dec = e[C - 1:C] if forward else e[0:1]
        o = _dot_nt((q * e_b).astype(BF16), st.astype(BF16))
        a = _dot_nt(q.astype(BF16), k.astype(BF16)) * k_ref[0]
        for l in range(len(HG_LEVELS)):
            e_l = e[(2 + l) * C:(3 + l) * C]
            a = a + _dot_nt((q * e_l).astype(BF16), (k * e_l).astype(BF16)) * k_ref[l + 1]
        o = o + _dot(a.astype(BF16), v)
        st = st * dec + _dot_tn(v, (k * e_s).astype(BF16))
        return sl, o, st

    st0 = jnp.zeros((dv, q_ref.shape[1]), F32)

    def both(i, carry):
        st_f, st_b = carry
        sl_f, o_f, st_f = chunk(i, st_f, mf_ref, kf_ref, zf_ref, lbf_ref, True)
        sl_b, o_b, st_b = chunk(n - 1 - i, st_b, mb_ref, kb_ref, zb_ref, lbb_ref, False)
        of_s[sl_f, :] = o_f
        ob_s[sl_b, :] = o_b
        return st_f, st_b

    lax.fori_loop(0, n, both, (st0, st0))

    def finish(c, _):
        sl = pl.ds(pl.multiple_of(c * C, C), C)
        tot = of_s[sl, :] + ob_s[sl, :]
        ms = jnp.mean(tot * tot, axis=-1, keepdims=True)
        y = tot * lax.rsqrt(ms + EPS) * og_ref[...]
        o_ref[sl, :] = (y * _silu(g_ref[sl, :].astype(F32))).astype(o_ref.dtype)
        return 0

    lax.fori_loop(0, n, finish, 0)


def _hgrn(zf3, pb3, lb_f, lb_b, out_g):
    B, L, _ = zf3.shape
    H, dk = HGRN_HEADS, HGRN_DK
    m_f, m_b, k_f, k_b = _hgrn_consts()
    nlev = k_f.shape[0]
    full2 = lambda b, h: (0, 0)
    full3 = lambda b, h: (0, 0, 0)
    seq = lambda off: pl.BlockSpec((None, L, dk), lambda b, h: (b, 0, off + h))
    return pl.pallas_call(
        _hgrn_kernel,
        grid=(B, H),
        in_specs=[
            pl.BlockSpec((lb_f.shape[0], dk), lambda b, h: (0, h)),
            pl.BlockSpec((lb_b.shape[0], dk), lambda b, h: (0, h)),
            pl.BlockSpec((1, dk), full2),
            pl.BlockSpec(m_f.shape, full2),
            pl.BlockSpec(m_b.shape, full2),
            pl.BlockSpec((nlev, HG_C, HG_C), full3),
            pl.BlockSpec((nlev, HG_C, HG_C), full3),
            seq(0),
            seq(0),
            seq(H),
            seq(H),
            seq(2 * H),
        ],
        out_specs=pl.BlockSpec((None, L, dk), lambda b, h: (b, 0, h)),
        out_shape=jax.ShapeDtypeStruct((B, L, H * dk), BF16),
        scratch_shapes=[pltpu.VMEM((L, dk), F32), pltpu.VMEM((L, dk), F32)],
        compiler_params=pltpu.CompilerParams(
            dimension_semantics=("parallel", "parallel"), vmem_limit_bytes=VMEM_LIMIT),
        name="hgrn",
    )(lb_f, lb_b, out_g, jnp.asarray(m_f, BF16), jnp.asarray(m_b, BF16),
      jnp.asarray(k_f), jnp.asarray(k_b), pb3, zf3, zf3, pb3, pb3)


def _t5_bucket(rel):
    nb = REL_BUCKETS // 2
    ret = jnp.where(rel > 0, nb, 0)
    n = jnp.abs(rel)
    max_exact = nb // 2
    nf = jnp.maximum(n, 1).astype(jnp.float32)
    large = max_exact + (jnp.log(nf / max_exact) / math.log(REL_MAX_DIST / max_exact)
                         * (nb - max_exact)).astype(jnp.int32)
    large = jnp.minimum(large, nb - 1)
    return ret + jnp.where(n < max_exact, n, large)


def _rel_bias_ext(rel_bias, L):
    j = jnp.arange(2 * L - LANES, dtype=jnp.int32)
    ql = jnp.arange(LANES, dtype=jnp.int32)
    rel = j[None, :] - (L - LANES) - ql[:, None]
    bucket = _t5_bucket(rel)
    tab = rel_bias.astype(F32)
    out = jnp.zeros((tab.shape[1],) + bucket.shape, F32)
    for b in range(REL_BUCKETS):
        out = jnp.where((bucket == b)[None], tab[b][:, None, None], out)
    return out


def _attn_kernel(lam_ref, q_ref, k_ref, v_ref, bias_ref, og_ref, o_ref, *, lam_init):
    tq = q_ref.shape[0]
    L = k_ref.shape[0]
    qi = pl.program_id(2)
    lam = lam_ref[0, 0]
    q = q_ref[...]
    k = k_ref[...]
    lane = lax.broadcasted_iota(jnp.int32, q.shape, 1)
    zero = jnp.zeros_like(q)
    q0 = jnp.where(lane < DIFF_HALF, q, zero)
    q1 = jnp.where(lane >= DIFF_HALF, q, zero)
    parts = []
    for j in range(tq // LANES):
        off = pl.multiple_of(L - LANES - (qi * tq + j * LANES), LANES)
        parts.append(bias_ref[:, pl.ds(off, L)])
    bias = jnp.concatenate(parts, axis=0) if len(parts) > 1 else parts[0]

    def soft(qm):
        s = _dot_nt(qm, k) + bias
        e = jnp.exp(s - jnp.max(s, axis=-1, keepdims=True))
        return e, 1.0 / jnp.sum(e, axis=-1, keepdims=True)

    e0, r0 = soft(q0)
    e1, r1 = soft(q1)
    w = e0 * r0 - e1 * (lam * r1)
    o = _dot(w.astype(BF16), v_ref[...])
    ms = jnp.mean(o * o, axis=-1, keepdims=True)
    y = o * lax.rsqrt(ms + EPS) * og_ref[...] * (1.0 - lam_init)
    o_ref[...] = y.astype(o_ref.dtype)


def _attn(pb3, bias_ext, lam, out_g):
    B, L, _ = pb3.shape
    H, dh = DIFF_HEADS, 2 * DIFF_HALF
    tq = ATTN_TQ
    base = (3 * HGRN_HEADS * HGRN_DK) // dh
    lam_init = 0.8 - 0.6 * math.exp(-0.3 * LAYER)
    return pl.pallas_call(
        functools.partial(_attn_kernel, lam_init=lam_init),
        grid=(B, H, L // tq),
        in_specs=[
            pl.BlockSpec(memory_space=pltpu.SMEM),
            pl.BlockSpec((None, tq, dh), lambda b, h, i: (b, i, base + h)),
            pl.BlockSpec((None, L, dh), lambda b, h, i: (b, 0, base + H + h)),
            pl.BlockSpec((None, L, dh), lambda b, h, i: (b, 0, base + 2 * H + h)),
            pl.BlockSpec((None, LANES, 2 * L - LANES), lambda b, h, i: (h, 0, 0)),
            pl.BlockSpec((1, dh), lambda b, h, i: (0, 0)),
        ],
        out_specs=pl.BlockSpec((None, tq, dh), lambda b, h, i: (b, i, h)),
        out_shape=jax.ShapeDtypeStruct((B, L, H * dh), BF16),
        compiler_params=pltpu.CompilerParams(
            dimension_semantics=("parallel", "parallel", "parallel"),
            vmem_limit_bytes=VMEM_LIMIT),
        name="attn",
    )(lam, pb3, pb3, pb3, bias_ext, out_g)


def _cand_layout():
    K = PEER_TOPK
    groups = [("a", 0, 0), ("a", 0, 8), ("a", 1, 0), ("a", 2, 0), ("a", 3, 0),
              ("b", 0, 8), ("b", 0, 0), ("b", 1, 0), ("b", 2, 0)]
    seen = set()
    pos, valid = [], []
    for kind, fixed, start in groups:
        for r in range(SUBLANES):
            a, b = (fixed, start + r) if kind == "a" else (start + r, fixed)
            ok = (a + 1) * (b + 1) <= K and (a, b) not in seen
            if ok:
                seen.add((a, b))
            pos.append(a * K + b if ok else K * K + len(pos))
            valid.append(ok)
    assert len(seen) == sum(K // (a + 1) for a in range(K))
    return groups, np.array(pos, np.int32), np.array(valid, bool)


def _extract_topk(s, key, payload, k, big):
    n, tm = s.shape
    S = SUBLANES
    s3 = s.reshape(n // S, S, tm)
    key3 = key.reshape(n // S, S, tm)
    pay3 = None if payload is key else payload.reshape(n // S, S, tm)
    slot = lax.broadcasted_iota(jnp.int32, (k // S, S, tm), 0) * S + lax.broadcasted_iota(
        jnp.int32, (k // S, S, tm), 1)

    def all_reduce(x3, op):
        r = x3[0]
        for g in range(1, x3.shape[0]):
            r = op(r, x3[g])
        for sh in (S // 2, S // 4, S // 8):
            r = op(r, pltpu.roll(r, sh, axis=0))
        return r

    def body(j, carry):
        s3, vals, pay = carry
        m = all_reduce(s3, jnp.maximum)
        kk = all_reduce(jnp.where(s3 == m[None], key3, big), jnp.minimum)
        sel = key3 == kk[None]
        p = kk if pay3 is None else all_reduce(jnp.where(sel, pay3, 0), jnp.add)
        vals = jnp.where(slot == j, m[None], vals)
        pay = jnp.where(slot == j, p[None], pay)
        s3 = jnp.where(sel, -jnp.inf, s3)
        return s3, vals, pay

    init = (s3, jnp.zeros((k // S, S, tm), F32), jnp.zeros((k // S, S, tm), jnp.int32))
    _, vals, pay = lax.fori_loop(0, k, body, init)
    return vals.reshape(k, tm), pay.reshape(k, tm)


def _route_kernel(x_ref, oh_ref, od_ref, woh_ref, wod_ref, g2_ref, wq_ref, sk_ref, cpos_ref, cmask_ref,
                  h_ref, hn_ref, idx_ref, gate_ref, q_s, tv_s, ti_s, gs_s, is_s, *, groups):
    K = PEER_TOPK
    tm = x_ref.shape[0]
    h = x_ref[...] + _dot(oh_ref[...], woh_ref[...]) + _dot(od_ref[...], wod_ref[...])
    h_ref[...] = h
    hn = h * lax.rsqrt(jnp.mean(h * h, axis=-1, keepdims=True) + EPS) * g2_ref[...]
    half = hn.shape[1] // 2
    hn_ref[...] = _pack_bf16_pair(hn[:, :half], hn[:, half:])
    q_s[...] = _dot(hn.astype(BF16), wq_ref[...]).astype(BF16)

    key_iota = lax.broadcasted_iota(jnp.int32, (PEER_NKEYS, tm), 0)

    def half_topk(hp, _):
        col = pl.multiple_of(hp * PEER_DKEY, PEER_DKEY)
        s = _dot_nt(sk_ref[hp], q_s[:, pl.ds(col, PEER_DKEY)])
        vals, idxs = _extract_topk(s, key_iota, key_iota, K, PEER_NKEYS)
        tv_s[hp] = vals
        ti_s[hp] = idxs
        return 0

    lax.fori_loop(0, 2 * PEER_HEADS, half_topk, 0)

    cpos = cpos_ref[...]
    cmask = cmask_ref[...]

    def head(hd, _):
        s0, s1 = tv_s[2 * hd], tv_s[2 * hd + 1]
        i0, i1 = ti_s[2 * hd] * PEER_NKEYS, ti_s[2 * hd + 1]
        cs, ci = [], []
        for kind, fixed, start in groups:
            if kind == "a":
                cs.append(s0[fixed:fixed + 1] + s1[start:start + SUBLANES])
                ci.append(i0[fixed:fixed + 1] + i1[start:start + SUBLANES])
            else:
                cs.append(s0[start:start + SUBLANES] + s1[fixed:fixed + 1])
                ci.append(i0[start:start + SUBLANES] + i1[fixed:fixed + 1])
        cand = jnp.concatenate(cs, axis=0) + cmask
        cidx = jnp.concatenate(ci, axis=0)
        best, eidx = _extract_topk(cand, cpos, cidx, K, 2 * K * K)
        ex = jnp.exp(best - best[0:1])
        gate = ex / jnp.sum(ex, axis=0, keepdims=True)
        row = pl.ds(pl.multiple_of(hd * K, K), K)
        gs_s[row, :] = gate
        is_s[row, :] = eidx.astype(F32)
        return 0

    lax.fori_loop(0, PEER_HEADS, head, 0)
    gate_ref[...] = gs_s[...].T
    idx_ref[...] = is_s[...].T.astype(jnp.int32)


def _route(x2, oh2, od2, w_out, g2, w_q, sub_keys):
    T, D = x2.shape
    tm = ROUTE_TM
    K = PEER_TOPK
    nh = oh2.shape[1]
    nq = w_q.shape[1]
    npk = PEER_HEADS * K
    groups, pos, valid = _cand_layout()
    ncand = pos.shape[0]
    cpos = jnp.asarray(np.broadcast_to(pos[:, None], (ncand, tm)))
    cmask = jnp.asarray(np.broadcast_to(np.where(valid, 0.0, -np.inf).astype(np.float32)[:, None], (ncand, tm)))
    row = lambda i: (i, 0)
    full2 = lambda i: (0, 0)
    return pl.pallas_call(
        functools.partial(_route_kernel, groups=groups),
        grid=(T // tm,),
        in_specs=[
            pl.BlockSpec((tm, D), row),
            pl.BlockSpec((tm, nh), row),
            pl.BlockSpec((tm, nh), row),
            pl.BlockSpec((nh, D), full2),
            pl.BlockSpec((nh, D), lambda i: (1, 0)),
            pl.BlockSpec((1, D), full2),
            pl.BlockSpec((D, nq), full2),
            pl.BlockSpec(sub_keys.shape, lambda i: (0, 0, 0)),
            pl.BlockSpec((ncand, tm), full2),
            pl.BlockSpec((ncand, tm), full2),
        ],
        out_specs=[
            pl.BlockSpec((tm, D), row),
            pl.BlockSpec((tm, D // 2), row),
            pl.BlockSpec((tm, npk), row),
            pl.BlockSpec((tm, npk), row),
        ],
        out_shape=[
            jax.ShapeDtypeStruct((T, D), F32),
            jax.ShapeDtypeStruct((T, D // 2), jnp.int32),
            jax.ShapeDtypeStruct((T, npk), jnp.int32),
            jax.ShapeDtypeStruct((T, npk), F32),
        ],
        scratch_shapes=[
            pltpu.VMEM((tm, nq), BF16),
            pltpu.VMEM((2 * PEER_HEADS, K, tm), F32),
            pltpu.VMEM((2 * PEER_HEADS, K, tm), jnp.int32),
            pltpu.VMEM((npk, tm), F32),
            pltpu.VMEM((npk, tm), F32),
        ],
        compiler_params=pltpu.CompilerParams(
            dimension_semantics=("parallel",), vmem_limit_bytes=VMEM_LIMIT),
        name="route",
    )(x2, oh2, od2, w_out, w_out, g2, w_q, sub_keys, cpos, cmask)


def _gelu(x):
    return 0.5 * x * (1.0 + lax.erf(x * (1.0 / math.sqrt(2.0))))


def _sc_mesh():
    return plsc.VectorSubcoreMesh(core_axis_name="c", subcore_axis_name="s")


def _sc_worker_id():
    return lax.axis_index("s") * SC_CORES + lax.axis_index("c")


def _sc_widen_pair(s):
    si = plsc.bitcast(s, jnp.int32)
    return plsc.bitcast(si & BF16_HI_MASK, F32), plsc.bitcast(si << 16, F32)


def _sc_gather_loop(tab_hbm, idx_v, bufs, sems, n_items, compute):
    nb = len(bufs)

    def start(item, b):
        pltpu.async_copy(tab_hbm.at[idx_v.at[item]], bufs[b], sems[b])

    def wait(b):
        pltpu.make_async_copy(tab_hbm.at[idx_v.at[0]], bufs[b], sems[b]).wait()

    for p in range(nb - 1):
        start(p, p)

    @pl.loop(0, pl.cdiv(n_items, nb))
    def _(i):
        for b in range(nb):
            it = i * nb + b

            @pl.when(it < n_items)
            def _():
                @pl.when(it + nb - 1 < n_items)
                def _():
                    start(it + nb - 1, (b + nb - 1) % nb)

                wait(b)
                compute(it, bufs[b])


def _peer_hidden(u, idx4, x):
    T, DW = x.shape
    n_rows, R = idx4.shape
    ipt = n_rows // T
    G = SC_TOKEN_GROUP
    nit = G * ipt
    tpw = T // SC_WORKERS
    NL = SC_LANES
    RB = 8
    NV = 4
    GW = NV * NL
    tile = (R, DW // LANES, LANES)

    @functools.partial(
        pl.kernel, mesh=_sc_mesh(), compiler_params=pltpu.CompilerParams(needs_layout_passes=False),
        out_type=jax.ShapeDtypeStruct((n_rows, R), F32),
        scratch_types=[pltpu.VMEM((nit, R), jnp.int32), pltpu.VMEM((G, DW), jnp.int32),
                       [pltpu.VMEM(tile, jnp.int32)] * SC_RING, [pltpu.SemaphoreType.DMA] * SC_RING,
                       pltpu.VMEM((nit, R), F32)],
        name="peer_hidden")
    def k(u_hbm, idx_hbm, x_hbm, h_hbm, idx_v, x_v, bufs, sems, h_v):
        tok0 = _sc_worker_id() * tpw
        lane = lax.broadcasted_iota(jnp.int32, (NL,), 0)
        zero = jnp.zeros((NL,), F32)

        def compute(item, buf):
            tok = item // ipt
            hvs = [zero for _ in range(R // NL)]
            for rb in range(R // RB):
                def body(g, accs):
                    sub = g // (LANES // GW)
                    base = (g % (LANES // GW)) * GW
                    xs = [plsc.bitcast(x_v[tok, pl.ds(pl.multiple_of(g * GW + jj * NL, NL), NL)], BF16)
                          for jj in range(NV)]
                    out = []
                    for j in range(RB):
                        ps = [plsc.bitcast(buf[rb * RB + j, sub, pl.ds(pl.multiple_of(base + jj * NL, NL), NL)],
                                           BF16) * xs[jj] for jj in range(NV)]
                        hi, lo = _sc_widen_pair((ps[0] + ps[1]) + (ps[2] + ps[3]))
                        out.append((accs[j] + hi) + lo)
                    return tuple(out)
                accs = lax.fori_loop(0, DW // GW, body, tuple(zero for _ in range(RB)))
                for j in range(RB):
                    r = rb * RB + j
                    hvs[r // NL] = jnp.where(lane == (r % NL), jnp.sum(accs[j]), hvs[r // NL])
            for q in range(R // NL):
                h_v[item, pl.ds(q * NL, NL)] = hvs[q]

        @pl.loop(0, tpw // G)
        def _(g):
            t0 = tok0 + g * G
            pltpu.sync_copy(idx_hbm.at[pl.ds(t0 * ipt, nit)], idx_v)
            pltpu.sync_copy(x_hbm.at[pl.ds(t0, G)], x_v)
            _sc_gather_loop(u_hbm, idx_v, bufs, sems, nit, compute)
            pltpu.sync_copy(h_v, h_hbm.at[pl.ds(t0 * ipt, nit)])

    return k(u, idx4, x)


def _peer_combine(v, idx4, w4, T, D):
    n_rows, R = idx4.shape
    ipt = n_rows // T
    G = SC_COMBINE_GROUP
    nit = G * ipt
    tpw = T // SC_WORKERS
    NL = SC_LANES
    DW = D // 2
    half = DW // 2
    nv = half // NL
    RG = 4
    tile = (R, DW // LANES, LANES)

    @functools.partial(
        pl.kernel, mesh=_sc_mesh(), compiler_params=pltpu.CompilerParams(needs_layout_passes=False),
        out_type=jax.ShapeDtypeStruct((T, D), F32),
        scratch_types=[pltpu.VMEM((nit, R), jnp.int32), pltpu.VMEM((nit, R), jnp.int32),
                       [pltpu.VMEM(tile, jnp.int32)] * SC_RING, [pltpu.SemaphoreType.DMA] * SC_RING,
                       pltpu.VMEM((G, D), F32)],
        name="peer_combine")
    def k(v_hbm, idx_hbm, w_hbm, o_hbm, idx_v, w_v, bufs, sems, out_v):
        tok0 = _sc_worker_id() * tpw
        zero = jnp.zeros((NL,), F32)

        def compute(item, buf):
            tok = item // ipt
            item_vec = jnp.full((NL,), item, jnp.int32)
            for hf in range(2):
                def body(rg, accs):
                    ws = [plsc.bitcast(plsc.load_gather(
                        w_v, [item_vec, jnp.full((NL,), rg * RG + rr, jnp.int32)]), BF16) for rr in range(RG)]
                    his, los = [], []
                    for i in range(nv):
                        word = hf * half + i * NL
                        ps = [plsc.bitcast(buf[rg * RG + rr, word // LANES, pl.ds(word % LANES, NL)], BF16) * ws[rr]
                              for rr in range(RG)]
                        hi, lo = _sc_widen_pair((ps[0] + ps[1]) + (ps[2] + ps[3]))
                        his.append(accs[i] + hi)
                        los.append(accs[nv + i] + lo)
                    return tuple(his + los)
                accs = lax.fori_loop(0, R // RG, body, tuple(zero for _ in range(2 * nv)))
                for i in range(nv):
                    word = hf * half + i * NL
                    plsc.addupdate(out_v.at[tok, pl.ds(word, NL)], accs[i])
                    plsc.addupdate(out_v.at[tok, pl.ds(DW + word, NL)], accs[nv + i])

        @pl.loop(0, tpw // G)
        def _(g):
            t0 = tok0 + g * G
            pltpu.sync_copy(idx_hbm.at[pl.ds(t0 * ipt, nit)], idx_v)
            pltpu.sync_copy(w_hbm.at[pl.ds(t0 * ipt, nit)], w_v)

            @pl.loop(0, G)
            def _(t):
                @pl.loop(0, D // NL)
                def _(i):
                    out_v[t, pl.ds(pl.multiple_of(i * NL, NL), NL)] = zero

            _sc_gather_loop(v_hbm, idx_v, bufs, sems, nit, compute)
            pltpu.sync_copy(out_v, o_hbm.at[pl.ds(t0, G)])

    return k(v, idx4, w4)


def _act_kernel(h_ref, g_ref, w_ref):
    w = _gelu(h_ref[...]) * g_ref[...]
    w_ref[...] = _pack_bf16_pair(w, w)


def _peer_act(hraw, gate):
    T, n = gate.shape
    tm = PEER_ACT_TM
    spec = pl.BlockSpec((tm, n), lambda i: (i, 0))
    return pl.pallas_call(
        _act_kernel, grid=(T // tm,), in_specs=[spec, spec], out_specs=spec,
        out_shape=jax.ShapeDtypeStruct((T, n), jnp.int32),
        compiler_params=pltpu.CompilerParams(dimension_semantics=("parallel",)),
        name="peer_act",
    )(hraw, gate)


def _final_kernel(h_ref, p_ref, g_ref, y_ref):
    y = h_ref[...] + p_ref[...]
    ms = jnp.mean(y * y, axis=-1, keepdims=True)
    y_ref[...] = y * lax.rsqrt(ms + EPS) * g_ref[...]


def _final(h, po, g):
    T, D = h.shape
    tm = FINAL_TM
    spec = pl.BlockSpec((tm, D), lambda i: (i, 0))
    return pl.pallas_call(
        _final_kernel, grid=(T // tm,),
        in_specs=[spec, spec, pl.BlockSpec((1, D), lambda i: (0, 0))], out_specs=spec,
        out_shape=jax.ShapeDtypeStruct((T, D), F32),
        compiler_params=pltpu.CompilerParams(dimension_semantics=("parallel",)),
        name="final_norm",
    )(h, po, g)


def kernel(x, norm1_g, w_in, hgrn_lb_fwd, hgrn_lb_bwd, hgrn_out_g, diff_lam_q1, diff_lam_k1,
           diff_lam_q2, diff_lam_k2, diff_out_g, rel_bias, w_out, norm2_g, peer_w_q,
           peer_sub_keys, peer_u, peer_v, final_g):
    B, L, D = x.shape
    hw = HGRN_HEADS * HGRN_DK

    w = w_in[LAYER]
    scale = DIFF_HALF ** -0.5
    cols = lambda j: w[:, j * hw:(j + 1) * hw]
    w_r = jnp.concatenate([cols(1), cols(2), cols(0), cols(3), cols(4), cols(5) * scale, cols(6), cols(7)],
                          axis=1).astype(BF16)
    f32 = jnp.float32
    lam_init = 0.8 - 0.6 * math.exp(-0.3 * LAYER)
    lam = (jnp.exp(jnp.sum(diff_lam_q1[LAYER].astype(f32) * diff_lam_k1[LAYER].astype(f32)))
           - jnp.exp(jnp.sum(diff_lam_q2[LAYER].astype(f32) * diff_lam_k2[LAYER].astype(f32))) + lam_init)
    lam = lam.reshape(1, 1)
    bias_ext = _rel_bias_ext(rel_bias, L)
    sk = peer_sub_keys[LAYER].reshape(2 * PEER_HEADS, PEER_NKEYS, PEER_DKEY).astype(BF16)
    wq = peer_w_q[LAYER].reshape(D, -1).astype(BF16)
    wo = w_out[LAYER].astype(BF16)
    def pack_table(tab):
        words = _pack_bf16_pair(tab[:, :D // 2], tab[:, D // 2:])
        return words.reshape(tab.shape[0], D // 2 // LANES, LANES)
    u3, v3 = pack_table(peer_u[LAYER]), pack_table(peer_v[LAYER])

    bc = B // BATCH_CHUNKS
    tc = bc * L
    rows = tc * PEER_HEADS * PEER_TOPK // SC_GATHER_ROWS
    outs = []
    pending = {}
    combined = None
    for c in range(BATCH_CHUNKS + CHUNK_LAG + 1):
        x2 = x[c * bc:(c + 1) * bc].reshape(tc, D) if c < BATCH_CHUNKS else None
        if combined is not None:
            h_prev, po = combined
            if x2 is not None:
                x2, po = lax.optimization_barrier((x2, po))
            outs.append(_final(h_prev, po, final_g[None, :]).reshape(bc, L, D))
            combined = None
        if CHUNK_LAG <= c < BATCH_CHUNKS + CHUNK_LAG:
            h, idx4, gate, hraw = pending.pop(c - CHUNK_LAG)
            wts = _peer_act(hraw.reshape(tc, -1), gate)
            if x2 is not None:
                x2, wts = lax.optimization_barrier((x2, wts))
            combined = (h, _peer_combine(v3, idx4, wts.reshape(rows, SC_GATHER_ROWS), tc, D))
        if x2 is not None:
            zf, pb = _inproj(x2, norm1_g[LAYER][None, :], w_r)
            zf3 = zf.reshape(bc, L, -1)
            pb3 = pb.reshape(bc, L, -1)
            o_h = _hgrn(zf3, pb3, hgrn_lb_fwd, hgrn_lb_bwd, hgrn_out_g[LAYER][None, :])
            o_d = _attn(pb3, bias_ext, lam, diff_out_g[LAYER][None, :])
            h, hn, idx, gate = _route(x2, o_h.reshape(tc, -1), o_d.reshape(tc, -1), wo,
                                      norm2_g[LAYER][None, :], wq, sk)
            idx4 = idx.reshape(rows, SC_GATHER_ROWS)
            pending[c] = (h, idx4, gate, _peer_hidden(u3, idx4, hn))
    return jnp.concatenate(outs, axis=0)
```

```python
import functools
import math

import numpy as np
import jax
import jax.numpy as jnp
from jax import lax
from jax.experimental import pallas as pl
from jax.experimental.pallas import tpu as pltpu
from jax.experimental.pallas import tpu_sc as plsc

F32 = jnp.float32
BF16 = jnp.bfloat16
EPS = 1e-6

HGRN_HEADS = 4
HGRN_DK = 128
DIFF_HEADS = 4
DIFF_HALF = 64
REL_BUCKETS = 32
REL_MAX_DIST = 128
PEER_HEADS = 8
PEER_NKEYS = 128
PEER_DKEY = 128
PEER_TOPK = 16
LAYER = 0

LANES = 128
SUBLANES = 8
VMEM_LIMIT = 48 * 1024 * 1024

INPROJ_TM = 512
HG_C = 64
HG_LEVELS = (64, 32, 16, 8, 4, 2)
ATTN_TQ = 256
ROUTE_TM = 256
BATCH_CHUNKS = 16
CHUNK_LAG = 2
FINAL_LAG = 2
PEER_ACT_TM = 2048
FINAL_TM = 512

SC_CORES = 2
SC_SUBCORES = 16
SC_LANES = 16
SC_WORKERS = SC_CORES * SC_SUBCORES
SC_GATHER_ROWS = 64
SC_TOKEN_GROUP = 32
SC_COMBINE_GROUP = 32
SC_RING = 2


def _dot(a, b):
    return jnp.dot(a, b, preferred_element_type=F32)


def _dot_nt(a, b):
    return lax.dot_general(a, b, (((1,), (1,)), ((), ())), preferred_element_type=F32)


def _dot_tn(a, b):
    return lax.dot_general(a, b, (((0,), (0,)), ((), ())), preferred_element_type=F32)


def _silu(x):
    return x * (1.0 / (1.0 + jnp.exp(-x)))


BF16_HI_MASK = -65536


def _pack_bf16_pair(hi, lo):
    bits = lambda a: lax.bitcast_convert_type(a.astype(BF16).astype(F32), jnp.int32)
    return (bits(hi) & BF16_HI_MASK) | lax.shift_right_logical(bits(lo), 16)


def _inproj_kernel(x_ref, g_ref, w_ref, zf_ref, pb_ref):
    x = x_ref[...]
    ms = jnp.mean(x * x, axis=-1, keepdims=True)
    xn = (x * lax.rsqrt(ms + EPS) * g_ref[...]).astype(BF16)
    nz = zf_ref.shape[1]
    zf_ref[...] = _dot(xn, w_ref[:, 0:nz])
    nb = pb_ref.shape[1]
    step = 1024
    for j in range(nb // step):
        pb_ref[:, j * step:(j + 1) * step] = _dot(
            xn, w_ref[:, nz + j * step: nz + (j + 1) * step]).astype(BF16)


def _inproj(x2, g, w):
    T, D = x2.shape
    N = w.shape[1]
    nz = 2 * HGRN_HEADS * HGRN_DK
    tm = INPROJ_TM
    return pl.pallas_call(
        _inproj_kernel,
        grid=(T // tm,),
        in_specs=[
            pl.BlockSpec((tm, D), lambda i: (i, 0)),
            pl.BlockSpec((1, D), lambda i: (0, 0)),
            pl.BlockSpec((D, N), lambda i: (0, 0)),
        ],
        out_specs=[
            pl.BlockSpec((tm, nz), lambda i: (i, 0)),
            pl.BlockSpec((tm, N - nz), lambda i: (i, 0)),
        ],
        out_shape=[
            jax.ShapeDtypeStruct((T, nz), F32),
            jax.ShapeDtypeStruct((T, N - nz), BF16),
        ],
        compiler_params=pltpu.CompilerParams(
            dimension_semantics=("parallel",), vmem_limit_bytes=VMEM_LIMIT),
        name="inproj",
    )(x2, g, w)


def _hgrn_consts():
    C = HG_C
    r = np.arange(C)
    t = r[:, None]
    u = r[None, :]
    blocks = [u <= t, u > t]
    masks = [np.eye(C, dtype=bool)]
    for B in HG_LEVELS:
        half = B // 2
        a = (r // B) * B
        m = (a + half - 1)[:, None]
        upper = (r - a) >= half
        blocks.append(np.where(upper[:, None], (u > m) & (u <= t), (u > t) & (u <= m)))
        same = a[:, None] == a[None, :]
        masks.append(same & upper[:, None] & (~upper)[None, :])
    m_f = np.concatenate(blocks, 0).astype(np.float32)
    m_b = np.concatenate([b[::-1, ::-1] for b in blocks], 0).astype(np.float32)
    k_f = np.stack(masks).astype(np.float32)
    k_b = np.ascontiguousarray(k_f.transpose(0, 2, 1))
    return m_f, m_b, k_f, k_b


def _hgrn_gates(z, tab):
    tabf = tab.astype(F32)
    e = jnp.exp(tabf - jnp.max(tabf, axis=0, keepdims=True))
    lb = jnp.sum(e[0:LAYER + 1], axis=0, keepdims=True) / jnp.sum(e, axis=0, keepdims=True)
    log_lb = jnp.log(lb)
    log_1m = jnp.log1p(-lb)
    ez = jnp.exp(-jnp.abs(z))
    l1p = jnp.log1p(ez)
    log_sig = jnp.minimum(z, 0.0) - l1p
    c = log_1m + log_sig
    hi = jnp.maximum(log_lb, c)
    lo = jnp.minimum(log_lb, c)
    log_f = hi + jnp.log1p(jnp.exp(lo - hi))
    sig_neg = jnp.where(z >= 0.0, ez, 1.0) / (1.0 + ez)
    k = (1.0 - lb) * sig_neg
    return log_f, k


def _hgrn_kernel(lbf_ref, lbb_ref, og_ref, mf_ref, mb_ref, kf_ref, kb_ref,
                 q_ref, zf_ref, zb_ref, v_ref, g_ref, o_ref, of_s, ob_s):
    C = HG_C
    L = q_ref.shape[0]
    n = L // C
    dv = v_ref.shape[1]

    def chunk(c, st, m_ref, k_ref, z_ref, tab_ref, forward):
        sl = pl.ds(pl.multiple_of(c * C, C), C)
        qh = q_ref[sl, :].astype(F32)
        q = _silu(qh)
        v = v_ref[sl, :]
        log_f, k = _hgrn_gates(z_ref[sl, :], tab_ref[...])
        lf_hi = log_f.astype(BF16)
        lf_lo = (log_f - lf_hi.astype(F32)).astype(BF16)
        m = m_ref[...]
        e = jnp.exp(_dot(m, lf_hi) + _dot(m, lf_lo))
        e_b = e[0:C]
        e_s = e[C:2 * C]
        dec = e[C - 1:C] if forward else e[0:1]
        o = _dot_nt((q * e_b).astype(BF16), st.astype(BF16))
        a = _dot_nt(q.astype(BF16), k.astype(BF16)) * k_ref[0]
        for l in range(len(HG_LEVELS)):
            e_l = e[(2 + l) * C:(3 + l) * C]
            a = a + _dot_nt((q * e_l).astype(BF16), (k * e_l).astype(BF16)) * k_ref[l + 1]
        o = o + _dot(a.astype(BF16), v)
        st = st * dec + _dot_tn(v, (k * e_s).astype(BF16))
        return sl, o, st

    st0 = jnp.zeros((dv, q_ref.shape[1]), F32)

    def both(i, carry):
        st_f, st_b = carry
        sl_f, o_f, st_f = chunk(i, st_f, mf_ref, kf_ref, zf_ref, lbf_ref, True)
        sl_b, o_b, st_b = chunk(n - 1 - i, st_b, mb_ref, kb_ref, zb_ref, lbb_ref, False)
        of_s[sl_f, :] = o_f
        ob_s[sl_b, :] = o_b
        return st_f, st_b

    lax.fori_loop(0, n, both, (st0, st0))

    def finish(c, _):
        sl = pl.ds(pl.multiple_of(c * C, C), C)
        tot = of_s[sl, :] + ob_s[sl, :]
        ms = jnp.mean(tot * tot, axis=-1, keepdims=True)
        y = tot * lax.rsqrt(ms + EPS) * og_ref[...]
        o_ref[sl, :] = (y * _silu(g_ref[sl, :].astype(F32))).astype(o_ref.dtype)
        return 0

    lax.fori_loop(0, n, finish, 0)


def _hgrn(zf3, pb3, lb_f, lb_b, out_g):
    B, L, _ = zf3.shape
    H, dk = HGRN_HEADS, HGRN_DK
    m_f, m_b, k_f, k_b = _hgrn_consts()
    nlev = k_f.shape[0]
    full2 = lambda b, h: (0, 0)
    full3 = lambda b, h: (0, 0, 0)
    seq = lambda off: pl.BlockSpec((None, L, dk), lambda b, h: (b, 0, off + h))
    return pl.pallas_call(
        _hgrn_kernel,
        grid=(B, H),
        in_specs=[
            pl.BlockSpec((lb_f.shape[0], dk), lambda b, h: (0, h)),
            pl.BlockSpec((lb_b.shape[0], dk), lambda b, h: (0, h)),
            pl.BlockSpec((1, dk), full2),
            pl.BlockSpec(m_f.shape, full2),
            pl.BlockSpec(m_b.shape, full2),
            pl.BlockSpec((nlev, HG_C, HG_C), full3),
            pl.BlockSpec((nlev, HG_C, HG_C), full3),
            seq(0),
            seq(0),
            seq(H),
            seq(H),
            seq(2 * H),
        ],
        out_specs=pl.BlockSpec((None, L, dk), lambda b, h: (b, 0, h)),
        out_shape=jax.ShapeDtypeStruct((B, L, H * dk), BF16),
        scratch_shapes=[pltpu.VMEM((L, dk), F32), pltpu.VMEM((L, dk), F32)],
        compiler_params=pltpu.CompilerParams(
            dimension_semantics=("parallel", "parallel"), vmem_limit_bytes=VMEM_LIMIT),
        name="hgrn",
    )(lb_f, lb_b, out_g, jnp.asarray(m_f, BF16), jnp.asarray(m_b, BF16),
      jnp.asarray(k_f), jnp.asarray(k_b), pb3, zf3, zf3, pb3, pb3)


def _t5_bucket(rel):
    nb = REL_BUCKETS // 2
    ret = jnp.where(rel > 0, nb, 0)
    n = jnp.abs(rel)
    max_exact = nb // 2
    nf = jnp.maximum(n, 1).astype(jnp.float32)
    large = max_exact + (jnp.log(nf / max_exact) / math.log(REL_MAX_DIST / max_exact)
                         * (nb - max_exact)).astype(jnp.int32)
    large = jnp.minimum(large, nb - 1)
    return ret + jnp.where(n < max_exact, n, large)


def _rel_bias_ext(rel_bias, L):
    j = jnp.arange(2 * L - LANES, dtype=jnp.int32)
    ql = jnp.arange(LANES, dtype=jnp.int32)
    rel = j[None, :] - (L - LANES) - ql[:, None]
    bucket = _t5_bucket(rel)
    tab = rel_bias.astype(F32)
    out = jnp.zeros((tab.shape[1],) + bucket.shape, F32)
    for b in range(REL_BUCKETS):
        out = jnp.where((bucket == b)[None], tab[b][:, None, None], out)
    return out


def _attn_kernel(lam_ref, q_ref, k_ref, v_ref, bias_ref, og_ref, o_ref, *, lam_init):
    tq = q_ref.shape[0]
    L = k_ref.shape[0]
    qi = pl.program_id(2)
    lam = lam_ref[0, 0]
    q = q_ref[...]
    k = k_ref[...]
    lane = lax.broadcasted_iota(jnp.int32, q.shape, 1)
    zero = jnp.zeros_like(q)
    q0 = jnp.where(lane < DIFF_HALF, q, zero)
    q1 = jnp.where(lane >= DIFF_HALF, q, zero)
    parts = []
    for j in range(tq // LANES):
        off = pl.multiple_of(L - LANES - (qi * tq + j * LANES), LANES)
        parts.append(bias_ref[:, pl.ds(off, L)])
    bias = jnp.concatenate(parts, axis=0) if len(parts) > 1 else parts[0]

    def soft(qm):
        s = _dot_nt(qm, k) + bias
        e = jnp.exp(s - jnp.max(s, axis=-1, keepdims=True))
        return e, 1.0 / jnp.sum(e, axis=-1, keepdims=True)

    e0, r0 = soft(q0)
    e1, r1 = soft(q1)
    w = e0 * r0 - e1 * (lam * r1)
    o = _dot(w.astype(BF16), v_ref[...])
    ms = jnp.mean(o * o, axis=-1, keepdims=True)
    y = o * lax.rsqrt(ms + EPS) * og_ref[...] * (1.0 - lam_init)
    o_ref[...] = y.astype(o_ref.dtype)


def _attn(pb3, bias_ext, lam, out_g):
    B, L, _ = pb3.shape
    H, dh = DIFF_HEADS, 2 * DIFF_HALF
    tq = ATTN_TQ
    base = (3 * HGRN_HEADS * HGRN_DK) // dh
    lam_init = 0.8 - 0.6 * math.exp(-0.3 * LAYER)
    return pl.pallas_call(
        functools.partial(_attn_kernel, lam_init=lam_init),
        grid=(B, H, L // tq),
        in_specs=[
            pl.BlockSpec(memory_space=pltpu.SMEM),
            pl.BlockSpec((None, tq, dh), lambda b, h, i: (b, i, base + h)),
            pl.BlockSpec((None, L, dh), lambda b, h, i: (b, 0, base + H + h)),
            pl.BlockSpec((None, L, dh), lambda b, h, i: (b, 0, base + 2 * H + h)),
            pl.BlockSpec((None, LANES, 2 * L - LANES), lambda b, h, i: (h, 0, 0)),
            pl.BlockSpec((1, dh), lambda b, h, i: (0, 0)),
        ],
        out_specs=pl.BlockSpec((None, tq, dh), lambda b, h, i: (b, i, h)),
        out_shape=jax.ShapeDtypeStruct((B, L, H * dh), BF16),
        compiler_params=pltpu.CompilerParams(
            dimension_semantics=("parallel", "parallel", "parallel"),
            vmem_limit_bytes=VMEM_LIMIT),
        name="attn",
    )(lam, pb3, pb3, pb3, bias_ext, out_g)


def _cand_layout():
    K = PEER_TOPK
    groups = [("a", 0, 0), ("a", 0, 8), ("a", 1, 0), ("a", 2, 0), ("a", 3, 0),
              ("b", 0, 8), ("b", 0, 0), ("b", 1, 0), ("b", 2, 0)]
    seen = set()
    pos, valid = [], []
    for kind, fixed, start in groups:
        for r in range(SUBLANES):
            a, b = (fixed, start + r) if kind == "a" else (start + r, fixed)
            ok = (a + 1) * (b + 1) <= K and (a, b) not in seen
            if ok:
                seen.add((a, b))
            pos.append(a * K + b if ok else K * K + len(pos))
            valid.append(ok)
    assert len(seen) == sum(K // (a + 1) for a in range(K))
    return groups, np.array(pos, np.int32), np.array(valid, bool)


def _extract_topk(s, key, payload, k, big):
    n, tm = s.shape
    S = SUBLANES
    s3 = s.reshape(n // S, S, tm)
    key3 = key.reshape(n // S, S, tm)
    pay3 = None if payload is key else payload.reshape(n // S, S, tm)
    slot = lax.broadcasted_iota(jnp.int32, (k // S, S, tm), 0) * S + lax.broadcasted_iota(
        jnp.int32, (k // S, S, tm), 1)

    def all_reduce(x3, op):
        r = x3[0]
        for g in range(1, x3.shape[0]):
            r = op(r, x3[g])
        for sh in (S // 2, S // 4, S // 8):
            r = op(r, pltpu.roll(r, sh, axis=0))
        return r

    def body(j, carry):
        s3, vals, pay = carry
        m = all_reduce(s3, jnp.maximum)
        kk = all_reduce(jnp.where(s3 == m[None], key3, big), jnp.minimum)
        sel = key3 == kk[None]
        p = kk if pay3 is None else all_reduce(jnp.where(sel, pay3, 0), jnp.add)
        vals = jnp.where(slot == j, m[None], vals)
        pay = jnp.where(slot == j, p[None], pay)
        s3 = jnp.where(sel, -jnp.inf, s3)
        return s3, vals, pay

    init = (s3, jnp.zeros((k // S, S, tm), F32), jnp.zeros((k // S, S, tm), jnp.int32))
    _, vals, pay = lax.fori_loop(0, k, body, init)
    return vals.reshape(k, tm), pay.reshape(k, tm)


def _route_kernel(x_ref, oh_ref, od_ref, woh_ref, wod_ref, g2_ref, wq_ref, sk_ref, cpos_ref, cmask_ref,
                  h_ref, hn_ref, idx_ref, gate_ref, q_s, tv_s, ti_s, gs_s, is_s, *, groups):
    K = PEER_TOPK
    tm = x_ref.shape[0]
    h = x_ref[...] + _dot(oh_ref[...], woh_ref[...]) + _dot(od_ref[...], wod_ref[...])
    h_ref[...] = h
    hn = h * lax.rsqrt(jnp.mean(h * h, axis=-1, keepdims=True) + EPS) * g2_ref[...]
    half = hn.shape[1] // 2
    hn_ref[...] = _pack_bf16_pair(hn[:, :half], hn[:, half:])
    q_s[...] = _dot(hn.astype(BF16), wq_ref[...]).astype(BF16)

    key_iota = lax.broadcasted_iota(jnp.int32, (PEER_NKEYS, tm), 0)

    def half_topk(hp, _):
        col = pl.multiple_of(hp * PEER_DKEY, PEER_DKEY)
        s = _dot_nt(sk_ref[hp], q_s[:, pl.ds(col, PEER_DKEY)])
        vals, idxs = _extract_topk(s, key_iota, key_iota, K, PEER_NKEYS)
        tv_s[hp] = vals
        ti_s[hp] = idxs
        return 0

    lax.fori_loop(0, 2 * PEER_HEADS, half_topk, 0)

    cpos = cpos_ref[...]
    cmask = cmask_ref[...]

    def head(hd, _):
        s0, s1 = tv_s[2 * hd], tv_s[2 * hd + 1]
        i0, i1 = ti_s[2 * hd] * PEER_NKEYS, ti_s[2 * hd + 1]
        cs, ci = [], []
        for kind, fixed, start in groups:
            if kind == "a":
                cs.append(s0[fixed:fixed + 1] + s1[start:start + SUBLANES])
                ci.append(i0[fixed:fixed + 1] + i1[start:start + SUBLANES])
            else:
                cs.append(s0[start:start + SUBLANES] + s1[fixed:fixed + 1])
                ci.append(i0[start:start + SUBLANES] + i1[fixed:fixed + 1])
        cand = jnp.concatenate(cs, axis=0) + cmask
        cidx = jnp.concatenate(ci, axis=0)
        best, eidx = _extract_topk(cand, cpos, cidx, K, 2 * K * K)
        ex = jnp.exp(best - best[0:1])
        gate = ex / jnp.sum(ex, axis=0, keepdims=True)
        row = pl.ds(pl.multiple_of(hd * K, K), K)
        gs_s[row, :] = gate
        is_s[row, :] = eidx.astype(F32)
        return 0

    lax.fori_loop(0, PEER_HEADS, head, 0)
    gate_ref[...] = gs_s[...].T
    idx_ref[...] = is_s[...].T.astype(jnp.int32)


def _route(x2, oh2, od2, w_out, g2, w_q, sub_keys):
    T, D = x2.shape
    tm = ROUTE_TM
    K = PEER_TOPK
    nh = oh2.shape[1]
    nq = w_q.shape[1]
    npk = PEER_HEADS * K
    groups, pos, valid = _cand_layout()
    ncand = pos.shape[0]
    cpos = jnp.asarray(np.broadcast_to(pos[:, None], (ncand, tm)))
    cmask = jnp.asarray(np.broadcast_to(np.where(valid, 0.0, -np.inf).astype(np.float32)[:, None], (ncand, tm)))
    row = lambda i: (i, 0)
    full2 = lambda i: (0, 0)
    return pl.pallas_call(
        functools.partial(_route_kernel, groups=groups),
        grid=(T // tm,),
        in_specs=[
            pl.BlockSpec((tm, D), row),
            pl.BlockSpec((tm, nh), row),
            pl.BlockSpec((tm, nh), row),
            pl.BlockSpec((nh, D), full2),
            pl.BlockSpec((nh, D), lambda i: (1, 0)),
            pl.BlockSpec((1, D), full2),
            pl.BlockSpec((D, nq), full2),
            pl.BlockSpec(sub_keys.shape, lambda i: (0, 0, 0)),
            pl.BlockSpec((ncand, tm), full2),
            pl.BlockSpec((ncand, tm), full2),
        ],
        out_specs=[
            pl.BlockSpec((tm, D), row),
            pl.BlockSpec((tm, D // 2), row),
            pl.BlockSpec((tm, npk), row),
            pl.BlockSpec((tm, npk), row),
        ],
        out_shape=[
            jax.ShapeDtypeStruct((T, D), F32),
            jax.ShapeDtypeStruct((T, D // 2), jnp.int32),
            jax.ShapeDtypeStruct((T, npk), jnp.int32),
            jax.ShapeDtypeStruct((T, npk), F32),
        ],
        scratch_shapes=[
            pltpu.VMEM((tm, nq), BF16),
            pltpu.VMEM((2 * PEER_HEADS, K, tm), F32),
            pltpu.VMEM((2 * PEER_HEADS, K, tm), jnp.int32),
            pltpu.VMEM((npk, tm), F32),
            pltpu.VMEM((npk, tm), F32),
        ],
        compiler_params=pltpu.CompilerParams(
            dimension_semantics=("parallel",), vmem_limit_bytes=VMEM_LIMIT),
        name="route",
    )(x2, oh2, od2, w_out, w_out, g2, w_q, sub_keys, cpos, cmask)


def _gelu(x):
    return 0.5 * x * (1.0 + lax.erf(x * (1.0 / math.sqrt(2.0))))


def _sc_mesh():
    return plsc.VectorSubcoreMesh(core_axis_name="c", subcore_axis_name="s")


def _sc_worker_id():
    return lax.axis_index("s") * SC_CORES + lax.axis_index("c")


def _sc_widen_pair(s):
    si = plsc.bitcast(s, jnp.int32)
    return plsc.bitcast(si & BF16_HI_MASK, F32), plsc.bitcast(si << 16, F32)


def _sc_gather_loop(tab_hbm, idx_v, bufs, sems, n_items, compute):
    nb = len(bufs)

    def start(item, b):
        pltpu.async_copy(tab_hbm.at[idx_v.at[item]], bufs[b], sems[b])

    def wait(b):
        pltpu.make_async_copy(tab_hbm.at[idx_v.at[0]], bufs[b], sems[b]).wait()

    for p in range(nb - 1):
        start(p, p)

    @pl.loop(0, pl.cdiv(n_items, nb))
    def _(i):
        for b in range(nb):
            it = i * nb + b

            @pl.when(it < n_items)
            def _():
                @pl.when(it + nb - 1 < n_items)
                def _():
                    start(it + nb - 1, (b + nb - 1) % nb)

                wait(b)
                compute(it, bufs[b])


def _peer_hidden(u, idx4, x):
    T, DW = x.shape
    n_rows, R = idx4.shape
    ipt = n_rows // T
    G = SC_TOKEN_GROUP
    nit = G * ipt
    tpw = T // SC_WORKERS
    NL = SC_LANES
    RB = 8
    NV = 4
    GW = NV * NL
    tile = (R, DW // LANES, LANES)

    @functools.partial(
        pl.kernel, mesh=_sc_mesh(), compiler_params=pltpu.CompilerParams(needs_layout_passes=False),
        out_type=jax.ShapeDtypeStruct((n_rows, R), F32),
        scratch_types=[pltpu.VMEM((nit, R), jnp.int32), pltpu.VMEM((G, DW), jnp.int32),
                       [pltpu.VMEM(tile, jnp.int32)] * SC_RING, [pltpu.SemaphoreType.DMA] * SC_RING,
                       pltpu.VMEM((nit, R), F32)],
        name="peer_hidden")
    def k(u_hbm, idx_hbm, x_hbm, h_hbm, idx_v, x_v, bufs, sems, h_v):
        tok0 = _sc_worker_id() * tpw
        lane = lax.broadcasted_iota(jnp.int32, (NL,), 0)
        zero = jnp.zeros((NL,), F32)

        def compute(item, buf):
            tok = item // ipt
            hvs = [zero for _ in range(R // NL)]
            for rb in range(R // RB):
                def body(g, accs):
                    sub = g // (LANES // GW)
                    base = (g % (LANES // GW)) * GW
                    xs = [plsc.bitcast(x_v[tok, pl.ds(pl.multiple_of(g * GW + jj * NL, NL), NL)], BF16)
                          for jj in range(NV)]
                    out = []
                    for j in range(RB):
                        ps = [plsc.bitcast(buf[rb * RB + j, sub, pl.ds(pl.multiple_of(base + jj * NL, NL), NL)],
                                           BF16) * xs[jj] for jj in range(NV)]
                        hi, lo = _sc_widen_pair((ps[0] + ps[1]) + (ps[2] + ps[3]))
                        out.append((accs[j] + hi) + lo)
                    return tuple(out)
                accs = lax.fori_loop(0, DW // GW, body, tuple(zero for _ in range(RB)))
                for j in range(RB):
                    r = rb * RB + j
                    hvs[r // NL] = jnp.where(lane == (r % NL), jnp.sum(accs[j]), hvs[r // NL])
            for q in range(R // NL):
                h_v[item, pl.ds(q * NL, NL)] = hvs[q]

        @pl.loop(0, tpw // G)
        def _(g):
            t0 = tok0 + g * G
            pltpu.sync_copy(idx_hbm.at[pl.ds(t0 * ipt, nit)], idx_v)
            pltpu.sync_copy(x_hbm.at[pl.ds(t0, G)], x_v)
            _sc_gather_loop(u_hbm, idx_v, bufs, sems, nit, compute)
            pltpu.sync_copy(h_v, h_hbm.at[pl.ds(t0 * ipt, nit)])

    return k(u, idx4, x)


def _peer_combine(v, idx4, w4, T, D):
    n_rows, R = idx4.shape
    ipt = n_rows // T
    G = SC_COMBINE_GROUP
    nit = G * ipt
    tpw = T // SC_WORKERS
    NL = SC_LANES
    DW = D // 2
    half = DW // 2
    nv = half // NL
    RG = 4
    tile = (R, DW // LANES, LANES)

    @functools.partial(
        pl.kernel, mesh=_sc_mesh(), compiler_params=pltpu.CompilerParams(needs_layout_passes=False),
        out_type=jax.ShapeDtypeStruct((T, D), F32),
        scratch_types=[pltpu.VMEM((nit, R), jnp.int32), pltpu.VMEM((nit, R), jnp.int32),
                       [pltpu.VMEM(tile, jnp.int32)] * SC_RING, [pltpu.SemaphoreType.DMA] * SC_RING,
                       pltpu.VMEM((G, D), F32)],
        name="peer_combine")
    def k(v_hbm, idx_hbm, w_hbm, o_hbm, idx_v, w_v, bufs, sems, out_v):
        tok0 = _sc_worker_id() * tpw
        zero = jnp.zeros((NL,), F32)

        def compute(item, buf):
            tok = item // ipt
            item_vec = jnp.full((NL,), item, jnp.int32)
            for hf in range(2):
                def body(rg, accs):
                    ws = [plsc.bitcast(plsc.load_gather(
                        w_v, [item_vec, jnp.full((NL,), rg * RG + rr, jnp.int32)]), BF16) for rr in range(RG)]
                    his, los = [], []
                    for i in range(nv):
                        word = hf * half + i * NL
                        ps = [plsc.bitcast(buf[rg * RG + rr, word // LANES, pl.ds(word % LANES, NL)], BF16) * ws[rr]
                              for rr in range(RG)]
                        hi, lo = _sc_widen_pair((ps[0] + ps[1]) + (ps[2] + ps[3]))
                        his.append(accs[i] + hi)
                        los.append(accs[nv + i] + lo)
                    return tuple(his + los)
                accs = lax.fori_loop(0, R // RG, body, tuple(zero for _ in range(2 * nv)))
                for i in range(nv):
                    word = hf * half + i * NL
                    plsc.addupdate(out_v.at[tok, pl.ds(word, NL)], accs[i])
                    plsc.addupdate(out_v.at[tok, pl.ds(DW + word, NL)], accs[nv + i])

        @pl.loop(0, tpw // G)
        def _(g):
            t0 = tok0 + g * G
            pltpu.sync_copy(idx_hbm.at[pl.ds(t0 * ipt, nit)], idx_v)
            pltpu.sync_copy(w_hbm.at[pl.ds(t0 * ipt, nit)], w_v)

            @pl.loop(0, G)
            def _(t):
                @pl.loop(0, D // NL)
                def _(i):
                    out_v[t, pl.ds(pl.multiple_of(i * NL, NL), NL)] = zero

            _sc_gather_loop(v_hbm, idx_v, bufs, sems, nit, compute)
            pltpu.sync_copy(out_v, o_hbm.at[pl.ds(t0, G)])

    return k(v, idx4, w4)


def _act_kernel(h_ref, g_ref, w_ref):
    w = _gelu(h_ref[...]) * g_ref[...]
    w_ref[...] = _pack_bf16_pair(w, w)


def _peer_act(hraw, gate):
    T, n = gate.shape
    tm = PEER_ACT_TM
    spec = pl.BlockSpec((tm, n), lambda i: (i, 0))
    return pl.pallas_call(
        _act_kernel, grid=(T // tm,), in_specs=[spec, spec], out_specs=spec,
        out_shape=jax.ShapeDtypeStruct((T, n), jnp.int32),
        compiler_params=pltpu.CompilerParams(dimension_semantics=("parallel",)),
        name="peer_act",
    )(hraw, gate)


def _final_kernel(h_ref, p_ref, g_ref, y_ref):
    y = h_ref[...] + p_ref[...]
    ms = jnp.mean(y * y, axis=-1, keepdims=True)
    y_ref[...] = y * lax.rsqrt(ms + EPS) * g_ref[...]


def _final(h, po, g):
    T, D = h.shape
    tm = FINAL_TM
    spec = pl.BlockSpec((tm, D), lambda i: (i, 0))
    return pl.pallas_call(
        _final_kernel, grid=(T // tm,),
        in_specs=[spec, spec, pl.BlockSpec((1, D), lambda i: (0, 0))], out_specs=spec,
        out_shape=jax.ShapeDtypeStruct((T, D), F32),
        compiler_params=pltpu.CompilerParams(dimension_semantics=("parallel",)),
        name="final_norm",
    )(h, po, g)


def kernel(x, norm1_g, w_in, hgrn_lb_fwd, hgrn_lb_bwd, hgrn_out_g, diff_lam_q1, diff_lam_k1,
           diff_lam_q2, diff_lam_k2, diff_out_g, rel_bias, w_out, norm2_g, peer_w_q,
           peer_sub_keys, peer_u, peer_v, final_g):
    B, L, D = x.shape
    hw = HGRN_HEADS * HGRN_DK

    w = w_in[LAYER]
    scale = DIFF_HALF ** -0.5
    cols = lambda j: w[:, j * hw:(j + 1) * hw]
    w_r = jnp.concatenate([cols(1), cols(2), cols(0), cols(3), cols(4), cols(5) * scale, cols(6), cols(7)],
                          axis=1).astype(BF16)
    f32 = jnp.float32
    lam_init = 0.8 - 0.6 * math.exp(-0.3 * LAYER)
    lam = (jnp.exp(jnp.sum(diff_lam_q1[LAYER].astype(f32) * diff_lam_k1[LAYER].astype(f32)))
           - jnp.exp(jnp.sum(diff_lam_q2[LAYER].astype(f32) * diff_lam_k2[LAYER].astype(f32))) + lam_init)
    lam = lam.reshape(1, 1)
    bias_ext = _rel_bias_ext(rel_bias, L)
    sk = peer_sub_keys[LAYER].reshape(2 * PEER_HEADS, PEER_NKEYS, PEER_DKEY).astype(BF16)
    wq = peer_w_q[LAYER].reshape(D, -1).astype(BF16)
    wo = w_out[LAYER].astype(BF16)
    def pack_table(tab):
        words = _pack_bf16_pair(tab[:, :D // 2], tab[:, D // 2:])
        return words.reshape(tab.shape[0], D // 2 // LANES, LANES)
    u3, v3 = pack_table(peer_u[LAYER]), pack_table(peer_v[LAYER])

    bc = B // BATCH_CHUNKS
    tc = bc * L
    rows = tc * PEER_HEADS * PEER_TOPK // SC_GATHER_ROWS
    outs = []
    pending = {}
    combined = {}
    for c in range(BATCH_CHUNKS + CHUNK_LAG + FINAL_LAG):
        x2 = x[c * bc:(c + 1) * bc].reshape(tc, D) if c < BATCH_CHUNKS else None
        if c - CHUNK_LAG - FINAL_LAG in combined:
            h_prev, po = combined.pop(c - CHUNK_LAG - FINAL_LAG)
            if x2 is not None:
                x2, po = lax.optimization_barrier((x2, po))
            outs.append(_final(h_prev, po, final_g[None, :]).reshape(bc, L, D))
        if CHUNK_LAG <= c < BATCH_CHUNKS + CHUNK_LAG:
            h, idx4, gate, hraw = pending.pop(c - CHUNK_LAG)
            wts = _peer_act(hraw.reshape(tc, -1), gate)
            if x2 is not None:
                x2, wts = lax.optimization_barrier((x2, wts))
            combined[c - CHUNK_LAG] = (h, _peer_combine(v3, idx4, wts.reshape(rows, SC_GATHER_ROWS), tc, D))
        if x2 is not None:
            zf, pb = _inproj(x2, norm1_g[LAYER][None, :], w_r)
            zf3 = zf.reshape(bc, L, -1)
            pb3 = pb.reshape(bc, L, -1)
            o_h = _hgrn(zf3, pb3, hgrn_lb_fwd, hgrn_lb_bwd, hgrn_out_g[LAYER][None, :])
            o_d = _attn(pb3, bias_ext, lam, diff_out_g[LAYER][None, :])
            h, hn, idx, gate = _route(x2, o_h.reshape(tc, -1), o_d.reshape(tc, -1), wo,
                                      norm2_g[LAYER][None, :], wq, sk)
            idx4 = idx.reshape(rows, SC_GATHER_ROWS)
            pending[c] = (h, idx4, gate, _peer_hidden(u3, idx4, hn))
    return jnp.concatenate(outs, axis=0)
```

```python
import functools
import math

import numpy as np
import jax
import jax.numpy as jnp
from jax import lax
from jax.experimental import pallas as pl
from jax.experimental.pallas import tpu as pltpu
from jax.experimental.pallas import tpu_sc as plsc

F32 = jnp.float32
BF16 = jnp.bfloat16
EPS = 1e-6

HGRN_HEADS = 4
HGRN_DK = 128
DIFF_HEADS = 4
DIFF_HALF = 64
REL_BUCKETS = 32
REL_MAX_DIST = 128
PEER_HEADS = 8
PEER_NKEYS = 128
PEER_DKEY = 128
PEER_TOPK = 16
LAYER = 0

LANES = 128
SUBLANES = 8
VMEM_LIMIT = 48 * 1024 * 1024

INPROJ_TM = 512
HG_C = 64
HG_LEVELS = (64, 32, 16, 8, 4, 2)
ATTN_TQ = 256
ROUTE_TM = 256
BATCH_CHUNKS = 16
CHUNK_LAG = 3
FINAL_LAG = 2
PEER_ACT_TM = 2048
FINAL_TM = 512

SC_CORES = 2
SC_SUBCORES = 16
SC_LANES = 16
SC_WORKERS = SC_CORES * SC_SUBCORES
SC_GATHER_ROWS = 64
SC_TOKEN_GROUP = 32
SC_COMBINE_GROUP = 32
SC_RING = 2


def _dot(a, b):
    return jnp.dot(a, b, preferred_element_type=F32)


def _dot_nt(a, b):
    return lax.dot_general(a, b, (((1,), (1,)), ((), ())), preferred_element_type=F32)


def _dot_tn(a, b):
    return lax.dot_general(a, b, (((0,), (0,)), ((), ())), preferred_element_type=F32)


def _silu(x):
    return x * (1.0 / (1.0 + jnp.exp(-x)))


BF16_HI_MASK = -65536


def _pack_bf16_pair(hi, lo):
    bits = lambda a: lax.bitcast_convert_type(a.astype(BF16).astype(F32), jnp.int32)
    return (bits(hi) & BF16_HI_MASK) | lax.shift_right_logical(bits(lo), 16)


def _inproj_kernel(x_ref, g_ref, w_ref, zf_ref, pb_ref):
    x = x_ref[...]
    ms = jnp.mean(x * x, axis=-1, keepdims=True)
    xn = (x * lax.rsqrt(ms + EPS) * g_ref[...]).astype(BF16)
    nz = zf_ref.shape[1]
    zf_ref[...] = _dot(xn, w_ref[:, 0:nz])
    nb = pb_ref.shape[1]
    step = 1024
    for j in range(nb // step):
        pb_ref[:, j * step:(j + 1) * step] = _dot(
            xn, w_ref[:, nz + j * step: nz + (j + 1) * step]).astype(BF16)


def _inproj(x2, g, w):
    T, D = x2.shape
    N = w.shape[1]
    nz = 2 * HGRN_HEADS * HGRN_DK
    tm = INPROJ_TM
    return pl.pallas_call(
        _inproj_kernel,
        grid=(T // tm,),
        in_specs=[
            pl.BlockSpec((tm, D), lambda i: (i, 0)),
            pl.BlockSpec((1, D), lambda i: (0, 0)),
            pl.BlockSpec((D, N), lambda i: (0, 0)),
        ],
        out_specs=[
            pl.BlockSpec((tm, nz), lambda i: (i, 0)),
            pl.BlockSpec((tm, N - nz), lambda i: (i, 0)),
        ],
        out_shape=[
            jax.ShapeDtypeStruct((T, nz), F32),
            jax.ShapeDtypeStruct((T, N - nz), BF16),
        ],
        compiler_params=pltpu.CompilerParams(
            dimension_semantics=("parallel",), vmem_limit_bytes=VMEM_LIMIT),
        name="inproj",
    )(x2, g, w)


def _hgrn_consts():
    C = HG_C
    r = np.arange(C)
    t = r[:, None]
    u = r[None, :]
    blocks = [u <= t, u > t]
    masks = [np.eye(C, dtype=bool)]
    for B in HG_LEVELS:
        half = B // 2
        a = (r // B) * B
        m = (a + half - 1)[:, None]
        upper = (r - a) >= half
        blocks.append(np.where(upper[:, None], (u > m) & (u <= t), (u > t) & (u <= m)))
        same = a[:, None] == a[None, :]
        masks.append(same & upper[:, None] & (~upper)[None, :])
    m_f = np.concatenate(blocks, 0).astype(np.float32)
    m_b = np.concatenate([b[::-1, ::-1] for b in blocks], 0).astype(np.float32)
    k_f = np.stack(masks).astype(np.float32)
    k_b = np.ascontiguousarray(k_f.transpose(0, 2, 1))
    return m_f, m_b, k_f, k_b


def _hgrn_gates(z, tab):
    tabf = tab.astype(F32)
    e = jnp.exp(tabf - jnp.max(tabf, axis=0, keepdims=True))
    lb = jnp.sum(e[0:LAYER + 1], axis=0, keepdims=True) / jnp.sum(e, axis=0, keepdims=True)
    log_lb = jnp.log(lb)
    log_1m = jnp.log1p(-lb)
    ez = jnp.exp(-jnp.abs(z))
    l1p = jnp.log1p(ez)
    log_sig = jnp.minimum(z, 0.0) - l1p
    c = log_1m + log_sig
    hi = jnp.maximum(log_lb, c)
    lo = jnp.minimum(log_lb, c)
    log_f = hi + jnp.log1p(jnp.exp(lo - hi))
    sig_neg = jnp.where(z >= 0.0, ez, 1.0) / (1.0 + ez)
    k = (1.0 - lb) * sig_neg
    return log_f, k


def _hgrn_kernel(lbf_ref, lbb_ref, og_ref, mf_ref, mb_ref, kf_ref, kb_ref,
                 q_ref, zf_ref, zb_ref, v_ref, g_ref, o_ref, of_s, ob_s):
    C = HG_C
    L = q_ref.shape[0]
    n = L // C
    dv = v_ref.shape[1]

    def chunk(c, st, m_ref, k_ref, z_ref, tab_ref, forward):
        sl = pl.ds(pl.multiple_of(c * C, C), C)
        qh = q_ref[sl, :].astype(F32)
        q = _silu(qh)
        v = v_ref[sl, :]
        log_f, k = _hgrn_gates(z_ref[sl, :], tab_ref[...])
        lf_hi = log_f.astype(BF16)
        lf_lo = (log_f - lf_hi.astype(F32)).astype(BF16)
        m = m_ref[...]
        e = jnp.exp(_dot(m, lf_hi) + _dot(m, lf_lo))
        e_b = e[0:C]
        e_s = e[C:2 * C]
        dec = e[C - 1:C] if forward else e[0:1]
        o = _dot_nt((q * e_b).astype(BF16), st.astype(BF16))
        a = _dot_nt(q.astype(BF16), k.astype(BF16)) * k_ref[0]
        for l in range(len(HG_LEVELS)):
            e_l = e[(2 + l) * C:(3 + l) * C]
            a = a + _dot_nt((q * e_l).astype(BF16), (k * e_l).astype(BF16)) * k_ref[l + 1]
        o = o + _dot(a.astype(BF16), v)
        st = st * dec + _dot_tn(v, (k * e_s).astype(BF16))
        return sl, o, st

    st0 = jnp.zeros((dv, q_ref.shape[1]), F32)

    def both(i, carry):
        st_f, st_b = carry
        sl_f, o_f, st_f = chunk(i, st_f, mf_ref, kf_ref, zf_ref, lbf_ref, True)
        sl_b, o_b, st_b = chunk(n - 1 - i, st_b, mb_ref, kb_ref, zb_ref, lbb_ref, False)
        of_s[sl_f, :] = o_f
        ob_s[sl_b, :] = o_b
        return st_f, st_b

    lax.fori_loop(0, n, both, (st0, st0))

    def finish(c, _):
        sl = pl.ds(pl.multiple_of(c * C, C), C)
        tot = of_s[sl, :] + ob_s[sl, :]
        ms = jnp.mean(tot * tot, axis=-1, keepdims=True)
        y = tot * lax.rsqrt(ms + EPS) * og_ref[...]
        o_ref[sl, :] = (y * _silu(g_ref[sl, :].astype(F32))).astype(o_ref.dtype)
        return 0

    lax.fori_loop(0, n, finish, 0)


def _hgrn(zf3, pb3, lb_f, lb_b, out_g):
    B, L, _ = zf3.shape
    H, dk = HGRN_HEADS, HGRN_DK
    m_f, m_b, k_f, k_b = _hgrn_consts()
    nlev = k_f.shape[0]
    full2 = lambda b, h: (0, 0)
    full3 = lambda b, h: (0, 0, 0)
    seq = lambda off: pl.BlockSpec((None, L, dk), lambda b, h: (b, 0, off + h))
    return pl.pallas_call(
        _hgrn_kernel,
        grid=(B, H),
        in_specs=[
            pl.BlockSpec((lb_f.shape[0], dk), lambda b, h: (0, h)),
            pl.BlockSpec((lb_b.shape[0], dk), lambda b, h: (0, h)),
            pl.BlockSpec((1, dk), full2),
            pl.BlockSpec(m_f.shape, full2),
            pl.BlockSpec(m_b.shape, full2),
            pl.BlockSpec((nlev, HG_C, HG_C), full3),
            pl.BlockSpec((nlev, HG_C, HG_C), full3),
            seq(0),
            seq(0),
            seq(H),
            seq(H),
            seq(2 * H),
        ],
        out_specs=pl.BlockSpec((None, L, dk), lambda b, h: (b, 0, h)),
        out_shape=jax.ShapeDtypeStruct((B, L, H * dk), BF16),
        scratch_shapes=[pltpu.VMEM((L, dk), F32), pltpu.VMEM((L, dk), F32)],
        compiler_params=pltpu.CompilerParams(
            dimension_semantics=("parallel", "parallel"), vmem_limit_bytes=VMEM_LIMIT),
        name="hgrn",
    )(lb_f, lb_b, out_g, jnp.asarray(m_f, BF16), jnp.asarray(m_b, BF16),
      jnp.asarray(k_f), jnp.asarray(k_b), pb3, zf3, zf3, pb3, pb3)


def _t5_bucket(rel):
    nb = REL_BUCKETS // 2
    ret = jnp.where(rel > 0, nb, 0)
    n = jnp.abs(rel)
    max_exact = nb // 2
    nf = jnp.maximum(n, 1).astype(jnp.float32)
    large = max_exact + (jnp.log(nf / max_exact) / math.log(REL_MAX_DIST / max_exact)
                         * (nb - max_exact)).astype(jnp.int32)
    large = jnp.minimum(large, nb - 1)
    return ret + jnp.where(n < max_exact, n, large)


def _rel_bias_ext(rel_bias, L):
    j = jnp.arange(2 * L - LANES, dtype=jnp.int32)
    ql = jnp.arange(LANES, dtype=jnp.int32)
    rel = j[None, :] - (L - LANES) - ql[:, None]
    bucket = _t5_bucket(rel)
    tab = rel_bias.astype(F32)
    out = jnp.zeros((tab.shape[1],) + bucket.shape, F32)
    for b in range(REL_BUCKETS):
        out = jnp.where((bucket == b)[None], tab[b][:, None, None], out)
    return out


def _attn_kernel(lam_ref, q_ref, k_ref, v_ref, bias_ref, og_ref, o_ref, *, lam_init):
    tq = q_ref.shape[0]
    L = k_ref.shape[0]
    qi = pl.program_id(2)
    lam = lam_ref[0, 0]
    q = q_ref[...]
    k = k_ref[...]
    lane = lax.broadcasted_iota(jnp.int32, q.shape, 1)
    zero = jnp.zeros_like(q)
    q0 = jnp.where(lane < DIFF_HALF, q, zero)
    q1 = jnp.where(lane >= DIFF_HALF, q, zero)
    parts = []
    for j in range(tq // LANES):
        off = pl.multiple_of(L - LANES - (qi * tq + j * LANES), LANES)
        parts.append(bias_ref[:, pl.ds(off, L)])
    bias = jnp.concatenate(parts, axis=0) if len(parts) > 1 else parts[0]

    def soft(qm):
        s = _dot_nt(qm, k) + bias
        e = jnp.exp(s - jnp.max(s, axis=-1, keepdims=True))
        return e, 1.0 / jnp.sum(e, axis=-1, keepdims=True)

    e0, r0 = soft(q0)
    e1, r1 = soft(q1)
    w = e0 * r0 - e1 * (lam * r1)
    o = _dot(w.astype(BF16), v_ref[...])
    ms = jnp.mean(o * o, axis=-1, keepdims=True)
    y = o * lax.rsqrt(ms + EPS) * og_ref[...] * (1.0 - lam_init)
    o_ref[...] = y.astype(o_ref.dtype)


def _attn(pb3, bias_ext, lam, out_g):
    B, L, _ = pb3.shape
    H, dh = DIFF_HEADS, 2 * DIFF_HALF
    tq = ATTN_TQ
    base = (3 * HGRN_HEADS * HGRN_DK) // dh
    lam_init = 0.8 - 0.6 * math.exp(-0.3 * LAYER)
    return pl.pallas_call(
        functools.partial(_attn_kernel, lam_init=lam_init),
        grid=(B, H, L // tq),
        in_specs=[
            pl.BlockSpec(memory_space=pltpu.SMEM),
            pl.BlockSpec((None, tq, dh), lambda b, h, i: (b, i, base + h)),
            pl.BlockSpec((None, L, dh), lambda b, h, i: (b, 0, base + H + h)),
            pl.BlockSpec((None, L, dh), lambda b, h, i: (b, 0, base + 2 * H + h)),
            pl.BlockSpec((None, LANES, 2 * L - LANES), lambda b, h, i: (h, 0, 0)),
            pl.BlockSpec((1, dh), lambda b, h, i: (0, 0)),
        ],
        out_specs=pl.BlockSpec((None, tq, dh), lambda b, h, i: (b, i, h)),
        out_shape=jax.ShapeDtypeStruct((B, L, H * dh), BF16),
        compiler_params=pltpu.CompilerParams(
            dimension_semantics=("parallel", "parallel", "parallel"),
            vmem_limit_bytes=VMEM_LIMIT),
        name="attn",
    )(lam, pb3, pb3, pb3, bias_ext, out_g)


def _cand_layout():
    K = PEER_TOPK
    groups = [("a", 0, 0), ("a", 0, 8), ("a", 1, 0), ("a", 2, 0), ("a", 3, 0),
              ("b", 0, 8), ("b", 0, 0), ("b", 1, 0), ("b", 2, 0)]
    seen = set()
    pos, valid = [], []
    for kind, fixed, start in groups:
        for r in range(SUBLANES):
            a, b = (fixed, start + r) if kind == "a" else (start + r, fixed)
            ok = (a + 1) * (b + 1) <= K and (a, b) not in seen
            if ok:
                seen.add((a, b))
            pos.append(a * K + b if ok else K * K + len(pos))
            valid.append(ok)
    assert len(seen) == sum(K // (a + 1) for a in range(K))
    return groups, np.array(pos, np.int32), np.array(valid, bool)


def _extract_topk(s, key, payload, k, big):
    n, tm = s.shape
    S = SUBLANES
    s3 = s.reshape(n // S, S, tm)
    key3 = key.reshape(n // S, S, tm)
    pay3 = None if payload is key else payload.reshape(n // S, S, tm)
    slot = lax.broadcasted_iota(jnp.int32, (k // S, S, tm), 0) * S + lax.broadcasted_iota(
        jnp.int32, (k // S, S, tm), 1)

    def all_reduce(x3, op):
        r = x3[0]
        for g in range(1, x3.shape[0]):
            r = op(r, x3[g])
        for sh in (S // 2, S // 4, S // 8):
            r = op(r, pltpu.roll(r, sh, axis=0))
        return r

    def body(j, carry):
        s3, vals, pay = carry
        m = all_reduce(s3, jnp.maximum)
        kk = all_reduce(jnp.where(s3 == m[None], key3, big), jnp.minimum)
        sel = key3 == kk[None]
        p = kk if pay3 is None else all_reduce(jnp.where(sel, pay3, 0), jnp.add)
        vals = jnp.where(slot == j, m[None], vals)
        pay = jnp.where(slot == j, p[None], pay)
        s3 = jnp.where(sel, -jnp.inf, s3)
        return s3, vals, pay

    init = (s3, jnp.zeros((k // S, S, tm), F32), jnp.zeros((k // S, S, tm), jnp.int32))
    _, vals, pay = lax.fori_loop(0, k, body, init)
    return vals.reshape(k, tm), pay.reshape(k, tm)


def _route_kernel(x_ref, oh_ref, od_ref, woh_ref, wod_ref, g2_ref, wq_ref, sk_ref, cpos_ref, cmask_ref,
                  h_ref, hn_ref, idx_ref, gate_ref, q_s, tv_s, ti_s, gs_s, is_s, *, groups):
    K = PEER_TOPK
    tm = x_ref.shape[0]
    h = x_ref[...] + _dot(oh_ref[...], woh_ref[...]) + _dot(od_ref[...], wod_ref[...])
    h_ref[...] = h
    hn = h * lax.rsqrt(jnp.mean(h * h, axis=-1, keepdims=True) + EPS) * g2_ref[...]
    half = hn.shape[1] // 2
    hn_ref[...] = _pack_bf16_pair(hn[:, :half], hn[:, half:])
    q_s[...] = _dot(hn.astype(BF16), wq_ref[...]).astype(BF16)

    key_iota = lax.broadcasted_iota(jnp.int32, (PEER_NKEYS, tm), 0)

    def half_topk(hp, _):
        col = pl.multiple_of(hp * PEER_DKEY, PEER_DKEY)
        s = _dot_nt(sk_ref[hp], q_s[:, pl.ds(col, PEER_DKEY)])
        vals, idxs = _extract_topk(s, key_iota, key_iota, K, PEER_NKEYS)
        tv_s[hp] = vals
        ti_s[hp] = idxs
        return 0

    lax.fori_loop(0, 2 * PEER_HEADS, half_topk, 0)

    cpos = cpos_ref[...]
    cmask = cmask_ref[...]

    def head(hd, _):
        s0, s1 = tv_s[2 * hd], tv_s[2 * hd + 1]
        i0, i1 = ti_s[2 * hd] * PEER_NKEYS, ti_s[2 * hd + 1]
        cs, ci = [], []
        for kind, fixed, start in groups:
            if kind == "a":
                cs.append(s0[fixed:fixed + 1] + s1[start:start + SUBLANES])
                ci.append(i0[fixed:fixed + 1] + i1[start:start + SUBLANES])
            else:
                cs.append(s0[start:start + SUBLANES] + s1[fixed:fixed + 1])
                ci.append(i0[start:start + SUBLANES] + i1[fixed:fixed + 1])
        cand = jnp.concatenate(cs, axis=0) + cmask
        cidx = jnp.concatenate(ci, axis=0)
        best, eidx = _extract_topk(cand, cpos, cidx, K, 2 * K * K)
        ex = jnp.exp(best - best[0:1])
        gate = ex / jnp.sum(ex, axis=0, keepdims=True)
        row = pl.ds(pl.multiple_of(hd * K, K), K)
        gs_s[row, :] = gate
        is_s[row, :] = eidx.astype(F32)
        return 0

    lax.fori_loop(0, PEER_HEADS, head, 0)
    gate_ref[...] = gs_s[...].T
    idx_ref[...] = is_s[...].T.astype(jnp.int32)


def _route(x2, oh2, od2, w_out, g2, w_q, sub_keys):
    T, D = x2.shape
    tm = ROUTE_TM
    K = PEER_TOPK
    nh = oh2.shape[1]
    nq = w_q.shape[1]
    npk = PEER_HEADS * K
    groups, pos, valid = _cand_layout()
    ncand = pos.shape[0]
    cpos = jnp.asarray(np.broadcast_to(pos[:, None], (ncand, tm)))
    cmask = jnp.asarray(np.broadcast_to(np.where(valid, 0.0, -np.inf).astype(np.float32)[:, None], (ncand, tm)))
    row = lambda i: (i, 0)
    full2 = lambda i: (0, 0)
    return pl.pallas_call(
        functools.partial(_route_kernel, groups=groups),
        grid=(T // tm,),
        in_specs=[
            pl.BlockSpec((tm, D), row),
            pl.BlockSpec((tm, nh), row),
            pl.BlockSpec((tm, nh), row),
            pl.BlockSpec((nh, D), full2),
            pl.BlockSpec((nh, D), lambda i: (1, 0)),
            pl.BlockSpec((1, D), full2),
            pl.BlockSpec((D, nq), full2),
            pl.BlockSpec(sub_keys.shape, lambda i: (0, 0, 0)),
            pl.BlockSpec((ncand, tm), full2),
            pl.BlockSpec((ncand, tm), full2),
        ],
        out_specs=[
            pl.BlockSpec((tm, D), row),
            pl.BlockSpec((tm, D // 2), row),
            pl.BlockSpec((tm, npk), row),
            pl.BlockSpec((tm, npk), row),
        ],
        out_shape=[
            jax.ShapeDtypeStruct((T, D), F32),
            jax.ShapeDtypeStruct((T, D // 2), jnp.int32),
            jax.ShapeDtypeStruct((T, npk), jnp.int32),
            jax.ShapeDtypeStruct((T, npk), F32),
        ],
        scratch_shapes=[
            pltpu.VMEM((tm, nq), BF16),
            pltpu.VMEM((2 * PEER_HEADS, K, tm), F32),
            pltpu.VMEM((2 * PEER_HEADS, K, tm), jnp.int32),
            pltpu.VMEM((npk, tm), F32),
            pltpu.VMEM((npk, tm), F32),
        ],
        compiler_params=pltpu.CompilerParams(
            dimension_semantics=("parallel",), vmem_limit_bytes=VMEM_LIMIT),
        name="route",
    )(x2, oh2, od2, w_out, w_out, g2, w_q, sub_keys, cpos, cmask)


def _gelu(x):
    return 0.5 * x * (1.0 + lax.erf(x * (1.0 / math.sqrt(2.0))))


def _sc_mesh():
    return plsc.VectorSubcoreMesh(core_axis_name="c", subcore_axis_name="s")


def _sc_worker_id():
    return lax.axis_index("s") * SC_CORES + lax.axis_index("c")


def _sc_widen_pair(s):
    si = plsc.bitcast(s, jnp.int32)
    return plsc.bitcast(si & BF16_HI_MASK, F32), plsc.bitcast(si << 16, F32)


def _sc_gather_loop(tab_hbm, idx_v, bufs, sems, n_items, compute):
    nb = len(bufs)

    def start(item, b):
        pltpu.async_copy(tab_hbm.at[idx_v.at[item]], bufs[b], sems[b])

    def wait(b):
        pltpu.make_async_copy(tab_hbm.at[idx_v.at[0]], bufs[b], sems[b]).wait()

    for p in range(nb - 1):
        start(p, p)

    @pl.loop(0, pl.cdiv(n_items, nb))
    def _(i):
        for b in range(nb):
            it = i * nb + b

            @pl.when(it < n_items)
            def _():
                @pl.when(it + nb - 1 < n_items)
                def _():
                    start(it + nb - 1, (b + nb - 1) % nb)

                wait(b)
                compute(it, bufs[b])


def _peer_hidden(u, idx4, x):
    T, DW = x.shape
    n_rows, R = idx4.shape
    ipt = n_rows // T
    G = SC_TOKEN_GROUP
    nit = G * ipt
    tpw = T // SC_WORKERS
    NL = SC_LANES
    RB = 8
    NV = 4
    GW = NV * NL
    tile = (R, DW // LANES, LANES)

    @functools.partial(
        pl.kernel, mesh=_sc_mesh(), compiler_params=pltpu.CompilerParams(needs_layout_passes=False),
        out_type=jax.ShapeDtypeStruct((n_rows, R), F32),
        scratch_types=[pltpu.VMEM((nit, R), jnp.int32), pltpu.VMEM((G, DW), jnp.int32),
                       [pltpu.VMEM(tile, jnp.int32)] * SC_RING, [pltpu.SemaphoreType.DMA] * SC_RING,
                       pltpu.VMEM((nit, R), F32)],
        name="peer_hidden")
    def k(u_hbm, idx_hbm, x_hbm, h_hbm, idx_v, x_v, bufs, sems, h_v):
        tok0 = _sc_worker_id() * tpw
        lane = lax.broadcasted_iota(jnp.int32, (NL,), 0)
        zero = jnp.zeros((NL,), F32)

        def compute(item, buf):
            tok = item // ipt
            hvs = [zero for _ in range(R // NL)]
            for rb in range(R // RB):
                def body(g, accs):
                    sub = g // (LANES // GW)
                    base = (g % (LANES // GW)) * GW
                    xs = [plsc.bitcast(x_v[tok, pl.ds(pl.multiple_of(g * GW + jj * NL, NL), NL)], BF16)
                          for jj in range(NV)]
                    out = []
                    for j in range(RB):
                        ps = [plsc.bitcast(buf[rb * RB + j, sub, pl.ds(pl.multiple_of(base + jj * NL, NL), NL)],
                                           BF16) * xs[jj] for jj in range(NV)]
                        hi, lo = _sc_widen_pair((ps[0] + ps[1]) + (ps[2] + ps[3]))
                        out.append((accs[j] + hi) + lo)
                    return tuple(out)
                accs = lax.fori_loop(0, DW // GW, body, tuple(zero for _ in range(RB)))
                for j in range(RB):
                    r = rb * RB + j
                    hvs[r // NL] = jnp.where(lane == (r % NL), jnp.sum(accs[j]), hvs[r // NL])
            for q in range(R // NL):
                h_v[item, pl.ds(q * NL, NL)] = hvs[q]

        @pl.loop(0, tpw // G)
        def _(g):
            t0 = tok0 + g * G
            pltpu.sync_copy(idx_hbm.at[pl.ds(t0 * ipt, nit)], idx_v)
            pltpu.sync_copy(x_hbm.at[pl.ds(t0, G)], x_v)
            _sc_gather_loop(u_hbm, idx_v, bufs, sems, nit, compute)
            pltpu.sync_copy(h_v, h_hbm.at[pl.ds(t0 * ipt, nit)])

    return k(u, idx4, x)


def _peer_combine(v, idx4, w4, T, D):
    n_rows, R = idx4.shape
    ipt = n_rows // T
    G = SC_COMBINE_GROUP
    nit = G * ipt
    tpw = T // SC_WORKERS
    NL = SC_LANES
    DW = D // 2
    half = DW // 2
    nv = half // NL
    RG = 4
    tile = (R, DW // LANES, LANES)

    @functools.partial(
        pl.kernel, mesh=_sc_mesh(), compiler_params=pltpu.CompilerParams(needs_layout_passes=False),
        out_type=jax.ShapeDtypeStruct((T, D), F32),
        scratch_types=[pltpu.VMEM((nit, R), jnp.int32), pltpu.VMEM((nit, R), jnp.int32),
                       [pltpu.VMEM(tile, jnp.int32)] * SC_RING, [pltpu.SemaphoreType.DMA] * SC_RING,
                       pltpu.VMEM((G, D), F32)],
        name="peer_combine")
    def k(v_hbm, idx_hbm, w_hbm, o_hbm, idx_v, w_v, bufs, sems, out_v):
        tok0 = _sc_worker_id() * tpw
        zero = jnp.zeros((NL,), F32)

        def compute(item, buf):
            tok = item // ipt
            item_vec = jnp.full((NL,), item, jnp.int32)
            for hf in range(2):
                def body(rg, accs):
                    ws = [plsc.bitcast(plsc.load_gather(
                        w_v, [item_vec, jnp.full((NL,), rg * RG + rr, jnp.int32)]), BF16) for rr in range(RG)]
                    his, los = [], []
                    for i in range(nv):
                        word = hf * half + i * NL
                        ps = [plsc.bitcast(buf[rg * RG + rr, word // LANES, pl.ds(word % LANES, NL)], BF16) * ws[rr]
                              for rr in range(RG)]
                        hi, lo = _sc_widen_pair((ps[0] + ps[1]) + (ps[2] + ps[3]))
                        his.append(accs[i] + hi)
                        los.append(accs[nv + i] + lo)
                    return tuple(his + los)
                accs = lax.fori_loop(0, R // RG, body, tuple(zero for _ in range(2 * nv)))
                for i in range(nv):
                    word = hf * half + i * NL
                    plsc.addupdate(out_v.at[tok, pl.ds(word, NL)], accs[i])
                    plsc.addupdate(out_v.at[tok, pl.ds(DW + word, NL)], accs[nv + i])

        @pl.loop(0, tpw // G)
        def _(g):
            t0 = tok0 + g * G
            pltpu.sync_copy(idx_hbm.at[pl.ds(t0 * ipt, nit)], idx_v)
            pltpu.sync_copy(w_hbm.at[pl.ds(t0 * ipt, nit)], w_v)

            @pl.loop(0, G)
            def _(t):
                @pl.loop(0, D // NL)
                def _(i):
                    out_v[t, pl.ds(pl.multiple_of(i * NL, NL), NL)] = zero

            _sc_gather_loop(v_hbm, idx_v, bufs, sems, nit, compute)
            pltpu.sync_copy(out_v, o_hbm.at[pl.ds(t0, G)])

    return k(v, idx4, w4)


def _act_kernel(h_ref, g_ref, w_ref):
    w = _gelu(h_ref[...]) * g_ref[...]
    w_ref[...] = _pack_bf16_pair(w, w)


def _peer_act(hraw, gate):
    T, n = gate.shape
    tm = PEER_ACT_TM
    spec = pl.BlockSpec((tm, n), lambda i: (i, 0))
    return pl.pallas_call(
        _act_kernel, grid=(T // tm,), in_specs=[spec, spec], out_specs=spec,
        out_shape=jax.ShapeDtypeStruct((T, n), jnp.int32),
        compiler_params=pltpu.CompilerParams(dimension_semantics=("parallel",)),
        name="peer_act",
    )(hraw, gate)


def _final_kernel(h_ref, p_ref, g_ref, y_ref):
    y = h_ref[...] + p_ref[...]
    ms = jnp.mean(y * y, axis=-1, keepdims=True)
    y_ref[...] = y * lax.rsqrt(ms + EPS) * g_ref[...]


def _final(h, po, g):
    T, D = h.shape
    tm = FINAL_TM
    spec = pl.BlockSpec((tm, D), lambda i: (i, 0))
    return pl.pallas_call(
        _final_kernel, grid=(T // tm,),
        in_specs=[spec, spec, pl.BlockSpec((1, D), lambda i: (0, 0))], out_specs=spec,
        out_shape=jax.ShapeDtypeStruct((T, D), F32),
        compiler_params=pltpu.CompilerParams(dimension_semantics=("parallel",)),
        name="final_norm",
    )(h, po, g)


def kernel(x, norm1_g, w_in, hgrn_lb_fwd, hgrn_lb_bwd, hgrn_out_g, diff_lam_q1, diff_lam_k1,
           diff_lam_q2, diff_lam_k2, diff_out_g, rel_bias, w_out, norm2_g, peer_w_q,
           peer_sub_keys, peer_u, peer_v, final_g):
    B, L, D = x.shape
    hw = HGRN_HEADS * HGRN_DK

    w = w_in[LAYER]
    scale = DIFF_HALF ** -0.5
    cols = lambda j: w[:, j * hw:(j + 1) * hw]
    w_r = jnp.concatenate([cols(1), cols(2), cols(0), cols(3), cols(4), cols(5) * scale, cols(6), cols(7)],
                          axis=1).astype(BF16)
    f32 = jnp.float32
    lam_init = 0.8 - 0.6 * math.exp(-0.3 * LAYER)
    lam = (jnp.exp(jnp.sum(diff_lam_q1[LAYER].astype(f32) * diff_lam_k1[LAYER].astype(f32)))
           - jnp.exp(jnp.sum(diff_lam_q2[LAYER].astype(f32) * diff_lam_k2[LAYER].astype(f32))) + lam_init)
    lam = lam.reshape(1, 1)
    bias_ext = _rel_bias_ext(rel_bias, L)
    sk = peer_sub_keys[LAYER].reshape(2 * PEER_HEADS, PEER_NKEYS, PEER_DKEY).astype(BF16)
    wq = peer_w_q[LAYER].reshape(D, -1).astype(BF16)
    wo = w_out[LAYER].astype(BF16)
    def pack_table(tab):
        words = _pack_bf16_pair(tab[:, :D // 2], tab[:, D // 2:])
        return words.reshape(tab.shape[0], D // 2 // LANES, LANES)
    u3, v3 = pack_table(peer_u[LAYER]), pack_table(peer_v[LAYER])

    bc = B // BATCH_CHUNKS
    tc = bc * L
    rows = tc * PEER_HEADS * PEER_TOPK // SC_GATHER_ROWS
    outs = []
    pending = {}
    combined = {}
    for c in range(BATCH_CHUNKS + CHUNK_LAG + FINAL_LAG):
        x2 = x[c * bc:(c + 1) * bc].reshape(tc, D) if c < BATCH_CHUNKS else None
        if c - CHUNK_LAG - FINAL_LAG in combined:
            h_prev, po = combined.pop(c - CHUNK_LAG - FINAL_LAG)
            if x2 is not None:
                x2, po = lax.optimization_barrier((x2, po))
            outs.append(_final(h_prev, po, final_g[None, :]).reshape(bc, L, D))
        if CHUNK_LAG <= c < BATCH_CHUNKS + CHUNK_LAG:
            h, idx4, gate, hraw = pending.pop(c - CHUNK_LAG)
            wts = _peer_act(hraw.reshape(tc, -1), gate)
            if x2 is not None:
                x2, wts = lax.optimization_barrier((x2, wts))
            combined[c - CHUNK_LAG] = (h, _peer_combine(v3, idx4, wts.reshape(rows, SC_GATHER_ROWS), tc, D))
        if x2 is not None:
            zf, pb = _inproj(x2, norm1_g[LAYER][None, :], w_r)
            zf3 = zf.reshape(bc, L, -1)
            pb3 = pb.reshape(bc, L, -1)
            o_h = _hgrn(zf3, pb3, hgrn_lb_fwd, hgrn_lb_bwd, hgrn_out_g[LAYER][None, :])
            o_d = _attn(pb3, bias_ext, lam, diff_out_g[LAYER][None, :])
            h, hn, idx, gate = _route(x2, o_h.reshape(tc, -1), o_d.reshape(tc, -1), wo,
                                      norm2_g[LAYER][None, :], wq, sk)
            idx4 = idx.reshape(rows, SC_GATHER_ROWS)
            pending[c] = (h, idx4, gate, _peer_hidden(u3, idx4, hn))
    return jnp.concatenate(outs, axis=0)
```

```python
import functools
import math

import numpy as np
import jax
import jax.numpy as jnp
from jax import lax
from jax.experimental import pallas as pl
from jax.experimental.pallas import tpu as pltpu
from jax.experimental.pallas import tpu_sc as plsc

F32 = jnp.float32
BF16 = jnp.bfloat16
EPS = 1e-6

HGRN_HEADS = 4
HGRN_DK = 128
DIFF_HEADS = 4
DIFF_HALF = 64
REL_BUCKETS = 32
REL_MAX_DIST = 128
PEER_HEADS = 8
PEER_NKEYS = 128
PEER_DKEY = 128
PEER_TOPK = 16
LAYER = 0

LANES = 128
SUBLANES = 8
VMEM_LIMIT = 48 * 1024 * 1024

INPROJ_TM = 512
HG_C = 64
HG_LEVELS = (64, 32, 16, 8, 4, 2)
ATTN_TQ = 256
ROUTE_TM = 256
BATCH_CHUNKS = 16
CHUNK_LAG = 2
FINAL_LAG = 1
PEER_ACT_TM = 2048
FINAL_TM = 512

SC_CORES = 2
SC_SUBCORES = 16
SC_LANES = 16
SC_WORKERS = SC_CORES * SC_SUBCORES
SC_GATHER_ROWS = 64
SC_TOKEN_GROUP = 32
SC_COMBINE_GROUP = 32
SC_RING = 2


def _dot(a, b):
    return jnp.dot(a, b, preferred_element_type=F32)


def _dot_nt(a, b):
    return lax.dot_general(a, b, (((1,), (1,)), ((), ())), preferred_element_type=F32)


def _dot_tn(a, b):
    return lax.dot_general(a, b, (((0,), (0,)), ((), ())), preferred_element_type=F32)


def _silu(x):
    return x * (1.0 / (1.0 + jnp.exp(-x)))


BF16_HI_MASK = -65536


def _pack_bf16_pair(hi, lo):
    bits = lambda a: lax.bitcast_convert_type(a.astype(BF16).astype(F32), jnp.int32)
    return (bits(hi) & BF16_HI_MASK) | lax.shift_right_logical(bits(lo), 16)


def _inproj_kernel(x_ref, g_ref, w_ref, zf_ref, pb_ref):
    x = x_ref[...]
    ms = jnp.mean(x * x, axis=-1, keepdims=True)
    xn = (x * lax.rsqrt(ms + EPS) * g_ref[...]).astype(BF16)
    hw = zf_ref.shape[1] // 2
    zf_ref[...] = _dot(xn, w_ref[:, hw:3 * hw])
    pb_ref[:, 0:hw] = _dot(xn, w_ref[:, 0:hw]).astype(BF16)
    for j in range(3, 8):
        y = _dot(xn, w_ref[:, j * hw:(j + 1) * hw])
        if j == 5:
            y = y * (DIFF_HALF ** -0.5)
        pb_ref[:, (j - 2) * hw:(j - 1) * hw] = y.astype(BF16)


def _inproj(x2, g, w):
    T, D = x2.shape
    N = w.shape[1]
    nz = 2 * HGRN_HEADS * HGRN_DK
    tm = INPROJ_TM
    return pl.pallas_call(
        _inproj_kernel,
        grid=(T // tm,),
        in_specs=[
            pl.BlockSpec((tm, D), lambda i: (i, 0)),
            pl.BlockSpec((1, D), lambda i: (0, 0)),
            pl.BlockSpec((D, N), lambda i: (0, 0)),
        ],
        out_specs=[
            pl.BlockSpec((tm, nz), lambda i: (i, 0)),
            pl.BlockSpec((tm, N - nz), lambda i: (i, 0)),
        ],
        out_shape=[
            jax.ShapeDtypeStruct((T, nz), F32),
            jax.ShapeDtypeStruct((T, N - nz), BF16),
        ],
        compiler_params=pltpu.CompilerParams(
            dimension_semantics=("parallel",), vmem_limit_bytes=VMEM_LIMIT),
        name="inproj",
    )(x2, g, w)


def _hgrn_consts():
    C = HG_C
    r = np.arange(C)
    t = r[:, None]
    u = r[None, :]
    blocks = [u <= t, u > t]
    masks = [np.eye(C, dtype=bool)]
    for B in HG_LEVELS:
        half = B // 2
        a = (r // B) * B
        m = (a + half - 1)[:, None]
        upper = (r - a) >= half
        blocks.append(np.where(upper[:, None], (u > m) & (u <= t), (u > t) & (u <= m)))
        same = a[:, None] == a[None, :]
        masks.append(same & upper[:, None] & (~upper)[None, :])
    m_f = np.concatenate(blocks, 0).astype(np.float32)
    m_b = np.concatenate([b[::-1, ::-1] for b in blocks], 0).astype(np.float32)
    k_f = np.stack(masks).astype(np.float32)
    k_b = np.ascontiguousarray(k_f.transpose(0, 2, 1))
    return m_f, m_b, k_f, k_b


def _hgrn_gates(z, tab):
    tabf = tab.astype(F32)
    e = jnp.exp(tabf - jnp.max(tabf, axis=0, keepdims=True))
    lb = jnp.sum(e[0:LAYER + 1], axis=0, keepdims=True) / jnp.sum(e, axis=0, keepdims=True)
    log_lb = jnp.log(lb)
    log_1m = jnp.log1p(-lb)
    ez = jnp.exp(-jnp.abs(z))
    l1p = jnp.log1p(ez)
    log_sig = jnp.minimum(z, 0.0) - l1p
    c = log_1m + log_sig
    hi = jnp.maximum(log_lb, c)
    lo = jnp.minimum(log_lb, c)
    log_f = hi + jnp.log1p(jnp.exp(lo - hi))
    sig_neg = jnp.where(z >= 0.0, ez, 1.0) / (1.0 + ez)
    k = (1.0 - lb) * sig_neg
    return log_f, k


def _hgrn_kernel(lbf_ref, lbb_ref, og_ref, mf_ref, mb_ref, kf_ref, kb_ref,
                 q_ref, zf_ref, zb_ref, v_ref, g_ref, o_ref, of_s, ob_s):
    C = HG_C
    L = q_ref.shape[0]
    n = L // C
    dv = v_ref.shape[1]

    def chunk(c, st, m_ref, k_ref, z_ref, tab_ref, forward):
        sl = pl.ds(pl.multiple_of(c * C, C), C)
        qh = q_ref[sl, :].astype(F32)
        q = _silu(qh)
        v = v_ref[sl, :]
        log_f, k = _hgrn_gates(z_ref[sl, :], tab_ref[...])
        lf_hi = log_f.astype(BF16)
        lf_lo = (log_f - lf_hi.astype(F32)).astype(BF16)
        m = m_ref[...]
        e = jnp.exp(_dot(m, lf_hi) + _dot(m, lf_lo))
        e_b = e[0:C]
        e_s = e[C:2 * C]
        dec = e[C - 1:C] if forward else e[0:1]
        o = _dot_nt((q * e_b).astype(BF16), st.astype(BF16))
        a = _dot_nt(q.astype(BF16), k.astype(BF16)) * k_ref[0]
        for l in range(len(HG_LEVELS)):
            e_l = e[(2 + l) * C:(3 + l) * C]
            a = a + _dot_nt((q * e_l).astype(BF16), (k * e_l).astype(BF16)) * k_ref[l + 1]
        o = o + _dot(a.astype(BF16), v)
        st = st * dec + _dot_tn(v, (k * e_s).astype(BF16))
        return sl, o, st

    st0 = jnp.zeros((dv, q_ref.shape[1]), F32)

    def both(i, carry):
        st_f, st_b = carry
        sl_f, o_f, st_f = chunk(i, st_f, mf_ref, kf_ref, zf_ref, lbf_ref, True)
        sl_b, o_b, st_b = chunk(n - 1 - i, st_b, mb_ref, kb_ref, zb_ref, lbb_ref, False)
        of_s[sl_f, :] = o_f
        ob_s[sl_b, :] = o_b
        return st_f, st_b

    lax.fori_loop(0, n, both, (st0, st0))

    def finish(c, _):
        sl = pl.ds(pl.multiple_of(c * C, C), C)
        tot = of_s[sl, :] + ob_s[sl, :]
        ms = jnp.mean(tot * tot, axis=-1, keepdims=True)
        y = tot * lax.rsqrt(ms + EPS) * og_ref[...]
        o_ref[sl, :] = (y * _silu(g_ref[sl, :].astype(F32))).astype(o_ref.dtype)
        return 0

    lax.fori_loop(0, n, finish, 0)


def _hgrn(zf3, pb3, lb_f, lb_b, out_g):
    B, L, _ = zf3.shape
    H, dk = HGRN_HEADS, HGRN_DK
    m_f, m_b, k_f, k_b = _hgrn_consts()
    nlev = k_f.shape[0]
    full2 = lambda b, h: (0, 0)
    full3 = lambda b, h: (0, 0, 0)
    seq = lambda off: pl.BlockSpec((None, L, dk), lambda b, h: (b, 0, off + h))
    return pl.pallas_call(
        _hgrn_kernel,
        grid=(B, H),
        in_specs=[
            pl.BlockSpec((lb_f.shape[0], dk), lambda b, h: (0, h)),
            pl.BlockSpec((lb_b.shape[0], dk), lambda b, h: (0, h)),
            pl.BlockSpec((1, dk), full2),
            pl.BlockSpec(m_f.shape, full2),
            pl.BlockSpec(m_b.shape, full2),
            pl.BlockSpec((nlev, HG_C, HG_C), full3),
            pl.BlockSpec((nlev, HG_C, HG_C), full3),
            seq(0),
            seq(0),
            seq(H),
            seq(H),
            seq(2 * H),
        ],
        out_specs=pl.BlockSpec((None, L, dk), lambda b, h: (b, 0, h)),
        out_shape=jax.ShapeDtypeStruct((B, L, H * dk), BF16),
        scratch_shapes=[pltpu.VMEM((L, dk), F32), pltpu.VMEM((L, dk), F32)],
        compiler_params=pltpu.CompilerParams(
            dimension_semantics=("parallel", "parallel"), vmem_limit_bytes=VMEM_LIMIT),
        name="hgrn",
    )(lb_f, lb_b, out_g, jnp.asarray(m_f, BF16), jnp.asarray(m_b, BF16),
      jnp.asarray(k_f), jnp.asarray(k_b), pb3, zf3, zf3, pb3, pb3)


def _t5_bucket(rel):
    nb = REL_BUCKETS // 2
    ret = jnp.where(rel > 0, nb, 0)
    n = jnp.abs(rel)
    max_exact = nb // 2
    nf = jnp.maximum(n, 1).astype(jnp.float32)
    large = max_exact + (jnp.log(nf / max_exact) / math.log(REL_MAX_DIST / max_exact)
                         * (nb - max_exact)).astype(jnp.int32)
    large = jnp.minimum(large, nb - 1)
    return ret + jnp.where(n < max_exact, n, large)


def _rel_bias_ext(rel_bias, L):
    j = jnp.arange(2 * L - LANES, dtype=jnp.int32)
    ql = jnp.arange(LANES, dtype=jnp.int32)
    rel = j[None, :] - (L - LANES) - ql[:, None]
    bucket = _t5_bucket(rel)
    tab = rel_bias.astype(F32)
    out = jnp.zeros((tab.shape[1],) + bucket.shape, F32)
    for b in range(REL_BUCKETS):
        out = jnp.where((bucket == b)[None], tab[b][:, None, None], out)
    return out


def _attn_kernel(lam_ref, q_ref, k_ref, v_ref, bias_ref, og_ref, o_ref, *, lam_init):
    tq = q_ref.shape[0]
    L = k_ref.shape[0]
    qi = pl.program_id(2)
    lam = lam_ref[0, 0]
    q = q_ref[...]
    k = k_ref[...]
    lane = lax.broadcasted_iota(jnp.int32, q.shape, 1)
    zero = jnp.zeros_like(q)
    q0 = jnp.where(lane < DIFF_HALF, q, zero)
    q1 = jnp.where(lane >= DIFF_HALF, q, zero)
    parts = []
    for j in range(tq // LANES):
        off = pl.multiple_of(L - LANES - (qi * tq + j * LANES), LANES)
        parts.append(bias_ref[:, pl.ds(off, L)])
    bias = jnp.concatenate(parts, axis=0) if len(parts) > 1 else parts[0]

    def soft(qm):
        s = _dot_nt(qm, k) + bias
        e = jnp.exp(s - jnp.max(s, axis=-1, keepdims=True))
        return e, 1.0 / jnp.sum(e, axis=-1, keepdims=True)

    e0, r0 = soft(q0)
    e1, r1 = soft(q1)
    w = e0 * r0 - e1 * (lam * r1)
    o = _dot(w.astype(BF16), v_ref[...])
    ms = jnp.mean(o * o, axis=-1, keepdims=True)
    y = o * lax.rsqrt(ms + EPS) * og_ref[...] * (1.0 - lam_init)
    o_ref[...] = y.astype(o_ref.dtype)


def _attn(pb3, bias_ext, lam, out_g):
    B, L, _ = pb3.shape
    H, dh = DIFF_HEADS, 2 * DIFF_HALF
    tq = ATTN_TQ
    base = (3 * HGRN_HEADS * HGRN_DK) // dh
    lam_init = 0.8 - 0.6 * math.exp(-0.3 * LAYER)
    return pl.pallas_call(
        functools.partial(_attn_kernel, lam_init=lam_init),
        grid=(B, H, L // tq),
        in_specs=[
            pl.BlockSpec(memory_space=pltpu.SMEM),
            pl.BlockSpec((None, tq, dh), lambda b, h, i: (b, i, base + h)),
            pl.BlockSpec((None, L, dh), lambda b, h, i: (b, 0, base + H + h)),
            pl.BlockSpec((None, L, dh), lambda b, h, i: (b, 0, base + 2 * H + h)),
            pl.BlockSpec((None, LANES, 2 * L - LANES), lambda b, h, i: (h, 0, 0)),
            pl.BlockSpec((1, dh), lambda b, h, i: (0, 0)),
        ],
        out_specs=pl.BlockSpec((None, tq, dh), lambda b, h, i: (b, i, h)),
        out_shape=jax.ShapeDtypeStruct((B, L, H * dh), BF16),
        compiler_params=pltpu.CompilerParams(
            dimension_semantics=("parallel", "parallel", "parallel"),
            vmem_limit_bytes=VMEM_LIMIT),
        name="attn",
    )(lam, pb3, pb3, pb3, bias_ext, out_g)


def _cand_layout():
    K = PEER_TOPK
    groups = [("a", 0, 0), ("a", 0, 8), ("a", 1, 0), ("a", 2, 0), ("a", 3, 0),
              ("b", 0, 8), ("b", 0, 0), ("b", 1, 0), ("b", 2, 0)]
    seen = set()
    pos, valid = [], []
    for kind, fixed, start in groups:
        for r in range(SUBLANES):
            a, b = (fixed, start + r) if kind == "a" else (start + r, fixed)
            ok = (a + 1) * (b + 1) <= K and (a, b) not in seen
            if ok:
                seen.add((a, b))
            pos.append(a * K + b if ok else K * K + len(pos))
            valid.append(ok)
    assert len(seen) == sum(K // (a + 1) for a in range(K))
    return groups, np.array(pos, np.int32), np.array(valid, bool)


def _extract_topk(s, key, payload, k, big):
    n, tm = s.shape
    S = SUBLANES
    s3 = s.reshape(n // S, S, tm)
    key3 = key.reshape(n // S, S, tm)
    pay3 = None if payload is key else payload.reshape(n // S, S, tm)
    slot = lax.broadcasted_iota(jnp.int32, (k // S, S, tm), 0) * S + lax.broadcasted_iota(
        jnp.int32, (k // S, S, tm), 1)

    def all_reduce(x3, op):
        r = x3[0]
        for g in range(1, x3.shape[0]):
            r = op(r, x3[g])
        for sh in (S // 2, S // 4, S // 8):
            r = op(r, pltpu.roll(r, sh, axis=0))
        return r

    def body(j, carry):
        s3, vals, pay = carry
        m = all_reduce(s3, jnp.maximum)
        kk = all_reduce(jnp.where(s3 == m[None], key3, big), jnp.minimum)
        sel = key3 == kk[None]
        p = kk if pay3 is None else all_reduce(jnp.where(sel, pay3, 0), jnp.add)
        vals = jnp.where(slot == j, m[None], vals)
        pay = jnp.where(slot == j, p[None], pay)
        s3 = jnp.where(sel, -jnp.inf, s3)
        return s3, vals, pay

    init = (s3, jnp.zeros((k // S, S, tm), F32), jnp.zeros((k // S, S, tm), jnp.int32))
    _, vals, pay = lax.fori_loop(0, k, body, init)
    return vals.reshape(k, tm), pay.reshape(k, tm)


def _route_kernel(x_ref, oh_ref, od_ref, woh_ref, wod_ref, g2_ref, wq_ref, sk_ref, cpos_ref, cmask_ref,
                  h_ref, hn_ref, idx_ref, gate_ref, q_s, tv_s, ti_s, gs_s, is_s, *, groups):
    K = PEER_TOPK
    tm = x_ref.shape[0]
    h = x_ref[...] + _dot(oh_ref[...], woh_ref[...]) + _dot(od_ref[...], wod_ref[...])
    h_ref[...] = h
    hn = h * lax.rsqrt(jnp.mean(h * h, axis=-1, keepdims=True) + EPS) * g2_ref[...]
    half = hn.shape[1] // 2
    hn_ref[...] = _pack_bf16_pair(hn[:, :half], hn[:, half:])
    q_s[...] = _dot(hn.astype(BF16), wq_ref[...]).astype(BF16)

    key_iota = lax.broadcasted_iota(jnp.int32, (PEER_NKEYS, tm), 0)

    def half_topk(hp, _):
        col = pl.multiple_of(hp * PEER_DKEY, PEER_DKEY)
        s = _dot_nt(sk_ref[hp], q_s[:, pl.ds(col, PEER_DKEY)])
        vals, idxs = _extract_topk(s, key_iota, key_iota, K, PEER_NKEYS)
        tv_s[hp] = vals
        ti_s[hp] = idxs
        return 0

    lax.fori_loop(0, 2 * PEER_HEADS, half_topk, 0)

    cpos = cpos_ref[...]
    cmask = cmask_ref[...]

    def head(hd, _):
        s0, s1 = tv_s[2 * hd], tv_s[2 * hd + 1]
        i0, i1 = ti_s[2 * hd] * PEER_NKEYS, ti_s[2 * hd + 1]
        cs, ci = [], []
        for kind, fixed, start in groups:
            if kind == "a":
                cs.append(s0[fixed:fixed + 1] + s1[start:start + SUBLANES])
                ci.append(i0[fixed:fixed + 1] + i1[start:start + SUBLANES])
            else:
                cs.append(s0[start:start + SUBLANES] + s1[fixed:fixed + 1])
                ci.append(i0[start:start + SUBLANES] + i1[fixed:fixed + 1])
        cand = jnp.concatenate(cs, axis=0) + cmask
        cidx = jnp.concatenate(ci, axis=0)
        best, eidx = _extract_topk(cand, cpos, cidx, K, 2 * K * K)
        ex = jnp.exp(best - best[0:1])
        gate = ex / jnp.sum(ex, axis=0, keepdims=True)
        row = pl.ds(pl.multiple_of(hd * K, K), K)
        gs_s[row, :] = gate
        is_s[row, :] = eidx.astype(F32)
        return 0

    lax.fori_loop(0, PEER_HEADS, head, 0)
    gate_ref[...] = gs_s[...].T
    idx_ref[...] = is_s[...].T.astype(jnp.int32)


def _route(x2, oh2, od2, w_out, g2, w_q, sub_keys):
    T, D = x2.shape
    tm = ROUTE_TM
    K = PEER_TOPK
    nh = oh2.shape[1]
    nq = w_q.shape[1]
    npk = PEER_HEADS * K
    groups, pos, valid = _cand_layout()
    ncand = pos.shape[0]
    cpos = jnp.asarray(np.broadcast_to(pos[:, None], (ncand, tm)))
    cmask = jnp.asarray(np.broadcast_to(np.where(valid, 0.0, -np.inf).astype(np.float32)[:, None], (ncand, tm)))
    row = lambda i: (i, 0)
    full2 = lambda i: (0, 0)
    return pl.pallas_call(
        functools.partial(_route_kernel, groups=groups),
        grid=(T // tm,),
        in_specs=[
            pl.BlockSpec((tm, D), row),
            pl.BlockSpec((tm, nh), row),
            pl.BlockSpec((tm, nh), row),
            pl.BlockSpec((nh, D), full2),
            pl.BlockSpec((nh, D), lambda i: (1, 0)),
            pl.BlockSpec((1, D), full2),
            pl.BlockSpec((D, nq), full2),
            pl.BlockSpec(sub_keys.shape, lambda i: (0, 0, 0)),
            pl.BlockSpec((ncand, tm), full2),
            pl.BlockSpec((ncand, tm), full2),
        ],
        out_specs=[
            pl.BlockSpec((tm, D), row),
            pl.BlockSpec((tm, D // 2), row),
            pl.BlockSpec((tm, npk), row),
            pl.BlockSpec((tm, npk), row),
        ],
        out_shape=[
            jax.ShapeDtypeStruct((T, D), F32),
            jax.ShapeDtypeStruct((T, D // 2), jnp.int32),
            jax.ShapeDtypeStruct((T, npk), jnp.int32),
            jax.ShapeDtypeStruct((T, npk), F32),
        ],
        scratch_shapes=[
            pltpu.VMEM((tm, nq), BF16),
            pltpu.VMEM((2 * PEER_HEADS, K, tm), F32),
            pltpu.VMEM((2 * PEER_HEADS, K, tm), jnp.int32),
            pltpu.VMEM((npk, tm), F32),
            pltpu.VMEM((npk, tm), F32),
        ],
        compiler_params=pltpu.CompilerParams(
            dimension_semantics=("parallel",), vmem_limit_bytes=VMEM_LIMIT),
        name="route",
    )(x2, oh2, od2, w_out, w_out, g2, w_q, sub_keys, cpos, cmask)


def _gelu(x):
    return 0.5 * x * (1.0 + lax.erf(x * (1.0 / math.sqrt(2.0))))


def _sc_mesh():
    return plsc.VectorSubcoreMesh(core_axis_name="c", subcore_axis_name="s")


def _sc_worker_id():
    return lax.axis_index("s") * SC_CORES + lax.axis_index("c")


def _sc_widen_pair(s):
    si = plsc.bitcast(s, jnp.int32)
    return plsc.bitcast(si & BF16_HI_MASK, F32), plsc.bitcast(si << 16, F32)


def _sc_gather_loop(tab_hbm, idx_v, bufs, sems, n_items, compute):
    nb = len(bufs)

    def start(item, b):
        pltpu.async_copy(tab_hbm.at[idx_v.at[item]], bufs[b], sems[b])

    def wait(b):
        pltpu.make_async_copy(tab_hbm.at[idx_v.at[0]], bufs[b], sems[b]).wait()

    for p in range(nb - 1):
        start(p, p)

    @pl.loop(0, pl.cdiv(n_items, nb))
    def _(i):
        for b in range(nb):
            it = i * nb + b

            @pl.when(it < n_items)
            def _():
                @pl.when(it + nb - 1 < n_items)
                def _():
                    start(it + nb - 1, (b + nb - 1) % nb)

                wait(b)
                compute(it, bufs[b])


def _peer_hidden(u, idx4, x):
    T, DW = x.shape
    n_rows, R = idx4.shape
    ipt = n_rows // T
    G = SC_TOKEN_GROUP
    nit = G * ipt
    tpw = T // SC_WORKERS
    NL = SC_LANES
    RB = 8
    NV = 4
    GW = NV * NL
    tile = (R, DW // LANES, LANES)

    @functools.partial(
        pl.kernel, mesh=_sc_mesh(), compiler_params=pltpu.CompilerParams(needs_layout_passes=False),
        out_type=jax.ShapeDtypeStruct((n_rows, R), F32),
        scratch_types=[pltpu.VMEM((nit, R), jnp.int32), pltpu.VMEM((G, DW), jnp.int32),
                       [pltpu.VMEM(tile, jnp.int32)] * SC_RING, [pltpu.SemaphoreType.DMA] * SC_RING,
                       pltpu.VMEM((nit, R), F32)],
        name="peer_hidden")
    def k(u_hbm, idx_hbm, x_hbm, h_hbm, idx_v, x_v, bufs, sems, h_v):
        tok0 = _sc_worker_id() * tpw
        lane = lax.broadcasted_iota(jnp.int32, (NL,), 0)
        zero = jnp.zeros((NL,), F32)

        def compute(item, buf):
            tok = item // ipt
            hvs = [zero for _ in range(R // NL)]
            for rb in range(R // RB):
                def body(g, accs):
                    sub = g // (LANES // GW)
                    base = (g % (LANES // GW)) * GW
                    xs = [plsc.bitcast(x_v[tok, pl.ds(pl.multiple_of(g * GW + jj * NL, NL), NL)], BF16)
                          for jj in range(NV)]
                    out = []
                    for j in range(RB):
                        ps = [plsc.bitcast(buf[rb * RB + j, sub, pl.ds(pl.multiple_of(base + jj * NL, NL), NL)],
                                           BF16) * xs[jj] for jj in range(NV)]
                        hi, lo = _sc_widen_pair((ps[0] + ps[1]) + (ps[2] + ps[3]))
                        out.append((accs[j] + hi) + lo)
                    return tuple(out)
                accs = lax.fori_loop(0, DW // GW, body, tuple(zero for _ in range(RB)))
                for j in range(RB):
                    r = rb * RB + j
                    hvs[r // NL] = jnp.where(lane == (r % NL), jnp.sum(accs[j]), hvs[r // NL])
            for q in range(R // NL):
                h_v[item, pl.ds(q * NL, NL)] = hvs[q]

        @pl.loop(0, tpw // G)
        def _(g):
            t0 = tok0 + g * G
            pltpu.sync_copy(idx_hbm.at[pl.ds(t0 * ipt, nit)], idx_v)
            pltpu.sync_copy(x_hbm.at[pl.ds(t0, G)], x_v)
            _sc_gather_loop(u_hbm, idx_v, bufs, sems, nit, compute)
            pltpu.sync_copy(h_v, h_hbm.at[pl.ds(t0 * ipt, nit)])

    return k(u, idx4, x)


def _peer_combine(v, idx4, w4, T, D):
    n_rows, R = idx4.shape
    ipt = n_rows // T
    G = SC_COMBINE_GROUP
    nit = G * ipt
    tpw = T // SC_WORKERS
    NL = SC_LANES
    DW = D // 2
    half = DW // 2
    nv = half // NL
    RG = 4
    tile = (R, DW // LANES, LANES)

    @functools.partial(
        pl.kernel, mesh=_sc_mesh(), compiler_params=pltpu.CompilerParams(needs_layout_passes=False),
        out_type=jax.ShapeDtypeStruct((T, D), F32),
        scratch_types=[pltpu.VMEM((nit, R), jnp.int32), pltpu.VMEM((nit, R), jnp.int32),
                       [pltpu.VMEM(tile, jnp.int32)] * SC_RING, [pltpu.SemaphoreType.DMA] * SC_RING,
                       pltpu.VMEM((G, D), F32)],
        name="peer_combine")
    def k(v_hbm, idx_hbm, w_hbm, o_hbm, idx_v, w_v, bufs, sems, out_v):
        tok0 = _sc_worker_id() * tpw
        zero = jnp.zeros((NL,), F32)

        def compute(item, buf):
            tok = item // ipt
            item_vec = jnp.full((NL,), item, jnp.int32)
            for hf in range(2):
                def body(rg, accs):
                    ws = [plsc.bitcast(plsc.load_gather(
                        w_v, [item_vec, jnp.full((NL,), rg * RG + rr, jnp.int32)]), BF16) for rr in range(RG)]
                    his, los = [], []
                    for i in range(nv):
                        word = hf * half + i * NL
                        ps = [plsc.bitcast(buf[rg * RG + rr, word // LANES, pl.ds(word % LANES, NL)], BF16) * ws[rr]
                              for rr in range(RG)]
                        hi, lo = _sc_widen_pair((ps[0] + ps[1]) + (ps[2] + ps[3]))
                        his.append(accs[i] + hi)
                        los.append(accs[nv + i] + lo)
                    return tuple(his + los)
                accs = lax.fori_loop(0, R // RG, body, tuple(zero for _ in range(2 * nv)))
                for i in range(nv):
                    word = hf * half + i * NL
                    plsc.addupdate(out_v.at[tok, pl.ds(word, NL)], accs[i])
                    plsc.addupdate(out_v.at[tok, pl.ds(DW + word, NL)], accs[nv + i])

        @pl.loop(0, tpw // G)
        def _(g):
            t0 = tok0 + g * G
            pltpu.sync_copy(idx_hbm.at[pl.ds(t0 * ipt, nit)], idx_v)
            pltpu.sync_copy(w_hbm.at[pl.ds(t0 * ipt, nit)], w_v)

            @pl.loop(0, G)
            def _(t):
                @pl.loop(0, D // NL)
                def _(i):
                    out_v[t, pl.ds(pl.multiple_of(i * NL, NL), NL)] = zero

            _sc_gather_loop(v_hbm, idx_v, bufs, sems, nit, compute)
            pltpu.sync_copy(out_v, o_hbm.at[pl.ds(t0, G)])

    return k(v, idx4, w4)


def _act_kernel(h_ref, g_ref, w_ref):
    w = _gelu(h_ref[...]) * g_ref[...]
    w_ref[...] = _pack_bf16_pair(w, w)


def _peer_act(hraw, gate):
    T, n = gate.shape
    tm = PEER_ACT_TM
    spec = pl.BlockSpec((tm, n), lambda i: (i, 0))
    return pl.pallas_call(
        _act_kernel, grid=(T // tm,), in_specs=[spec, spec], out_specs=spec,
        out_shape=jax.ShapeDtypeStruct((T, n), jnp.int32),
        compiler_params=pltpu.CompilerParams(dimension_semantics=("parallel",)),
        name="peer_act",
    )(hraw, gate)


def _final_kernel(h_ref, p_ref, g_ref, y_ref):
    y = h_ref[...] + p_ref[...]
    ms = jnp.mean(y * y, axis=-1, keepdims=True)
    y_ref[...] = y * lax.rsqrt(ms + EPS) * g_ref[...]


def _final(h, po, g):
    T, D = h.shape
    tm = FINAL_TM
    spec = pl.BlockSpec((tm, D), lambda i: (i, 0))
    return pl.pallas_call(
        _final_kernel, grid=(T // tm,),
        in_specs=[spec, spec, pl.BlockSpec((1, D), lambda i: (0, 0))], out_specs=spec,
        out_shape=jax.ShapeDtypeStruct((T, D), F32),
        compiler_params=pltpu.CompilerParams(dimension_semantics=("parallel",)),
        name="final_norm",
    )(h, po, g)


def kernel(x, norm1_g, w_in, hgrn_lb_fwd, hgrn_lb_bwd, hgrn_out_g, diff_lam_q1, diff_lam_k1,
           diff_lam_q2, diff_lam_k2, diff_out_g, rel_bias, w_out, norm2_g, peer_w_q,
           peer_sub_keys, peer_u, peer_v, final_g):
    B, L, D = x.shape

    w_r = w_in[LAYER].astype(BF16)
    f32 = jnp.float32
    lam_init = 0.8 - 0.6 * math.exp(-0.3 * LAYER)
    lam = (jnp.exp(jnp.sum(diff_lam_q1[LAYER].astype(f32) * diff_lam_k1[LAYER].astype(f32)))
           - jnp.exp(jnp.sum(diff_lam_q2[LAYER].astype(f32) * diff_lam_k2[LAYER].astype(f32))) + lam_init)
    lam = lam.reshape(1, 1)
    bias_ext = _rel_bias_ext(rel_bias, L)
    sk = peer_sub_keys[LAYER].reshape(2 * PEER_HEADS, PEER_NKEYS, PEER_DKEY).astype(BF16)
    wq = peer_w_q[LAYER].reshape(D, -1).astype(BF16)
    wo = w_out[LAYER].astype(BF16)
    def pack_table(tab):
        words = _pack_bf16_pair(tab[:, :D // 2], tab[:, D // 2:])
        return words.reshape(tab.shape[0], D // 2 // LANES, LANES)
    u3, v3 = pack_table(peer_u[LAYER]), pack_table(peer_v[LAYER])

    bc = B // BATCH_CHUNKS
    tc = bc * L
    rows = tc * PEER_HEADS * PEER_TOPK // SC_GATHER_ROWS
    outs = []
    pending = {}
    combined = {}
    for c in range(BATCH_CHUNKS + CHUNK_LAG + FINAL_LAG):
        x2 = x[c * bc:(c + 1) * bc].reshape(tc, D) if c < BATCH_CHUNKS else None
        if c - CHUNK_LAG - FINAL_LAG in combined:
            h_prev, po = combined.pop(c - CHUNK_LAG - FINAL_LAG)
            if x2 is not None:
                x2, po = lax.optimization_barrier((x2, po))
            outs.append(_final(h_prev, po, final_g[None, :]).reshape(bc, L, D))
        if CHUNK_LAG <= c < BATCH_CHUNKS + CHUNK_LAG:
            h, idx4, gate, hraw = pending.pop(c - CHUNK_LAG)
            wts = _peer_act(hraw.reshape(tc, -1), gate)
            if x2 is not None:
                x2, wts = lax.optimization_barrier((x2, wts))
            combined[c - CHUNK_LAG] = (h, _peer_combine(v3, idx4, wts.reshape(rows, SC_GATHER_ROWS), tc, D))
        if x2 is not None:
            zf, pb = _inproj(x2, norm1_g[LAYER][None, :], w_r)
            zf3 = zf.reshape(bc, L, -1)
            pb3 = pb.reshape(bc, L, -1)
            o_h = _hgrn(zf3, pb3, hgrn_lb_fwd, hgrn_lb_bwd, hgrn_out_g[LAYER][None, :])
            o_d = _attn(pb3, bias_ext, lam, diff_out_g[LAYER][None, :])
            h, hn, idx, gate = _route(x2, o_h.reshape(tc, -1), o_d.reshape(tc, -1), wo,
                                      norm2_g[LAYER][None, :], wq, sk)
            idx4 = idx.reshape(rows, SC_GATHER_ROWS)
            pending[c] = (h, idx4, gate, _peer_hidden(u3, idx4, hn))
    return jnp.concatenate(outs, axis=0)
```

```python
import functools
import math

import numpy as np
import jax
import jax.numpy as jnp
from jax import lax
from jax.experimental import pallas as pl
from jax.experimental.pallas import tpu as pltpu
from jax.experimental.pallas import tpu_sc as plsc

F32 = jnp.float32
BF16 = jnp.bfloat16
EPS = 1e-6

HGRN_HEADS = 4
HGRN_DK = 128
DIFF_HEADS = 4
DIFF_HALF = 64
REL_BUCKETS = 32
REL_MAX_DIST = 128
PEER_HEADS = 8
PEER_NKEYS = 128
PEER_DKEY = 128
PEER_TOPK = 16
LAYER = 0

LANES = 128
SUBLANES = 8
VMEM_LIMIT = 48 * 1024 * 1024

INPROJ_TM = 512
HG_C = 64
HG_LEVELS = (64, 32, 16, 8, 4, 2)
ATTN_TQ = 256
ROUTE_TM = 256
BATCH_CHUNKS = 16
CHUNK_LAG = 2
FINAL_LAG = 1
PEER_ACT_TM = 2048
FINAL_TM = 512

SC_CORES = 2
SC_SUBCORES = 16
SC_LANES = 16
SC_WORKERS = SC_CORES * SC_SUBCORES
SC_GATHER_ROWS = 64
SC_TOKEN_GROUP = 32
SC_COMBINE_GROUP = 32
SC_RING = 2


def _dot(a, b):
    return jnp.dot(a, b, preferred_element_type=F32)


def _dot_nt(a, b):
    return lax.dot_general(a, b, (((1,), (1,)), ((), ())), preferred_element_type=F32)


def _dot_tn(a, b):
    return lax.dot_general(a, b, (((0,), (0,)), ((), ())), preferred_element_type=F32)


def _silu(x):
    return x * (1.0 / (1.0 + jnp.exp(-x)))


BF16_HI_MASK = -65536


def _pack_bf16_pair(hi, lo):
    bits = lambda a: lax.bitcast_convert_type(a.astype(BF16).astype(F32), jnp.int32)
    return (bits(hi) & BF16_HI_MASK) | lax.shift_right_logical(bits(lo), 16)


def _inproj_kernel(x_ref, g_ref, w_ref, zf_ref, pb_ref):
    x = x_ref[...]
    ms = jnp.mean(x * x, axis=-1, keepdims=True)
    xn = (x * lax.rsqrt(ms + EPS) * g_ref[...]).astype(BF16)
    hw = zf_ref.shape[1] // 2
    zf_ref[...] = _dot(xn, w_ref[:, hw:3 * hw])
    pb_ref[:, 0:hw] = _dot(xn, w_ref[:, 0:hw]).astype(BF16)
    for j in range(3, 8):
        y = _dot(xn, w_ref[:, j * hw:(j + 1) * hw])
        if j == 5:
            y = y * (DIFF_HALF ** -0.5)
        pb_ref[:, (j - 2) * hw:(j - 1) * hw] = y.astype(BF16)


def _inproj(x2, g, w, tok0, T):
    D = x2.shape[1]
    N = w.shape[1]
    nz = 2 * HGRN_HEADS * HGRN_DK
    tm = INPROJ_TM
    off = tok0 // tm
    return pl.pallas_call(
        _inproj_kernel,
        grid=(T // tm,),
        in_specs=[
            pl.BlockSpec((tm, D), lambda i: (i + off, 0)),
            pl.BlockSpec((1, D), lambda i: (0, 0)),
            pl.BlockSpec((D, N), lambda i: (0, 0)),
        ],
        out_specs=[
            pl.BlockSpec((tm, nz), lambda i: (i, 0)),
            pl.BlockSpec((tm, N - nz), lambda i: (i, 0)),
        ],
        out_shape=[
            jax.ShapeDtypeStruct((T, nz), F32),
            jax.ShapeDtypeStruct((T, N - nz), BF16),
        ],
        compiler_params=pltpu.CompilerParams(
            dimension_semantics=("parallel",), vmem_limit_bytes=VMEM_LIMIT),
        name="inproj",
    )(x2, g, w)


def _hgrn_consts():
    C = HG_C
    r = np.arange(C)
    t = r[:, None]
    u = r[None, :]
    blocks = [u <= t, u > t]
    masks = [np.eye(C, dtype=bool)]
    for B in HG_LEVELS:
        half = B // 2
        a = (r // B) * B
        m = (a + half - 1)[:, None]
        upper = (r - a) >= half
        blocks.append(np.where(upper[:, None], (u > m) & (u <= t), (u > t) & (u <= m)))
        same = a[:, None] == a[None, :]
        masks.append(same & upper[:, None] & (~upper)[None, :])
    m_f = np.concatenate(blocks, 0).astype(np.float32)
    m_b = np.concatenate([b[::-1, ::-1] for b in blocks], 0).astype(np.float32)
    k_f = np.stack(masks).astype(np.float32)
    k_b = np.ascontiguousarray(k_f.transpose(0, 2, 1))
    return m_f, m_b, k_f, k_b


def _hgrn_gates(z, tab):
    tabf = tab.astype(F32)
    e = jnp.exp(tabf - jnp.max(tabf, axis=0, keepdims=True))
    lb = jnp.sum(e[0:LAYER + 1], axis=0, keepdims=True) / jnp.sum(e, axis=0, keepdims=True)
    log_lb = jnp.log(lb)
    log_1m = jnp.log1p(-lb)
    ez = jnp.exp(-jnp.abs(z))
    l1p = jnp.log1p(ez)
    log_sig = jnp.minimum(z, 0.0) - l1p
    c = log_1m + log_sig
    hi = jnp.maximum(log_lb, c)
    lo = jnp.minimum(log_lb, c)
    log_f = hi + jnp.log1p(jnp.exp(lo - hi))
    sig_neg = jnp.where(z >= 0.0, ez, 1.0) / (1.0 + ez)
    k = (1.0 - lb) * sig_neg
    return log_f, k


def _hgrn_kernel(lbf_ref, lbb_ref, og_ref, mf_ref, mb_ref, kf_ref, kb_ref,
                 q_ref, zf_ref, zb_ref, v_ref, g_ref, o_ref, of_s, ob_s):
    C = HG_C
    L = q_ref.shape[0]
    n = L // C
    dv = v_ref.shape[1]

    def chunk(c, st, m_ref, k_ref, z_ref, tab_ref, forward):
        sl = pl.ds(pl.multiple_of(c * C, C), C)
        qh = q_ref[sl, :].astype(F32)
        q = _silu(qh)
        v = v_ref[sl, :]
        log_f, k = _hgrn_gates(z_ref[sl, :], tab_ref[...])
        lf_hi = log_f.astype(BF16)
        lf_lo = (log_f - lf_hi.astype(F32)).astype(BF16)
        m = m_ref[...]
        e = jnp.exp(_dot(m, lf_hi) + _dot(m, lf_lo))
        e_b = e[0:C]
        e_s = e[C:2 * C]
        dec = e[C - 1:C] if forward else e[0:1]
        o = _dot_nt((q * e_b).astype(BF16), st.astype(BF16))
        a = _dot_nt(q.astype(BF16), k.astype(BF16)) * k_ref[0]
        for l in range(len(HG_LEVELS)):
            e_l = e[(2 + l) * C:(3 + l) * C]
            a = a + _dot_nt((q * e_l).astype(BF16), (k * e_l).astype(BF16)) * k_ref[l + 1]
        o = o + _dot(a.astype(BF16), v)
        st = st * dec + _dot_tn(v, (k * e_s).astype(BF16))
        return sl, o, st

    st0 = jnp.zeros((dv, q_ref.shape[1]), F32)

    def both(i, carry):
        st_f, st_b = carry
        sl_f, o_f, st_f = chunk(i, st_f, mf_ref, kf_ref, zf_ref, lbf_ref, True)
        sl_b, o_b, st_b = chunk(n - 1 - i, st_b, mb_ref, kb_ref, zb_ref, lbb_ref, False)
        of_s[sl_f, :] = o_f
        ob_s[sl_b, :] = o_b
        return st_f, st_b

    lax.fori_loop(0, n, both, (st0, st0))

    def finish(c, _):
        sl = pl.ds(pl.multiple_of(c * C, C), C)
        tot = of_s[sl, :] + ob_s[sl, :]
        ms = jnp.mean(tot * tot, axis=-1, keepdims=True)
        y = tot * lax.rsqrt(ms + EPS) * og_ref[...]
        o_ref[sl, :] = (y * _silu(g_ref[sl, :].astype(F32))).astype(o_ref.dtype)
        return 0

    lax.fori_loop(0, n, finish, 0)


def _hgrn(zf3, pb3, lb_f, lb_b, out_g):
    B, L, _ = zf3.shape
    H, dk = HGRN_HEADS, HGRN_DK
    m_f, m_b, k_f, k_b = _hgrn_consts()
    nlev = k_f.shape[0]
    full2 = lambda b, h: (0, 0)
    full3 = lambda b, h: (0, 0, 0)
    seq = lambda off: pl.BlockSpec((None, L, dk), lambda b, h: (b, 0, off + h))
    return pl.pallas_call(
        _hgrn_kernel,
        grid=(B, H),
        in_specs=[
            pl.BlockSpec((lb_f.shape[0], dk), lambda b, h: (0, h)),
            pl.BlockSpec((lb_b.shape[0], dk), lambda b, h: (0, h)),
            pl.BlockSpec((1, dk), full2),
            pl.BlockSpec(m_f.shape, full2),
            pl.BlockSpec(m_b.shape, full2),
            pl.BlockSpec((nlev, HG_C, HG_C), full3),
            pl.BlockSpec((nlev, HG_C, HG_C), full3),
            seq(0),
            seq(0),
            seq(H),
            seq(H),
            seq(2 * H),
        ],
        out_specs=pl.BlockSpec((None, L, dk), lambda b, h: (b, 0, h)),
        out_shape=jax.ShapeDtypeStruct((B, L, H * dk), BF16),
        scratch_shapes=[pltpu.VMEM((L, dk), F32), pltpu.VMEM((L, dk), F32)],
        compiler_params=pltpu.CompilerParams(
            dimension_semantics=("parallel", "parallel"), vmem_limit_bytes=VMEM_LIMIT),
        name="hgrn",
    )(lb_f, lb_b, out_g, jnp.asarray(m_f, BF16), jnp.asarray(m_b, BF16),
      jnp.asarray(k_f), jnp.asarray(k_b), pb3, zf3, zf3, pb3, pb3)


def _t5_bucket(rel):
    nb = REL_BUCKETS // 2
    ret = jnp.where(rel > 0, nb, 0)
    n = jnp.abs(rel)
    max_exact = nb // 2
    nf = jnp.maximum(n, 1).astype(jnp.float32)
    large = max_exact + (jnp.log(nf / max_exact) / math.log(REL_MAX_DIST / max_exact)
                         * (nb - max_exact)).astype(jnp.int32)
    large = jnp.minimum(large, nb - 1)
    return ret + jnp.where(n < max_exact, n, large)


def _rel_bias_ext(rel_bias, L):
    j = jnp.arange(2 * L - LANES, dtype=jnp.int32)
    ql = jnp.arange(LANES, dtype=jnp.int32)
    rel = j[None, :] - (L - LANES) - ql[:, None]
    bucket = _t5_bucket(rel)
    tab = rel_bias.astype(F32)
    out = jnp.zeros((tab.shape[1],) + bucket.shape, F32)
    for b in range(REL_BUCKETS):
        out = jnp.where((bucket == b)[None], tab[b][:, None, None], out)
    return out


def _attn_kernel(lam_ref, q_ref, k_ref, v_ref, bias_ref, og_ref, o_ref, *, lam_init):
    tq = q_ref.shape[0]
    L = k_ref.shape[0]
    qi = pl.program_id(2)
    lam = lam_ref[0, 0]
    q = q_ref[...]
    k = k_ref[...]
    lane = lax.broadcasted_iota(jnp.int32, q.shape, 1)
    zero = jnp.zeros_like(q)
    q0 = jnp.where(lane < DIFF_HALF, q, zero)
    q1 = jnp.where(lane >= DIFF_HALF, q, zero)
    parts = []
    for j in range(tq // LANES):
        off = pl.multiple_of(L - LANES - (qi * tq + j * LANES), LANES)
        parts.append(bias_ref[:, pl.ds(off, L)])
    bias = jnp.concatenate(parts, axis=0) if len(parts) > 1 else parts[0]

    def soft(qm):
        s = _dot_nt(qm, k) + bias
        e = jnp.exp(s - jnp.max(s, axis=-1, keepdims=True))
        return e, 1.0 / jnp.sum(e, axis=-1, keepdims=True)

    e0, r0 = soft(q0)
    e1, r1 = soft(q1)
    w = e0 * r0 - e1 * (lam * r1)
    o = _dot(w.astype(BF16), v_ref[...])
    ms = jnp.mean(o * o, axis=-1, keepdims=True)
    y = o * lax.rsqrt(ms + EPS) * og_ref[...] * (1.0 - lam_init)
    o_ref[...] = y.astype(o_ref.dtype)


def _attn(pb3, bias_ext, lam, out_g):
    B, L, _ = pb3.shape
    H, dh = DIFF_HEADS, 2 * DIFF_HALF
    tq = ATTN_TQ
    base = (3 * HGRN_HEADS * HGRN_DK) // dh
    lam_init = 0.8 - 0.6 * math.exp(-0.3 * LAYER)
    return pl.pallas_call(
        functools.partial(_attn_kernel, lam_init=lam_init),
        grid=(B, H, L // tq),
        in_specs=[
            pl.BlockSpec(memory_space=pltpu.SMEM),
            pl.BlockSpec((None, tq, dh), lambda b, h, i: (b, i, base + h)),
            pl.BlockSpec((None, L, dh), lambda b, h, i: (b, 0, base + H + h)),
            pl.BlockSpec((None, L, dh), lambda b, h, i: (b, 0, base + 2 * H + h)),
            pl.BlockSpec((None, LANES, 2 * L - LANES), lambda b, h, i: (h, 0, 0)),
            pl.BlockSpec((1, dh), lambda b, h, i: (0, 0)),
        ],
        out_specs=pl.BlockSpec((None, tq, dh), lambda b, h, i: (b, i, h)),
        out_shape=jax.ShapeDtypeStruct((B, L, H * dh), BF16),
        compiler_params=pltpu.CompilerParams(
            dimension_semantics=("parallel", "parallel", "parallel"),
            vmem_limit_bytes=VMEM_LIMIT),
        name="attn",
    )(lam, pb3, pb3, pb3, bias_ext, out_g)


def _cand_layout():
    K = PEER_TOPK
    groups = [("a", 0, 0), ("a", 0, 8), ("a", 1, 0), ("a", 2, 0), ("a", 3, 0),
              ("b", 0, 8), ("b", 0, 0), ("b", 1, 0), ("b", 2, 0)]
    seen = set()
    pos, valid = [], []
    for kind, fixed, start in groups:
        for r in range(SUBLANES):
            a, b = (fixed, start + r) if kind == "a" else (start + r, fixed)
            ok = (a + 1) * (b + 1) <= K and (a, b) not in seen
            if ok:
                seen.add((a, b))
            pos.append(a * K + b if ok else K * K + len(pos))
            valid.append(ok)
    assert len(seen) == sum(K // (a + 1) for a in range(K))
    return groups, np.array(pos, np.int32), np.array(valid, bool)


def _extract_topk(s, key, payload, k, big):
    n, tm = s.shape
    S = SUBLANES
    s3 = s.reshape(n // S, S, tm)
    key3 = key.reshape(n // S, S, tm)
    pay3 = None if payload is key else payload.reshape(n // S, S, tm)
    slot = lax.broadcasted_iota(jnp.int32, (k // S, S, tm), 0) * S + lax.broadcasted_iota(
        jnp.int32, (k // S, S, tm), 1)

    def all_reduce(x3, op):
        r = x3[0]
        for g in range(1, x3.shape[0]):
            r = op(r, x3[g])
        for sh in (S // 2, S // 4, S // 8):
            r = op(r, pltpu.roll(r, sh, axis=0))
        return r

    def body(j, carry):
        s3, vals, pay = carry
        m = all_reduce(s3, jnp.maximum)
        kk = all_reduce(jnp.where(s3 == m[None], key3, big), jnp.minimum)
        sel = key3 == kk[None]
        p = kk if pay3 is None else all_reduce(jnp.where(sel, pay3, 0), jnp.add)
        vals = jnp.where(slot == j, m[None], vals)
        pay = jnp.where(slot == j, p[None], pay)
        s3 = jnp.where(sel, -jnp.inf, s3)
        return s3, vals, pay

    init = (s3, jnp.zeros((k // S, S, tm), F32), jnp.zeros((k // S, S, tm), jnp.int32))
    _, vals, pay = lax.fori_loop(0, k, body, init)
    return vals.reshape(k, tm), pay.reshape(k, tm)


def _route_kernel(x_ref, oh_ref, od_ref, woh_ref, wod_ref, g2_ref, wq_ref, sk_ref, cpos_ref, cmask_ref,
                  h_ref, hn_ref, idx_ref, gate_ref, q_s, tv_s, ti_s, gs_s, is_s, *, groups):
    K = PEER_TOPK
    tm = x_ref.shape[0]
    h = x_ref[...] + _dot(oh_ref[...], woh_ref[...]) + _dot(od_ref[...], wod_ref[...])
    h_ref[...] = h
    hn = h * lax.rsqrt(jnp.mean(h * h, axis=-1, keepdims=True) + EPS) * g2_ref[...]
    half = hn.shape[1] // 2
    hn_ref[...] = _pack_bf16_pair(hn[:, :half], hn[:, half:])
    q_s[...] = _dot(hn.astype(BF16), wq_ref[...]).astype(BF16)

    key_iota = lax.broadcasted_iota(jnp.int32, (PEER_NKEYS, tm), 0)

    def half_topk(hp, _):
        col = pl.multiple_of(hp * PEER_DKEY, PEER_DKEY)
        s = _dot_nt(sk_ref[hp], q_s[:, pl.ds(col, PEER_DKEY)])
        vals, idxs = _extract_topk(s, key_iota, key_iota, K, PEER_NKEYS)
        tv_s[hp] = vals
        ti_s[hp] = idxs
        return 0

    lax.fori_loop(0, 2 * PEER_HEADS, half_topk, 0)

    cpos = cpos_ref[...]
    cmask = cmask_ref[...]

    def head(hd, _):
        s0, s1 = tv_s[2 * hd], tv_s[2 * hd + 1]
        i0, i1 = ti_s[2 * hd] * PEER_NKEYS, ti_s[2 * hd + 1]
        cs, ci = [], []
        for kind, fixed, start in groups:
            if kind == "a":
                cs.append(s0[fixed:fixed + 1] + s1[start:start + SUBLANES])
                ci.append(i0[fixed:fixed + 1] + i1[start:start + SUBLANES])
            else:
                cs.append(s0[start:start + SUBLANES] + s1[fixed:fixed + 1])
                ci.append(i0[start:start + SUBLANES] + i1[fixed:fixed + 1])
        cand = jnp.concatenate(cs, axis=0) + cmask
        cidx = jnp.concatenate(ci, axis=0)
        best, eidx = _extract_topk(cand, cpos, cidx, K, 2 * K * K)
        ex = jnp.exp(best - best[0:1])
        gate = ex / jnp.sum(ex, axis=0, keepdims=True)
        row = pl.ds(pl.multiple_of(hd * K, K), K)
        gs_s[row, :] = gate
        is_s[row, :] = eidx.astype(F32)
        return 0

    lax.fori_loop(0, PEER_HEADS, head, 0)
    gate_ref[...] = gs_s[...].T
    idx_ref[...] = is_s[...].T.astype(jnp.int32)


def _route(x2, tok0, oh2, od2, w_out, g2, w_q, sub_keys):
    T = oh2.shape[0]
    D = x2.shape[1]
    tm = ROUTE_TM
    off = tok0 // tm
    K = PEER_TOPK
    nh = oh2.shape[1]
    nq = w_q.shape[1]
    npk = PEER_HEADS * K
    groups, pos, valid = _cand_layout()
    ncand = pos.shape[0]
    cpos = jnp.asarray(np.broadcast_to(pos[:, None], (ncand, tm)))
    cmask = jnp.asarray(np.broadcast_to(np.where(valid, 0.0, -np.inf).astype(np.float32)[:, None], (ncand, tm)))
    row = lambda i: (i, 0)
    full2 = lambda i: (0, 0)
    return pl.pallas_call(
        functools.partial(_route_kernel, groups=groups),
        grid=(T // tm,),
        in_specs=[
            pl.BlockSpec((tm, D), lambda i: (i + off, 0)),
            pl.BlockSpec((tm, nh), row),
            pl.BlockSpec((tm, nh), row),
            pl.BlockSpec((nh, D), full2),
            pl.BlockSpec((nh, D), lambda i: (1, 0)),
            pl.BlockSpec((1, D), full2),
            pl.BlockSpec((D, nq), full2),
            pl.BlockSpec(sub_keys.shape, lambda i: (0, 0, 0)),
            pl.BlockSpec((ncand, tm), full2),
            pl.BlockSpec((ncand, tm), full2),
        ],
        out_specs=[
            pl.BlockSpec((tm, D), row),
            pl.BlockSpec((tm, D // 2), row),
            pl.BlockSpec((tm, npk), row),
            pl.BlockSpec((tm, npk), row),
        ],
        out_shape=[
            jax.ShapeDtypeStruct((T, D), F32),
            jax.ShapeDtypeStruct((T, D // 2), jnp.int32),
            jax.ShapeDtypeStruct((T, npk), jnp.int32),
            jax.ShapeDtypeStruct((T, npk), F32),
        ],
        scratch_shapes=[
            pltpu.VMEM((tm, nq), BF16),
            pltpu.VMEM((2 * PEER_HEADS, K, tm), F32),
            pltpu.VMEM((2 * PEER_HEADS, K, tm), jnp.int32),
            pltpu.VMEM((npk, tm), F32),
            pltpu.VMEM((npk, tm), F32),
        ],
        compiler_params=pltpu.CompilerParams(
            dimension_semantics=("parallel",), vmem_limit_bytes=VMEM_LIMIT),
        name="route",
    )(x2, oh2, od2, w_out, w_out, g2, w_q, sub_keys, cpos, cmask)


def _gelu(x):
    return 0.5 * x * (1.0 + lax.erf(x * (1.0 / math.sqrt(2.0))))


def _sc_mesh():
    return plsc.VectorSubcoreMesh(core_axis_name="c", subcore_axis_name="s")


def _sc_worker_id():
    return lax.axis_index("s") * SC_CORES + lax.axis_index("c")


def _sc_widen_pair(s):
    si = plsc.bitcast(s, jnp.int32)
    return plsc.bitcast(si & BF16_HI_MASK, F32), plsc.bitcast(si << 16, F32)


def _sc_gather_loop(tab_hbm, idx_v, bufs, sems, n_items, compute):
    nb = len(bufs)

    def start(item, b):
        pltpu.async_copy(tab_hbm.at[idx_v.at[item]], bufs[b], sems[b])

    def wait(b):
        pltpu.make_async_copy(tab_hbm.at[idx_v.at[0]], bufs[b], sems[b]).wait()

    for p in range(nb - 1):
        start(p, p)

    @pl.loop(0, pl.cdiv(n_items, nb))
    def _(i):
        for b in range(nb):
            it = i * nb + b

            @pl.when(it < n_items)
            def _():
                @pl.when(it + nb - 1 < n_items)
                def _():
                    start(it + nb - 1, (b + nb - 1) % nb)

                wait(b)
                compute(it, bufs[b])


def _peer_hidden(u, idx4, x):
    T, DW = x.shape
    n_rows, R = idx4.shape
    ipt = n_rows // T
    G = SC_TOKEN_GROUP
    nit = G * ipt
    tpw = T // SC_WORKERS
    NL = SC_LANES
    RB = 8
    NV = 4
    GW = NV * NL
    tile = (R, DW // LANES, LANES)

    @functools.partial(
        pl.kernel, mesh=_sc_mesh(), compiler_params=pltpu.CompilerParams(needs_layout_passes=False),
        out_type=jax.ShapeDtypeStruct((n_rows, R), F32),
        scratch_types=[pltpu.VMEM((nit, R), jnp.int32), pltpu.VMEM((G, DW), jnp.int32),
                       [pltpu.VMEM(tile, jnp.int32)] * SC_RING, [pltpu.SemaphoreType.DMA] * SC_RING,
                       pltpu.VMEM((nit, R), F32)],
        name="peer_hidden")
    def k(u_hbm, idx_hbm, x_hbm, h_hbm, idx_v, x_v, bufs, sems, h_v):
        tok0 = _sc_worker_id() * tpw
        lane = lax.broadcasted_iota(jnp.int32, (NL,), 0)
        zero = jnp.zeros((NL,), F32)

        def compute(item, buf):
            tok = item // ipt
            hvs = [zero for _ in range(R // NL)]
            for rb in range(R // RB):
                def body(g, accs):
                    sub = g // (LANES // GW)
                    base = (g % (LANES // GW)) * GW
                    xs = [plsc.bitcast(x_v[tok, pl.ds(pl.multiple_of(g * GW + jj * NL, NL), NL)], BF16)
                          for jj in range(NV)]
                    out = []
                    for j in range(RB):
                        ps = [plsc.bitcast(buf[rb * RB + j, sub, pl.ds(pl.multiple_of(base + jj * NL, NL), NL)],
                                           BF16) * xs[jj] for jj in range(NV)]
                        hi, lo = _sc_widen_pair((ps[0] + ps[1]) + (ps[2] + ps[3]))
                        out.append((accs[j] + hi) + lo)
                    return tuple(out)
                accs = lax.fori_loop(0, DW // GW, body, tuple(zero for _ in range(RB)))
                for j in range(RB):
                    r = rb * RB + j
                    hvs[r // NL] = jnp.where(lane == (r % NL), jnp.sum(accs[j]), hvs[r // NL])
            for q in range(R // NL):
                h_v[item, pl.ds(q * NL, NL)] = hvs[q]

        @pl.loop(0, tpw // G)
        def _(g):
            t0 = tok0 + g * G
            pltpu.sync_copy(idx_hbm.at[pl.ds(t0 * ipt, nit)], idx_v)
            pltpu.sync_copy(x_hbm.at[pl.ds(t0, G)], x_v)
            _sc_gather_loop(u_hbm, idx_v, bufs, sems, nit, compute)
            pltpu.sync_copy(h_v, h_hbm.at[pl.ds(t0 * ipt, nit)])

    return k(u, idx4, x)


def _peer_combine(v, idx4, w4, T, D):
    n_rows, R = idx4.shape
    ipt = n_rows // T
    G = SC_COMBINE_GROUP
    nit = G * ipt
    tpw = T // SC_WORKERS
    NL = SC_LANES
    DW = D // 2
    half = DW // 2
    nv = half // NL
    RG = 4
    tile = (R, DW // LANES, LANES)

    @functools.partial(
        pl.kernel, mesh=_sc_mesh(), compiler_params=pltpu.CompilerParams(needs_layout_passes=False),
        out_type=jax.ShapeDtypeStruct((T, D), F32),
        scratch_types=[pltpu.VMEM((nit, R), jnp.int32), pltpu.VMEM((nit, R), jnp.int32),
                       [pltpu.VMEM(tile, jnp.int32)] * SC_RING, [pltpu.SemaphoreType.DMA] * SC_RING,
                       pltpu.VMEM((G, D), F32)],
        name="peer_combine")
    def k(v_hbm, idx_hbm, w_hbm, o_hbm, idx_v, w_v, bufs, sems, out_v):
        tok0 = _sc_worker_id() * tpw
        zero = jnp.zeros((NL,), F32)

        def compute(item, buf):
            tok = item // ipt
            item_vec = jnp.full((NL,), item, jnp.int32)
            for hf in range(2):
                def body(rg, accs):
                    ws = [plsc.bitcast(plsc.load_gather(
                        w_v, [item_vec, jnp.full((NL,), rg * RG + rr, jnp.int32)]), BF16) for rr in range(RG)]
                    his, los = [], []
                    for i in range(nv):
                        word = hf * half + i * NL
                        ps = [plsc.bitcast(buf[rg * RG + rr, word // LANES, pl.ds(word % LANES, NL)], BF16) * ws[rr]
                              for rr in range(RG)]
                        hi, lo = _sc_widen_pair((ps[0] + ps[1]) + (ps[2] + ps[3]))
                        his.append(accs[i] + hi)
                        los.append(accs[nv + i] + lo)
                    return tuple(his + los)
                accs = lax.fori_loop(0, R // RG, body, tuple(zero for _ in range(2 * nv)))
                for i in range(nv):
                    word = hf * half + i * NL
                    plsc.addupdate(out_v.at[tok, pl.ds(word, NL)], accs[i])
                    plsc.addupdate(out_v.at[tok, pl.ds(DW + word, NL)], accs[nv + i])

        @pl.loop(0, tpw // G)
        def _(g):
            t0 = tok0 + g * G
            pltpu.sync_copy(idx_hbm.at[pl.ds(t0 * ipt, nit)], idx_v)
            pltpu.sync_copy(w_hbm.at[pl.ds(t0 * ipt, nit)], w_v)

            @pl.loop(0, G)
            def _(t):
                @pl.loop(0, D // NL)
                def _(i):
                    out_v[t, pl.ds(pl.multiple_of(i * NL, NL), NL)] = zero

            _sc_gather_loop(v_hbm, idx_v, bufs, sems, nit, compute)
            pltpu.sync_copy(out_v, o_hbm.at[pl.ds(t0, G)])

    return k(v, idx4, w4)


def _act_kernel(h_ref, g_ref, w_ref):
    w = _gelu(h_ref[...]) * g_ref[...]
    w_ref[...] = _pack_bf16_pair(w, w)


def _peer_act(hraw, gate):
    T, n = gate.shape
    tm = PEER_ACT_TM
    spec = pl.BlockSpec((tm, n), lambda i: (i, 0))
    return pl.pallas_call(
        _act_kernel, grid=(T // tm,), in_specs=[spec, spec], out_specs=spec,
        out_shape=jax.ShapeDtypeStruct((T, n), jnp.int32),
        compiler_params=pltpu.CompilerParams(dimension_semantics=("parallel",)),
        name="peer_act",
    )(hraw, gate)


def _final_kernel(h_ref, p_ref, g_ref, y_ref):
    y = h_ref[...] + p_ref[...]
    ms = jnp.mean(y * y, axis=-1, keepdims=True)
    y_ref[...] = y * lax.rsqrt(ms + EPS) * g_ref[...]


def _final(h, po, g):
    T, D = h.shape
    tm = FINAL_TM
    spec = pl.BlockSpec((tm, D), lambda i: (i, 0))
    return pl.pallas_call(
        _final_kernel, grid=(T // tm,),
        in_specs=[spec, spec, pl.BlockSpec((1, D), lambda i: (0, 0))], out_specs=spec,
        out_shape=jax.ShapeDtypeStruct((T, D), F32),
        compiler_params=pltpu.CompilerParams(dimension_semantics=("parallel",)),
        name="final_norm",
    )(h, po, g)


def kernel(x, norm1_g, w_in, hgrn_lb_fwd, hgrn_lb_bwd, hgrn_out_g, diff_lam_q1, diff_lam_k1,
           diff_lam_q2, diff_lam_k2, diff_out_g, rel_bias, w_out, norm2_g, peer_w_q,
           peer_sub_keys, peer_u, peer_v, final_g):
    B, L, D = x.shape

    w_r = w_in[LAYER].astype(BF16)
    f32 = jnp.float32
    lam_init = 0.8 - 0.6 * math.exp(-0.3 * LAYER)
    lam = (jnp.exp(jnp.sum(diff_lam_q1[LAYER].astype(f32) * diff_lam_k1[LAYER].astype(f32)))
           - jnp.exp(jnp.sum(diff_lam_q2[LAYER].astype(f32) * diff_lam_k2[LAYER].astype(f32))) + lam_init)
    lam = lam.reshape(1, 1)
    bias_ext = _rel_bias_ext(rel_bias, L)
    sk = peer_sub_keys[LAYER].reshape(2 * PEER_HEADS, PEER_NKEYS, PEER_DKEY).astype(BF16)
    wq = peer_w_q[LAYER].reshape(D, -1).astype(BF16)
    wo = w_out[LAYER].astype(BF16)
    def pack_table(tab):
        words = _pack_bf16_pair(tab[:, :D // 2], tab[:, D // 2:])
        return words.reshape(tab.shape[0], D // 2 // LANES, LANES)
    u3, v3 = pack_table(peer_u[LAYER]), pack_table(peer_v[LAYER])

    bc = B // BATCH_CHUNKS
    tc = bc * L
    rows = tc * PEER_HEADS * PEER_TOPK // SC_GATHER_ROWS
    outs = []
    pending = {}
    combined = {}
    x2 = x.reshape(B * L, D)
    for c in range(BATCH_CHUNKS + CHUNK_LAG + FINAL_LAG):
        live = c < BATCH_CHUNKS
        g1 = norm1_g[LAYER][None, :]
        if c - CHUNK_LAG - FINAL_LAG in combined:
            h_prev, po = combined.pop(c - CHUNK_LAG - FINAL_LAG)
            if live:
                g1, po = lax.optimization_barrier((g1, po))
            outs.append(_final(h_prev, po, final_g[None, :]).reshape(bc, L, D))
        if CHUNK_LAG <= c < BATCH_CHUNKS + CHUNK_LAG:
            h, idx4, gate, hraw = pending.pop(c - CHUNK_LAG)
            wts = _peer_act(hraw.reshape(tc, -1), gate)
            if live:
                g1, wts = lax.optimization_barrier((g1, wts))
            combined[c - CHUNK_LAG] = (h, _peer_combine(v3, idx4, wts.reshape(rows, SC_GATHER_ROWS), tc, D))
        if live:
            zf, pb = _inproj(x2, g1, w_r, c * tc, tc)
            zf3 = zf.reshape(bc, L, -1)
            pb3 = pb.reshape(bc, L, -1)
            o_h = _hgrn(zf3, pb3, hgrn_lb_fwd, hgrn_lb_bwd, hgrn_out_g[LAYER][None, :])
            o_d = _attn(pb3, bias_ext, lam, diff_out_g[LAYER][None, :])
            h, hn, idx, gate = _route(x2, c * tc, o_h.reshape(tc, -1), o_d.reshape(tc, -1), wo,
                                      norm2_g[LAYER][None, :], wq, sk)
            idx4 = idx.reshape(rows, SC_GATHER_ROWS)
            pending[c] = (h, idx4, gate, _peer_hidden(u3, idx4, hn))
    return jnp.concatenate(outs, axis=0)
```

```python
import functools
import math

import numpy as np
import jax
import jax.numpy as jnp
from jax import lax
from jax.experimental import pallas as pl
from jax.experimental.pallas import tpu as pltpu
from jax.experimental.pallas import tpu_sc as plsc

F32 = jnp.float32
BF16 = jnp.bfloat16
EPS = 1e-6

HGRN_HEADS = 4
HGRN_DK = 128
DIFF_HEADS = 4
DIFF_HALF = 64
REL_BUCKETS = 32
REL_MAX_DIST = 128
PEER_HEADS = 8
PEER_NKEYS = 128
PEER_DKEY = 128
PEER_TOPK = 16
LAYER = 0

LANES = 128
SUBLANES = 8
VMEM_LIMIT = 48 * 1024 * 1024

INPROJ_TM = 512
HG_C = 64
HG_LEVELS = (64, 32, 16, 8, 4, 2)
ATTN_TQ = 256
ROUTE_TM = 256
CHUNK_ROWS = (1, 1, 2, 2, 2, 2, 2, 2, 1, 1)
CHUNK_LAG = 2
FINAL_LAG = 1
PEER_ACT_TM = 2048
FINAL_TM = 512

SC_CORES = 2
SC_SUBCORES = 16
SC_LANES = 16
SC_WORKERS = SC_CORES * SC_SUBCORES
SC_GATHER_ROWS = 64
SC_TOKEN_GROUP = 32
SC_COMBINE_GROUP = 32
SC_RING = 2


def _dot(a, b):
    return jnp.dot(a, b, preferred_element_type=F32)


def _dot_nt(a, b):
    return lax.dot_general(a, b, (((1,), (1,)), ((), ())), preferred_element_type=F32)


def _dot_tn(a, b):
    return lax.dot_general(a, b, (((0,), (0,)), ((), ())), preferred_element_type=F32)


def _silu(x):
    return x * (1.0 / (1.0 + jnp.exp(-x)))


BF16_HI_MASK = -65536


def _pack_bf16_pair(hi, lo):
    bits = lambda a: lax.bitcast_convert_type(a.astype(BF16).astype(F32), jnp.int32)
    return (bits(hi) & BF16_HI_MASK) | lax.shift_right_logical(bits(lo), 16)


def _inproj_kernel(x_ref, g_ref, w_ref, zf_ref, pb_ref):
    x = x_ref[...]
    ms = jnp.mean(x * x, axis=-1, keepdims=True)
    xn = (x * lax.rsqrt(ms + EPS) * g_ref[...]).astype(BF16)
    hw = zf_ref.shape[1] // 2
    zf_ref[...] = _dot(xn, w_ref[:, hw:3 * hw])
    pb_ref[:, 0:hw] = _dot(xn, w_ref[:, 0:hw]).astype(BF16)
    for j in range(3, 8):
        y = _dot(xn, w_ref[:, j * hw:(j + 1) * hw])
        if j == 5:
            y = y * (DIFF_HALF ** -0.5)
        pb_ref[:, (j - 2) * hw:(j - 1) * hw] = y.astype(BF16)


def _inproj(x2, g, w, tok0, T):
    D = x2.shape[1]
    N = w.shape[1]
    nz = 2 * HGRN_HEADS * HGRN_DK
    tm = INPROJ_TM
    off = tok0 // tm
    return pl.pallas_call(
        _inproj_kernel,
        grid=(T // tm,),
        in_specs=[
            pl.BlockSpec((tm, D), lambda i: (i + off, 0)),
            pl.BlockSpec((1, D), lambda i: (0, 0)),
            pl.BlockSpec((D, N), lambda i: (0, 0)),
        ],
        out_specs=[
            pl.BlockSpec((tm, nz), lambda i: (i, 0)),
            pl.BlockSpec((tm, N - nz), lambda i: (i, 0)),
        ],
        out_shape=[
            jax.ShapeDtypeStruct((T, nz), F32),
            jax.ShapeDtypeStruct((T, N - nz), BF16),
        ],
        compiler_params=pltpu.CompilerParams(
            dimension_semantics=("parallel",), vmem_limit_bytes=VMEM_LIMIT),
        name="inproj",
    )(x2, g, w)


def _hgrn_consts():
    C = HG_C
    r = np.arange(C)
    t = r[:, None]
    u = r[None, :]
    blocks = [u <= t, u > t]
    masks = [np.eye(C, dtype=bool)]
    for B in HG_LEVELS:
        half = B // 2
        a = (r // B) * B
        m = (a + half - 1)[:, None]
        upper = (r - a) >= half
        blocks.append(np.where(upper[:, None], (u > m) & (u <= t), (u > t) & (u <= m)))
        same = a[:, None] == a[None, :]
        masks.append(same & upper[:, None] & (~upper)[None, :])
    m_f = np.concatenate(blocks, 0).astype(np.float32)
    m_b = np.concatenate([b[::-1, ::-1] for b in blocks], 0).astype(np.float32)
    k_f = np.stack(masks).astype(np.float32)
    k_b = np.ascontiguousarray(k_f.transpose(0, 2, 1))
    return m_f, m_b, k_f, k_b


def _hgrn_gates(z, tab):
    tabf = tab.astype(F32)
    e = jnp.exp(tabf - jnp.max(tabf, axis=0, keepdims=True))
    lb = jnp.sum(e[0:LAYER + 1], axis=0, keepdims=True) / jnp.sum(e, axis=0, keepdims=True)
    log_lb = jnp.log(lb)
    log_1m = jnp.log1p(-lb)
    ez = jnp.exp(-jnp.abs(z))
    l1p = jnp.log1p(ez)
    log_sig = jnp.minimum(z, 0.0) - l1p
    c = log_1m + log_sig
    hi = jnp.maximum(log_lb, c)
    lo = jnp.minimum(log_lb, c)
    log_f = hi + jnp.log1p(jnp.exp(lo - hi))
    sig_neg = jnp.where(z >= 0.0, ez, 1.0) / (1.0 + ez)
    k = (1.0 - lb) * sig_neg
    return log_f, k


def _hgrn_kernel(lbf_ref, lbb_ref, og_ref, mf_ref, mb_ref, kf_ref, kb_ref,
                 q_ref, zf_ref, zb_ref, v_ref, g_ref, o_ref, of_s, ob_s):
    C = HG_C
    L = q_ref.shape[0]
    n = L // C
    dv = v_ref.shape[1]

    def chunk(c, st, m_ref, k_ref, z_ref, tab_ref, forward):
        sl = pl.ds(pl.multiple_of(c * C, C), C)
        qh = q_ref[sl, :].astype(F32)
        q = _silu(qh)
        v = v_ref[sl, :]
        log_f, k = _hgrn_gates(z_ref[sl, :], tab_ref[...])
        lf_hi = log_f.astype(BF16)
        lf_lo = (log_f - lf_hi.astype(F32)).astype(BF16)
        m = m_ref[...]
        e = jnp.exp(_dot(m, lf_hi) + _dot(m, lf_lo))
        e_b = e[0:C]
        e_s = e[C:2 * C]
        dec = e[C - 1:C] if forward else e[0:1]
        o = _dot_nt((q * e_b).astype(BF16), st.astype(BF16))
        a = _dot_nt(q.astype(BF16), k.astype(BF16)) * k_ref[0]
        for l in range(len(HG_LEVELS)):
            e_l = e[(2 + l) * C:(3 + l) * C]
            a = a + _dot_nt((q * e_l).astype(BF16), (k * e_l).astype(BF16)) * k_ref[l + 1]
        o = o + _dot(a.astype(BF16), v)
        st = st * dec + _dot_tn(v, (k * e_s).astype(BF16))
        return sl, o, st

    st0 = jnp.zeros((dv, q_ref.shape[1]), F32)

    def both(i, carry):
        st_f, st_b = carry
        sl_f, o_f, st_f = chunk(i, st_f, mf_ref, kf_ref, zf_ref, lbf_ref, True)
        sl_b, o_b, st_b = chunk(n - 1 - i, st_b, mb_ref, kb_ref, zb_ref, lbb_ref, False)
        of_s[sl_f, :] = o_f
        ob_s[sl_b, :] = o_b
        return st_f, st_b

    lax.fori_loop(0, n, both, (st0, st0))

    def finish(c, _):
        sl = pl.ds(pl.multiple_of(c * C, C), C)
        tot = of_s[sl, :] + ob_s[sl, :]
        ms = jnp.mean(tot * tot, axis=-1, keepdims=True)
        y = tot * lax.rsqrt(ms + EPS) * og_ref[...]
        o_ref[sl, :] = (y * _silu(g_ref[sl, :].astype(F32))).astype(o_ref.dtype)
        return 0

    lax.fori_loop(0, n, finish, 0)


def _hgrn(zf3, pb3, lb_f, lb_b, out_g):
    B, L, _ = zf3.shape
    H, dk = HGRN_HEADS, HGRN_DK
    m_f, m_b, k_f, k_b = _hgrn_consts()
    nlev = k_f.shape[0]
    full2 = lambda b, h: (0, 0)
    full3 = lambda b, h: (0, 0, 0)
    seq = lambda off: pl.BlockSpec((None, L, dk), lambda b, h: (b, 0, off + h))
    return pl.pallas_call(
        _hgrn_kernel,
        grid=(B, H),
        in_specs=[
            pl.BlockSpec((lb_f.shape[0], dk), lambda b, h: (0, h)),
            pl.BlockSpec((lb_b.shape[0], dk), lambda b, h: (0, h)),
            pl.BlockSpec((1, dk), full2),
            pl.BlockSpec(m_f.shape, full2),
            pl.BlockSpec(m_b.shape, full2),
            pl.BlockSpec((nlev, HG_C, HG_C), full3),
            pl.BlockSpec((nlev, HG_C, HG_C), full3),
            seq(0),
            seq(0),
            seq(H),
            seq(H),
            seq(2 * H),
        ],
        out_specs=pl.BlockSpec((None, L, dk), lambda b, h: (b, 0, h)),
        out_shape=jax.ShapeDtypeStruct((B, L, H * dk), BF16),
        scratch_shapes=[pltpu.VMEM((L, dk), F32), pltpu.VMEM((L, dk), F32)],
        compiler_params=pltpu.CompilerParams(
            dimension_semantics=("parallel", "parallel"), vmem_limit_bytes=VMEM_LIMIT),
        name="hgrn",
    )(lb_f, lb_b, out_g, jnp.asarray(m_f, BF16), jnp.asarray(m_b, BF16),
      jnp.asarray(k_f), jnp.asarray(k_b), pb3, zf3, zf3, pb3, pb3)


def _t5_bucket(rel):
    nb = REL_BUCKETS // 2
    ret = jnp.where(rel > 0, nb, 0)
    n = jnp.abs(rel)
    max_exact = nb // 2
    nf = jnp.maximum(n, 1).astype(jnp.float32)
    large = max_exact + (jnp.log(nf / max_exact) / math.log(REL_MAX_DIST / max_exact)
                         * (nb - max_exact)).astype(jnp.int32)
    large = jnp.minimum(large, nb - 1)
    return ret + jnp.where(n < max_exact, n, large)


def _rel_bias_ext(rel_bias, L):
    j = jnp.arange(2 * L - LANES, dtype=jnp.int32)
    ql = jnp.arange(LANES, dtype=jnp.int32)
    rel = j[None, :] - (L - LANES) - ql[:, None]
    bucket = _t5_bucket(rel)
    tab = rel_bias.astype(F32)
    out = jnp.zeros((tab.shape[1],) + bucket.shape, F32)
    for b in range(REL_BUCKETS):
        out = jnp.where((bucket == b)[None], tab[b][:, None, None], out)
    return out


def _attn_kernel(lam_ref, q_ref, k_ref, v_ref, bias_ref, og_ref, o_ref, *, lam_init):
    tq = q_ref.shape[0]
    L = k_ref.shape[0]
    qi = pl.program_id(2)
    lam = lam_ref[0, 0]
    q = q_ref[...]
    k = k_ref[...]
    lane = lax.broadcasted_iota(jnp.int32, q.shape, 1)
    zero = jnp.zeros_like(q)
    q0 = jnp.where(lane < DIFF_HALF, q, zero)
    q1 = jnp.where(lane >= DIFF_HALF, q, zero)
    parts = []
    for j in range(tq // LANES):
        off = pl.multiple_of(L - LANES - (qi * tq + j * LANES), LANES)
        parts.append(bias_ref[:, pl.ds(off, L)])
    bias = jnp.concatenate(parts, axis=0) if len(parts) > 1 else parts[0]

    def soft(qm):
        s = _dot_nt(qm, k) + bias
        e = jnp.exp(s - jnp.max(s, axis=-1, keepdims=True))
        return e, 1.0 / jnp.sum(e, axis=-1, keepdims=True)

    e0, r0 = soft(q0)
    e1, r1 = soft(q1)
    w = e0 * r0 - e1 * (lam * r1)
    o = _dot(w.astype(BF16), v_ref[...])
    ms = jnp.mean(o * o, axis=-1, keepdims=True)
    y = o * lax.rsqrt(ms + EPS) * og_ref[...] * (1.0 - lam_init)
    o_ref[...] = y.astype(o_ref.dtype)


def _attn(pb3, bias_ext, lam, out_g):
    B, L, _ = pb3.shape
    H, dh = DIFF_HEADS, 2 * DIFF_HALF
    tq = ATTN_TQ
    base = (3 * HGRN_HEADS * HGRN_DK) // dh
    lam_init = 0.8 - 0.6 * math.exp(-0.3 * LAYER)
    return pl.pallas_call(
        functools.partial(_attn_kernel, lam_init=lam_init),
        grid=(B, H, L // tq),
        in_specs=[
            pl.BlockSpec(memory_space=pltpu.SMEM),
            pl.BlockSpec((None, tq, dh), lambda b, h, i: (b, i, base + h)),
            pl.BlockSpec((None, L, dh), lambda b, h, i: (b, 0, base + H + h)),
            pl.BlockSpec((None, L, dh), lambda b, h, i: (b, 0, base + 2 * H + h)),
            pl.BlockSpec((None, LANES, 2 * L - LANES), lambda b, h, i: (h, 0, 0)),
            pl.BlockSpec((1, dh), lambda b, h, i: (0, 0)),
        ],
        out_specs=pl.BlockSpec((None, tq, dh), lambda b, h, i: (b, i, h)),
        out_shape=jax.ShapeDtypeStruct((B, L, H * dh), BF16),
        compiler_params=pltpu.CompilerParams(
            dimension_semantics=("parallel", "parallel", "parallel"),
            vmem_limit_bytes=VMEM_LIMIT),
        name="attn",
    )(lam, pb3, pb3, pb3, bias_ext, out_g)


def _cand_layout():
    K = PEER_TOPK
    groups = [("a", 0, 0), ("a", 0, 8), ("a", 1, 0), ("a", 2, 0), ("a", 3, 0),
              ("b", 0, 8), ("b", 0, 0), ("b", 1, 0), ("b", 2, 0)]
    seen = set()
    pos, valid = [], []
    for kind, fixed, start in groups:
        for r in range(SUBLANES):
            a, b = (fixed, start + r) if kind == "a" else (start + r, fixed)
            ok = (a + 1) * (b + 1) <= K and (a, b) not in seen
            if ok:
                seen.add((a, b))
            pos.append(a * K + b if ok else K * K + len(pos))
            valid.append(ok)
    assert len(seen) == sum(K // (a + 1) for a in range(K))
    return groups, np.array(pos, np.int32), np.array(valid, bool)


def _extract_topk(s, key, payload, k, big):
    n, tm = s.shape
    S = SUBLANES
    s3 = s.reshape(n // S, S, tm)
    key3 = key.reshape(n // S, S, tm)
    pay3 = None if payload is key else payload.reshape(n // S, S, tm)
    slot = lax.broadcasted_iota(jnp.int32, (k // S, S, tm), 0) * S + lax.broadcasted_iota(
        jnp.int32, (k // S, S, tm), 1)

    def all_reduce(x3, op):
        r = x3[0]
        for g in range(1, x3.shape[0]):
            r = op(r, x3[g])
        for sh in (S // 2, S // 4, S // 8):
            r = op(r, pltpu.roll(r, sh, axis=0))
        return r

    def body(j, carry):
        s3, vals, pay = carry
        m = all_reduce(s3, jnp.maximum)
        kk = all_reduce(jnp.where(s3 == m[None], key3, big), jnp.minimum)
        sel = key3 == kk[None]
        p = kk if pay3 is None else all_reduce(jnp.where(sel, pay3, 0), jnp.add)
        vals = jnp.where(slot == j, m[None], vals)
        pay = jnp.where(slot == j, p[None], pay)
        s3 = jnp.where(sel, -jnp.inf, s3)
        return s3, vals, pay

    init = (s3, jnp.zeros((k // S, S, tm), F32), jnp.zeros((k // S, S, tm), jnp.int32))
    _, vals, pay = lax.fori_loop(0, k, body, init)
    return vals.reshape(k, tm), pay.reshape(k, tm)


def _route_kernel(x_ref, oh_ref, od_ref, woh_ref, wod_ref, g2_ref, wq_ref, sk_ref, cpos_ref, cmask_ref,
                  h_ref, hn_ref, idx_ref, gate_ref, q_s, tv_s, ti_s, gs_s, is_s, *, groups):
    K = PEER_TOPK
    tm = x_ref.shape[0]
    h = x_ref[...] + _dot(oh_ref[...], woh_ref[...]) + _dot(od_ref[...], wod_ref[...])
    h_ref[...] = h
    hn = h * lax.rsqrt(jnp.mean(h * h, axis=-1, keepdims=True) + EPS) * g2_ref[...]
    half = hn.shape[1] // 2
    hn_ref[...] = _pack_bf16_pair(hn[:, :half], hn[:, half:])
    q_s[...] = _dot(hn.astype(BF16), wq_ref[...]).astype(BF16)

    key_iota = lax.broadcasted_iota(jnp.int32, (PEER_NKEYS, tm), 0)

    def half_topk(hp, _):
        col = pl.multiple_of(hp * PEER_DKEY, PEER_DKEY)
        s = _dot_nt(sk_ref[hp], q_s[:, pl.ds(col, PEER_DKEY)])
        vals, idxs = _extract_topk(s, key_iota, key_iota, K, PEER_NKEYS)
        tv_s[hp] = vals
        ti_s[hp] = idxs
        return 0

    lax.fori_loop(0, 2 * PEER_HEADS, half_topk, 0)

    cpos = cpos_ref[...]
    cmask = cmask_ref[...]

    def head(hd, _):
        s0, s1 = tv_s[2 * hd], tv_s[2 * hd + 1]
        i0, i1 = ti_s[2 * hd] * PEER_NKEYS, ti_s[2 * hd + 1]
        cs, ci = [], []
        for kind, fixed, start in groups:
            if kind == "a":
                cs.append(s0[fixed:fixed + 1] + s1[start:start + SUBLANES])
                ci.append(i0[fixed:fixed + 1] + i1[start:start + SUBLANES])
            else:
                cs.append(s0[start:start + SUBLANES] + s1[fixed:fixed + 1])
                ci.append(i0[start:start + SUBLANES] + i1[fixed:fixed + 1])
        cand = jnp.concatenate(cs, axis=0) + cmask
        cidx = jnp.concatenate(ci, axis=0)
        best, eidx = _extract_topk(cand, cpos, cidx, K, 2 * K * K)
        ex = jnp.exp(best - best[0:1])
        gate = ex / jnp.sum(ex, axis=0, keepdims=True)
        row = pl.ds(pl.multiple_of(hd * K, K), K)
        gs_s[row, :] = gate
        is_s[row, :] = eidx.astype(F32)
        return 0

    lax.fori_loop(0, PEER_HEADS, head, 0)
    gate_ref[...] = gs_s[...].T
    idx_ref[...] = is_s[...].T.astype(jnp.int32)


def _route(x2, tok0, oh2, od2, w_out, g2, w_q, sub_keys):
    T = oh2.shape[0]
    D = x2.shape[1]
    tm = ROUTE_TM
    off = tok0 // tm
    K = PEER_TOPK
    nh = oh2.shape[1]
    nq = w_q.shape[1]
    npk = PEER_HEADS * K
    groups, pos, valid = _cand_layout()
    ncand = pos.shape[0]
    cpos = jnp.asarray(np.broadcast_to(pos[:, None], (ncand, tm)))
    cmask = jnp.asarray(np.broadcast_to(np.where(valid, 0.0, -np.inf).astype(np.float32)[:, None], (ncand, tm)))
    row = lambda i: (i, 0)
    full2 = lambda i: (0, 0)
    return pl.pallas_call(
        functools.partial(_route_kernel, groups=groups),
        grid=(T // tm,),
        in_specs=[
            pl.BlockSpec((tm, D), lambda i: (i + off, 0)),
            pl.BlockSpec((tm, nh), row),
            pl.BlockSpec((tm, nh), row),
            pl.BlockSpec((nh, D), full2),
            pl.BlockSpec((nh, D), lambda i: (1, 0)),
            pl.BlockSpec((1, D), full2),
            pl.BlockSpec((D, nq), full2),
            pl.BlockSpec(sub_keys.shape, lambda i: (0, 0, 0)),
            pl.BlockSpec((ncand, tm), full2),
            pl.BlockSpec((ncand, tm), full2),
        ],
        out_specs=[
            pl.BlockSpec((tm, D), row),
            pl.BlockSpec((tm, D // 2), row),
            pl.BlockSpec((tm, npk), row),
            pl.BlockSpec((tm, npk), row),
        ],
        out_shape=[
            jax.ShapeDtypeStruct((T, D), F32),
            jax.ShapeDtypeStruct((T, D // 2), jnp.int32),
            jax.ShapeDtypeStruct((T, npk), jnp.int32),
            jax.ShapeDtypeStruct((T, npk), F32),
        ],
        scratch_shapes=[
            pltpu.VMEM((tm, nq), BF16),
            pltpu.VMEM((2 * PEER_HEADS, K, tm), F32),
            pltpu.VMEM((2 * PEER_HEADS, K, tm), jnp.int32),
            pltpu.VMEM((npk, tm), F32),
            pltpu.VMEM((npk, tm), F32),
        ],
        compiler_params=pltpu.CompilerParams(
            dimension_semantics=("parallel",), vmem_limit_bytes=VMEM_LIMIT),
        name="route",
    )(x2, oh2, od2, w_out, w_out, g2, w_q, sub_keys, cpos, cmask)


def _gelu(x):
    return 0.5 * x * (1.0 + lax.erf(x * (1.0 / math.sqrt(2.0))))


def _sc_mesh():
    return plsc.VectorSubcoreMesh(core_axis_name="c", subcore_axis_name="s")


def _sc_worker_id():
    return lax.axis_index("s") * SC_CORES + lax.axis_index("c")


def _sc_widen_pair(s):
    si = plsc.bitcast(s, jnp.int32)
    return plsc.bitcast(si & BF16_HI_MASK, F32), plsc.bitcast(si << 16, F32)


def _sc_gather_loop(tab_hbm, idx_v, bufs, sems, n_items, compute):
    nb = len(bufs)

    def start(item, b):
        pltpu.async_copy(tab_hbm.at[idx_v.at[item]], bufs[b], sems[b])

    def wait(b):
        pltpu.make_async_copy(tab_hbm.at[idx_v.at[0]], bufs[b], sems[b]).wait()

    for p in range(nb - 1):
        start(p, p)

    @pl.loop(0, pl.cdiv(n_items, nb))
    def _(i):
        for b in range(nb):
            it = i * nb + b

            @pl.when(it < n_items)
            def _():
                @pl.when(it + nb - 1 < n_items)
                def _():
                    start(it + nb - 1, (b + nb - 1) % nb)

                wait(b)
                compute(it, bufs[b])


def _peer_hidden(u, idx4, x):
    T, DW = x.shape
    n_rows, R = idx4.shape
    ipt = n_rows // T
    G = SC_TOKEN_GROUP
    nit = G * ipt
    tpw = T // SC_WORKERS
    NL = SC_LANES
    RB = 8
    NV = 4
    GW = NV * NL
    tile = (R, DW // LANES, LANES)

    @functools.partial(
        pl.kernel, mesh=_sc_mesh(), compiler_params=pltpu.CompilerParams(needs_layout_passes=False),
        out_type=jax.ShapeDtypeStruct((n_rows, R), F32),
        scratch_types=[pltpu.VMEM((nit, R), jnp.int32), pltpu.VMEM((G, DW), jnp.int32),
                       [pltpu.VMEM(tile, jnp.int32)] * SC_RING, [pltpu.SemaphoreType.DMA] * SC_RING,
                       pltpu.VMEM((nit, R), F32)],
        name="peer_hidden")
    def k(u_hbm, idx_hbm, x_hbm, h_hbm, idx_v, x_v, bufs, sems, h_v):
        tok0 = _sc_worker_id() * tpw
        lane = lax.broadcasted_iota(jnp.int32, (NL,), 0)
        zero = jnp.zeros((NL,), F32)

        def compute(item, buf):
            tok = item // ipt
            hvs = [zero for _ in range(R // NL)]
            for rb in range(R // RB):
                def body(g, accs):
                    sub = g // (LANES // GW)
                    base = (g % (LANES // GW)) * GW
                    xs = [plsc.bitcast(x_v[tok, pl.ds(pl.multiple_of(g * GW + jj * NL, NL), NL)], BF16)
                          for jj in range(NV)]
                    out = []
                    for j in range(RB):
                        ps = [plsc.bitcast(buf[rb * RB + j, sub, pl.ds(pl.multiple_of(base + jj * NL, NL), NL)],
                                           BF16) * xs[jj] for jj in range(NV)]
                        hi, lo = _sc_widen_pair((ps[0] + ps[1]) + (ps[2] + ps[3]))
                        out.append((accs[j] + hi) + lo)
                    return tuple(out)
                accs = lax.fori_loop(0, DW // GW, body, tuple(zero for _ in range(RB)))
                for j in range(RB):
                    r = rb * RB + j
                    hvs[r // NL] = jnp.where(lane == (r % NL), jnp.sum(accs[j]), hvs[r // NL])
            for q in range(R // NL):
                h_v[item, pl.ds(q * NL, NL)] = hvs[q]

        @pl.loop(0, tpw // G)
        def _(g):
            t0 = tok0 + g * G
            pltpu.sync_copy(idx_hbm.at[pl.ds(t0 * ipt, nit)], idx_v)
            pltpu.sync_copy(x_hbm.at[pl.ds(t0, G)], x_v)
            _sc_gather_loop(u_hbm, idx_v, bufs, sems, nit, compute)
            pltpu.sync_copy(h_v, h_hbm.at[pl.ds(t0 * ipt, nit)])

    return k(u, idx4, x)


def _peer_combine(v, idx4, w4, T, D):
    n_rows, R = idx4.shape
    ipt = n_rows // T
    G = SC_COMBINE_GROUP
    nit = G * ipt
    tpw = T // SC_WORKERS
    NL = SC_LANES
    DW = D // 2
    half = DW // 2
    nv = half // NL
    RG = 4
    tile = (R, DW // LANES, LANES)

    @functools.partial(
        pl.kernel, mesh=_sc_mesh(), compiler_params=pltpu.CompilerParams(needs_layout_passes=False),
        out_type=jax.ShapeDtypeStruct((T, D), F32),
        scratch_types=[pltpu.VMEM((nit, R), jnp.int32), pltpu.VMEM((nit, R), jnp.int32),
                       [pltpu.VMEM(tile, jnp.int32)] * SC_RING, [pltpu.SemaphoreType.DMA] * SC_RING,
                       pltpu.VMEM((G, D), F32)],
        name="peer_combine")
    def k(v_hbm, idx_hbm, w_hbm, o_hbm, idx_v, w_v, bufs, sems, out_v):
        tok0 = _sc_worker_id() * tpw
        zero = jnp.zeros((NL,), F32)

        def compute(item, buf):
            tok = item // ipt
            item_vec = jnp.full((NL,), item, jnp.int32)
            for hf in range(2):
                def body(rg, accs):
                    ws = [plsc.bitcast(plsc.load_gather(
                        w_v, [item_vec, jnp.full((NL,), rg * RG + rr, jnp.int32)]), BF16) for rr in range(RG)]
                    his, los = [], []
                    for i in range(nv):
                        word = hf * half + i * NL
                        ps = [plsc.bitcast(buf[rg * RG + rr, word // LANES, pl.ds(word % LANES, NL)], BF16) * ws[rr]
                              for rr in range(RG)]
                        hi, lo = _sc_widen_pair((ps[0] + ps[1]) + (ps[2] + ps[3]))
                        his.append(accs[i] + hi)
                        los.append(accs[nv + i] + lo)
                    return tuple(his + los)
                accs = lax.fori_loop(0, R // RG, body, tuple(zero for _ in range(2 * nv)))
                for i in range(nv):
                    word = hf * half + i * NL
                    plsc.addupdate(out_v.at[tok, pl.ds(word, NL)], accs[i])
                    plsc.addupdate(out_v.at[tok, pl.ds(DW + word, NL)], accs[nv + i])

        @pl.loop(0, tpw // G)
        def _(g):
            t0 = tok0 + g * G
            pltpu.sync_copy(idx_hbm.at[pl.ds(t0 * ipt, nit)], idx_v)
            pltpu.sync_copy(w_hbm.at[pl.ds(t0 * ipt, nit)], w_v)

            @pl.loop(0, G)
            def _(t):
                @pl.loop(0, D // NL)
                def _(i):
                    out_v[t, pl.ds(pl.multiple_of(i * NL, NL), NL)] = zero

            _sc_gather_loop(v_hbm, idx_v, bufs, sems, nit, compute)
            pltpu.sync_copy(out_v, o_hbm.at[pl.ds(t0, G)])

    return k(v, idx4, w4)


def _act_kernel(h_ref, g_ref, w_ref):
    w = _gelu(h_ref[...]) * g_ref[...]
    w_ref[...] = _pack_bf16_pair(w, w)


def _peer_act(hraw, gate):
    T, n = gate.shape
    tm = PEER_ACT_TM
    spec = pl.BlockSpec((tm, n), lambda i: (i, 0))
    return pl.pallas_call(
        _act_kernel, grid=(T // tm,), in_specs=[spec, spec], out_specs=spec,
        out_shape=jax.ShapeDtypeStruct((T, n), jnp.int32),
        compiler_params=pltpu.CompilerParams(dimension_semantics=("parallel",)),
        name="peer_act",
    )(hraw, gate)


def _final_kernel(h_ref, p_ref, g_ref, y_ref):
    y = h_ref[...] + p_ref[...]
    ms = jnp.mean(y * y, axis=-1, keepdims=True)
    y_ref[...] = y * lax.rsqrt(ms + EPS) * g_ref[...]


def _final(h, po, g):
    T, D = h.shape
    tm = FINAL_TM
    spec = pl.BlockSpec((tm, D), lambda i: (i, 0))
    return pl.pallas_call(
        _final_kernel, grid=(T // tm,),
        in_specs=[spec, spec, pl.BlockSpec((1, D), lambda i: (0, 0))], out_specs=spec,
        out_shape=jax.ShapeDtypeStruct((T, D), F32),
        compiler_params=pltpu.CompilerParams(dimension_semantics=("parallel",)),
        name="final_norm",
    )(h, po, g)


def kernel(x, norm1_g, w_in, hgrn_lb_fwd, hgrn_lb_bwd, hgrn_out_g, diff_lam_q1, diff_lam_k1,
           diff_lam_q2, diff_lam_k2, diff_out_g, rel_bias, w_out, norm2_g, peer_w_q,
           peer_sub_keys, peer_u, peer_v, final_g):
    B, L, D = x.shape

    w_r = w_in[LAYER].astype(BF16)
    f32 = jnp.float32
    lam_init = 0.8 - 0.6 * math.exp(-0.3 * LAYER)
    lam = (jnp.exp(jnp.sum(diff_lam_q1[LAYER].astype(f32) * diff_lam_k1[LAYER].astype(f32)))
           - jnp.exp(jnp.sum(diff_lam_q2[LAYER].astype(f32) * diff_lam_k2[LAYER].astype(f32))) + lam_init)
    lam = lam.reshape(1, 1)
    bias_ext = _rel_bias_ext(rel_bias, L)
    sk = peer_sub_keys[LAYER].reshape(2 * PEER_HEADS, PEER_NKEYS, PEER_DKEY).astype(BF16)
    wq = peer_w_q[LAYER].reshape(D, -1).astype(BF16)
    wo = w_out[LAYER].astype(BF16)
    def pack_table(tab):
        words = _pack_bf16_pair(tab[:, :D // 2], tab[:, D // 2:])
        return words.reshape(tab.shape[0], D // 2 // LANES, LANES)
    u3, v3 = pack_table(peer_u[LAYER]), pack_table(peer_v[LAYER])

    assert sum(CHUNK_ROWS) == B
    starts = [sum(CHUNK_ROWS[:c]) for c in range(len(CHUNK_ROWS))]
    nchunks = len(CHUNK_ROWS)
    outs = []
    pending = {}
    combined = {}
    x2 = x.reshape(B * L, D)
    for c in range(nchunks + CHUNK_LAG + FINAL_LAG):
        live = c < nchunks
        g1 = norm1_g[LAYER][None, :]
        if c - CHUNK_LAG - FINAL_LAG in combined:
            h_prev, po = combined.pop(c - CHUNK_LAG - FINAL_LAG)
            if live:
                g1, po = lax.optimization_barrier((g1, po))
            outs.append(_final(h_prev, po, final_g[None, :]).reshape(-1, L, D))
        if CHUNK_LAG <= c < nchunks + CHUNK_LAG:
            h, idx4, gate, hraw = pending.pop(c - CHUNK_LAG)
            wts = _peer_act(hraw.reshape(gate.shape[0], -1), gate)
            if live:
                g1, wts = lax.optimization_barrier((g1, wts))
            combined[c - CHUNK_LAG] = (h, _peer_combine(v3, idx4, wts.reshape(idx4.shape), gate.shape[0], D))
        if live:
            bc = CHUNK_ROWS[c]
            tc = bc * L
            zf, pb = _inproj(x2, g1, w_r, starts[c] * L, tc)
            zf3 = zf.reshape(bc, L, -1)
            pb3 = pb.reshape(bc, L, -1)
            o_h = _hgrn(zf3, pb3, hgrn_lb_fwd, hgrn_lb_bwd, hgrn_out_g[LAYER][None, :])
            o_d = _attn(pb3, bias_ext, lam, diff_out_g[LAYER][None, :])
            h, hn, idx, gate = _route(x2, starts[c] * L, o_h.reshape(tc, -1), o_d.reshape(tc, -1), wo,
                                      norm2_g[LAYER][None, :], wq, sk)
            idx4 = idx.reshape(-1, SC_GATHER_ROWS)
            pending[c] = (h, idx4, gate, _peer_hidden(u3, idx4, hn))
    return jnp.concatenate(outs, axis=0)
```

```python
import functools
import math

import numpy as np
import jax
import jax.numpy as jnp
from jax import lax
from jax.experimental import pallas as pl
from jax.experimental.pallas import tpu as pltpu
from jax.experimental.pallas import tpu_sc as plsc

F32 = jnp.float32
BF16 = jnp.bfloat16
EPS = 1e-6

HGRN_HEADS = 4
HGRN_DK = 128
DIFF_HEADS = 4
DIFF_HALF = 64
REL_BUCKETS = 32
REL_MAX_DIST = 128
PEER_HEADS = 8
PEER_NKEYS = 128
PEER_DKEY = 128
PEER_TOPK = 16
LAYER = 0

LANES = 128
SUBLANES = 8
VMEM_LIMIT = 48 * 1024 * 1024

INPROJ_TM = 512
HG_C = 64
HG_LEVELS = (64, 32, 16, 8, 4, 2)
ATTN_TQ = 256
ROUTE_TM = 256
BATCH_CHUNKS = 16
CHUNK_LAG = 2
FINAL_LAG = 1
PEER_ACT_TM = 2048
FINAL_TM = 512

SC_CORES = 2
SC_SUBCORES = 16
SC_LANES = 16
SC_WORKERS = SC_CORES * SC_SUBCORES
SC_GATHER_ROWS = 64
SC_TOKEN_GROUP = 32
SC_COMBINE_GROUP = 32
SC_RING = 2


def _dot(a, b):
    return jnp.dot(a, b, preferred_element_type=F32)


def _dot_nt(a, b):
    return lax.dot_general(a, b, (((1,), (1,)), ((), ())), preferred_element_type=F32)


def _dot_tn(a, b):
    return lax.dot_general(a, b, (((0,), (0,)), ((), ())), preferred_element_type=F32)


def _silu(x):
    return x * (1.0 / (1.0 + jnp.exp(-x)))


BF16_HI_MASK = -65536


def _pack_bf16_pair(hi, lo):
    bits = lambda a: lax.bitcast_convert_type(a.astype(BF16).astype(F32), jnp.int32)
    return (bits(hi) & BF16_HI_MASK) | lax.shift_right_logical(bits(lo), 16)


def _inproj_kernel(x_ref, g_ref, w_ref, zf_ref, pb_ref):
    x = x_ref[...]
    ms = jnp.mean(x * x, axis=-1, keepdims=True)
    xn = (x * lax.rsqrt(ms + EPS) * g_ref[...]).astype(BF16)
    hw = zf_ref.shape[1] // 2
    zf_ref[...] = _dot(xn, w_ref[:, hw:3 * hw])
    pb_ref[:, 0:hw] = _dot(xn, w_ref[:, 0:hw]).astype(BF16)
    for j in range(3, 8):
        y = _dot(xn, w_ref[:, j * hw:(j + 1) * hw])
        if j == 5:
            y = y * (DIFF_HALF ** -0.5)
        pb_ref[:, (j - 2) * hw:(j - 1) * hw] = y.astype(BF16)


def _inproj(x2, g, w, tok0, T):
    D = x2.shape[1]
    N = w.shape[1]
    nz = 2 * HGRN_HEADS * HGRN_DK
    tm = INPROJ_TM
    off = tok0 // tm
    return pl.pallas_call(
        _inproj_kernel,
        grid=(T // tm,),
        in_specs=[
            pl.BlockSpec((tm, D), lambda i: (i + off, 0)),
            pl.BlockSpec((1, D), lambda i: (0, 0)),
            pl.BlockSpec((D, N), lambda i: (0, 0)),
        ],
        out_specs=[
            pl.BlockSpec((tm, nz), lambda i: (i, 0)),
            pl.BlockSpec((tm, N - nz), lambda i: (i, 0)),
        ],
        out_shape=[
            jax.ShapeDtypeStruct((T, nz), F32),
            jax.ShapeDtypeStruct((T, N - nz), BF16),
        ],
        compiler_params=pltpu.CompilerParams(
            dimension_semantics=("parallel",), vmem_limit_bytes=VMEM_LIMIT),
        name="inproj",
    )(x2, g, w)


def _hgrn_consts():
    C = HG_C
    r = np.arange(C)
    t = r[:, None]
    u = r[None, :]
    blocks = [u <= t, u > t]
    masks = [np.eye(C, dtype=bool)]
    for B in HG_LEVELS:
        half = B // 2
        a = (r // B) * B
        m = (a + half - 1)[:, None]
        upper = (r - a) >= half
        blocks.append(np.where(upper[:, None], (u > m) & (u <= t), (u > t) & (u <= m)))
        same = a[:, None] == a[None, :]
        masks.append(same & upper[:, None] & (~upper)[None, :])
    m_f = np.concatenate(blocks, 0).astype(np.float32)
    m_b = np.concatenate([b[::-1, ::-1] for b in blocks], 0).astype(np.float32)
    k_f = np.stack(masks).astype(np.float32)
    k_b = np.ascontiguousarray(k_f.transpose(0, 2, 1))
    return m_f, m_b, k_f, k_b


def _hgrn_gates(z, tab):
    tabf = tab.astype(F32)
    e = jnp.exp(tabf - jnp.max(tabf, axis=0, keepdims=True))
    lb = jnp.sum(e[0:LAYER + 1], axis=0, keepdims=True) / jnp.sum(e, axis=0, keepdims=True)
    log_lb = jnp.log(lb)
    log_1m = jnp.log1p(-lb)
    ez = jnp.exp(-jnp.abs(z))
    l1p = jnp.log1p(ez)
    log_sig = jnp.minimum(z, 0.0) - l1p
    c = log_1m + log_sig
    hi = jnp.maximum(log_lb, c)
    lo = jnp.minimum(log_lb, c)
    log_f = hi + jnp.log1p(jnp.exp(lo - hi))
    sig_neg = jnp.where(z >= 0.0, ez, 1.0) / (1.0 + ez)
    k = (1.0 - lb) * sig_neg
    return log_f, k


def _hgrn_kernel(lbf_ref, lbb_ref, og_ref, mf_ref, mb_ref, kf_ref, kb_ref,
                 q_ref, zf_ref, zb_ref, v_ref, g_ref, o_ref, of_s, ob_s):
    C = HG_C
    L = q_ref.shape[0]
    n = L // C
    dv = v_ref.shape[1]

    def chunk(c, st, m_ref, k_ref, z_ref, tab_ref, forward):
        sl = pl.ds(pl.multiple_of(c * C, C), C)
        qh = q_ref[sl, :].astype(F32)
        q = _silu(qh)
        v = v_ref[sl, :]
        log_f, k = _hgrn_gates(z_ref[sl, :], tab_ref[...])
        lf_hi = log_f.astype(BF16)
        lf_lo = (log_f - lf_hi.astype(F32)).astype(BF16)
        m = m_ref[...]
        e = jnp.exp(_dot(m, lf_hi) + _dot(m, lf_lo))
        e_b = e[0:C]
        e_s = e[C:2 * C]
        dec = e[C - 1:C] if forward else e[0:1]
        o = _dot_nt((q * e_b).astype(BF16), st.astype(BF16))
        a = _dot_nt(q.astype(BF16), k.astype(BF16)) * k_ref[0]
        for l in range(len(HG_LEVELS)):
            e_l = e[(2 + l) * C:(3 + l) * C]
            a = a + _dot_nt((q * e_l).astype(BF16), (k * e_l).astype(BF16)) * k_ref[l + 1]
        o = o + _dot(a.astype(BF16), v)
        st = st * dec + _dot_tn(v, (k * e_s).astype(BF16))
        return sl, o, st

    st0 = jnp.zeros((dv, q_ref.shape[1]), F32)

    def both(i, carry):
        st_f, st_b = carry
        sl_f, o_f, st_f = chunk(i, st_f, mf_ref, kf_ref, zf_ref, lbf_ref, True)
        sl_b, o_b, st_b = chunk(n - 1 - i, st_b, mb_ref, kb_ref, zb_ref, lbb_ref, False)
        of_s[sl_f, :] = o_f
        ob_s[sl_b, :] = o_b
        return st_f, st_b

    lax.fori_loop(0, n, both, (st0, st0))

    def finish(c, _):
        sl = pl.ds(pl.multiple_of(c * C, C), C)
        tot = of_s[sl, :] + ob_s[sl, :]
        ms = jnp.mean(tot * tot, axis=-1, keepdims=True)
        y = tot * lax.rsqrt(ms + EPS) * og_ref[...]
        o_ref[sl, :] = (y * _silu(g_ref[sl, :].astype(F32))).astype(o_ref.dtype)
        return 0

    lax.fori_loop(0, n, finish, 0)


def _hgrn(zf3, pb3, lb_f, lb_b, out_g):
    B, L, _ = zf3.shape
    H, dk = HGRN_HEADS, HGRN_DK
    m_f, m_b, k_f, k_b = _hgrn_consts()
    nlev = k_f.shape[0]
    full2 = lambda b, h: (0, 0)
    full3 = lambda b, h: (0, 0, 0)
    seq = lambda off: pl.BlockSpec((None, L, dk), lambda b, h: (b, 0, off + h))
    return pl.pallas_call(
        _hgrn_kernel,
        grid=(B, H),
        in_specs=[
            pl.BlockSpec((lb_f.shape[0], dk), lambda b, h: (0, h)),
            pl.BlockSpec((lb_b.shape[0], dk), lambda b, h: (0, h)),
            pl.BlockSpec((1, dk), full2),
            pl.BlockSpec(m_f.shape, full2),
            pl.BlockSpec(m_b.shape, full2),
            pl.BlockSpec((nlev, HG_C, HG_C), full3),
            pl.BlockSpec((nlev, HG_C, HG_C), full3),
            seq(0),
            seq(0),
            seq(H),
            seq(H),
            seq(2 * H),
        ],
        out_specs=pl.BlockSpec((None, L, dk), lambda b, h: (b, 0, h)),
        out_shape=jax.ShapeDtypeStruct((B, L, H * dk), BF16),
        scratch_shapes=[pltpu.VMEM((L, dk), F32), pltpu.VMEM((L, dk), F32)],
        compiler_params=pltpu.CompilerParams(
            dimension_semantics=("parallel", "parallel"), vmem_limit_bytes=VMEM_LIMIT),
        name="hgrn",
    )(lb_f, lb_b, out_g, jnp.asarray(m_f, BF16), jnp.asarray(m_b, BF16),
      jnp.asarray(k_f), jnp.asarray(k_b), pb3, zf3, zf3, pb3, pb3)


def _t5_bucket(rel):
    nb = REL_BUCKETS // 2
    ret = jnp.where(rel > 0, nb, 0)
    n = jnp.abs(rel)
    max_exact = nb // 2
    nf = jnp.maximum(n, 1).astype(jnp.float32)
    large = max_exact + (jnp.log(nf / max_exact) / math.log(REL_MAX_DIST / max_exact)
                         * (nb - max_exact)).astype(jnp.int32)
    large = jnp.minimum(large, nb - 1)
    return ret + jnp.where(n < max_exact, n, large)


def _rel_bias_ext(rel_bias, L):
    j = jnp.arange(2 * L - LANES, dtype=jnp.int32)
    ql = jnp.arange(LANES, dtype=jnp.int32)
    rel = j[None, :] - (L - LANES) - ql[:, None]
    bucket = _t5_bucket(rel)
    tab = rel_bias.astype(F32)
    out = jnp.zeros((tab.shape[1],) + bucket.shape, F32)
    for b in range(REL_BUCKETS):
        out = jnp.where((bucket == b)[None], tab[b][:, None, None], out)
    return out


def _attn_kernel(lam_ref, q_ref, k_ref, v_ref, bias_ref, og_ref, o_ref, *, lam_init):
    tq = q_ref.shape[0]
    L = k_ref.shape[0]
    qi = pl.program_id(2)
    lam = lam_ref[0, 0]
    q = q_ref[...]
    k = k_ref[...]
    lane = lax.broadcasted_iota(jnp.int32, q.shape, 1)
    zero = jnp.zeros_like(q)
    q0 = jnp.where(lane < DIFF_HALF, q, zero)
    q1 = jnp.where(lane >= DIFF_HALF, q, zero)
    parts = []
    for j in range(tq // LANES):
        off = pl.multiple_of(L - LANES - (qi * tq + j * LANES), LANES)
        parts.append(bias_ref[:, pl.ds(off, L)])
    bias = jnp.concatenate(parts, axis=0) if len(parts) > 1 else parts[0]

    def soft(qm):
        s = _dot_nt(qm, k) + bias
        e = jnp.exp(s - jnp.max(s, axis=-1, keepdims=True))
        return e, 1.0 / jnp.sum(e, axis=-1, keepdims=True)

    e0, r0 = soft(q0)
    e1, r1 = soft(q1)
    w = e0 * r0 - e1 * (lam * r1)
    o = _dot(w.astype(BF16), v_ref[...])
    ms = jnp.mean(o * o, axis=-1, keepdims=True)
    y = o * lax.rsqrt(ms + EPS) * og_ref[...] * (1.0 - lam_init)
    o_ref[...] = y.astype(o_ref.dtype)


def _attn(pb3, bias_ext, lam, out_g):
    B, L, _ = pb3.shape
    H, dh = DIFF_HEADS, 2 * DIFF_HALF
    tq = ATTN_TQ
    base = (3 * HGRN_HEADS * HGRN_DK) // dh
    lam_init = 0.8 - 0.6 * math.exp(-0.3 * LAYER)
    return pl.pallas_call(
        functools.partial(_attn_kernel, lam_init=lam_init),
        grid=(B, H, L // tq),
        in_specs=[
            pl.BlockSpec(memory_space=pltpu.SMEM),
            pl.BlockSpec((None, tq, dh), lambda b, h, i: (b, i, base + h)),
            pl.BlockSpec((None, L, dh), lambda b, h, i: (b, 0, base + H + h)),
            pl.BlockSpec((None, L, dh), lambda b, h, i: (b, 0, base + 2 * H + h)),
            pl.BlockSpec((None, LANES, 2 * L - LANES), lambda b, h, i: (h, 0, 0)),
            pl.BlockSpec((1, dh), lambda b, h, i: (0, 0)),
        ],
        out_specs=pl.BlockSpec((None, tq, dh), lambda b, h, i: (b, i, h)),
        out_shape=jax.ShapeDtypeStruct((B, L, H * dh), BF16),
        compiler_params=pltpu.CompilerParams(
            dimension_semantics=("parallel", "parallel", "parallel"),
            vmem_limit_bytes=VMEM_LIMIT),
        name="attn",
    )(lam, pb3, pb3, pb3, bias_ext, out_g)


def _cand_layout():
    K = PEER_TOPK
    groups = [("a", 0, 0), ("a", 0, 8), ("a", 1, 0), ("a", 2, 0), ("a", 3, 0),
              ("b", 0, 8), ("b", 0, 0), ("b", 1, 0), ("b", 2, 0)]
    seen = set()
    pos, valid = [], []
    for kind, fixed, start in groups:
        for r in range(SUBLANES):
            a, b = (fixed, start + r) if kind == "a" else (start + r, fixed)
            ok = (a + 1) * (b + 1) <= K and (a, b) not in seen
            if ok:
                seen.add((a, b))
            pos.append(a * K + b if ok else K * K + len(pos))
            valid.append(ok)
    assert len(seen) == sum(K // (a + 1) for a in range(K))
    return groups, np.array(pos, np.int32), np.array(valid, bool)


def _extract_topk(s, key, payload, k, big):
    n, tm = s.shape
    S = SUBLANES
    s3 = s.reshape(n // S, S, tm)
    key3 = key.reshape(n // S, S, tm)
    pay3 = None if payload is key else payload.reshape(n // S, S, tm)
    slot = lax.broadcasted_iota(jnp.int32, (k // S, S, tm), 0) * S + lax.broadcasted_iota(
        jnp.int32, (k // S, S, tm), 1)

    def all_reduce(x3, op):
        r = x3[0]
        for g in range(1, x3.shape[0]):
            r = op(r, x3[g])
        for sh in (S // 2, S // 4, S // 8):
            r = op(r, pltpu.roll(r, sh, axis=0))
        return r

    def body(j, carry):
        s3, vals, pay = carry
        m = all_reduce(s3, jnp.maximum)
        kk = all_reduce(jnp.where(s3 == m[None], key3, big), jnp.minimum)
        sel = key3 == kk[None]
        p = kk if pay3 is None else all_reduce(jnp.where(sel, pay3, 0), jnp.add)
        vals = jnp.where(slot == j, m[None], vals)
        pay = jnp.where(slot == j, p[None], pay)
        s3 = jnp.where(sel, -jnp.inf, s3)
        return s3, vals, pay

    init = (s3, jnp.zeros((k // S, S, tm), F32), jnp.zeros((k // S, S, tm), jnp.int32))
    _, vals, pay = lax.fori_loop(0, k, body, init)
    return vals.reshape(k, tm), pay.reshape(k, tm)


def _route_kernel(x_ref, oh_ref, od_ref, woh_ref, wod_ref, g2_ref, wq_ref, sk_ref, cpos_ref, cmask_ref,
                  h_ref, hn_ref, idx_ref, gate_ref, q_s, tv_s, ti_s, gs_s, is_s, *, groups):
    K = PEER_TOPK
    tm = x_ref.shape[0]
    h = x_ref[...] + _dot(oh_ref[...], woh_ref[...]) + _dot(od_ref[...], wod_ref[...])
    h_ref[...] = h
    hn = h * lax.rsqrt(jnp.mean(h * h, axis=-1, keepdims=True) + EPS) * g2_ref[...]
    half = hn.shape[1] // 2
    hn_ref[...] = _pack_bf16_pair(hn[:, :half], hn[:, half:])
    q_s[...] = _dot(hn.astype(BF16), wq_ref[...]).astype(BF16)

    key_iota = lax.broadcasted_iota(jnp.int32, (PEER_NKEYS, tm), 0)

    def half_topk(hp, _):
        col = pl.multiple_of(hp * PEER_DKEY, PEER_DKEY)
        s = _dot_nt(sk_ref[hp], q_s[:, pl.ds(col, PEER_DKEY)])
        vals, idxs = _extract_topk(s, key_iota, key_iota, K, PEER_NKEYS)
        tv_s[hp] = vals
        ti_s[hp] = idxs
        return 0

    lax.fori_loop(0, 2 * PEER_HEADS, half_topk, 0)

    cpos = cpos_ref[...]
    cmask = cmask_ref[...]

    def head(hd, _):
        s0, s1 = tv_s[2 * hd], tv_s[2 * hd + 1]
        i0, i1 = ti_s[2 * hd] * PEER_NKEYS, ti_s[2 * hd + 1]
        cs, ci = [], []
        for kind, fixed, start in groups:
            if kind == "a":
                cs.append(s0[fixed:fixed + 1] + s1[start:start + SUBLANES])
                ci.append(i0[fixed:fixed + 1] + i1[start:start + SUBLANES])
            else:
                cs.append(s0[start:start + SUBLANES] + s1[fixed:fixed + 1])
                ci.append(i0[start:start + SUBLANES] + i1[fixed:fixed + 1])
        cand = jnp.concatenate(cs, axis=0) + cmask
        cidx = jnp.concatenate(ci, axis=0)
        best, eidx = _extract_topk(cand, cpos, cidx, K, 2 * K * K)
        ex = jnp.exp(best - best[0:1])
        gate = ex / jnp.sum(ex, axis=0, keepdims=True)
        row = pl.ds(pl.multiple_of(hd * K, K), K)
        gs_s[row, :] = gate
        is_s[row, :] = eidx.astype(F32)
        return 0

    lax.fori_loop(0, PEER_HEADS, head, 0)
    gate_ref[...] = gs_s[...].T
    idx_ref[...] = is_s[...].T.astype(jnp.int32)


def _route(x2, tok0, oh2, od2, w_out, g2, w_q, sub_keys):
    T = oh2.shape[0]
    D = x2.shape[1]
    tm = ROUTE_TM
    off = tok0 // tm
    K = PEER_TOPK
    nh = oh2.shape[1]
    nq = w_q.shape[1]
    npk = PEER_HEADS * K
    groups, pos, valid = _cand_layout()
    ncand = pos.shape[0]
    cpos = jnp.asarray(np.broadcast_to(pos[:, None], (ncand, tm)))
    cmask = jnp.asarray(np.broadcast_to(np.where(valid, 0.0, -np.inf).astype(np.float32)[:, None], (ncand, tm)))
    row = lambda i: (i, 0)
    full2 = lambda i: (0, 0)
    return pl.pallas_call(
        functools.partial(_route_kernel, groups=groups),
        grid=(T // tm,),
        in_specs=[
            pl.BlockSpec((tm, D), lambda i: (i + off, 0)),
            pl.BlockSpec((tm, nh), row),
            pl.BlockSpec((tm, nh), row),
            pl.BlockSpec((nh, D), full2),
            pl.BlockSpec((nh, D), lambda i: (1, 0)),
            pl.BlockSpec((1, D), full2),
            pl.BlockSpec((D, nq), full2),
            pl.BlockSpec(sub_keys.shape, lambda i: (0, 0, 0)),
            pl.BlockSpec((ncand, tm), full2),
            pl.BlockSpec((ncand, tm), full2),
        ],
        out_specs=[
            pl.BlockSpec((tm, D), row),
            pl.BlockSpec((tm, D // 2), row),
            pl.BlockSpec((tm, npk), row),
            pl.BlockSpec((tm, npk), row),
        ],
        out_shape=[
            jax.ShapeDtypeStruct((T, D), F32),
            jax.ShapeDtypeStruct((T, D // 2), jnp.int32),
            jax.ShapeDtypeStruct((T, npk), jnp.int32),
            jax.ShapeDtypeStruct((T, npk), F32),
        ],
        scratch_shapes=[
            pltpu.VMEM((tm, nq), BF16),
            pltpu.VMEM((2 * PEER_HEADS, K, tm), F32),
            pltpu.VMEM((2 * PEER_HEADS, K, tm), jnp.int32),
            pltpu.VMEM((npk, tm), F32),
            pltpu.VMEM((npk, tm), F32),
        ],
        compiler_params=pltpu.CompilerParams(
            dimension_semantics=("parallel",), vmem_limit_bytes=VMEM_LIMIT),
        name="route",
    )(x2, oh2, od2, w_out, w_out, g2, w_q, sub_keys, cpos, cmask)


def _gelu(x):
    return 0.5 * x * (1.0 + lax.erf(x * (1.0 / math.sqrt(2.0))))


def _sc_mesh():
    return plsc.VectorSubcoreMesh(core_axis_name="c", subcore_axis_name="s")


def _sc_worker_id():
    return lax.axis_index("s") * SC_CORES + lax.axis_index("c")


def _sc_widen_pair(s):
    si = plsc.bitcast(s, jnp.int32)
    return plsc.bitcast(si & BF16_HI_MASK, F32), plsc.bitcast(si << 16, F32)


def _sc_gather_loop(tab_hbm, idx_v, bufs, sems, n_items, compute):
    nb = len(bufs)

    def start(item, b):
        pltpu.async_copy(tab_hbm.at[idx_v.at[item]], bufs[b], sems[b])

    def wait(b):
        pltpu.make_async_copy(tab_hbm.at[idx_v.at[0]], bufs[b], sems[b]).wait()

    for p in range(nb - 1):
        start(p, p)

    @pl.loop(0, pl.cdiv(n_items, nb))
    def _(i):
        for b in range(nb):
            it = i * nb + b

            @pl.when(it < n_items)
            def _():
                @pl.when(it + nb - 1 < n_items)
                def _():
                    start(it + nb - 1, (b + nb - 1) % nb)

                wait(b)
                compute(it, bufs[b])


def _peer_hidden(u, idx4, x):
    T, DW = x.shape
    n_rows, R = idx4.shape
    ipt = n_rows // T
    G = SC_TOKEN_GROUP
    nit = G * ipt
    tpw = T // SC_WORKERS
    NL = SC_LANES
    RB = 8
    NV = 4
    GW = NV * NL
    tile = (R, DW // LANES, LANES)

    @functools.partial(
        pl.kernel, mesh=_sc_mesh(), compiler_params=pltpu.CompilerParams(needs_layout_passes=False),
        out_type=jax.ShapeDtypeStruct((n_rows, R), F32),
        scratch_types=[pltpu.VMEM((nit, R), jnp.int32), pltpu.VMEM((G, DW), jnp.int32),
                       [pltpu.VMEM(tile, jnp.int32)] * SC_RING, [pltpu.SemaphoreType.DMA] * SC_RING,
                       pltpu.VMEM((nit, R), F32)],
        name="peer_hidden")
    def k(u_hbm, idx_hbm, x_hbm, h_hbm, idx_v, x_v, bufs, sems, h_v):
        tok0 = _sc_worker_id() * tpw
        lane = lax.broadcasted_iota(jnp.int32, (NL,), 0)
        zero = jnp.zeros((NL,), F32)

        def compute(item, buf):
            tok = item // ipt
            hvs = [zero for _ in range(R // NL)]
            for rb in range(R // RB):
                def body(g, accs):
                    sub = g // (LANES // GW)
                    base = (g % (LANES // GW)) * GW
                    xs = [plsc.bitcast(x_v[tok, pl.ds(pl.multiple_of(g * GW + jj * NL, NL), NL)], BF16)
                          for jj in range(NV)]
                    out = []
                    for j in range(RB):
                        ps = [plsc.bitcast(buf[rb * RB + j, sub, pl.ds(pl.multiple_of(base + jj * NL, NL), NL)],
                                           BF16) * xs[jj] for jj in range(NV)]
                        hi, lo = _sc_widen_pair((ps[0] + ps[1]) + (ps[2] + ps[3]))
                        out.append((accs[j] + hi) + lo)
                    return tuple(out)
                accs = lax.fori_loop(0, DW // GW, body, tuple(zero for _ in range(RB)))
                for j in range(RB):
                    r = rb * RB + j
                    hvs[r // NL] = jnp.where(lane == (r % NL), jnp.sum(accs[j]), hvs[r // NL])
            for q in range(R // NL):
                h_v[item, pl.ds(q * NL, NL)] = hvs[q]

        @pl.loop(0, tpw // G)
        def _(g):
            t0 = tok0 + g * G
            pltpu.sync_copy(idx_hbm.at[pl.ds(t0 * ipt, nit)], idx_v)
            pltpu.sync_copy(x_hbm.at[pl.ds(t0, G)], x_v)
            _sc_gather_loop(u_hbm, idx_v, bufs, sems, nit, compute)
            pltpu.sync_copy(h_v, h_hbm.at[pl.ds(t0 * ipt, nit)])

    return k(u, idx4, x)


def _peer_combine(v, idx4, w4, T, D):
    n_rows, R = idx4.shape
    ipt = n_rows // T
    G = SC_COMBINE_GROUP
    nit = G * ipt
    tpw = T // SC_WORKERS
    NL = SC_LANES
    DW = D // 2
    half = DW // 2
    nv = half // NL
    RG = 4
    tile = (R, DW // LANES, LANES)

    @functools.partial(
        pl.kernel, mesh=_sc_mesh(), compiler_params=pltpu.CompilerParams(needs_layout_passes=False),
        out_type=jax.ShapeDtypeStruct((T, D), F32),
        scratch_types=[pltpu.VMEM((nit, R), jnp.int32), pltpu.VMEM((nit, R), jnp.int32),
                       [pltpu.VMEM(tile, jnp.int32)] * SC_RING, [pltpu.SemaphoreType.DMA] * SC_RING,
                       pltpu.VMEM((G, D), F32)],
        name="peer_combine")
    def k(v_hbm, idx_hbm, w_hbm, o_hbm, idx_v, w_v, bufs, sems, out_v):
        tok0 = _sc_worker_id() * tpw
        zero = jnp.zeros((NL,), F32)

        def compute(item, buf):
            tok = item // ipt
            item_vec = jnp.full((NL,), item, jnp.int32)
            for hf in range(2):
                def body(rg, accs):
                    ws = [plsc.bitcast(plsc.load_gather(
                        w_v, [item_vec, jnp.full((NL,), rg * RG + rr, jnp.int32)]), BF16) for rr in range(RG)]
                    his, los = [], []
                    for i in range(nv):
                        word = hf * half + i * NL
                        ps = [plsc.bitcast(buf[rg * RG + rr, word // LANES, pl.ds(word % LANES, NL)], BF16) * ws[rr]
                              for rr in range(RG)]
                        hi, lo = _sc_widen_pair((ps[0] + ps[1]) + (ps[2] + ps[3]))
                        his.append(accs[i] + hi)
                        los.append(accs[nv + i] + lo)
                    return tuple(his + los)
                accs = lax.fori_loop(0, R // RG, body, tuple(zero for _ in range(2 * nv)))
                for i in range(nv):
                    word = hf * half + i * NL
                    plsc.addupdate(out_v.at[tok, pl.ds(word, NL)], accs[i])
                    plsc.addupdate(out_v.at[tok, pl.ds(DW + word, NL)], accs[nv + i])

        @pl.loop(0, tpw // G)
        def _(g):
            t0 = tok0 + g * G
            pltpu.sync_copy(idx_hbm.at[pl.ds(t0 * ipt, nit)], idx_v)
            pltpu.sync_copy(w_hbm.at[pl.ds(t0 * ipt, nit)], w_v)

            @pl.loop(0, G)
            def _(t):
                @pl.loop(0, D // NL)
                def _(i):
                    out_v[t, pl.ds(pl.multiple_of(i * NL, NL), NL)] = zero

            _sc_gather_loop(v_hbm, idx_v, bufs, sems, nit, compute)
            pltpu.sync_copy(out_v, o_hbm.at[pl.ds(t0, G)])

    return k(v, idx4, w4)


def _act_kernel(h_ref, g_ref, w_ref):
    w = _gelu(h_ref[...]) * g_ref[...]
    w_ref[...] = _pack_bf16_pair(w, w)


def _peer_act(hraw, gate):
    T, n = gate.shape
    tm = PEER_ACT_TM
    spec = pl.BlockSpec((tm, n), lambda i: (i, 0))
    return pl.pallas_call(
        _act_kernel, grid=(T // tm,), in_specs=[spec, spec], out_specs=spec,
        out_shape=jax.ShapeDtypeStruct((T, n), jnp.int32),
        compiler_params=pltpu.CompilerParams(dimension_semantics=("parallel",)),
        name="peer_act",
    )(hraw, gate)


def _final_kernel(h_ref, p_ref, g_ref, y_ref):
    y = h_ref[...] + p_ref[...]
    ms = jnp.mean(y * y, axis=-1, keepdims=True)
    y_ref[...] = y * lax.rsqrt(ms + EPS) * g_ref[...]


def _final(h, po, g):
    T, D = h.shape
    tm = FINAL_TM
    spec = pl.BlockSpec((tm, D), lambda i: (i, 0))
    return pl.pallas_call(
        _final_kernel, grid=(T // tm,),
        in_specs=[spec, spec, pl.BlockSpec((1, D), lambda i: (0, 0))], out_specs=spec,
        out_shape=jax.ShapeDtypeStruct((T, D), F32),
        compiler_params=pltpu.CompilerParams(dimension_semantics=("parallel",)),
        name="final_norm",
    )(h, po, g)


def kernel(x, norm1_g, w_in, hgrn_lb_fwd, hgrn_lb_bwd, hgrn_out_g, diff_lam_q1, diff_lam_k1,
           diff_lam_q2, diff_lam_k2, diff_out_g, rel_bias, w_out, norm2_g, peer_w_q,
           peer_sub_keys, peer_u, peer_v, final_g):
    B, L, D = x.shape

    w_r = w_in[LAYER].astype(BF16)
    f32 = jnp.float32
    lam_init = 0.8 - 0.6 * math.exp(-0.3 * LAYER)
    lam = (jnp.exp(jnp.sum(diff_lam_q1[LAYER].astype(f32) * diff_lam_k1[LAYER].astype(f32)))
           - jnp.exp(jnp.sum(diff_lam_q2[LAYER].astype(f32) * diff_lam_k2[LAYER].astype(f32))) + lam_init)
    lam = lam.reshape(1, 1)
    bias_ext = _rel_bias_ext(rel_bias, L)
    sk = peer_sub_keys[LAYER].reshape(2 * PEER_HEADS, PEER_NKEYS, PEER_DKEY).astype(BF16)
    wq = peer_w_q[LAYER].reshape(D, -1).astype(BF16)
    wo = w_out[LAYER].astype(BF16)
    def pack_table(tab):
        words = _pack_bf16_pair(tab[:, :D // 2], tab[:, D // 2:])
        return words.reshape(tab.shape[0], D // 2 // LANES, LANES)
    u3, v3 = pack_table(peer_u[LAYER]), pack_table(peer_v[LAYER])

    bc = B // BATCH_CHUNKS
    tc = bc * L
    rows = tc * PEER_HEADS * PEER_TOPK // SC_GATHER_ROWS
    outs = []
    pending = {}
    combined = {}
    x2 = x.reshape(B * L, D)
    for c in range(BATCH_CHUNKS + CHUNK_LAG + FINAL_LAG):
        live = c < BATCH_CHUNKS
        tie = c < BATCH_CHUNKS - 1
        g1 = norm1_g[LAYER][None, :]
        if c - CHUNK_LAG - FINAL_LAG in combined:
            h_prev, po = combined.pop(c - CHUNK_LAG - FINAL_LAG)
            if tie:
                g1, po = lax.optimization_barrier((g1, po))
            outs.append(_final(h_prev, po, final_g[None, :]).reshape(bc, L, D))
        if CHUNK_LAG <= c < BATCH_CHUNKS + CHUNK_LAG:
            h, idx4, gate, hraw = pending.pop(c - CHUNK_LAG)
            wts = _peer_act(hraw.reshape(tc, -1), gate)
            if tie:
                g1, wts = lax.optimization_barrier((g1, wts))
            combined[c - CHUNK_LAG] = (h, _peer_combine(v3, idx4, wts.reshape(rows, SC_GATHER_ROWS), tc, D))
        if live:
            zf, pb = _inproj(x2, g1, w_r, c * tc, tc)
            zf3 = zf.reshape(bc, L, -1)
            pb3 = pb.reshape(bc, L, -1)
            o_h = _hgrn(zf3, pb3, hgrn_lb_fwd, hgrn_lb_bwd, hgrn_out_g[LAYER][None, :])
            o_d = _attn(pb3, bias_ext, lam, diff_out_g[LAYER][None, :])
            h, hn, idx, gate = _route(x2, c * tc, o_h.reshape(tc, -1), o_d.reshape(tc, -1), wo,
                                      norm2_g[LAYER][None, :], wq, sk)
            idx4 = idx.reshape(rows, SC_GATHER_ROWS)
            pending[c] = (h, idx4, gate, _peer_hidden(u3, idx4, hn))
    return jnp.concatenate(outs, axis=0)
```

```python
import functools
import math

import numpy as np
import jax
import jax.numpy as jnp
from jax import lax
from jax.experimental import pallas as pl
from jax.experimental.pallas import tpu as pltpu
from jax.experimental.pallas import tpu_sc as plsc

F32 = jnp.float32
BF16 = jnp.bfloat16
EPS = 1e-6

HGRN_HEADS = 4
HGRN_DK = 128
DIFF_HEADS = 4
DIFF_HALF = 64
REL_BUCKETS = 32
REL_MAX_DIST = 128
PEER_HEADS = 8
PEER_NKEYS = 128
PEER_DKEY = 128
PEER_TOPK = 16
LAYER = 0

LANES = 128
SUBLANES = 8
VMEM_LIMIT = 48 * 1024 * 1024

INPROJ_TM = 512
HG_C = 64
HG_LEVELS = (64, 32, 16, 8, 4, 2)
ATTN_TQ = 256
ROUTE_TM = 256
BATCH_CHUNKS = 16
CHUNK_LAG = 2
FINAL_LAG = 1
PEER_ACT_TM = 2048
FINAL_TM = 512

SC_CORES = 2
SC_SUBCORES = 16
SC_LANES = 16
SC_WORKERS = SC_CORES * SC_SUBCORES
SC_GATHER_ROWS = 64
SC_TOKEN_GROUP = 32
SC_COMBINE_GROUP = 16
SC_RING = 2


def _dot(a, b):
    return jnp.dot(a, b, preferred_element_type=F32)


def _dot_nt(a, b):
    return lax.dot_general(a, b, (((1,), (1,)), ((), ())), preferred_element_type=F32)


def _dot_tn(a, b):
    return lax.dot_general(a, b, (((0,), (0,)), ((), ())), preferred_element_type=F32)


def _silu(x):
    return x * (1.0 / (1.0 + jnp.exp(-x)))


BF16_HI_MASK = -65536


def _pack_bf16_pair(hi, lo):
    bits = lambda a: lax.bitcast_convert_type(a.astype(BF16).astype(F32), jnp.int32)
    return (bits(hi) & BF16_HI_MASK) | lax.shift_right_logical(bits(lo), 16)


def _inproj_kernel(x_ref, g_ref, w_ref, zf_ref, pb_ref):
    x = x_ref[...]
    ms = jnp.mean(x * x, axis=-1, keepdims=True)
    xn = (x * lax.rsqrt(ms + EPS) * g_ref[...]).astype(BF16)
    hw = zf_ref.shape[1] // 2
    zf_ref[...] = _dot(xn, w_ref[:, hw:3 * hw])
    pb_ref[:, 0:hw] = _dot(xn, w_ref[:, 0:hw]).astype(BF16)
    for j in range(3, 8):
        y = _dot(xn, w_ref[:, j * hw:(j + 1) * hw])
        if j == 5:
            y = y * (DIFF_HALF ** -0.5)
        pb_ref[:, (j - 2) * hw:(j - 1) * hw] = y.astype(BF16)


def _inproj(x2, g, w, tok0, T):
    D = x2.shape[1]
    N = w.shape[1]
    nz = 2 * HGRN_HEADS * HGRN_DK
    tm = INPROJ_TM
    off = tok0 // tm
    return pl.pallas_call(
        _inproj_kernel,
        grid=(T // tm,),
        in_specs=[
            pl.BlockSpec((tm, D), lambda i: (i + off, 0)),
            pl.BlockSpec((1, D), lambda i: (0, 0)),
            pl.BlockSpec((D, N), lambda i: (0, 0)),
        ],
        out_specs=[
            pl.BlockSpec((tm, nz), lambda i: (i, 0)),
            pl.BlockSpec((tm, N - nz), lambda i: (i, 0)),
        ],
        out_shape=[
            jax.ShapeDtypeStruct((T, nz), F32),
            jax.ShapeDtypeStruct((T, N - nz), BF16),
        ],
        compiler_params=pltpu.CompilerParams(
            dimension_semantics=("parallel",), vmem_limit_bytes=VMEM_LIMIT),
        name="inproj",
    )(x2, g, w)


def _hgrn_consts():
    C = HG_C
    r = np.arange(C)
    t = r[:, None]
    u = r[None, :]
    blocks = [u <= t, u > t]
    masks = [np.eye(C, dtype=bool)]
    for B in HG_LEVELS:
        half = B // 2
        a = (r // B) * B
        m = (a + half - 1)[:, None]
        upper = (r - a) >= half
        blocks.append(np.where(upper[:, None], (u > m) & (u <= t), (u > t) & (u <= m)))
        same = a[:, None] == a[None, :]
        masks.append(same & upper[:, None] & (~upper)[None, :])
    m_f = np.concatenate(blocks, 0).astype(np.float32)
    m_b = np.concatenate([b[::-1, ::-1] for b in blocks], 0).astype(np.float32)
    k_f = np.stack(masks).astype(np.float32)
    k_b = np.ascontiguousarray(k_f.transpose(0, 2, 1))
    return m_f, m_b, k_f, k_b


def _hgrn_gates(z, tab):
    tabf = tab.astype(F32)
    e = jnp.exp(tabf - jnp.max(tabf, axis=0, keepdims=True))
    lb = jnp.sum(e[0:LAYER + 1], axis=0, keepdims=True) / jnp.sum(e, axis=0, keepdims=True)
    log_lb = jnp.log(lb)
    log_1m = jnp.log1p(-lb)
    ez = jnp.exp(-jnp.abs(z))
    l1p = jnp.log1p(ez)
    log_sig = jnp.minimum(z, 0.0) - l1p
    c = log_1m + log_sig
    hi = jnp.maximum(log_lb, c)
    lo = jnp.minimum(log_lb, c)
    log_f = hi + jnp.log1p(jnp.exp(lo - hi))
    sig_neg = jnp.where(z >= 0.0, ez, 1.0) / (1.0 + ez)
    k = (1.0 - lb) * sig_neg
    return log_f, k


def _hgrn_kernel(lbf_ref, lbb_ref, og_ref, mf_ref, mb_ref, kf_ref, kb_ref,
                 q_ref, zf_ref, zb_ref, v_ref, g_ref, o_ref, of_s, ob_s):
    C = HG_C
    L = q_ref.shape[0]
    n = L // C
    dv = v_ref.shape[1]

    def chunk(c, st, m_ref, k_ref, z_ref, tab_ref, forward):
        sl = pl.ds(pl.multiple_of(c * C, C), C)
        qh = q_ref[sl, :].astype(F32)
        q = _silu(qh)
        v = v_ref[sl, :]
        log_f, k = _hgrn_gates(z_ref[sl, :], tab_ref[...])
        lf_hi = log_f.astype(BF16)
        lf_lo = (log_f - lf_hi.astype(F32)).astype(BF16)
        m = m_ref[...]
        e = jnp.exp(_dot(m, lf_hi) + _dot(m, lf_lo))
        e_b = e[0:C]
        e_s = e[C:2 * C]
        dec = e[C - 1:C] if forward else e[0:1]
        o = _dot_nt((q * e_b).astype(BF16), st.astype(BF16))
        a = _dot_nt(q.astype(BF16), k.astype(BF16)) * k_ref[0]
        for l in range(len(HG_LEVELS)):
            e_l = e[(2 + l) * C:(3 + l) * C]
            a = a + _dot_nt((q * e_l).astype(BF16), (k * e_l).astype(BF16)) * k_ref[l + 1]
        o = o + _dot(a.astype(BF16), v)
        st = st * dec + _dot_tn(v, (k * e_s).astype(BF16))
        return sl, o, st

    st0 = jnp.zeros((dv, q_ref.shape[1]), F32)

    def both(i, carry):
        st_f, st_b = carry
        sl_f, o_f, st_f = chunk(i, st_f, mf_ref, kf_ref, zf_ref, lbf_ref, True)
        sl_b, o_b, st_b = chunk(n - 1 - i, st_b, mb_ref, kb_ref, zb_ref, lbb_ref, False)
        of_s[sl_f, :] = o_f
        ob_s[sl_b, :] = o_b
        return st_f, st_b

    lax.fori_loop(0, n, both, (st0, st0))

    def finish(c, _):
        sl = pl.ds(pl.multiple_of(c * C, C), C)
        tot = of_s[sl, :] + ob_s[sl, :]
        ms = jnp.mean(tot * tot, axis=-1, keepdims=True)
        y = tot * lax.rsqrt(ms + EPS) * og_ref[...]
        o_ref[sl, :] = (y * _silu(g_ref[sl, :].astype(F32))).astype(o_ref.dtype)
        return 0

    lax.fori_loop(0, n, finish, 0)


def _hgrn(zf3, pb3, lb_f, lb_b, out_g):
    B, L, _ = zf3.shape
    H, dk = HGRN_HEADS, HGRN_DK
    m_f, m_b, k_f, k_b = _hgrn_consts()
    nlev = k_f.shape[0]
    full2 = lambda b, h: (0, 0)
    full3 = lambda b, h: (0, 0, 0)
    seq = lambda off: pl.BlockSpec((None, L, dk), lambda b, h: (b, 0, off + h))
    return pl.pallas_call(
        _hgrn_kernel,
        grid=(B, H),
        in_specs=[
            pl.BlockSpec((lb_f.shape[0], dk), lambda b, h: (0, h)),
            pl.BlockSpec((lb_b.shape[0], dk), lambda b, h: (0, h)),
            pl.BlockSpec((1, dk), full2),
            pl.BlockSpec(m_f.shape, full2),
            pl.BlockSpec(m_b.shape, full2),
            pl.BlockSpec((nlev, HG_C, HG_C), full3),
            pl.BlockSpec((nlev, HG_C, HG_C), full3),
            seq(0),
            seq(0),
            seq(H),
            seq(H),
            seq(2 * H),
        ],
        out_specs=pl.BlockSpec((None, L, dk), lambda b, h: (b, 0, h)),
        out_shape=jax.ShapeDtypeStruct((B, L, H * dk), BF16),
        scratch_shapes=[pltpu.VMEM((L, dk), F32), pltpu.VMEM((L, dk), F32)],
        compiler_params=pltpu.CompilerParams(
            dimension_semantics=("parallel", "parallel"), vmem_limit_bytes=VMEM_LIMIT),
        name="hgrn",
    )(lb_f, lb_b, out_g, jnp.asarray(m_f, BF16), jnp.asarray(m_b, BF16),
      jnp.asarray(k_f), jnp.asarray(k_b), pb3, zf3, zf3, pb3, pb3)


def _t5_bucket(rel):
    nb = REL_BUCKETS // 2
    ret = jnp.where(rel > 0, nb, 0)
    n = jnp.abs(rel)
    max_exact = nb // 2
    nf = jnp.maximum(n, 1).astype(jnp.float32)
    large = max_exact + (jnp.log(nf / max_exact) / math.log(REL_MAX_DIST / max_exact)
                         * (nb - max_exact)).astype(jnp.int32)
    large = jnp.minimum(large, nb - 1)
    return ret + jnp.where(n < max_exact, n, large)


def _rel_bias_ext(rel_bias, L):
    j = jnp.arange(2 * L - LANES, dtype=jnp.int32)
    ql = jnp.arange(LANES, dtype=jnp.int32)
    rel = j[None, :] - (L - LANES) - ql[:, None]
    bucket = _t5_bucket(rel)
    tab = rel_bias.astype(F32)
    out = jnp.zeros((tab.shape[1],) + bucket.shape, F32)
    for b in range(REL_BUCKETS):
        out = jnp.where((bucket == b)[None], tab[b][:, None, None], out)
    return out


def _attn_kernel(lam_ref, q_ref, k_ref, v_ref, bias_ref, og_ref, o_ref, *, lam_init):
    tq = q_ref.shape[0]
    L = k_ref.shape[0]
    qi = pl.program_id(2)
    lam = lam_ref[0, 0]
    q = q_ref[...]
    k = k_ref[...]
    lane = lax.broadcasted_iota(jnp.int32, q.shape, 1)
    zero = jnp.zeros_like(q)
    q0 = jnp.where(lane < DIFF_HALF, q, zero)
    q1 = jnp.where(lane >= DIFF_HALF, q, zero)
    parts = []
    for j in range(tq // LANES):
        off = pl.multiple_of(L - LANES - (qi * tq + j * LANES), LANES)
        parts.append(bias_ref[:, pl.ds(off, L)])
    bias = jnp.concatenate(parts, axis=0) if len(parts) > 1 else parts[0]

    def soft(qm):
        s = _dot_nt(qm, k) + bias
        e = jnp.exp(s - jnp.max(s, axis=-1, keepdims=True))
        return e, 1.0 / jnp.sum(e, axis=-1, keepdims=True)

    e0, r0 = soft(q0)
    e1, r1 = soft(q1)
    w = e0 * r0 - e1 * (lam * r1)
    o = _dot(w.astype(BF16), v_ref[...])
    ms = jnp.mean(o * o, axis=-1, keepdims=True)
    y = o * lax.rsqrt(ms + EPS) * og_ref[...] * (1.0 - lam_init)
    o_ref[...] = y.astype(o_ref.dtype)


def _attn(pb3, bias_ext, lam, out_g):
    B, L, _ = pb3.shape
    H, dh = DIFF_HEADS, 2 * DIFF_HALF
    tq = ATTN_TQ
    base = (3 * HGRN_HEADS * HGRN_DK) // dh
    lam_init = 0.8 - 0.6 * math.exp(-0.3 * LAYER)
    return pl.pallas_call(
        functools.partial(_attn_kernel, lam_init=lam_init),
        grid=(B, H, L // tq),
        in_specs=[
            pl.BlockSpec(memory_space=pltpu.SMEM),
            pl.BlockSpec((None, tq, dh), lambda b, h, i: (b, i, base + h)),
            pl.BlockSpec((None, L, dh), lambda b, h, i: (b, 0, base + H + h)),
            pl.BlockSpec((None, L, dh), lambda b, h, i: (b, 0, base + 2 * H + h)),
            pl.BlockSpec((None, LANES, 2 * L - LANES), lambda b, h, i: (h, 0, 0)),
            pl.BlockSpec((1, dh), lambda b, h, i: (0, 0)),
        ],
        out_specs=pl.BlockSpec((None, tq, dh), lambda b, h, i: (b, i, h)),
        out_shape=jax.ShapeDtypeStruct((B, L, H * dh), BF16),
        compiler_params=pltpu.CompilerParams(
            dimension_semantics=("parallel", "parallel", "parallel"),
            vmem_limit_bytes=VMEM_LIMIT),
        name="attn",
    )(lam, pb3, pb3, pb3, bias_ext, out_g)


def _cand_layout():
    K = PEER_TOPK
    groups = [("a", 0, 0), ("a", 0, 8), ("a", 1, 0), ("a", 2, 0), ("a", 3, 0),
              ("b", 0, 8), ("b", 0, 0), ("b", 1, 0), ("b", 2, 0)]
    seen = set()
    pos, valid = [], []
    for kind, fixed, start in groups:
        for r in range(SUBLANES):
            a, b = (fixed, start + r) if kind == "a" else (start + r, fixed)
            ok = (a + 1) * (b + 1) <= K and (a, b) not in seen
            if ok:
                seen.add((a, b))
            pos.append(a * K + b if ok else K * K + len(pos))
            valid.append(ok)
    assert len(seen) == sum(K // (a + 1) for a in range(K))
    return groups, np.array(pos, np.int32), np.array(valid, bool)


def _extract_topk(s, key, payload, k, big):
    n, tm = s.shape
    S = SUBLANES
    s3 = s.reshape(n // S, S, tm)
    key3 = key.reshape(n // S, S, tm)
    pay3 = None if payload is key else payload.reshape(n // S, S, tm)
    slot = lax.broadcasted_iota(jnp.int32, (k // S, S, tm), 0) * S + lax.broadcasted_iota(
        jnp.int32, (k // S, S, tm), 1)

    def all_reduce(x3, op):
        r = x3[0]
        for g in range(1, x3.shape[0]):
            r = op(r, x3[g])
        for sh in (S // 2, S // 4, S // 8):
            r = op(r, pltpu.roll(r, sh, axis=0))
        return r

    def body(j, carry):
        s3, vals, pay = carry
        m = all_reduce(s3, jnp.maximum)
        kk = all_reduce(jnp.where(s3 == m[None], key3, big), jnp.minimum)
        sel = key3 == kk[None]
        p = kk if pay3 is None else all_reduce(jnp.where(sel, pay3, 0), jnp.add)
        vals = jnp.where(slot == j, m[None], vals)
        pay = jnp.where(slot == j, p[None], pay)
        s3 = jnp.where(sel, -jnp.inf, s3)
        return s3, vals, pay

    init = (s3, jnp.zeros((k // S, S, tm), F32), jnp.zeros((k // S, S, tm), jnp.int32))
    _, vals, pay = lax.fori_loop(0, k, body, init)
    return vals.reshape(k, tm), pay.reshape(k, tm)


def _route_kernel(x_ref, oh_ref, od_ref, woh_ref, wod_ref, g2_ref, wq_ref, sk_ref, cpos_ref, cmask_ref,
                  h_ref, hn_ref, idx_ref, gate_ref, q_s, tv_s, ti_s, gs_s, is_s, *, groups):
    K = PEER_TOPK
    tm = x_ref.shape[0]
    h = x_ref[...] + _dot(oh_ref[...], woh_ref[...]) + _dot(od_ref[...], wod_ref[...])
    h_ref[...] = h
    hn = h * lax.rsqrt(jnp.mean(h * h, axis=-1, keepdims=True) + EPS) * g2_ref[...]
    half = hn.shape[1] // 2
    hn_ref[...] = _pack_bf16_pair(hn[:, :half], hn[:, half:])
    q_s[...] = _dot(hn.astype(BF16), wq_ref[...]).astype(BF16)

    key_iota = lax.broadcasted_iota(jnp.int32, (PEER_NKEYS, tm), 0)

    def half_topk(hp, _):
        col = pl.multiple_of(hp * PEER_DKEY, PEER_DKEY)
        s = _dot_nt(sk_ref[hp], q_s[:, pl.ds(col, PEER_DKEY)])
        vals, idxs = _extract_topk(s, key_iota, key_iota, K, PEER_NKEYS)
        tv_s[hp] = vals
        ti_s[hp] = idxs
        return 0

    lax.fori_loop(0, 2 * PEER_HEADS, half_topk, 0)

    cpos = cpos_ref[...]
    cmask = cmask_ref[...]

    def head(hd, _):
        s0, s1 = tv_s[2 * hd], tv_s[2 * hd + 1]
        i0, i1 = ti_s[2 * hd] * PEER_NKEYS, ti_s[2 * hd + 1]
        cs, ci = [], []
        for kind, fixed, start in groups:
            if kind == "a":
                cs.append(s0[fixed:fixed + 1] + s1[start:start + SUBLANES])
                ci.append(i0[fixed:fixed + 1] + i1[start:start + SUBLANES])
            else:
                cs.append(s0[start:start + SUBLANES] + s1[fixed:fixed + 1])
                ci.append(i0[start:start + SUBLANES] + i1[fixed:fixed + 1])
        cand = jnp.concatenate(cs, axis=0) + cmask
        cidx = jnp.concatenate(ci, axis=0)
        best, eidx = _extract_topk(cand, cpos, cidx, K, 2 * K * K)
        ex = jnp.exp(best - best[0:1])
        gate = ex / jnp.sum(ex, axis=0, keepdims=True)
        row = pl.ds(pl.multiple_of(hd * K, K), K)
        gs_s[row, :] = gate
        is_s[row, :] = eidx.astype(F32)
        return 0

    lax.fori_loop(0, PEER_HEADS, head, 0)
    gate_ref[...] = gs_s[...].T
    idx_ref[...] = is_s[...].T.astype(jnp.int32)


def _route(x2, tok0, oh2, od2, w_out, g2, w_q, sub_keys):
    T = oh2.shape[0]
    D = x2.shape[1]
    tm = ROUTE_TM
    off = tok0 // tm
    K = PEER_TOPK
    nh = oh2.shape[1]
    nq = w_q.shape[1]
    npk = PEER_HEADS * K
    groups, pos, valid = _cand_layout()
    ncand = pos.shape[0]
    cpos = jnp.asarray(np.broadcast_to(pos[:, None], (ncand, tm)))
    cmask = jnp.asarray(np.broadcast_to(np.where(valid, 0.0, -np.inf).astype(np.float32)[:, None], (ncand, tm)))
    row = lambda i: (i, 0)
    full2 = lambda i: (0, 0)
    return pl.pallas_call(
        functools.partial(_route_kernel, groups=groups),
        grid=(T // tm,),
        in_specs=[
            pl.BlockSpec((tm, D), lambda i: (i + off, 0)),
            pl.BlockSpec((tm, nh), row),
            pl.BlockSpec((tm, nh), row),
            pl.BlockSpec((nh, D), full2),
            pl.BlockSpec((nh, D), lambda i: (1, 0)),
            pl.BlockSpec((1, D), full2),
            pl.BlockSpec((D, nq), full2),
            pl.BlockSpec(sub_keys.shape, lambda i: (0, 0, 0)),
            pl.BlockSpec((ncand, tm), full2),
            pl.BlockSpec((ncand, tm), full2),
        ],
        out_specs=[
            pl.BlockSpec((tm, D), row),
            pl.BlockSpec((tm, D // 2), row),
            pl.BlockSpec((tm, npk), row),
            pl.BlockSpec((tm, npk), row),
        ],
        out_shape=[
            jax.ShapeDtypeStruct((T, D), F32),
            jax.ShapeDtypeStruct((T, D // 2), jnp.int32),
            jax.ShapeDtypeStruct((T, npk), jnp.int32),
            jax.ShapeDtypeStruct((T, npk), F32),
        ],
        scratch_shapes=[
            pltpu.VMEM((tm, nq), BF16),
            pltpu.VMEM((2 * PEER_HEADS, K, tm), F32),
            pltpu.VMEM((2 * PEER_HEADS, K, tm), jnp.int32),
            pltpu.VMEM((npk, tm), F32),
            pltpu.VMEM((npk, tm), F32),
        ],
        compiler_params=pltpu.CompilerParams(
            dimension_semantics=("parallel",), vmem_limit_bytes=VMEM_LIMIT),
        name="route",
    )(x2, oh2, od2, w_out, w_out, g2, w_q, sub_keys, cpos, cmask)


def _gelu(x):
    return 0.5 * x * (1.0 + lax.erf(x * (1.0 / math.sqrt(2.0))))


def _sc_mesh():
    return plsc.VectorSubcoreMesh(core_axis_name="c", subcore_axis_name="s")


def _sc_worker_id():
    return lax.axis_index("s") * SC_CORES + lax.axis_index("c")


def _sc_widen_pair(s):
    si = plsc.bitcast(s, jnp.int32)
    return plsc.bitcast(si & BF16_HI_MASK, F32), plsc.bitcast(si << 16, F32)


def _sc_gather_loop(tab_hbm, idx_v, bufs, sems, n_items, compute):
    nb = len(bufs)

    def start(item, b):
        pltpu.async_copy(tab_hbm.at[idx_v.at[item]], bufs[b], sems[b])

    def wait(b):
        pltpu.make_async_copy(tab_hbm.at[idx_v.at[0]], bufs[b], sems[b]).wait()

    for p in range(nb - 1):
        start(p, p)

    @pl.loop(0, pl.cdiv(n_items, nb))
    def _(i):
        for b in range(nb):
            it = i * nb + b

            @pl.when(it < n_items)
            def _():
                @pl.when(it + nb - 1 < n_items)
                def _():
                    start(it + nb - 1, (b + nb - 1) % nb)

                wait(b)
                compute(it, bufs[b])


def _peer_hidden(u, idx4, x):
    T, DW = x.shape
    n_rows, R = idx4.shape
    ipt = n_rows // T
    G = SC_TOKEN_GROUP
    nit = G * ipt
    tpw = T // SC_WORKERS
    NL = SC_LANES
    RB = 8
    NV = 4
    GW = NV * NL
    tile = (R, DW // LANES, LANES)

    @functools.partial(
        pl.kernel, mesh=_sc_mesh(), compiler_params=pltpu.CompilerParams(needs_layout_passes=False),
        out_type=jax.ShapeDtypeStruct((n_rows, R), F32),
        scratch_types=[pltpu.VMEM((tpw * ipt, R), jnp.int32), pltpu.VMEM((G, DW), jnp.int32),
                       [pltpu.VMEM(tile, jnp.int32)] * SC_RING, [pltpu.SemaphoreType.DMA] * SC_RING,
                       pltpu.VMEM((tpw * ipt, R), F32)],
        name="peer_hidden")
    def k(u_hbm, idx_hbm, x_hbm, h_hbm, idx_v, x_v, bufs, sems, h_v):
        tok0 = _sc_worker_id() * tpw
        lane = lax.broadcasted_iota(jnp.int32, (NL,), 0)
        zero = jnp.zeros((NL,), F32)

        def compute(item, buf):
            @pl.when(item % nit == 0)
            def _():
                pltpu.sync_copy(x_hbm.at[pl.ds(tok0 + (item // nit) * G, G)], x_v)

            tok = (item % nit) // ipt
            hvs = [zero for _ in range(R // NL)]
            for rb in range(R // RB):
                def body(g, accs):
                    sub = g // (LANES // GW)
                    base = (g % (LANES // GW)) * GW
                    xs = [plsc.bitcast(x_v[tok, pl.ds(pl.multiple_of(g * GW + jj * NL, NL), NL)], BF16)
                          for jj in range(NV)]
                    out = []
                    for j in range(RB):
                        ps = [plsc.bitcast(buf[rb * RB + j, sub, pl.ds(pl.multiple_of(base + jj * NL, NL), NL)],
                                           BF16) * xs[jj] for jj in range(NV)]
                        hi, lo = _sc_widen_pair((ps[0] + ps[1]) + (ps[2] + ps[3]))
                        out.append((accs[j] + hi) + lo)
                    return tuple(out)
                accs = lax.fori_loop(0, DW // GW, body, tuple(zero for _ in range(RB)))
                for j in range(RB):
                    r = rb * RB + j
                    hvs[r // NL] = jnp.where(lane == (r % NL), jnp.sum(accs[j]), hvs[r // NL])
            for q in range(R // NL):
                h_v[item, pl.ds(q * NL, NL)] = hvs[q]

        pltpu.sync_copy(idx_hbm.at[pl.ds(tok0 * ipt, tpw * ipt)], idx_v)
        _sc_gather_loop(u_hbm, idx_v, bufs, sems, tpw * ipt, compute)
        pltpu.sync_copy(h_v, h_hbm.at[pl.ds(tok0 * ipt, tpw * ipt)])

    return k(u, idx4, x)


def _peer_combine(v, idx4, w4, T, D):
    n_rows, R = idx4.shape
    ipt = n_rows // T
    G = SC_COMBINE_GROUP
    nit = G * ipt
    tpw = T // SC_WORKERS
    NL = SC_LANES
    DW = D // 2
    half = DW // 2
    nv = half // NL
    RG = 4
    tile = (R, DW // LANES, LANES)

    @functools.partial(
        pl.kernel, mesh=_sc_mesh(), compiler_params=pltpu.CompilerParams(needs_layout_passes=False),
        out_type=jax.ShapeDtypeStruct((T, D), F32),
        scratch_types=[pltpu.VMEM((tpw * ipt, R), jnp.int32), pltpu.VMEM((tpw * ipt, R), jnp.int32),
                       [pltpu.VMEM(tile, jnp.int32)] * SC_RING, [pltpu.SemaphoreType.DMA] * SC_RING,
                       pltpu.VMEM((G, D), F32)],
        name="peer_combine")
    def k(v_hbm, idx_hbm, w_hbm, o_hbm, idx_v, w_v, bufs, sems, out_v):
        tok0 = _sc_worker_id() * tpw
        zero = jnp.zeros((NL,), F32)

        def flush(group):
            pltpu.sync_copy(out_v, o_hbm.at[pl.ds(tok0 + group * G, G)])

        def compute(item, buf):
            @pl.when(item % nit == 0)
            def _():
                @pl.when(item > 0)
                def _():
                    flush(item // nit - 1)

                @pl.loop(0, G)
                def _(t):
                    @pl.loop(0, D // NL)
                    def _(i):
                        out_v[t, pl.ds(pl.multiple_of(i * NL, NL), NL)] = zero

            tok = (item % nit) // ipt
            item_vec = jnp.full((NL,), item, jnp.int32)
            for hf in range(2):
                def body(rg, accs):
                    ws = [plsc.bitcast(plsc.load_gather(
                        w_v, [item_vec, jnp.full((NL,), rg * RG + rr, jnp.int32)]), BF16) for rr in range(RG)]
                    his, los = [], []
                    for i in range(nv):
                        word = hf * half + i * NL
                        ps = [plsc.bitcast(buf[rg * RG + rr, word // LANES, pl.ds(word % LANES, NL)], BF16) * ws[rr]
                              for rr in range(RG)]
                        hi, lo = _sc_widen_pair((ps[0] + ps[1]) + (ps[2] + ps[3]))
                        his.append(accs[i] + hi)
                        los.append(accs[nv + i] + lo)
                    return tuple(his + los)
                accs = lax.fori_loop(0, R // RG, body, tuple(zero for _ in range(2 * nv)))
                for i in range(nv):
                    word = hf * half + i * NL
                    plsc.addupdate(out_v.at[tok, pl.ds(word, NL)], accs[i])
                    plsc.addupdate(out_v.at[tok, pl.ds(DW + word, NL)], accs[nv + i])

        pltpu.sync_copy(idx_hbm.at[pl.ds(tok0 * ipt, tpw * ipt)], idx_v)
        pltpu.sync_copy(w_hbm.at[pl.ds(tok0 * ipt, tpw * ipt)], w_v)
        _sc_gather_loop(v_hbm, idx_v, bufs, sems, tpw * ipt, compute)
        flush(tpw // G - 1)

    return k(v, idx4, w4)


def _act_kernel(h_ref, g_ref, w_ref):
    w = _gelu(h_ref[...]) * g_ref[...]
    w_ref[...] = _pack_bf16_pair(w, w)


def _peer_act(hraw, gate):
    T, n = gate.shape
    tm = PEER_ACT_TM
    spec = pl.BlockSpec((tm, n), lambda i: (i, 0))
    return pl.pallas_call(
        _act_kernel, grid=(T // tm,), in_specs=[spec, spec], out_specs=spec,
        out_shape=jax.ShapeDtypeStruct((T, n), jnp.int32),
        compiler_params=pltpu.CompilerParams(dimension_semantics=("parallel",)),
        name="peer_act",
    )(hraw, gate)


def _final_kernel(h_ref, p_ref, g_ref, y_ref):
    y = h_ref[...] + p_ref[...]
    ms = jnp.mean(y * y, axis=-1, keepdims=True)
    y_ref[...] = y * lax.rsqrt(ms + EPS) * g_ref[...]


def _final(h, po, g):
    T, D = h.shape
    tm = FINAL_TM
    spec = pl.BlockSpec((tm, D), lambda i: (i, 0))
    return pl.pallas_call(
        _final_kernel, grid=(T // tm,),
        in_specs=[spec, spec, pl.BlockSpec((1, D), lambda i: (0, 0))], out_specs=spec,
        out_shape=jax.ShapeDtypeStruct((T, D), F32),
        compiler_params=pltpu.CompilerParams(dimension_semantics=("parallel",)),
        name="final_norm",
    )(h, po, g)


def kernel(x, norm1_g, w_in, hgrn_lb_fwd, hgrn_lb_bwd, hgrn_out_g, diff_lam_q1, diff_lam_k1,
           diff_lam_q2, diff_lam_k2, diff_out_g, rel_bias, w_out, norm2_g, peer_w_q,
           peer_sub_keys, peer_u, peer_v, final_g):
    B, L, D = x.shape

    w_r = w_in[LAYER].astype(BF16)
    f32 = jnp.float32
    lam_init = 0.8 - 0.6 * math.exp(-0.3 * LAYER)
    lam = (jnp.exp(jnp.sum(diff_lam_q1[LAYER].astype(f32) * diff_lam_k1[LAYER].astype(f32)))
           - jnp.exp(jnp.sum(diff_lam_q2[LAYER].astype(f32) * diff_lam_k2[LAYER].astype(f32))) + lam_init)
    lam = lam.reshape(1, 1)
    bias_ext = _rel_bias_ext(rel_bias, L)
    sk = peer_sub_keys[LAYER].reshape(2 * PEER_HEADS, PEER_NKEYS, PEER_DKEY).astype(BF16)
    wq = peer_w_q[LAYER].reshape(D, -1).astype(BF16)
    wo = w_out[LAYER].astype(BF16)
    def pack_table(tab):
        words = _pack_bf16_pair(tab[:, :D // 2], tab[:, D // 2:])
        return words.reshape(tab.shape[0], D // 2 // LANES, LANES)
    u3, v3 = pack_table(peer_u[LAYER]), pack_table(peer_v[LAYER])

    bc = B // BATCH_CHUNKS
    tc = bc * L
    rows = tc * PEER_HEADS * PEER_TOPK // SC_GATHER_ROWS
    outs = []
    pending = {}
    combined = {}
    x2 = x.reshape(B * L, D)
    for c in range(BATCH_CHUNKS + CHUNK_LAG + FINAL_LAG):
        live = c < BATCH_CHUNKS
        g1 = norm1_g[LAYER][None, :]
        if c - CHUNK_LAG - FINAL_LAG in combined:
            h_prev, po = combined.pop(c - CHUNK_LAG - FINAL_LAG)
            if live:
                g1, po = lax.optimization_barrier((g1, po))
            outs.append(_final(h_prev, po, final_g[None, :]).reshape(bc, L, D))
        if CHUNK_LAG <= c < BATCH_CHUNKS + CHUNK_LAG:
            h, idx4, gate, hraw = pending.pop(c - CHUNK_LAG)
            wts = _peer_act(hraw.reshape(tc, -1), gate)
            if live:
                g1, wts = lax.optimization_barrier((g1, wts))
            combined[c - CHUNK_LAG] = (h, _peer_combine(v3, idx4, wts.reshape(rows, SC_GATHER_ROWS), tc, D))
        if live:
            zf, pb = _inproj(x2, g1, w_r, c * tc, tc)
            zf3 = zf.reshape(bc, L, -1)
            pb3 = pb.reshape(bc, L, -1)
            o_h = _hgrn(zf3, pb3, hgrn_lb_fwd, hgrn_lb_bwd, hgrn_out_g[LAYER][None, :])
            o_d = _attn(pb3, bias_ext, lam, diff_out_g[LAYER][None, :])
            h, hn, idx, gate = _route(x2, c * tc, o_h.reshape(tc, -1), o_d.reshape(tc, -1), wo,
                                      norm2_g[LAYER][None, :], wq, sk)
            idx4 = idx.reshape(rows, SC_GATHER_ROWS)
            pending[c] = (h, idx4, gate, _peer_hidden(u3, idx4, hn))
    return jnp.concatenate(outs, axis=0)
```

```python
import functools
import math

import numpy as np
import jax
import jax.numpy as jnp
from jax import lax
from jax.experimental import pallas as pl
from jax.experimental.pallas import tpu as pltpu
from jax.experimental.pallas import tpu_sc as plsc

F32 = jnp.float32
BF16 = jnp.bfloat16
EPS = 1e-6

HGRN_HEADS = 4
HGRN_DK = 128
DIFF_HEADS = 4
DIFF_HALF = 64
REL_BUCKETS = 32
REL_MAX_DIST = 128
PEER_HEADS = 8
PEER_NKEYS = 128
PEER_DKEY = 128
PEER_TOPK = 16
LAYER = 0

LANES = 128
SUBLANES = 8
VMEM_LIMIT = 48 * 1024 * 1024

INPROJ_TM = 512
HG_C = 64
HG_LEVELS = (64, 32, 16, 8, 4, 2)
ATTN_TQ = 256
ROUTE_TM = 256
BATCH_CHUNKS = 16
CHUNK_LAG = 2
FINAL_LAG = 1
PEER_ACT_TM = 2048
FINAL_TM = 512

SC_CORES = 2
SC_SUBCORES = 16
SC_LANES = 16
SC_WORKERS = SC_CORES * SC_SUBCORES
SC_GATHER_ROWS = 64
SC_TOKEN_GROUP = 32
SC_COMBINE_GROUP = 16
SC_RING = 2


def _dot(a, b):
    return jnp.dot(a, b, preferred_element_type=F32)


def _dot_nt(a, b):
    return lax.dot_general(a, b, (((1,), (1,)), ((), ())), preferred_element_type=F32)


def _dot_tn(a, b):
    return lax.dot_general(a, b, (((0,), (0,)), ((), ())), preferred_element_type=F32)


def _silu(x):
    return x * (1.0 / (1.0 + jnp.exp(-x)))


BF16_HI_MASK = -65536


def _pack_bf16_pair(hi, lo):
    bits = lambda a: lax.bitcast_convert_type(a.astype(BF16).astype(F32), jnp.int32)
    return (bits(hi) & BF16_HI_MASK) | lax.shift_right_logical(bits(lo), 16)


def _inproj_kernel(x_ref, g_ref, w_ref, zf_ref, pb_ref):
    x = x_ref[...]
    ms = jnp.mean(x * x, axis=-1, keepdims=True)
    xn = (x * lax.rsqrt(ms + EPS) * g_ref[...]).astype(BF16)
    hw = zf_ref.shape[1] // 2
    zf_ref[...] = _dot(xn, w_ref[:, hw:3 * hw])
    pb_ref[:, 0:hw] = _dot(xn, w_ref[:, 0:hw]).astype(BF16)
    for j in range(3, 8):
        y = _dot(xn, w_ref[:, j * hw:(j + 1) * hw])
        if j == 5:
            y = y * (DIFF_HALF ** -0.5)
        pb_ref[:, (j - 2) * hw:(j - 1) * hw] = y.astype(BF16)


def _inproj(x2, g, w, tok0, T):
    D = x2.shape[1]
    N = w.shape[1]
    nz = 2 * HGRN_HEADS * HGRN_DK
    tm = INPROJ_TM
    off = tok0 // tm
    return pl.pallas_call(
        _inproj_kernel,
        grid=(T // tm,),
        in_specs=[
            pl.BlockSpec((tm, D), lambda i: (i + off, 0)),
            pl.BlockSpec((1, D), lambda i: (0, 0)),
            pl.BlockSpec((D, N), lambda i: (0, 0)),
        ],
        out_specs=[
            pl.BlockSpec((tm, nz), lambda i: (i, 0)),
            pl.BlockSpec((tm, N - nz), lambda i: (i, 0)),
        ],
        out_shape=[
            jax.ShapeDtypeStruct((T, nz), F32),
            jax.ShapeDtypeStruct((T, N - nz), BF16),
        ],
        compiler_params=pltpu.CompilerParams(
            dimension_semantics=("parallel",), vmem_limit_bytes=VMEM_LIMIT),
        name="inproj",
    )(x2, g, w)


def _hgrn_consts():
    C = HG_C
    r = np.arange(C)
    t = r[:, None]
    u = r[None, :]
    blocks = [u <= t, u > t]
    masks = [np.eye(C, dtype=bool)]
    for B in HG_LEVELS:
        half = B // 2
        a = (r // B) * B
        m = (a + half - 1)[:, None]
        upper = (r - a) >= half
        blocks.append(np.where(upper[:, None], (u > m) & (u <= t), (u > t) & (u <= m)))
        same = a[:, None] == a[None, :]
        masks.append(same & upper[:, None] & (~upper)[None, :])
    m_f = np.concatenate(blocks, 0).astype(np.float32)
    m_b = np.concatenate([b[::-1, ::-1] for b in blocks], 0).astype(np.float32)
    k_f = np.stack(masks).astype(np.float32)
    k_b = np.ascontiguousarray(k_f.transpose(0, 2, 1))
    return m_f, m_b, k_f, k_b


def _hgrn_gates(z, tab):
    tabf = tab.astype(F32)
    e = jnp.exp(tabf - jnp.max(tabf, axis=0, keepdims=True))
    lb = jnp.sum(e[0:LAYER + 1], axis=0, keepdims=True) / jnp.sum(e, axis=0, keepdims=True)
    log_lb = jnp.log(lb)
    log_1m = jnp.log1p(-lb)
    ez = jnp.exp(-jnp.abs(z))
    l1p = jnp.log1p(ez)
    log_sig = jnp.minimum(z, 0.0) - l1p
    c = log_1m + log_sig
    hi = jnp.maximum(log_lb, c)
    lo = jnp.minimum(log_lb, c)
    log_f = hi + jnp.log1p(jnp.exp(lo - hi))
    sig_neg = jnp.where(z >= 0.0, ez, 1.0) / (1.0 + ez)
    k = (1.0 - lb) * sig_neg
    return log_f, k


def _hgrn_kernel(lbf_ref, lbb_ref, og_ref, mf_ref, mb_ref, kf_ref, kb_ref,
                 q_ref, zf_ref, zb_ref, v_ref, g_ref, o_ref, of_s, ob_s):
    C = HG_C
    L = q_ref.shape[0]
    n = L // C
    dv = v_ref.shape[1]

    def chunk(c, st, m_ref, k_ref, z_ref, tab_ref, forward):
        sl = pl.ds(pl.multiple_of(c * C, C), C)
        qh = q_ref[sl, :].astype(F32)
        q = _silu(qh)
        v = v_ref[sl, :]
        log_f, k = _hgrn_gates(z_ref[sl, :], tab_ref[...])
        lf_hi = log_f.astype(BF16)
        lf_lo = (log_f - lf_hi.astype(F32)).astype(BF16)
        m = m_ref[...]
        e = jnp.exp(_dot(m, lf_hi) + _dot(m, lf_lo))
        e_b = e[0:C]
        e_s = e[C:2 * C]
        dec = e[C - 1:C] if forward else e[0:1]
        o = _dot_nt((q * e_b).astype(BF16), st.astype(BF16))
        a = _dot_nt(q.astype(BF16), k.astype(BF16)) * k_ref[0]
        for l in range(len(HG_LEVELS)):
            e_l = e[(2 + l) * C:(3 + l) * C]
            a = a + _dot_nt((q * e_l).astype(BF16), (k * e_l).astype(BF16)) * k_ref[l + 1]
        o = o + _dot(a.astype(BF16), v)
        st = st * dec + _dot_tn(v, (k * e_s).astype(BF16))
        return sl, o, st

    st0 = jnp.zeros((dv, q_ref.shape[1]), F32)

    def both(i, carry):
        st_f, st_b = carry
        sl_f, o_f, st_f = chunk(i, st_f, mf_ref, kf_ref, zf_ref, lbf_ref, True)
        sl_b, o_b, st_b = chunk(n - 1 - i, st_b, mb_ref, kb_ref, zb_ref, lbb_ref, False)
        of_s[sl_f, :] = o_f
        ob_s[sl_b, :] = o_b
        return st_f, st_b

    lax.fori_loop(0, n, both, (st0, st0))

    def finish(c, _):
        sl = pl.ds(pl.multiple_of(c * C, C), C)
        tot = of_s[sl, :] + ob_s[sl, :]
        ms = jnp.mean(tot * tot, axis=-1, keepdims=True)
        y = tot * lax.rsqrt(ms + EPS) * og_ref[...]
        o_ref[sl, :] = (y * _silu(g_ref[sl, :].astype(F32))).astype(o_ref.dtype)
        return 0

    lax.fori_loop(0, n, finish, 0)


def _hgrn(zf3, pb3, lb_f, lb_b, out_g):
    B, L, _ = zf3.shape
    H, dk = HGRN_HEADS, HGRN_DK
    m_f, m_b, k_f, k_b = _hgrn_consts()
    nlev = k_f.shape[0]
    full2 = lambda b, h: (0, 0)
    full3 = lambda b, h: (0, 0, 0)
    seq = lambda off: pl.BlockSpec((None, L, dk), lambda b, h: (b, 0, off + h))
    return pl.pallas_call(
        _hgrn_kernel,
        grid=(B, H),
        in_specs=[
            pl.BlockSpec((lb_f.shape[0], dk), lambda b, h: (0, h)),
            pl.BlockSpec((lb_b.shape[0], dk), lambda b, h: (0, h)),
            pl.BlockSpec((1, dk), full2),
            pl.BlockSpec(m_f.shape, full2),
            pl.BlockSpec(m_b.shape, full2),
            pl.BlockSpec((nlev, HG_C, HG_C), full3),
            pl.BlockSpec((nlev, HG_C, HG_C), full3),
            seq(0),
            seq(0),
            seq(H),
            seq(H),
            seq(2 * H),
        ],
        out_specs=pl.BlockSpec((None, L, dk), lambda b, h: (b, 0, h)),
        out_shape=jax.ShapeDtypeStruct((B, L, H * dk), BF16),
        scratch_shapes=[pltpu.VMEM((L, dk), F32), pltpu.VMEM((L, dk), F32)],
        compiler_params=pltpu.CompilerParams(
            dimension_semantics=("parallel", "parallel"), vmem_limit_bytes=VMEM_LIMIT),
        name="hgrn",
    )(lb_f, lb_b, out_g, jnp.asarray(m_f, BF16), jnp.asarray(m_b, BF16),
      jnp.asarray(k_f), jnp.asarray(k_b), pb3, zf3, zf3, pb3, pb3)


def _t5_bucket(rel):
    nb = REL_BUCKETS // 2
    ret = jnp.where(rel > 0, nb, 0)
    n = jnp.abs(rel)
    max_exact = nb // 2
    nf = jnp.maximum(n, 1).astype(jnp.float32)
    large = max_exact + (jnp.log(nf / max_exact) / math.log(REL_MAX_DIST / max_exact)
                         * (nb - max_exact)).astype(jnp.int32)
    large = jnp.minimum(large, nb - 1)
    return ret + jnp.where(n < max_exact, n, large)


def _rel_bias_ext(rel_bias, L):
    j = jnp.arange(2 * L - LANES, dtype=jnp.int32)
    ql = jnp.arange(LANES, dtype=jnp.int32)
    rel = j[None, :] - (L - LANES) - ql[:, None]
    bucket = _t5_bucket(rel)
    tab = rel_bias.astype(F32)
    out = jnp.zeros((tab.shape[1],) + bucket.shape, F32)
    for b in range(REL_BUCKETS):
        out = jnp.where((bucket == b)[None], tab[b][:, None, None], out)
    return out


def _attn_kernel(lam_ref, q_ref, k_ref, v_ref, bias_ref, og_ref, o_ref, *, lam_init):
    tq = q_ref.shape[0]
    L = k_ref.shape[0]
    qi = pl.program_id(2)
    lam = lam_ref[0, 0]
    q = q_ref[...]
    k = k_ref[...]
    lane = lax.broadcasted_iota(jnp.int32, q.shape, 1)
    zero = jnp.zeros_like(q)
    q0 = jnp.where(lane < DIFF_HALF, q, zero)
    q1 = jnp.where(lane >= DIFF_HALF, q, zero)
    parts = []
    for j in range(tq // LANES):
        off = pl.multiple_of(L - LANES - (qi * tq + j * LANES), LANES)
        parts.append(bias_ref[:, pl.ds(off, L)])
    bias = jnp.concatenate(parts, axis=0) if len(parts) > 1 else parts[0]

    def soft(qm):
        s = _dot_nt(qm, k) + bias
        e = jnp.exp(s - jnp.max(s, axis=-1, keepdims=True))
        return e, 1.0 / jnp.sum(e, axis=-1, keepdims=True)

    e0, r0 = soft(q0)
    e1, r1 = soft(q1)
    w = e0 * r0 - e1 * (lam * r1)
    o = _dot(w.astype(BF16), v_ref[...])
    ms = jnp.mean(o * o, axis=-1, keepdims=True)
    y = o * lax.rsqrt(ms + EPS) * og_ref[...] * (1.0 - lam_init)
    o_ref[...] = y.astype(o_ref.dtype)


def _attn(pb3, bias_ext, lam, out_g):
    B, L, _ = pb3.shape
    H, dh = DIFF_HEADS, 2 * DIFF_HALF
    tq = ATTN_TQ
    base = (3 * HGRN_HEADS * HGRN_DK) // dh
    lam_init = 0.8 - 0.6 * math.exp(-0.3 * LAYER)
    return pl.pallas_call(
        functools.partial(_attn_kernel, lam_init=lam_init),
        grid=(B, H, L // tq),
        in_specs=[
            pl.BlockSpec(memory_space=pltpu.SMEM),
            pl.BlockSpec((None, tq, dh), lambda b, h, i: (b, i, base + h)),
            pl.BlockSpec((None, L, dh), lambda b, h, i: (b, 0, base + H + h)),
            pl.BlockSpec((None, L, dh), lambda b, h, i: (b, 0, base + 2 * H + h)),
            pl.BlockSpec((None, LANES, 2 * L - LANES), lambda b, h, i: (h, 0, 0)),
            pl.BlockSpec((1, dh), lambda b, h, i: (0, 0)),
        ],
        out_specs=pl.BlockSpec((None, tq, dh), lambda b, h, i: (b, i, h)),
        out_shape=jax.ShapeDtypeStruct((B, L, H * dh), BF16),
        compiler_params=pltpu.CompilerParams(
            dimension_semantics=("parallel", "parallel", "parallel"),
            vmem_limit_bytes=VMEM_LIMIT),
        name="attn",
    )(lam, pb3, pb3, pb3, bias_ext, out_g)


def _cand_layout():
    K = PEER_TOPK
    groups = [("a", 0, 0), ("a", 0, 8), ("a", 1, 0), ("a", 2, 0), ("a", 3, 0),
              ("b", 0, 8), ("b", 0, 0), ("b", 1, 0), ("b", 2, 0)]
    seen = set()
    pos, valid = [], []
    for kind, fixed, start in groups:
        for r in range(SUBLANES):
            a, b = (fixed, start + r) if kind == "a" else (start + r, fixed)
            ok = (a + 1) * (b + 1) <= K and (a, b) not in seen
            if ok:
                seen.add((a, b))
            pos.append(a * K + b if ok else K * K + len(pos))
            valid.append(ok)
    assert len(seen) == sum(K // (a + 1) for a in range(K))
    return groups, np.array(pos, np.int32), np.array(valid, bool)


def _extract_topk(s, key, payload, k, big):
    n, tm = s.shape
    S = SUBLANES
    s3 = s.reshape(n // S, S, tm)
    key3 = key.reshape(n // S, S, tm)
    pay3 = None if payload is key else payload.reshape(n // S, S, tm)
    slot = lax.broadcasted_iota(jnp.int32, (k // S, S, tm), 0) * S + lax.broadcasted_iota(
        jnp.int32, (k // S, S, tm), 1)

    def all_reduce(x3, op):
        r = x3[0]
        for g in range(1, x3.shape[0]):
            r = op(r, x3[g])
        for sh in (S // 2, S // 4, S // 8):
            r = op(r, pltpu.roll(r, sh, axis=0))
        return r

    def body(j, carry):
        s3, vals, pay = carry
        m = all_reduce(s3, jnp.maximum)
        kk = all_reduce(jnp.where(s3 == m[None], key3, big), jnp.minimum)
        sel = key3 == kk[None]
        p = kk if pay3 is None else all_reduce(jnp.where(sel, pay3, 0), jnp.add)
        vals = jnp.where(slot == j, m[None], vals)
        pay = jnp.where(slot == j, p[None], pay)
        s3 = jnp.where(sel, -jnp.inf, s3)
        return s3, vals, pay

    init = (s3, jnp.zeros((k // S, S, tm), F32), jnp.zeros((k // S, S, tm), jnp.int32))
    _, vals, pay = lax.fori_loop(0, k, body, init)
    return vals.reshape(k, tm), pay.reshape(k, tm)


def _route_kernel(x_ref, oh_ref, od_ref, woh_ref, wod_ref, g2_ref, wq_ref, sk_ref, cpos_ref, cmask_ref,
                  h_ref, hn_ref, idx_ref, gate_ref, q_s, tv_s, ti_s, gs_s, is_s, *, groups):
    K = PEER_TOPK
    tm = x_ref.shape[0]
    h = x_ref[...] + _dot(oh_ref[...], woh_ref[...]) + _dot(od_ref[...], wod_ref[...])
    h_ref[...] = h
    hn = h * lax.rsqrt(jnp.mean(h * h, axis=-1, keepdims=True) + EPS) * g2_ref[...]
    half = hn.shape[1] // 2
    hn_ref[...] = _pack_bf16_pair(hn[:, :half], hn[:, half:])
    q_s[...] = _dot(hn.astype(BF16), wq_ref[...]).astype(BF16)

    key_iota = lax.broadcasted_iota(jnp.int32, (PEER_NKEYS, tm), 0)

    def half_topk(hp, _):
        col = pl.multiple_of(hp * PEER_DKEY, PEER_DKEY)
        s = _dot_nt(sk_ref[hp], q_s[:, pl.ds(col, PEER_DKEY)])
        vals, idxs = _extract_topk(s, key_iota, key_iota, K, PEER_NKEYS)
        tv_s[hp] = vals
        ti_s[hp] = idxs
        return 0

    lax.fori_loop(0, 2 * PEER_HEADS, half_topk, 0)

    cpos = cpos_ref[...]
    cmask = cmask_ref[...]

    def head(hd, _):
        s0, s1 = tv_s[2 * hd], tv_s[2 * hd + 1]
        i0, i1 = ti_s[2 * hd] * PEER_NKEYS, ti_s[2 * hd + 1]
        cs, ci = [], []
        for kind, fixed, start in groups:
            if kind == "a":
                cs.append(s0[fixed:fixed + 1] + s1[start:start + SUBLANES])
                ci.append(i0[fixed:fixed + 1] + i1[start:start + SUBLANES])
            else:
                cs.append(s0[start:start + SUBLANES] + s1[fixed:fixed + 1])
                ci.append(i0[start:start + SUBLANES] + i1[fixed:fixed + 1])
        cand = jnp.concatenate(cs, axis=0) + cmask
        cidx = jnp.concatenate(ci, axis=0)
        best, eidx = _extract_topk(cand, cpos, cidx, K, 2 * K * K)
        ex = jnp.exp(best - best[0:1])
        gate = ex / jnp.sum(ex, axis=0, keepdims=True)
        row = pl.ds(pl.multiple_of(hd * K, K), K)
        gs_s[row, :] = gate
        is_s[row, :] = eidx.astype(F32)
        return 0

    lax.fori_loop(0, PEER_HEADS, head, 0)
    gate_ref[...] = gs_s[...].T
    idx_ref[...] = is_s[...].T.astype(jnp.int32)


def _route(x2, tok0, oh2, od2, w_out, g2, w_q, sub_keys):
    T = oh2.shape[0]
    D = x2.shape[1]
    tm = ROUTE_TM
    off = tok0 // tm
    K = PEER_TOPK
    nh = oh2.shape[1]
    nq = w_q.shape[1]
    npk = PEER_HEADS * K
    groups, pos, valid = _cand_layout()
    ncand = pos.shape[0]
    cpos = jnp.asarray(np.broadcast_to(pos[:, None], (ncand, tm)))
    cmask = jnp.asarray(np.broadcast_to(np.where(valid, 0.0, -np.inf).astype(np.float32)[:, None], (ncand, tm)))
    row = lambda i: (i, 0)
    full2 = lambda i: (0, 0)
    return pl.pallas_call(
        functools.partial(_route_kernel, groups=groups),
        grid=(T // tm,),
        in_specs=[
            pl.BlockSpec((tm, D), lambda i: (i + off, 0)),
            pl.BlockSpec((tm, nh), row),
            pl.BlockSpec((tm, nh), row),
            pl.BlockSpec((nh, D), full2),
            pl.BlockSpec((nh, D), lambda i: (1, 0)),
            pl.BlockSpec((1, D), full2),
            pl.BlockSpec((D, nq), full2),
            pl.BlockSpec(sub_keys.shape, lambda i: (0, 0, 0)),
            pl.BlockSpec((ncand, tm), full2),
            pl.BlockSpec((ncand, tm), full2),
        ],
        out_specs=[
            pl.BlockSpec((tm, D), row),
            pl.BlockSpec((tm, D // 2), row),
            pl.BlockSpec((tm, npk), row),
            pl.BlockSpec((tm, npk), row),
        ],
        out_shape=[
            jax.ShapeDtypeStruct((T, D), F32),
            jax.ShapeDtypeStruct((T, D // 2), jnp.int32),
            jax.ShapeDtypeStruct((T, npk), jnp.int32),
            jax.ShapeDtypeStruct((T, npk), F32),
        ],
        scratch_shapes=[
            pltpu.VMEM((tm, nq), BF16),
            pltpu.VMEM((2 * PEER_HEADS, K, tm), F32),
            pltpu.VMEM((2 * PEER_HEADS, K, tm), jnp.int32),
            pltpu.VMEM((npk, tm), F32),
            pltpu.VMEM((npk, tm), F32),
        ],
        compiler_params=pltpu.CompilerParams(
            dimension_semantics=("parallel",), vmem_limit_bytes=VMEM_LIMIT),
        name="route",
    )(x2, oh2, od2, w_out, w_out, g2, w_q, sub_keys, cpos, cmask)


def _gelu(x):
    return 0.5 * x * (1.0 + lax.erf(x * (1.0 / math.sqrt(2.0))))


def _sc_mesh():
    return plsc.VectorSubcoreMesh(core_axis_name="c", subcore_axis_name="s")


def _sc_worker_id():
    return lax.axis_index("s") * SC_CORES + lax.axis_index("c")


def _sc_widen_pair(s):
    si = plsc.bitcast(s, jnp.int32)
    return plsc.bitcast(si & BF16_HI_MASK, F32), plsc.bitcast(si << 16, F32)


def _pack_table_sc(tab):
    E, D = tab.shape
    DW = D // 2
    NL = SC_LANES
    rpw = E // SC_WORKERS
    RBK = 32

    def rne(x):
        bits = plsc.bitcast(x, jnp.int32)
        return bits + (0x7FFF + ((bits >> 16) & 1))

    @functools.partial(
        pl.kernel, mesh=_sc_mesh(), compiler_params=pltpu.CompilerParams(needs_layout_passes=False),
        out_type=jax.ShapeDtypeStruct((E, DW // LANES, LANES), jnp.int32),
        scratch_types=[pltpu.VMEM((RBK, D), F32), pltpu.VMEM((RBK, DW // LANES, LANES), jnp.int32)],
        name="peer_pack")
    def k(t_hbm, o_hbm, in_v, out_v):
        r0 = _sc_worker_id() * rpw

        @pl.loop(0, rpw // RBK)
        def _(blk):
            base = r0 + blk * RBK
            pltpu.sync_copy(t_hbm.at[pl.ds(base, RBK)], in_v)

            @pl.loop(0, RBK)
            def _(r):
                for j in range(DW // NL):
                    hi = rne(in_v[r, pl.ds(j * NL, NL)])
                    lo = rne(in_v[r, pl.ds(DW + j * NL, NL)])
                    out_v[r, (j * NL) // LANES, pl.ds((j * NL) % LANES, NL)] = (
                        (hi & BF16_HI_MASK) | lax.shift_right_logical(lo, 16))

            pltpu.sync_copy(out_v, o_hbm.at[pl.ds(base, RBK)])

    return k(tab)


def _sc_gather_loop(tab_hbm, idx_v, bufs, sems, n_items, compute):
    nb = len(bufs)

    def start(item, b):
        pltpu.async_copy(tab_hbm.at[idx_v.at[item]], bufs[b], sems[b])

    def wait(b):
        pltpu.make_async_copy(tab_hbm.at[idx_v.at[0]], bufs[b], sems[b]).wait()

    for p in range(nb - 1):
        start(p, p)

    @pl.loop(0, pl.cdiv(n_items, nb))
    def _(i):
        for b in range(nb):
            it = i * nb + b

            @pl.when(it < n_items)
            def _():
                @pl.when(it + nb - 1 < n_items)
                def _():
                    start(it + nb - 1, (b + nb - 1) % nb)

                wait(b)
                compute(it, bufs[b])


def _peer_hidden(u, idx4, x):
    T, DW = x.shape
    n_rows, R = idx4.shape
    ipt = n_rows // T
    G = SC_TOKEN_GROUP
    nit = G * ipt
    tpw = T // SC_WORKERS
    NL = SC_LANES
    RB = 8
    NV = 4
    GW = NV * NL
    tile = (R, DW // LANES, LANES)

    @functools.partial(
        pl.kernel, mesh=_sc_mesh(), compiler_params=pltpu.CompilerParams(needs_layout_passes=False),
        out_type=jax.ShapeDtypeStruct((n_rows, R), F32),
        scratch_types=[pltpu.VMEM((tpw * ipt, R), jnp.int32), pltpu.VMEM((G, DW), jnp.int32),
                       [pltpu.VMEM(tile, jnp.int32)] * SC_RING, [pltpu.SemaphoreType.DMA] * SC_RING,
                       pltpu.VMEM((tpw * ipt, R), F32)],
        name="peer_hidden")
    def k(u_hbm, idx_hbm, x_hbm, h_hbm, idx_v, x_v, bufs, sems, h_v):
        tok0 = _sc_worker_id() * tpw
        lane = lax.broadcasted_iota(jnp.int32, (NL,), 0)
        zero = jnp.zeros((NL,), F32)

        def compute(item, buf):
            @pl.when(item % nit == 0)
            def _():
                pltpu.sync_copy(x_hbm.at[pl.ds(tok0 + (item // nit) * G, G)], x_v)

            tok = (item % nit) // ipt
            hvs = [zero for _ in range(R // NL)]
            for rb in range(R // RB):
                def body(g, accs):
                    sub = g // (LANES // GW)
                    base = (g % (LANES // GW)) * GW
                    xs = [plsc.bitcast(x_v[tok, pl.ds(pl.multiple_of(g * GW + jj * NL, NL), NL)], BF16)
                          for jj in range(NV)]
                    out = []
                    for j in range(RB):
                        ps = [plsc.bitcast(buf[rb * RB + j, sub, pl.ds(pl.multiple_of(base + jj * NL, NL), NL)],
                                           BF16) * xs[jj] for jj in range(NV)]
                        hi, lo = _sc_widen_pair((ps[0] + ps[1]) + (ps[2] + ps[3]))
                        out.append((accs[j] + hi) + lo)
                    return tuple(out)
                accs = lax.fori_loop(0, DW // GW, body, tuple(zero for _ in range(RB)))
                for j in range(RB):
                    r = rb * RB + j
                    hvs[r // NL] = jnp.where(lane == (r % NL), jnp.sum(accs[j]), hvs[r // NL])
            for q in range(R // NL):
                h_v[item, pl.ds(q * NL, NL)] = hvs[q]

        pltpu.sync_copy(idx_hbm.at[pl.ds(tok0 * ipt, tpw * ipt)], idx_v)
        _sc_gather_loop(u_hbm, idx_v, bufs, sems, tpw * ipt, compute)
        pltpu.sync_copy(h_v, h_hbm.at[pl.ds(tok0 * ipt, tpw * ipt)])

    return k(u, idx4, x)


def _peer_combine(v, idx4, w4, T, D):
    n_rows, R = idx4.shape
    ipt = n_rows // T
    G = SC_COMBINE_GROUP
    nit = G * ipt
    tpw = T // SC_WORKERS
    NL = SC_LANES
    DW = D // 2
    half = DW // 2
    nv = half // NL
    RG = 4
    tile = (R, DW // LANES, LANES)

    @functools.partial(
        pl.kernel, mesh=_sc_mesh(), compiler_params=pltpu.CompilerParams(needs_layout_passes=False),
        out_type=jax.ShapeDtypeStruct((T, D), F32),
        scratch_types=[pltpu.VMEM((tpw * ipt, R), jnp.int32), pltpu.VMEM((tpw * ipt, R), jnp.int32),
                       [pltpu.VMEM(tile, jnp.int32)] * SC_RING, [pltpu.SemaphoreType.DMA] * SC_RING,
                       pltpu.VMEM((G, D), F32)],
        name="peer_combine")
    def k(v_hbm, idx_hbm, w_hbm, o_hbm, idx_v, w_v, bufs, sems, out_v):
        tok0 = _sc_worker_id() * tpw
        zero = jnp.zeros((NL,), F32)

        def flush(group):
            pltpu.sync_copy(out_v, o_hbm.at[pl.ds(tok0 + group * G, G)])

        def compute(item, buf):
            @pl.when(item % nit == 0)
            def _():
                @pl.when(item > 0)
                def _():
                    flush(item // nit - 1)

                @pl.loop(0, G)
                def _(t):
                    @pl.loop(0, D // NL)
                    def _(i):
                        out_v[t, pl.ds(pl.multiple_of(i * NL, NL), NL)] = zero

            tok = (item % nit) // ipt
            item_vec = jnp.full((NL,), item, jnp.int32)
            for hf in range(2):
                def body(rg, accs):
                    ws = [plsc.bitcast(plsc.load_gather(
                        w_v, [item_vec, jnp.full((NL,), rg * RG + rr, jnp.int32)]), BF16) for rr in range(RG)]
                    his, los = [], []
                    for i in range(nv):
                        word = hf * half + i * NL
                        ps = [plsc.bitcast(buf[rg * RG + rr, word // LANES, pl.ds(word % LANES, NL)], BF16) * ws[rr]
                              for rr in range(RG)]
                        hi, lo = _sc_widen_pair((ps[0] + ps[1]) + (ps[2] + ps[3]))
                        his.append(accs[i] + hi)
                        los.append(accs[nv + i] + lo)
                    return tuple(his + los)
                accs = lax.fori_loop(0, R // RG, body, tuple(zero for _ in range(2 * nv)))
                for i in range(nv):
                    word = hf * half + i * NL
                    plsc.addupdate(out_v.at[tok, pl.ds(word, NL)], accs[i])
                    plsc.addupdate(out_v.at[tok, pl.ds(DW + word, NL)], accs[nv + i])

        pltpu.sync_copy(idx_hbm.at[pl.ds(tok0 * ipt, tpw * ipt)], idx_v)
        pltpu.sync_copy(w_hbm.at[pl.ds(tok0 * ipt, tpw * ipt)], w_v)
        _sc_gather_loop(v_hbm, idx_v, bufs, sems, tpw * ipt, compute)
        flush(tpw // G - 1)

    return k(v, idx4, w4)


def _act_kernel(h_ref, g_ref, w_ref):
    w = _gelu(h_ref[...]) * g_ref[...]
    w_ref[...] = _pack_bf16_pair(w, w)


def _peer_act(hraw, gate):
    T, n = gate.shape
    tm = PEER_ACT_TM
    spec = pl.BlockSpec((tm, n), lambda i: (i, 0))
    return pl.pallas_call(
        _act_kernel, grid=(T // tm,), in_specs=[spec, spec], out_specs=spec,
        out_shape=jax.ShapeDtypeStruct((T, n), jnp.int32),
        compiler_params=pltpu.CompilerParams(dimension_semantics=("parallel",)),
        name="peer_act",
    )(hraw, gate)


def _final_kernel(h_ref, p_ref, g_ref, y_ref):
    y = h_ref[...] + p_ref[...]
    ms = jnp.mean(y * y, axis=-1, keepdims=True)
    y_ref[...] = y * lax.rsqrt(ms + EPS) * g_ref[...]


def _final(h, po, g):
    T, D = h.shape
    tm = FINAL_TM
    spec = pl.BlockSpec((tm, D), lambda i: (i, 0))
    return pl.pallas_call(
        _final_kernel, grid=(T // tm,),
        in_specs=[spec, spec, pl.BlockSpec((1, D), lambda i: (0, 0))], out_specs=spec,
        out_shape=jax.ShapeDtypeStruct((T, D), F32),
        compiler_params=pltpu.CompilerParams(dimension_semantics=("parallel",)),
        name="final_norm",
    )(h, po, g)


def kernel(x, norm1_g, w_in, hgrn_lb_fwd, hgrn_lb_bwd, hgrn_out_g, diff_lam_q1, diff_lam_k1,
           diff_lam_q2, diff_lam_k2, diff_out_g, rel_bias, w_out, norm2_g, peer_w_q,
           peer_sub_keys, peer_u, peer_v, final_g):
    B, L, D = x.shape

    w_r = w_in[LAYER].astype(BF16)
    f32 = jnp.float32
    lam_init = 0.8 - 0.6 * math.exp(-0.3 * LAYER)
    lam = (jnp.exp(jnp.sum(diff_lam_q1[LAYER].astype(f32) * diff_lam_k1[LAYER].astype(f32)))
           - jnp.exp(jnp.sum(diff_lam_q2[LAYER].astype(f32) * diff_lam_k2[LAYER].astype(f32))) + lam_init)
    lam = lam.reshape(1, 1)
    bias_ext = _rel_bias_ext(rel_bias, L)
    sk = peer_sub_keys[LAYER].reshape(2 * PEER_HEADS, PEER_NKEYS, PEER_DKEY).astype(BF16)
    wq = peer_w_q[LAYER].reshape(D, -1).astype(BF16)
    wo = w_out[LAYER].astype(BF16)
    u3, v3 = _pack_table_sc(peer_u[LAYER]), _pack_table_sc(peer_v[LAYER])

    bc = B // BATCH_CHUNKS
    tc = bc * L
    rows = tc * PEER_HEADS * PEER_TOPK // SC_GATHER_ROWS
    outs = []
    pending = {}
    combined = {}
    x2 = x.reshape(B * L, D)
    for c in range(BATCH_CHUNKS + CHUNK_LAG + FINAL_LAG):
        live = c < BATCH_CHUNKS
        g1 = norm1_g[LAYER][None, :]
        if c - CHUNK_LAG - FINAL_LAG in combined:
            h_prev, po = combined.pop(c - CHUNK_LAG - FINAL_LAG)
            if live:
                g1, po = lax.optimization_barrier((g1, po))
            outs.append(_final(h_prev, po, final_g[None, :]).reshape(bc, L, D))
        if CHUNK_LAG <= c < BATCH_CHUNKS + CHUNK_LAG:
            h, idx4, gate, hraw = pending.pop(c - CHUNK_LAG)
            wts = _peer_act(hraw.reshape(tc, -1), gate)
            if live:
                g1, wts = lax.optimization_barrier((g1, wts))
            combined[c - CHUNK_LAG] = (h, _peer_combine(v3, idx4, wts.reshape(rows, SC_GATHER_ROWS), tc, D))
        if live:
            zf, pb = _inproj(x2, g1, w_r, c * tc, tc)
            zf3 = zf.reshape(bc, L, -1)
            pb3 = pb.reshape(bc, L, -1)
            o_h = _hgrn(zf3, pb3, hgrn_lb_fwd, hgrn_lb_bwd, hgrn_out_g[LAYER][None, :])
            o_d = _attn(pb3, bias_ext, lam, diff_out_g[LAYER][None, :])
            h, hn, idx, gate = _route(x2, c * tc, o_h.reshape(tc, -1), o_d.reshape(tc, -1), wo,
                                      norm2_g[LAYER][None, :], wq, sk)
            idx4 = idx.reshape(rows, SC_GATHER_ROWS)
            pending[c] = (h, idx4, gate, _peer_hidden(u3, idx4, hn))
    return jnp.concatenate(outs, axis=0)
```

```python
import functools
import math

import numpy as np
import jax
import jax.numpy as jnp
from jax import lax
from jax.experimental import pallas as pl
from jax.experimental.pallas import tpu as pltpu
from jax.experimental.pallas import tpu_sc as plsc

F32 = jnp.float32
BF16 = jnp.bfloat16
EPS = 1e-6

HGRN_HEADS = 4
HGRN_DK = 128
DIFF_HEADS = 4
DIFF_HALF = 64
REL_BUCKETS = 32
REL_MAX_DIST = 128
PEER_HEADS = 8
PEER_NKEYS = 128
PEER_DKEY = 128
PEER_TOPK = 16
LAYER = 0

LANES = 128
SUBLANES = 8
VMEM_LIMIT = 48 * 1024 * 1024

INPROJ_TM = 512
HG_C = 64
HG_LEVELS = (64, 32, 16, 8, 4, 2)
ATTN_TQ = 256
ROUTE_TM = 256
UNITS_PER_ROW = 2
UNIT_LAG = 4
UNIT_FINAL_LAG = 2
PEER_ACT_TM = 2048
FINAL_TM = 512

SC_CORES = 2
SC_SUBCORES = 16
SC_LANES = 16
SC_WORKERS = SC_CORES * SC_SUBCORES
SC_GATHER_ROWS = 64
SC_TOKEN_GROUP = 32
SC_COMBINE_GROUP = 16
SC_RING = 2


def _dot(a, b):
    return jnp.dot(a, b, preferred_element_type=F32)


def _dot_nt(a, b):
    return lax.dot_general(a, b, (((1,), (1,)), ((), ())), preferred_element_type=F32)


def _dot_tn(a, b):
    return lax.dot_general(a, b, (((0,), (0,)), ((), ())), preferred_element_type=F32)


def _silu(x):
    return x * (1.0 / (1.0 + jnp.exp(-x)))


BF16_HI_MASK = -65536


def _pack_bf16_pair(hi, lo):
    bits = lambda a: lax.bitcast_convert_type(a.astype(BF16).astype(F32), jnp.int32)
    return (bits(hi) & BF16_HI_MASK) | lax.shift_right_logical(bits(lo), 16)


def _inproj_kernel(x_ref, g_ref, w_ref, zf_ref, pb_ref):
    x = x_ref[...]
    ms = jnp.mean(x * x, axis=-1, keepdims=True)
    xn = (x * lax.rsqrt(ms + EPS) * g_ref[...]).astype(BF16)
    hw = zf_ref.shape[1] // 2
    zf_ref[...] = _dot(xn, w_ref[:, hw:3 * hw])
    pb_ref[:, 0:hw] = _dot(xn, w_ref[:, 0:hw]).astype(BF16)
    for j in range(3, 8):
        y = _dot(xn, w_ref[:, j * hw:(j + 1) * hw])
        if j == 5:
            y = y * (DIFF_HALF ** -0.5)
        pb_ref[:, (j - 2) * hw:(j - 1) * hw] = y.astype(BF16)


def _inproj(x2, g, w, tok0, T):
    D = x2.shape[1]
    N = w.shape[1]
    nz = 2 * HGRN_HEADS * HGRN_DK
    tm = INPROJ_TM
    off = tok0 // tm
    return pl.pallas_call(
        _inproj_kernel,
        grid=(T // tm,),
        in_specs=[
            pl.BlockSpec((tm, D), lambda i: (i + off, 0)),
            pl.BlockSpec((1, D), lambda i: (0, 0)),
            pl.BlockSpec((D, N), lambda i: (0, 0)),
        ],
        out_specs=[
            pl.BlockSpec((tm, nz), lambda i: (i, 0)),
            pl.BlockSpec((tm, N - nz), lambda i: (i, 0)),
        ],
        out_shape=[
            jax.ShapeDtypeStruct((T, nz), F32),
            jax.ShapeDtypeStruct((T, N - nz), BF16),
        ],
        compiler_params=pltpu.CompilerParams(
            dimension_semantics=("parallel",), vmem_limit_bytes=VMEM_LIMIT),
        name="inproj",
    )(x2, g, w)


def _hgrn_consts():
    C = HG_C
    r = np.arange(C)
    t = r[:, None]
    u = r[None, :]
    blocks = [u <= t, u > t]
    masks = [np.eye(C, dtype=bool)]
    for B in HG_LEVELS:
        half = B // 2
        a = (r // B) * B
        m = (a + half - 1)[:, None]
        upper = (r - a) >= half
        blocks.append(np.where(upper[:, None], (u > m) & (u <= t), (u > t) & (u <= m)))
        same = a[:, None] == a[None, :]
        masks.append(same & upper[:, None] & (~upper)[None, :])
    m_f = np.concatenate(blocks, 0).astype(np.float32)
    m_b = np.concatenate([b[::-1, ::-1] for b in blocks], 0).astype(np.float32)
    k_f = np.stack(masks).astype(np.float32)
    k_b = np.ascontiguousarray(k_f.transpose(0, 2, 1))
    return m_f, m_b, k_f, k_b


def _hgrn_gates(z, tab):
    tabf = tab.astype(F32)
    e = jnp.exp(tabf - jnp.max(tabf, axis=0, keepdims=True))
    lb = jnp.sum(e[0:LAYER + 1], axis=0, keepdims=True) / jnp.sum(e, axis=0, keepdims=True)
    log_lb = jnp.log(lb)
    log_1m = jnp.log1p(-lb)
    ez = jnp.exp(-jnp.abs(z))
    l1p = jnp.log1p(ez)
    log_sig = jnp.minimum(z, 0.0) - l1p
    c = log_1m + log_sig
    hi = jnp.maximum(log_lb, c)
    lo = jnp.minimum(log_lb, c)
    log_f = hi + jnp.log1p(jnp.exp(lo - hi))
    sig_neg = jnp.where(z >= 0.0, ez, 1.0) / (1.0 + ez)
    k = (1.0 - lb) * sig_neg
    return log_f, k


def _hgrn_kernel(lbf_ref, lbb_ref, og_ref, mf_ref, mb_ref, kf_ref, kb_ref,
                 q_ref, zf_ref, zb_ref, v_ref, g_ref, o_ref, of_s, ob_s):
    C = HG_C
    L = q_ref.shape[0]
    n = L // C
    dv = v_ref.shape[1]

    def chunk(c, st, m_ref, k_ref, z_ref, tab_ref, forward):
        sl = pl.ds(pl.multiple_of(c * C, C), C)
        qh = q_ref[sl, :].astype(F32)
        q = _silu(qh)
        v = v_ref[sl, :]
        log_f, k = _hgrn_gates(z_ref[sl, :], tab_ref[...])
        lf_hi = log_f.astype(BF16)
        lf_lo = (log_f - lf_hi.astype(F32)).astype(BF16)
        m = m_ref[...]
        e = jnp.exp(_dot(m, lf_hi) + _dot(m, lf_lo))
        e_b = e[0:C]
        e_s = e[C:2 * C]
        dec = e[C - 1:C] if forward else e[0:1]
        o = _dot_nt((q * e_b).astype(BF16), st.astype(BF16))
        a = _dot_nt(q.astype(BF16), k.astype(BF16)) * k_ref[0]
        for l in range(len(HG_LEVELS)):
            e_l = e[(2 + l) * C:(3 + l) * C]
            a = a + _dot_nt((q * e_l).astype(BF16), (k * e_l).astype(BF16)) * k_ref[l + 1]
        o = o + _dot(a.astype(BF16), v)
        st = st * dec + _dot_tn(v, (k * e_s).astype(BF16))
        return sl, o, st

    st0 = jnp.zeros((dv, q_ref.shape[1]), F32)

    def both(i, carry):
        st_f, st_b = carry
        sl_f, o_f, st_f = chunk(i, st_f, mf_ref, kf_ref, zf_ref, lbf_ref, True)
        sl_b, o_b, st_b = chunk(n - 1 - i, st_b, mb_ref, kb_ref, zb_ref, lbb_ref, False)
        of_s[sl_f, :] = o_f
        ob_s[sl_b, :] = o_b
        return st_f, st_b

    lax.fori_loop(0, n, both, (st0, st0))

    def finish(c, _):
        sl = pl.ds(pl.multiple_of(c * C, C), C)
        tot = of_s[sl, :] + ob_s[sl, :]
        ms = jnp.mean(tot * tot, axis=-1, keepdims=True)
        y = tot * lax.rsqrt(ms + EPS) * og_ref[...]
        o_ref[sl, :] = (y * _silu(g_ref[sl, :].astype(F32))).astype(o_ref.dtype)
        return 0

    lax.fori_loop(0, n, finish, 0)


def _hgrn(zf3, pb3, lb_f, lb_b, out_g):
    B, L, _ = zf3.shape
    H, dk = HGRN_HEADS, HGRN_DK
    m_f, m_b, k_f, k_b = _hgrn_consts()
    nlev = k_f.shape[0]
    full2 = lambda b, h: (0, 0)
    full3 = lambda b, h: (0, 0, 0)
    seq = lambda off: pl.BlockSpec((None, L, dk), lambda b, h: (b, 0, off + h))
    return pl.pallas_call(
        _hgrn_kernel,
        grid=(B, H),
        in_specs=[
            pl.BlockSpec((lb_f.shape[0], dk), lambda b, h: (0, h)),
            pl.BlockSpec((lb_b.shape[0], dk), lambda b, h: (0, h)),
            pl.BlockSpec((1, dk), full2),
            pl.BlockSpec(m_f.shape, full2),
            pl.BlockSpec(m_b.shape, full2),
            pl.BlockSpec((nlev, HG_C, HG_C), full3),
            pl.BlockSpec((nlev, HG_C, HG_C), full3),
            seq(0),
            seq(0),
            seq(H),
            seq(H),
            seq(2 * H),
        ],
        out_specs=pl.BlockSpec((None, L, dk), lambda b, h: (b, 0, h)),
        out_shape=jax.ShapeDtypeStruct((B, L, H * dk), BF16),
        scratch_shapes=[pltpu.VMEM((L, dk), F32), pltpu.VMEM((L, dk), F32)],
        compiler_params=pltpu.CompilerParams(
            dimension_semantics=("parallel", "parallel"), vmem_limit_bytes=VMEM_LIMIT),
        name="hgrn",
    )(lb_f, lb_b, out_g, jnp.asarray(m_f, BF16), jnp.asarray(m_b, BF16),
      jnp.asarray(k_f), jnp.asarray(k_b), pb3, zf3, zf3, pb3, pb3)


def _t5_bucket(rel):
    nb = REL_BUCKETS // 2
    ret = jnp.where(rel > 0, nb, 0)
    n = jnp.abs(rel)
    max_exact = nb // 2
    nf = jnp.maximum(n, 1).astype(jnp.float32)
    large = max_exact + (jnp.log(nf / max_exact) / math.log(REL_MAX_DIST / max_exact)
                         * (nb - max_exact)).astype(jnp.int32)
    large = jnp.minimum(large, nb - 1)
    return ret + jnp.where(n < max_exact, n, large)


def _rel_bias_ext(rel_bias, L):
    j = jnp.arange(2 * L - LANES, dtype=jnp.int32)
    ql = jnp.arange(LANES, dtype=jnp.int32)
    rel = j[None, :] - (L - LANES) - ql[:, None]
    bucket = _t5_bucket(rel)
    tab = rel_bias.astype(F32)
    out = jnp.zeros((tab.shape[1],) + bucket.shape, F32)
    for b in range(REL_BUCKETS):
        out = jnp.where((bucket == b)[None], tab[b][:, None, None], out)
    return out


def _attn_kernel(lam_ref, q_ref, k_ref, v_ref, bias_ref, og_ref, o_ref, *, lam_init):
    tq = q_ref.shape[0]
    L = k_ref.shape[0]
    qi = pl.program_id(2)
    lam = lam_ref[0, 0]
    q = q_ref[...]
    k = k_ref[...]
    lane = lax.broadcasted_iota(jnp.int32, q.shape, 1)
    zero = jnp.zeros_like(q)
    q0 = jnp.where(lane < DIFF_HALF, q, zero)
    q1 = jnp.where(lane >= DIFF_HALF, q, zero)
    parts = []
    for j in range(tq // LANES):
        off = pl.multiple_of(L - LANES - (qi * tq + j * LANES), LANES)
        parts.append(bias_ref[:, pl.ds(off, L)])
    bias = jnp.concatenate(parts, axis=0) if len(parts) > 1 else parts[0]

    def soft(qm):
        s = _dot_nt(qm, k) + bias
        e = jnp.exp(s - jnp.max(s, axis=-1, keepdims=True))
        return e, 1.0 / jnp.sum(e, axis=-1, keepdims=True)

    e0, r0 = soft(q0)
    e1, r1 = soft(q1)
    w = e0 * r0 - e1 * (lam * r1)
    o = _dot(w.astype(BF16), v_ref[...])
    ms = jnp.mean(o * o, axis=-1, keepdims=True)
    y = o * lax.rsqrt(ms + EPS) * og_ref[...] * (1.0 - lam_init)
    o_ref[...] = y.astype(o_ref.dtype)


def _attn(pb3, bias_ext, lam, out_g):
    B, L, _ = pb3.shape
    H, dh = DIFF_HEADS, 2 * DIFF_HALF
    tq = ATTN_TQ
    base = (3 * HGRN_HEADS * HGRN_DK) // dh
    lam_init = 0.8 - 0.6 * math.exp(-0.3 * LAYER)
    return pl.pallas_call(
        functools.partial(_attn_kernel, lam_init=lam_init),
        grid=(B, H, L // tq),
        in_specs=[
            pl.BlockSpec(memory_space=pltpu.SMEM),
            pl.BlockSpec((None, tq, dh), lambda b, h, i: (b, i, base + h)),
            pl.BlockSpec((None, L, dh), lambda b, h, i: (b, 0, base + H + h)),
            pl.BlockSpec((None, L, dh), lambda b, h, i: (b, 0, base + 2 * H + h)),
            pl.BlockSpec((None, LANES, 2 * L - LANES), lambda b, h, i: (h, 0, 0)),
            pl.BlockSpec((1, dh), lambda b, h, i: (0, 0)),
        ],
        out_specs=pl.BlockSpec((None, tq, dh), lambda b, h, i: (b, i, h)),
        out_shape=jax.ShapeDtypeStruct((B, L, H * dh), BF16),
        compiler_params=pltpu.CompilerParams(
            dimension_semantics=("parallel", "parallel", "parallel"),
            vmem_limit_bytes=VMEM_LIMIT),
        name="attn",
    )(lam, pb3, pb3, pb3, bias_ext, out_g)


def _cand_layout():
    K = PEER_TOPK
    groups = [("a", 0, 0), ("a", 0, 8), ("a", 1, 0), ("a", 2, 0), ("a", 3, 0),
              ("b", 0, 8), ("b", 0, 0), ("b", 1, 0), ("b", 2, 0)]
    seen = set()
    pos, valid = [], []
    for kind, fixed, start in groups:
        for r in range(SUBLANES):
            a, b = (fixed, start + r) if kind == "a" else (start + r, fixed)
            ok = (a + 1) * (b + 1) <= K and (a, b) not in seen
            if ok:
                seen.add((a, b))
            pos.append(a * K + b if ok else K * K + len(pos))
            valid.append(ok)
    assert len(seen) == sum(K // (a + 1) for a in range(K))
    return groups, np.array(pos, np.int32), np.array(valid, bool)


def _extract_topk(s, key, payload, k, big):
    n, tm = s.shape
    S = SUBLANES
    s3 = s.reshape(n // S, S, tm)
    key3 = key.reshape(n // S, S, tm)
    pay3 = None if payload is key else payload.reshape(n // S, S, tm)
    slot = lax.broadcasted_iota(jnp.int32, (k // S, S, tm), 0) * S + lax.broadcasted_iota(
        jnp.int32, (k // S, S, tm), 1)

    def all_reduce(x3, op):
        r = x3[0]
        for g in range(1, x3.shape[0]):
            r = op(r, x3[g])
        for sh in (S // 2, S // 4, S // 8):
            r = op(r, pltpu.roll(r, sh, axis=0))
        return r

    def body(j, carry):
        s3, vals, pay = carry
        m = all_reduce(s3, jnp.maximum)
        kk = all_reduce(jnp.where(s3 == m[None], key3, big), jnp.minimum)
        sel = key3 == kk[None]
        p = kk if pay3 is None else all_reduce(jnp.where(sel, pay3, 0), jnp.add)
        vals = jnp.where(slot == j, m[None], vals)
        pay = jnp.where(slot == j, p[None], pay)
        s3 = jnp.where(sel, -jnp.inf, s3)
        return s3, vals, pay

    init = (s3, jnp.zeros((k // S, S, tm), F32), jnp.zeros((k // S, S, tm), jnp.int32))
    _, vals, pay = lax.fori_loop(0, k, body, init)
    return vals.reshape(k, tm), pay.reshape(k, tm)


def _route_kernel(x_ref, oh_ref, od_ref, woh_ref, wod_ref, g2_ref, wq_ref, sk_ref, cpos_ref, cmask_ref,
                  h_ref, hn_ref, idx_ref, gate_ref, q_s, tv_s, ti_s, gs_s, is_s, *, groups):
    K = PEER_TOPK
    tm = x_ref.shape[0]
    h = x_ref[...] + _dot(oh_ref[...], woh_ref[...]) + _dot(od_ref[...], wod_ref[...])
    h_ref[...] = h
    hn = h * lax.rsqrt(jnp.mean(h * h, axis=-1, keepdims=True) + EPS) * g2_ref[...]
    half = hn.shape[1] // 2
    hn_ref[...] = _pack_bf16_pair(hn[:, :half], hn[:, half:])
    q_s[...] = _dot(hn.astype(BF16), wq_ref[...]).astype(BF16)

    key_iota = lax.broadcasted_iota(jnp.int32, (PEER_NKEYS, tm), 0)

    def half_topk(hp, _):
        col = pl.multiple_of(hp * PEER_DKEY, PEER_DKEY)
        s = _dot_nt(sk_ref[hp], q_s[:, pl.ds(col, PEER_DKEY)])
        vals, idxs = _extract_topk(s, key_iota, key_iota, K, PEER_NKEYS)
        tv_s[hp] = vals
        ti_s[hp] = idxs
        return 0

    lax.fori_loop(0, 2 * PEER_HEADS, half_topk, 0)

    cpos = cpos_ref[...]
    cmask = cmask_ref[...]

    def head(hd, _):
        s0, s1 = tv_s[2 * hd], tv_s[2 * hd + 1]
        i0, i1 = ti_s[2 * hd] * PEER_NKEYS, ti_s[2 * hd + 1]
        cs, ci = [], []
        for kind, fixed, start in groups:
            if kind == "a":
                cs.append(s0[fixed:fixed + 1] + s1[start:start + SUBLANES])
                ci.append(i0[fixed:fixed + 1] + i1[start:start + SUBLANES])
            else:
                cs.append(s0[start:start + SUBLANES] + s1[fixed:fixed + 1])
                ci.append(i0[start:start + SUBLANES] + i1[fixed:fixed + 1])
        cand = jnp.concatenate(cs, axis=0) + cmask
        cidx = jnp.concatenate(ci, axis=0)
        best, eidx = _extract_topk(cand, cpos, cidx, K, 2 * K * K)
        ex = jnp.exp(best - best[0:1])
        gate = ex / jnp.sum(ex, axis=0, keepdims=True)
        row = pl.ds(pl.multiple_of(hd * K, K), K)
        gs_s[row, :] = gate
        is_s[row, :] = eidx.astype(F32)
        return 0

    lax.fori_loop(0, PEER_HEADS, head, 0)
    gate_ref[...] = gs_s[...].T
    idx_ref[...] = is_s[...].T.astype(jnp.int32)


def _route(x2, tok0, oh2, od2, w_out, g2, w_q, sub_keys):
    T = oh2.shape[0]
    D = x2.shape[1]
    tm = ROUTE_TM
    off = tok0 // tm
    K = PEER_TOPK
    nh = oh2.shape[1]
    nq = w_q.shape[1]
    npk = PEER_HEADS * K
    groups, pos, valid = _cand_layout()
    ncand = pos.shape[0]
    cpos = jnp.asarray(np.broadcast_to(pos[:, None], (ncand, tm)))
    cmask = jnp.asarray(np.broadcast_to(np.where(valid, 0.0, -np.inf).astype(np.float32)[:, None], (ncand, tm)))
    row = lambda i: (i, 0)
    full2 = lambda i: (0, 0)
    return pl.pallas_call(
        functools.partial(_route_kernel, groups=groups),
        grid=(T // tm,),
        in_specs=[
            pl.BlockSpec((tm, D), lambda i: (i + off, 0)),
            pl.BlockSpec((tm, nh), row),
            pl.BlockSpec((tm, nh), row),
            pl.BlockSpec((nh, D), full2),
            pl.BlockSpec((nh, D), lambda i: (1, 0)),
            pl.BlockSpec((1, D), full2),
            pl.BlockSpec((D, nq), full2),
            pl.BlockSpec(sub_keys.shape, lambda i: (0, 0, 0)),
            pl.BlockSpec((ncand, tm), full2),
            pl.BlockSpec((ncand, tm), full2),
        ],
        out_specs=[
            pl.BlockSpec((tm, D), row),
            pl.BlockSpec((tm, D // 2), row),
            pl.BlockSpec((tm, npk), row),
            pl.BlockSpec((tm, npk), row),
        ],
        out_shape=[
            jax.ShapeDtypeStruct((T, D), F32),
            jax.ShapeDtypeStruct((T, D // 2), jnp.int32),
            jax.ShapeDtypeStruct((T, npk), jnp.int32),
            jax.ShapeDtypeStruct((T, npk), F32),
        ],
        scratch_shapes=[
            pltpu.VMEM((tm, nq), BF16),
            pltpu.VMEM((2 * PEER_HEADS, K, tm), F32),
            pltpu.VMEM((2 * PEER_HEADS, K, tm), jnp.int32),
            pltpu.VMEM((npk, tm), F32),
            pltpu.VMEM((npk, tm), F32),
        ],
        compiler_params=pltpu.CompilerParams(
            dimension_semantics=("parallel",), vmem_limit_bytes=VMEM_LIMIT),
        name="route",
    )(x2, oh2, od2, w_out, w_out, g2, w_q, sub_keys, cpos, cmask)


def _gelu(x):
    return 0.5 * x * (1.0 + lax.erf(x * (1.0 / math.sqrt(2.0))))


def _sc_mesh():
    return plsc.VectorSubcoreMesh(core_axis_name="c", subcore_axis_name="s")


def _sc_worker_id():
    return lax.axis_index("s") * SC_CORES + lax.axis_index("c")


def _sc_widen_pair(s):
    si = plsc.bitcast(s, jnp.int32)
    return plsc.bitcast(si & BF16_HI_MASK, F32), plsc.bitcast(si << 16, F32)


def _pack_table_sc(tab):
    E, D = tab.shape
    DW = D // 2
    NL = SC_LANES
    rpw = E // SC_WORKERS
    RBK = 32

    def rne(x):
        bits = plsc.bitcast(x, jnp.int32)
        return bits + (0x7FFF + ((bits >> 16) & 1))

    @functools.partial(
        pl.kernel, mesh=_sc_mesh(), compiler_params=pltpu.CompilerParams(needs_layout_passes=False),
        out_type=jax.ShapeDtypeStruct((E, DW // LANES, LANES), jnp.int32),
        scratch_types=[pltpu.VMEM((RBK, D), F32), pltpu.VMEM((RBK, DW // LANES, LANES), jnp.int32)],
        name="peer_pack")
    def k(t_hbm, o_hbm, in_v, out_v):
        r0 = _sc_worker_id() * rpw

        @pl.loop(0, rpw // RBK)
        def _(blk):
            base = r0 + blk * RBK
            pltpu.sync_copy(t_hbm.at[pl.ds(base, RBK)], in_v)

            @pl.loop(0, RBK)
            def _(r):
                for j in range(DW // NL):
                    hi = rne(in_v[r, pl.ds(j * NL, NL)])
                    lo = rne(in_v[r, pl.ds(DW + j * NL, NL)])
                    out_v[r, (j * NL) // LANES, pl.ds((j * NL) % LANES, NL)] = (
                        (hi & BF16_HI_MASK) | lax.shift_right_logical(lo, 16))

            pltpu.sync_copy(out_v, o_hbm.at[pl.ds(base, RBK)])

    return k(tab)


def _sc_gather_loop(tab_hbm, idx_v, bufs, sems, n_items, compute):
    nb = len(bufs)

    def start(item, b):
        pltpu.async_copy(tab_hbm.at[idx_v.at[item]], bufs[b], sems[b])

    def wait(b):
        pltpu.make_async_copy(tab_hbm.at[idx_v.at[0]], bufs[b], sems[b]).wait()

    for p in range(nb - 1):
        start(p, p)

    @pl.loop(0, pl.cdiv(n_items, nb))
    def _(i):
        for b in range(nb):
            it = i * nb + b

            @pl.when(it < n_items)
            def _():
                @pl.when(it + nb - 1 < n_items)
                def _():
                    start(it + nb - 1, (b + nb - 1) % nb)

                wait(b)
                compute(it, bufs[b])


def _peer_hidden(u, idx4, x):
    T, DW = x.shape
    n_rows, R = idx4.shape
    ipt = n_rows // T
    G = SC_TOKEN_GROUP
    nit = G * ipt
    tpw = T // SC_WORKERS
    NL = SC_LANES
    RB = 8
    NV = 4
    GW = NV * NL
    tile = (R, DW // LANES, LANES)

    @functools.partial(
        pl.kernel, mesh=_sc_mesh(), compiler_params=pltpu.CompilerParams(needs_layout_passes=False),
        out_type=jax.ShapeDtypeStruct((n_rows, R), F32),
        scratch_types=[pltpu.VMEM((tpw * ipt, R), jnp.int32), pltpu.VMEM((G, DW), jnp.int32),
                       [pltpu.VMEM(tile, jnp.int32)] * SC_RING, [pltpu.SemaphoreType.DMA] * SC_RING,
                       pltpu.VMEM((tpw * ipt, R), F32)],
        name="peer_hidden")
    def k(u_hbm, idx_hbm, x_hbm, h_hbm, idx_v, x_v, bufs, sems, h_v):
        tok0 = _sc_worker_id() * tpw
        lane = lax.broadcasted_iota(jnp.int32, (NL,), 0)
        zero = jnp.zeros((NL,), F32)

        def compute(item, buf):
            @pl.when(item % nit == 0)
            def _():
                pltpu.sync_copy(x_hbm.at[pl.ds(tok0 + (item // nit) * G, G)], x_v)

            tok = (item % nit) // ipt
            hvs = [zero for _ in range(R // NL)]
            for rb in range(R // RB):
                def body(g, accs):
                    sub = g // (LANES // GW)
                    base = (g % (LANES // GW)) * GW
                    xs = [plsc.bitcast(x_v[tok, pl.ds(pl.multiple_of(g * GW + jj * NL, NL), NL)], BF16)
                          for jj in range(NV)]
                    out = []
                    for j in range(RB):
                        ps = [plsc.bitcast(buf[rb * RB + j, sub, pl.ds(pl.multiple_of(base + jj * NL, NL), NL)],
                                           BF16) * xs[jj] for jj in range(NV)]
                        hi, lo = _sc_widen_pair((ps[0] + ps[1]) + (ps[2] + ps[3]))
                        out.append((accs[j] + hi) + lo)
                    return tuple(out)
                accs = lax.fori_loop(0, DW // GW, body, tuple(zero for _ in range(RB)))
                for j in range(RB):
                    r = rb * RB + j
                    hvs[r // NL] = jnp.where(lane == (r % NL), jnp.sum(accs[j]), hvs[r // NL])
            for q in range(R // NL):
                h_v[item, pl.ds(q * NL, NL)] = hvs[q]

        pltpu.sync_copy(idx_hbm.at[pl.ds(tok0 * ipt, tpw * ipt)], idx_v)
        _sc_gather_loop(u_hbm, idx_v, bufs, sems, tpw * ipt, compute)
        pltpu.sync_copy(h_v, h_hbm.at[pl.ds(tok0 * ipt, tpw * ipt)])

    return k(u, idx4, x)


def _peer_combine(v, idx4, w4, T, D):
    n_rows, R = idx4.shape
    ipt = n_rows // T
    G = SC_COMBINE_GROUP
    nit = G * ipt
    tpw = T // SC_WORKERS
    NL = SC_LANES
    DW = D // 2
    half = DW // 2
    nv = half // NL
    RG = 4
    tile = (R, DW // LANES, LANES)

    @functools.partial(
        pl.kernel, mesh=_sc_mesh(), compiler_params=pltpu.CompilerParams(needs_layout_passes=False),
        out_type=jax.ShapeDtypeStruct((T, D), F32),
        scratch_types=[pltpu.VMEM((tpw * ipt, R), jnp.int32), pltpu.VMEM((tpw * ipt, R), jnp.int32),
                       [pltpu.VMEM(tile, jnp.int32)] * SC_RING, [pltpu.SemaphoreType.DMA] * SC_RING,
                       pltpu.VMEM((G, D), F32)],
        name="peer_combine")
    def k(v_hbm, idx_hbm, w_hbm, o_hbm, idx_v, w_v, bufs, sems, out_v):
        tok0 = _sc_worker_id() * tpw
        zero = jnp.zeros((NL,), F32)

        def flush(group):
            pltpu.sync_copy(out_v, o_hbm.at[pl.ds(tok0 + group * G, G)])

        def compute(item, buf):
            @pl.when(item % nit == 0)
            def _():
                @pl.when(item > 0)
                def _():
                    flush(item // nit - 1)

                @pl.loop(0, G)
                def _(t):
                    @pl.loop(0, D // NL)
                    def _(i):
                        out_v[t, pl.ds(pl.multiple_of(i * NL, NL), NL)] = zero

            tok = (item % nit) // ipt
            item_vec = jnp.full((NL,), item, jnp.int32)
            for hf in range(2):
                def body(rg, accs):
                    ws = [plsc.bitcast(plsc.load_gather(
                        w_v, [item_vec, jnp.full((NL,), rg * RG + rr, jnp.int32)]), BF16) for rr in range(RG)]
                    his, los = [], []
                    for i in range(nv):
                        word = hf * half + i * NL
                        ps = [plsc.bitcast(buf[rg * RG + rr, word // LANES, pl.ds(word % LANES, NL)], BF16) * ws[rr]
                              for rr in range(RG)]
                        hi, lo = _sc_widen_pair((ps[0] + ps[1]) + (ps[2] + ps[3]))
                        his.append(accs[i] + hi)
                        los.append(accs[nv + i] + lo)
                    return tuple(his + los)
                accs = lax.fori_loop(0, R // RG, body, tuple(zero for _ in range(2 * nv)))
                for i in range(nv):
                    word = hf * half + i * NL
                    plsc.addupdate(out_v.at[tok, pl.ds(word, NL)], accs[i])
                    plsc.addupdate(out_v.at[tok, pl.ds(DW + word, NL)], accs[nv + i])

        pltpu.sync_copy(idx_hbm.at[pl.ds(tok0 * ipt, tpw * ipt)], idx_v)
        pltpu.sync_copy(w_hbm.at[pl.ds(tok0 * ipt, tpw * ipt)], w_v)
        _sc_gather_loop(v_hbm, idx_v, bufs, sems, tpw * ipt, compute)
        flush(tpw // G - 1)

    return k(v, idx4, w4)


def _act_kernel(h_ref, g_ref, w_ref):
    w = _gelu(h_ref[...]) * g_ref[...]
    w_ref[...] = _pack_bf16_pair(w, w)


def _peer_act(hraw, gate):
    T, n = gate.shape
    tm = min(PEER_ACT_TM, T)
    spec = pl.BlockSpec((tm, n), lambda i: (i, 0))
    return pl.pallas_call(
        _act_kernel, grid=(T // tm,), in_specs=[spec, spec], out_specs=spec,
        out_shape=jax.ShapeDtypeStruct((T, n), jnp.int32),
        compiler_params=pltpu.CompilerParams(dimension_semantics=("parallel",)),
        name="peer_act",
    )(hraw, gate)


def _final_kernel(h_ref, p_ref, g_ref, y_ref):
    y = h_ref[...] + p_ref[...]
    ms = jnp.mean(y * y, axis=-1, keepdims=True)
    y_ref[...] = y * lax.rsqrt(ms + EPS) * g_ref[...]


def _final(h, po, g):
    T, D = h.shape
    tm = FINAL_TM
    spec = pl.BlockSpec((tm, D), lambda i: (i, 0))
    return pl.pallas_call(
        _final_kernel, grid=(T // tm,),
        in_specs=[spec, spec, pl.BlockSpec((1, D), lambda i: (0, 0))], out_specs=spec,
        out_shape=jax.ShapeDtypeStruct((T, D), F32),
        compiler_params=pltpu.CompilerParams(dimension_semantics=("parallel",)),
        name="final_norm",
    )(h, po, g)


def kernel(x, norm1_g, w_in, hgrn_lb_fwd, hgrn_lb_bwd, hgrn_out_g, diff_lam_q1, diff_lam_k1,
           diff_lam_q2, diff_lam_k2, diff_out_g, rel_bias, w_out, norm2_g, peer_w_q,
           peer_sub_keys, peer_u, peer_v, final_g):
    B, L, D = x.shape

    w_r = w_in[LAYER].astype(BF16)
    f32 = jnp.float32
    lam_init = 0.8 - 0.6 * math.exp(-0.3 * LAYER)
    lam = (jnp.exp(jnp.sum(diff_lam_q1[LAYER].astype(f32) * diff_lam_k1[LAYER].astype(f32)))
           - jnp.exp(jnp.sum(diff_lam_q2[LAYER].astype(f32) * diff_lam_k2[LAYER].astype(f32))) + lam_init)
    lam = lam.reshape(1, 1)
    bias_ext = _rel_bias_ext(rel_bias, L)
    sk = peer_sub_keys[LAYER].reshape(2 * PEER_HEADS, PEER_NKEYS, PEER_DKEY).astype(BF16)
    wq = peer_w_q[LAYER].reshape(D, -1).astype(BF16)
    wo = w_out[LAYER].astype(BF16)
    u3, v3 = _pack_table_sc(peer_u[LAYER]), _pack_table_sc(peer_v[LAYER])

    tu = L // UNITS_PER_ROW
    n_units = B * UNITS_PER_ROW
    outs = []
    pending = {}
    combined = {}
    x2 = x.reshape(B * L, D)
    for u in range(n_units + UNIT_LAG + UNIT_FINAL_LAG):
        live = u < n_units
        g2 = norm2_g[LAYER][None, :]
        if u - UNIT_LAG - UNIT_FINAL_LAG in combined:
            h_prev, po = combined.pop(u - UNIT_LAG - UNIT_FINAL_LAG)
            if live:
                g2, po = lax.optimization_barrier((g2, po))
            outs.append(_final(h_prev, po, final_g[None, :]))
        if UNIT_LAG <= u < n_units + UNIT_LAG:
            h, idx4, gate, hraw = pending.pop(u - UNIT_LAG)
            wts = _peer_act(hraw.reshape(tu, -1), gate)
            if live:
                g2, wts = lax.optimization_barrier((g2, wts))
            combined[u - UNIT_LAG] = (h, _peer_combine(v3, idx4, wts.reshape(idx4.shape), tu, D))
        if live:
            part = u % UNITS_PER_ROW
            if part == 0:
                row = u // UNITS_PER_ROW
                zf, pb = _inproj(x2, norm1_g[LAYER][None, :], w_r, row * L, L)
                zf3 = zf.reshape(1, L, -1)
                pb3 = pb.reshape(1, L, -1)
                o_h = _hgrn(zf3, pb3, hgrn_lb_fwd, hgrn_lb_bwd, hgrn_out_g[LAYER][None, :]).reshape(L, -1)
                o_d = _attn(pb3, bias_ext, lam, diff_out_g[LAYER][None, :]).reshape(L, -1)
            sl = slice(part * tu, (part + 1) * tu)
            h, hn, idx, gate = _route(x2, u * tu, o_h[sl], o_d[sl], wo, g2, wq, sk)
            idx4 = idx.reshape(-1, SC_GATHER_ROWS)
            pending[u] = (h, idx4, gate, _peer_hidden(u3, idx4, hn))
    return jnp.concatenate(outs, axis=0).reshape(B, L, D)
```

```python
import functools
import math

import numpy as np
import jax
import jax.numpy as jnp
from jax import lax
from jax.experimental import pallas as pl
from jax.experimental.pallas import tpu as pltpu
from jax.experimental.pallas import tpu_sc as plsc

F32 = jnp.float32
BF16 = jnp.bfloat16
EPS = 1e-6

HGRN_HEADS = 4
HGRN_DK = 128
DIFF_HEADS = 4
DIFF_HALF = 64
REL_BUCKETS = 32
REL_MAX_DIST = 128
PEER_HEADS = 8
PEER_NKEYS = 128
PEER_DKEY = 128
PEER_TOPK = 16
LAYER = 0

LANES = 128
SUBLANES = 8
VMEM_LIMIT = 48 * 1024 * 1024

INPROJ_TM = 512
HG_C = 64
HG_LEVELS = (64, 32, 16, 8, 4, 2)
ATTN_TQ = 256
ROUTE_TM = 256
BATCH_CHUNKS = 16
CHUNK_LAG = 2
FINAL_LAG = 1
PEER_ACT_TM = 2048
FINAL_TM = 512

SC_CORES = 2
SC_SUBCORES = 16
SC_LANES = 16
SC_WORKERS = SC_CORES * SC_SUBCORES
SC_GATHER_ROWS = 64
SC_TOKEN_GROUP = 32
SC_COMBINE_GROUP = 16
SC_RING = 2


def _dot(a, b):
    return jnp.dot(a, b, preferred_element_type=F32)


def _dot_nt(a, b):
    return lax.dot_general(a, b, (((1,), (1,)), ((), ())), preferred_element_type=F32)


def _dot_tn(a, b):
    return lax.dot_general(a, b, (((0,), (0,)), ((), ())), preferred_element_type=F32)


def _silu(x):
    return x * (1.0 / (1.0 + jnp.exp(-x)))


BF16_HI_MASK = -65536


def _pack_bf16_pair(hi, lo):
    bits = lambda a: lax.bitcast_convert_type(a.astype(BF16).astype(F32), jnp.int32)
    return (bits(hi) & BF16_HI_MASK) | lax.shift_right_logical(bits(lo), 16)


def _inproj_kernel(x_ref, g_ref, w_ref, zf_ref, pb_ref):
    x = x_ref[...]
    ms = jnp.mean(x * x, axis=-1, keepdims=True)
    xn = (x * lax.rsqrt(ms + EPS) * g_ref[...]).astype(BF16)
    hw = zf_ref.shape[1] // 2
    zf_ref[...] = _dot(xn, w_ref[:, hw:3 * hw])
    pb_ref[:, 0:hw] = _dot(xn, w_ref[:, 0:hw]).astype(BF16)
    for j in range(3, 8):
        y = _dot(xn, w_ref[:, j * hw:(j + 1) * hw])
        if j == 5:
            y = y * (DIFF_HALF ** -0.5)
        pb_ref[:, (j - 2) * hw:(j - 1) * hw] = y.astype(BF16)


def _inproj(x2, g, w, tok0, T):
    D = x2.shape[1]
    N = w.shape[1]
    nz = 2 * HGRN_HEADS * HGRN_DK
    tm = INPROJ_TM
    off = tok0 // tm
    return pl.pallas_call(
        _inproj_kernel,
        grid=(T // tm,),
        in_specs=[
            pl.BlockSpec((tm, D), lambda i: (i + off, 0)),
            pl.BlockSpec((1, D), lambda i: (0, 0)),
            pl.BlockSpec((D, N), lambda i: (0, 0)),
        ],
        out_specs=[
            pl.BlockSpec((tm, nz), lambda i: (i, 0)),
            pl.BlockSpec((tm, N - nz), lambda i: (i, 0)),
        ],
        out_shape=[
            jax.ShapeDtypeStruct((T, nz), F32),
            jax.ShapeDtypeStruct((T, N - nz), BF16),
        ],
        compiler_params=pltpu.CompilerParams(
            dimension_semantics=("parallel",), vmem_limit_bytes=VMEM_LIMIT),
        name="inproj",
    )(x2, g, w)


def _hgrn_consts():
    C = HG_C
    r = np.arange(C)
    t = r[:, None]
    u = r[None, :]
    blocks = [u <= t, u > t]
    masks = [np.eye(C, dtype=bool)]
    for B in HG_LEVELS:
        half = B // 2
        a = (r // B) * B
        m = (a + half - 1)[:, None]
        upper = (r - a) >= half
        blocks.append(np.where(upper[:, None], (u > m) & (u <= t), (u > t) & (u <= m)))
        same = a[:, None] == a[None, :]
        masks.append(same & upper[:, None] & (~upper)[None, :])
    m_f = np.concatenate(blocks, 0).astype(np.float32)
    m_b = np.concatenate([b[::-1, ::-1] for b in blocks], 0).astype(np.float32)
    k_f = np.stack(masks).astype(np.float32)
    k_b = np.ascontiguousarray(k_f.transpose(0, 2, 1))
    return m_f, m_b, k_f, k_b


def _hgrn_gates(z, tab):
    tabf = tab.astype(F32)
    e = jnp.exp(tabf - jnp.max(tabf, axis=0, keepdims=True))
    lb = jnp.sum(e[0:LAYER + 1], axis=0, keepdims=True) / jnp.sum(e, axis=0, keepdims=True)
    log_lb = jnp.log(lb)
    log_1m = jnp.log1p(-lb)
    ez = jnp.exp(-jnp.abs(z))
    l1p = jnp.log1p(ez)
    log_sig = jnp.minimum(z, 0.0) - l1p
    c = log_1m + log_sig
    hi = jnp.maximum(log_lb, c)
    lo = jnp.minimum(log_lb, c)
    log_f = hi + jnp.log1p(jnp.exp(lo - hi))
    sig_neg = jnp.where(z >= 0.0, ez, 1.0) / (1.0 + ez)
    k = (1.0 - lb) * sig_neg
    return log_f, k


def _hgrn_kernel(lbf_ref, lbb_ref, og_ref, mf_ref, mb_ref, kf_ref, kb_ref,
                 q_ref, zf_ref, zb_ref, v_ref, g_ref, o_ref, of_s, ob_s):
    C = HG_C
    L = q_ref.shape[0]
    n = L // C
    dv = v_ref.shape[1]

    def chunk(c, st, m_ref, k_ref, z_ref, tab_ref, forward):
        sl = pl.ds(pl.multiple_of(c * C, C), C)
        qh = q_ref[sl, :].astype(F32)
        q = _silu(qh)
        v = v_ref[sl, :]
        log_f, k = _hgrn_gates(z_ref[sl, :], tab_ref[...])
        lf_hi = log_f.astype(BF16)
        lf_lo = (log_f - lf_hi.astype(F32)).astype(BF16)
        m = m_ref[...]
        e = jnp.exp(_dot(m, lf_hi) + _dot(m, lf_lo))
        e_b = e[0:C]
        e_s = e[C:2 * C]
        dec = e[C - 1:C] if forward else e[0:1]
        o = _dot_nt((q * e_b).astype(BF16), st.astype(BF16))
        a = _dot_nt(q.astype(BF16), k.astype(BF16)) * k_ref[0]
        for l in range(len(HG_LEVELS)):
            e_l = e[(2 + l) * C:(3 + l) * C]
            a = a + _dot_nt((q * e_l).astype(BF16), (k * e_l).astype(BF16)) * k_ref[l + 1]
        o = o + _dot(a.astype(BF16), v)
        st = st * dec + _dot_tn(v, (k * e_s).astype(BF16))
        return sl, o, st

    st0 = jnp.zeros((dv, q_ref.shape[1]), F32)

    def both(i, carry):
        st_f, st_b = carry
        sl_f, o_f, st_f = chunk(i, st_f, mf_ref, kf_ref, zf_ref, lbf_ref, True)
        sl_b, o_b, st_b = chunk(n - 1 - i, st_b, mb_ref, kb_ref, zb_ref, lbb_ref, False)
        of_s[sl_f, :] = o_f
        ob_s[sl_b, :] = o_b
        return st_f, st_b

    lax.fori_loop(0, n, both, (st0, st0))

    def finish(c, _):
        sl = pl.ds(pl.multiple_of(c * C, C), C)
        tot = of_s[sl, :] + ob_s[sl, :]
        ms = jnp.mean(tot * tot, axis=-1, keepdims=True)
        y = tot * lax.rsqrt(ms + EPS) * og_ref[...]
        o_ref[sl, :] = (y * _silu(g_ref[sl, :].astype(F32))).astype(o_ref.dtype)
        return 0

    lax.fori_loop(0, n, finish, 0)


def _hgrn(zf3, pb3, lb_f, lb_b, out_g):
    B, L, _ = zf3.shape
    H, dk = HGRN_HEADS, HGRN_DK
    m_f, m_b, k_f, k_b = _hgrn_consts()
    nlev = k_f.shape[0]
    full2 = lambda b, h: (0, 0)
    full3 = lambda b, h: (0, 0, 0)
    seq = lambda off: pl.BlockSpec((None, L, dk), lambda b, h: (b, 0, off + h))
    return pl.pallas_call(
        _hgrn_kernel,
        grid=(B, H),
        in_specs=[
            pl.BlockSpec((lb_f.shape[0], dk), lambda b, h: (0, h)),
            pl.BlockSpec((lb_b.shape[0], dk), lambda b, h: (0, h)),
            pl.BlockSpec((1, dk), full2),
            pl.BlockSpec(m_f.shape, full2),
            pl.BlockSpec(m_b.shape, full2),
            pl.BlockSpec((nlev, HG_C, HG_C), full3),
            pl.BlockSpec((nlev, HG_C, HG_C), full3),
            seq(0),
            seq(0),
            seq(H),
            seq(H),
            seq(2 * H),
        ],
        out_specs=pl.BlockSpec((None, L, dk), lambda b, h: (b, 0, h)),
        out_shape=jax.ShapeDtypeStruct((B, L, H * dk), BF16),
        scratch_shapes=[pltpu.VMEM((L, dk), F32), pltpu.VMEM((L, dk), F32)],
        compiler_params=pltpu.CompilerParams(
            dimension_semantics=("parallel", "parallel"), vmem_limit_bytes=VMEM_LIMIT),
        name="hgrn",
    )(lb_f, lb_b, out_g, jnp.asarray(m_f, BF16), jnp.asarray(m_b, BF16),
      jnp.asarray(k_f), jnp.asarray(k_b), pb3, zf3, zf3, pb3, pb3)


def _t5_bucket(rel):
    nb = REL_BUCKETS // 2
    ret = jnp.where(rel > 0, nb, 0)
    n = jnp.abs(rel)
    max_exact = nb // 2
    nf = jnp.maximum(n, 1).astype(jnp.float32)
    large = max_exact + (jnp.log(nf / max_exact) / math.log(REL_MAX_DIST / max_exact)
                         * (nb - max_exact)).astype(jnp.int32)
    large = jnp.minimum(large, nb - 1)
    return ret + jnp.where(n < max_exact, n, large)


def _rel_bias_ext(rel_bias, L):
    j = jnp.arange(2 * L - LANES, dtype=jnp.int32)
    ql = jnp.arange(LANES, dtype=jnp.int32)
    rel = j[None, :] - (L - LANES) - ql[:, None]
    bucket = _t5_bucket(rel)
    tab = rel_bias.astype(F32)
    out = jnp.zeros((tab.shape[1],) + bucket.shape, F32)
    for b in range(REL_BUCKETS):
        out = jnp.where((bucket == b)[None], tab[b][:, None, None], out)
    return out


def _attn_kernel(lam_ref, q_ref, k_ref, v_ref, bias_ref, og_ref, o_ref, *, lam_init):
    tq = q_ref.shape[0]
    L = k_ref.shape[0]
    qi = pl.program_id(2)
    lam = lam_ref[0, 0]
    q = q_ref[...]
    k = k_ref[...]
    lane = lax.broadcasted_iota(jnp.int32, q.shape, 1)
    zero = jnp.zeros_like(q)
    q0 = jnp.where(lane < DIFF_HALF, q, zero)
    q1 = jnp.where(lane >= DIFF_HALF, q, zero)
    parts = []
    for j in range(tq // LANES):
        off = pl.multiple_of(L - LANES - (qi * tq + j * LANES), LANES)
        parts.append(bias_ref[:, pl.ds(off, L)])
    bias = jnp.concatenate(parts, axis=0) if len(parts) > 1 else parts[0]

    def soft(qm):
        s = _dot_nt(qm, k) + bias
        e = jnp.exp(s - jnp.max(s, axis=-1, keepdims=True))
        return e, 1.0 / jnp.sum(e, axis=-1, keepdims=True)

    e0, r0 = soft(q0)
    e1, r1 = soft(q1)
    w = e0 * r0 - e1 * (lam * r1)
    o = _dot(w.astype(BF16), v_ref[...])
    ms = jnp.mean(o * o, axis=-1, keepdims=True)
    y = o * lax.rsqrt(ms + EPS) * og_ref[...] * (1.0 - lam_init)
    o_ref[...] = y.astype(o_ref.dtype)


def _attn(pb3, bias_ext, lam, out_g):
    B, L, _ = pb3.shape
    H, dh = DIFF_HEADS, 2 * DIFF_HALF
    tq = ATTN_TQ
    base = (3 * HGRN_HEADS * HGRN_DK) // dh
    lam_init = 0.8 - 0.6 * math.exp(-0.3 * LAYER)
    return pl.pallas_call(
        functools.partial(_attn_kernel, lam_init=lam_init),
        grid=(B, H, L // tq),
        in_specs=[
            pl.BlockSpec(memory_space=pltpu.SMEM),
            pl.BlockSpec((None, tq, dh), lambda b, h, i: (b, i, base + h)),
            pl.BlockSpec((None, L, dh), lambda b, h, i: (b, 0, base + H + h)),
            pl.BlockSpec((None, L, dh), lambda b, h, i: (b, 0, base + 2 * H + h)),
            pl.BlockSpec((None, LANES, 2 * L - LANES), lambda b, h, i: (h, 0, 0)),
            pl.BlockSpec((1, dh), lambda b, h, i: (0, 0)),
        ],
        out_specs=pl.BlockSpec((None, tq, dh), lambda b, h, i: (b, i, h)),
        out_shape=jax.ShapeDtypeStruct((B, L, H * dh), BF16),
        compiler_params=pltpu.CompilerParams(
            dimension_semantics=("parallel", "parallel", "parallel"),
            vmem_limit_bytes=VMEM_LIMIT),
        name="attn",
    )(lam, pb3, pb3, pb3, bias_ext, out_g)


def _cand_layout():
    K = PEER_TOPK
    groups = [("a", 0, 0), ("a", 0, 8), ("a", 1, 0), ("a", 2, 0), ("a", 3, 0),
              ("b", 0, 8), ("b", 0, 0), ("b", 1, 0), ("b", 2, 0)]
    seen = set()
    pos, valid = [], []
    for kind, fixed, start in groups:
        for r in range(SUBLANES):
            a, b = (fixed, start + r) if kind == "a" else (start + r, fixed)
            ok = (a + 1) * (b + 1) <= K and (a, b) not in seen
            if ok:
                seen.add((a, b))
            pos.append(a * K + b if ok else K * K + len(pos))
            valid.append(ok)
    assert len(seen) == sum(K // (a + 1) for a in range(K))
    return groups, np.array(pos, np.int32), np.array(valid, bool)


def _extract_topk(s, key, payload, k, big):
    n, tm = s.shape
    S = SUBLANES
    s3 = s.reshape(n // S, S, tm)
    key3 = key.reshape(n // S, S, tm)
    pay3 = None if payload is key else payload.reshape(n // S, S, tm)
    slot = lax.broadcasted_iota(jnp.int32, (k // S, S, tm), 0) * S + lax.broadcasted_iota(
        jnp.int32, (k // S, S, tm), 1)

    def all_reduce(x3, op):
        r = x3[0]
        for g in range(1, x3.shape[0]):
            r = op(r, x3[g])
        for sh in (S // 2, S // 4, S // 8):
            r = op(r, pltpu.roll(r, sh, axis=0))
        return r

    def body(j, carry):
        s3, vals, pay = carry
        m = all_reduce(s3, jnp.maximum)
        kk = all_reduce(jnp.where(s3 == m[None], key3, big), jnp.minimum)
        sel = key3 == kk[None]
        p = kk if pay3 is None else all_reduce(jnp.where(sel, pay3, 0), jnp.add)
        vals = jnp.where(slot == j, m[None], vals)
        pay = jnp.where(slot == j, p[None], pay)
        s3 = jnp.where(sel, -jnp.inf, s3)
        return s3, vals, pay

    init = (s3, jnp.zeros((k // S, S, tm), F32), jnp.zeros((k // S, S, tm), jnp.int32))
    _, vals, pay = lax.fori_loop(0, k, body, init)
    return vals.reshape(k, tm), pay.reshape(k, tm)


def _route_kernel(x_ref, oh_ref, od_ref, woh_ref, wod_ref, g2_ref, wq_ref, sk_ref, cpos_ref, cmask_ref,
                  h_ref, hn_ref, idx_ref, gate_ref, q_s, tv_s, ti_s, gs_s, is_s, *, groups):
    K = PEER_TOPK
    tm = x_ref.shape[0]
    h = x_ref[...] + _dot(oh_ref[...], woh_ref[...]) + _dot(od_ref[...], wod_ref[...])
    h_ref[...] = h
    hn = h * lax.rsqrt(jnp.mean(h * h, axis=-1, keepdims=True) + EPS) * g2_ref[...]
    half = hn.shape[1] // 2
    hn_ref[...] = _pack_bf16_pair(hn[:, :half], hn[:, half:])
    q_s[...] = _dot(hn.astype(BF16), wq_ref[...]).astype(BF16)

    key_iota = lax.broadcasted_iota(jnp.int32, (PEER_NKEYS, tm), 0)

    def half_topk(hp, _):
        col = pl.multiple_of(hp * PEER_DKEY, PEER_DKEY)
        s = _dot_nt(sk_ref[hp], q_s[:, pl.ds(col, PEER_DKEY)])
        vals, idxs = _extract_topk(s, key_iota, key_iota, K, PEER_NKEYS)
        tv_s[hp] = vals
        ti_s[hp] = idxs
        return 0

    lax.fori_loop(0, 2 * PEER_HEADS, half_topk, 0)

    cpos = cpos_ref[...]
    cmask = cmask_ref[...]

    def head(hd, _):
        s0, s1 = tv_s[2 * hd], tv_s[2 * hd + 1]
        i0, i1 = ti_s[2 * hd] * PEER_NKEYS, ti_s[2 * hd + 1]
        cs, ci = [], []
        for kind, fixed, start in groups:
            if kind == "a":
                cs.append(s0[fixed:fixed + 1] + s1[start:start + SUBLANES])
                ci.append(i0[fixed:fixed + 1] + i1[start:start + SUBLANES])
            else:
                cs.append(s0[start:start + SUBLANES] + s1[fixed:fixed + 1])
                ci.append(i0[start:start + SUBLANES] + i1[fixed:fixed + 1])
        cand = jnp.concatenate(cs, axis=0) + cmask
        cidx = jnp.concatenate(ci, axis=0)
        best, eidx = _extract_topk(cand, cpos, cidx, K, 2 * K * K)
        ex = jnp.exp(best - best[0:1])
        gate = ex / jnp.sum(ex, axis=0, keepdims=True)
        row = pl.ds(pl.multiple_of(hd * K, K), K)
        gs_s[row, :] = gate
        is_s[row, :] = eidx.astype(F32)
        return 0

    lax.fori_loop(0, PEER_HEADS, head, 0)
    gate_ref[...] = gs_s[...].T
    idx_ref[...] = is_s[...].T.astype(jnp.int32)


def _route(x2, tok0, oh2, od2, w_out, g2, w_q, sub_keys):
    T = oh2.shape[0]
    D = x2.shape[1]
    tm = ROUTE_TM
    off = tok0 // tm
    K = PEER_TOPK
    nh = oh2.shape[1]
    nq = w_q.shape[1]
    npk = PEER_HEADS * K
    groups, pos, valid = _cand_layout()
    ncand = pos.shape[0]
    cpos = jnp.asarray(np.broadcast_to(pos[:, None], (ncand, tm)))
    cmask = jnp.asarray(np.broadcast_to(np.where(valid, 0.0, -np.inf).astype(np.float32)[:, None], (ncand, tm)))
    row = lambda i: (i, 0)
    full2 = lambda i: (0, 0)
    return pl.pallas_call(
        functools.partial(_route_kernel, groups=groups),
        grid=(T // tm,),
        in_specs=[
            pl.BlockSpec((tm, D), lambda i: (i + off, 0)),
            pl.BlockSpec((tm, nh), row),
            pl.BlockSpec((tm, nh), row),
            pl.BlockSpec((nh, D), full2),
            pl.BlockSpec((nh, D), lambda i: (1, 0)),
            pl.BlockSpec((1, D), full2),
            pl.BlockSpec((D, nq), full2),
            pl.BlockSpec(sub_keys.shape, lambda i: (0, 0, 0)),
            pl.BlockSpec((ncand, tm), full2),
            pl.BlockSpec((ncand, tm), full2),
        ],
        out_specs=[
            pl.BlockSpec((tm, D), row),
            pl.BlockSpec((tm, D // 2), row),
            pl.BlockSpec((tm, npk), row),
            pl.BlockSpec((tm, npk), row),
        ],
        out_shape=[
            jax.ShapeDtypeStruct((T, D), F32),
            jax.ShapeDtypeStruct((T, D // 2), jnp.int32),
            jax.ShapeDtypeStruct((T, npk), jnp.int32),
            jax.ShapeDtypeStruct((T, npk), F32),
        ],
        scratch_shapes=[
            pltpu.VMEM((tm, nq), BF16),
            pltpu.VMEM((2 * PEER_HEADS, K, tm), F32),
            pltpu.VMEM((2 * PEER_HEADS, K, tm), jnp.int32),
            pltpu.VMEM((npk, tm), F32),
            pltpu.VMEM((npk, tm), F32),
        ],
        compiler_params=pltpu.CompilerParams(
            dimension_semantics=("parallel",), vmem_limit_bytes=VMEM_LIMIT),
        name="route",
    )(x2, oh2, od2, w_out, w_out, g2, w_q, sub_keys, cpos, cmask)


def _gelu(x):
    return 0.5 * x * (1.0 + lax.erf(x * (1.0 / math.sqrt(2.0))))


def _sc_mesh():
    return plsc.VectorSubcoreMesh(core_axis_name="c", subcore_axis_name="s")


def _sc_worker_id():
    return lax.axis_index("s") * SC_CORES + lax.axis_index("c")


def _sc_widen_pair(s):
    si = plsc.bitcast(s, jnp.int32)
    return plsc.bitcast(si & BF16_HI_MASK, F32), plsc.bitcast(si << 16, F32)


def _pack_tables_sc(tab_a, tab_b):
    E, D = tab_a.shape
    DW = D // 2
    NL = SC_LANES
    rpw = E // SC_WORKERS
    RBK = 32

    def rne(x):
        bits = plsc.bitcast(x, jnp.int32)
        return bits + (0x7FFF + ((bits >> 16) & 1))

    @functools.partial(
        pl.kernel, mesh=_sc_mesh(), compiler_params=pltpu.CompilerParams(needs_layout_passes=False),
        out_type=[jax.ShapeDtypeStruct((E, DW // LANES, LANES), jnp.int32)] * 2,
        scratch_types=[pltpu.VMEM((RBK, D), F32), pltpu.VMEM((RBK, DW // LANES, LANES), jnp.int32)],
        name="peer_pack")
    def k(ta_hbm, tb_hbm, oa_hbm, ob_hbm, in_v, out_v):
        r0 = _sc_worker_id() * rpw

        for t_hbm, o_hbm in ((ta_hbm, oa_hbm), (tb_hbm, ob_hbm)):
            @pl.loop(0, rpw // RBK)
            def _(blk):
                base = r0 + blk * RBK
                pltpu.sync_copy(t_hbm.at[pl.ds(base, RBK)], in_v)

                @pl.loop(0, RBK)
                def _(r):
                    for j in range(DW // NL):
                        hi = rne(in_v[r, pl.ds(j * NL, NL)])
                        lo = rne(in_v[r, pl.ds(DW + j * NL, NL)])
                        out_v[r, (j * NL) // LANES, pl.ds((j * NL) % LANES, NL)] = (
                            (hi & BF16_HI_MASK) | lax.shift_right_logical(lo, 16))

                pltpu.sync_copy(out_v, o_hbm.at[pl.ds(base, RBK)])

    return k(tab_a, tab_b)


def _sc_gather_loop(tab_hbm, idx_v, bufs, sems, n_items, compute):
    nb = len(bufs)

    def start(item, b):
        pltpu.async_copy(tab_hbm.at[idx_v.at[item]], bufs[b], sems[b])

    def wait(b):
        pltpu.make_async_copy(tab_hbm.at[idx_v.at[0]], bufs[b], sems[b]).wait()

    for p in range(nb - 1):
        start(p, p)

    @pl.loop(0, pl.cdiv(n_items, nb))
    def _(i):
        for b in range(nb):
            it = i * nb + b

            @pl.when(it < n_items)
            def _():
                @pl.when(it + nb - 1 < n_items)
                def _():
                    start(it + nb - 1, (b + nb - 1) % nb)

                wait(b)
                compute(it, bufs[b])


def _peer_hidden(u, idx4, x):
    T, DW = x.shape
    n_rows, R = idx4.shape
    ipt = n_rows // T
    G = SC_TOKEN_GROUP
    nit = G * ipt
    tpw = T // SC_WORKERS
    NL = SC_LANES
    RB = 8
    NV = 4
    GW = NV * NL
    tile = (R, DW // LANES, LANES)

    @functools.partial(
        pl.kernel, mesh=_sc_mesh(), compiler_params=pltpu.CompilerParams(needs_layout_passes=False),
        out_type=jax.ShapeDtypeStruct((n_rows, R), F32),
        scratch_types=[pltpu.VMEM((tpw * ipt, R), jnp.int32), pltpu.VMEM((G, DW), jnp.int32),
                       [pltpu.VMEM(tile, jnp.int32)] * SC_RING, [pltpu.SemaphoreType.DMA] * SC_RING,
                       pltpu.VMEM((tpw * ipt, R), F32)],
        name="peer_hidden")
    def k(u_hbm, idx_hbm, x_hbm, h_hbm, idx_v, x_v, bufs, sems, h_v):
        tok0 = _sc_worker_id() * tpw
        lane = lax.broadcasted_iota(jnp.int32, (NL,), 0)
        zero = jnp.zeros((NL,), F32)

        def compute(item, buf):
            @pl.when(item % nit == 0)
            def _():
                pltpu.sync_copy(x_hbm.at[pl.ds(tok0 + (item // nit) * G, G)], x_v)

            tok = (item % nit) // ipt
            hvs = [zero for _ in range(R // NL)]
            for rb in range(R // RB):
                def body(g, accs):
                    sub = g // (LANES // GW)
                    base = (g % (LANES // GW)) * GW
                    xs = [plsc.bitcast(x_v[tok, pl.ds(pl.multiple_of(g * GW + jj * NL, NL), NL)], BF16)
                          for jj in range(NV)]
                    out = []
                    for j in range(RB):
                        ps = [plsc.bitcast(buf[rb * RB + j, sub, pl.ds(pl.multiple_of(base + jj * NL, NL), NL)],
                                           BF16) * xs[jj] for jj in range(NV)]
                        hi, lo = _sc_widen_pair((ps[0] + ps[1]) + (ps[2] + ps[3]))
                        out.append((accs[j] + hi) + lo)
                    return tuple(out)
                accs = lax.fori_loop(0, DW // GW, body, tuple(zero for _ in range(RB)))
                for j in range(RB):
                    r = rb * RB + j
                    hvs[r // NL] = jnp.where(lane == (r % NL), jnp.sum(accs[j]), hvs[r // NL])
            for q in range(R // NL):
                h_v[item, pl.ds(q * NL, NL)] = hvs[q]

        pltpu.sync_copy(idx_hbm.at[pl.ds(tok0 * ipt, tpw * ipt)], idx_v)
        _sc_gather_loop(u_hbm, idx_v, bufs, sems, tpw * ipt, compute)
        pltpu.sync_copy(h_v, h_hbm.at[pl.ds(tok0 * ipt, tpw * ipt)])

    return k(u, idx4, x)


def _peer_combine(v, idx4, w4, T, D):
    n_rows, R = idx4.shape
    ipt = n_rows // T
    G = SC_COMBINE_GROUP
    nit = G * ipt
    tpw = T // SC_WORKERS
    NL = SC_LANES
    DW = D // 2
    half = DW // 2
    nv = half // NL
    RG = 4
    tile = (R, DW // LANES, LANES)

    @functools.partial(
        pl.kernel, mesh=_sc_mesh(), compiler_params=pltpu.CompilerParams(needs_layout_passes=False),
        out_type=jax.ShapeDtypeStruct((T, D), F32),
        scratch_types=[pltpu.VMEM((tpw * ipt, R), jnp.int32), pltpu.VMEM((tpw * ipt, R), jnp.int32),
                       [pltpu.VMEM(tile, jnp.int32)] * SC_RING, [pltpu.SemaphoreType.DMA] * SC_RING,
                       pltpu.VMEM((G, D), F32)],
        name="peer_combine")
    def k(v_hbm, idx_hbm, w_hbm, o_hbm, idx_v, w_v, bufs, sems, out_v):
        tok0 = _sc_worker_id() * tpw
        zero = jnp.zeros((NL,), F32)

        def flush(group):
            pltpu.sync_copy(out_v, o_hbm.at[pl.ds(tok0 + group * G, G)])

        def compute(item, buf):
            @pl.when(item % nit == 0)
            def _():
                @pl.when(item > 0)
                def _():
                    flush(item // nit - 1)

                @pl.loop(0, G)
                def _(t):
                    @pl.loop(0, D // NL)
                    def _(i):
                        out_v[t, pl.ds(pl.multiple_of(i * NL, NL), NL)] = zero

            tok = (item % nit) // ipt
            item_vec = jnp.full((NL,), item, jnp.int32)
            for hf in range(2):
                def body(rg, accs):
                    ws = [plsc.bitcast(plsc.load_gather(
                        w_v, [item_vec, jnp.full((NL,), rg * RG + rr, jnp.int32)]), BF16) for rr in range(RG)]
                    his, los = [], []
                    for i in range(nv):
                        word = hf * half + i * NL
                        ps = [plsc.bitcast(buf[rg * RG + rr, word // LANES, pl.ds(word % LANES, NL)], BF16) * ws[rr]
                              for rr in range(RG)]
                        hi, lo = _sc_widen_pair((ps[0] + ps[1]) + (ps[2] + ps[3]))
                        his.append(accs[i] + hi)
                        los.append(accs[nv + i] + lo)
                    return tuple(his + los)
                accs = lax.fori_loop(0, R // RG, body, tuple(zero for _ in range(2 * nv)))
                for i in range(nv):
                    word = hf * half + i * NL
                    plsc.addupdate(out_v.at[tok, pl.ds(word, NL)], accs[i])
                    plsc.addupdate(out_v.at[tok, pl.ds(DW + word, NL)], accs[nv + i])

        pltpu.sync_copy(idx_hbm.at[pl.ds(tok0 * ipt, tpw * ipt)], idx_v)
        pltpu.sync_copy(w_hbm.at[pl.ds(tok0 * ipt, tpw * ipt)], w_v)
        _sc_gather_loop(v_hbm, idx_v, bufs, sems, tpw * ipt, compute)
        flush(tpw // G - 1)

    return k(v, idx4, w4)


def _act_kernel(h_ref, g_ref, w_ref):
    w = _gelu(h_ref[...]) * g_ref[...]
    w_ref[...] = _pack_bf16_pair(w, w)


def _peer_act(hraw, gate):
    T, n = gate.shape
    tm = PEER_ACT_TM
    spec = pl.BlockSpec((tm, n), lambda i: (i, 0))
    return pl.pallas_call(
        _act_kernel, grid=(T // tm,), in_specs=[spec, spec], out_specs=spec,
        out_shape=jax.ShapeDtypeStruct((T, n), jnp.int32),
        compiler_params=pltpu.CompilerParams(dimension_semantics=("parallel",)),
        name="peer_act",
    )(hraw, gate)


def _final_kernel(h_ref, p_ref, g_ref, y_ref):
    y = h_ref[...] + p_ref[...]
    ms = jnp.mean(y * y, axis=-1, keepdims=True)
    y_ref[...] = y * lax.rsqrt(ms + EPS) * g_ref[...]


def _final(h, po, g):
    T, D = h.shape
    tm = FINAL_TM
    spec = pl.BlockSpec((tm, D), lambda i: (i, 0))
    return pl.pallas_call(
        _final_kernel, grid=(T // tm,),
        in_specs=[spec, spec, pl.BlockSpec((1, D), lambda i: (0, 0))], out_specs=spec,
        out_shape=jax.ShapeDtypeStruct((T, D), F32),
        compiler_params=pltpu.CompilerParams(dimension_semantics=("parallel",)),
        name="final_norm",
    )(h, po, g)


def kernel(x, norm1_g, w_in, hgrn_lb_fwd, hgrn_lb_bwd, hgrn_out_g, diff_lam_q1, diff_lam_k1,
           diff_lam_q2, diff_lam_k2, diff_out_g, rel_bias, w_out, norm2_g, peer_w_q,
           peer_sub_keys, peer_u, peer_v, final_g):
    B, L, D = x.shape

    w_r = w_in[LAYER].astype(BF16)
    f32 = jnp.float32
    lam_init = 0.8 - 0.6 * math.exp(-0.3 * LAYER)
    lam = (jnp.exp(jnp.sum(diff_lam_q1[LAYER].astype(f32) * diff_lam_k1[LAYER].astype(f32)))
           - jnp.exp(jnp.sum(diff_lam_q2[LAYER].astype(f32) * diff_lam_k2[LAYER].astype(f32))) + lam_init)
    lam = lam.reshape(1, 1)
    bias_ext = _rel_bias_ext(rel_bias, L)
    sk = peer_sub_keys[LAYER].reshape(2 * PEER_HEADS, PEER_NKEYS, PEER_DKEY).astype(BF16)
    wq = peer_w_q[LAYER].reshape(D, -1).astype(BF16)
    wo = w_out[LAYER].astype(BF16)
    u3, v3 = _pack_tables_sc(peer_u[LAYER], peer_v[LAYER])

    bc = B // BATCH_CHUNKS
    tc = bc * L
    rows = tc * PEER_HEADS * PEER_TOPK // SC_GATHER_ROWS
    outs = []
    pending = {}
    combined = {}
    x2 = x.reshape(B * L, D)
    for c in range(BATCH_CHUNKS + CHUNK_LAG + FINAL_LAG):
        live = c < BATCH_CHUNKS
        g1 = norm1_g[LAYER][None, :]
        if c - CHUNK_LAG - FINAL_LAG in combined:
            h_prev, po = combined.pop(c - CHUNK_LAG - FINAL_LAG)
            if live:
                g1, po = lax.optimization_barrier((g1, po))
            outs.append(_final(h_prev, po, final_g[None, :]).reshape(bc, L, D))
        if CHUNK_LAG <= c < BATCH_CHUNKS + CHUNK_LAG:
            h, idx4, gate, hraw = pending.pop(c - CHUNK_LAG)
            wts = _peer_act(hraw.reshape(tc, -1), gate)
            if live:
                g1, wts = lax.optimization_barrier((g1, wts))
            combined[c - CHUNK_LAG] = (h, _peer_combine(v3, idx4, wts.reshape(rows, SC_GATHER_ROWS), tc, D))
        if live:
            zf, pb = _inproj(x2, g1, w_r, c * tc, tc)
            zf3 = zf.reshape(bc, L, -1)
            pb3 = pb.reshape(bc, L, -1)
            o_h = _hgrn(zf3, pb3, hgrn_lb_fwd, hgrn_lb_bwd, hgrn_out_g[LAYER][None, :])
            o_d = _attn(pb3, bias_ext, lam, diff_out_g[LAYER][None, :])
            h, hn, idx, gate = _route(x2, c * tc, o_h.reshape(tc, -1), o_d.reshape(tc, -1), wo,
                                      norm2_g[LAYER][None, :], wq, sk)
            idx4 = idx.reshape(rows, SC_GATHER_ROWS)
            pending[c] = (h, idx4, gate, _peer_hidden(u3, idx4, hn))
    return jnp.concatenate(outs, axis=0)
```
